```python
import math
import jax, jax.numpy as jnp
from jax import lax
import numpy as np

D_MODEL = 2048
BATCH = 16
SEQ = 2048
DEPTH = 4

N_META = 16
GRID_W = 64
Q_BLOCK = 128
ROPE_THETA = 10000.0
NORM_EPS = 1e-6
N_BRANCH = 3

C_CONV = D_MODEL // 2
CONV_K = 31

GQA_HEADS = 8
GQA_KV_HEADS = 2
GQA_HEAD_DIM = 128

MLA_HEADS = 8
MLA_Q_RANK = D_MODEL // 4
MLA_KV_RANK = D_MODEL // 4
MLA_NOPE_DIM = 128
MLA_ROPE_DIM = 64
MLA_V_DIM = 128
MLA_QK_DIM = MLA_NOPE_DIM + MLA_ROPE_DIM

D_FF = -(-8 * D_MODEL // (3 * 256)) * 256

IN_SIZES = (
    2 * C_CONV,
    GQA_HEADS * GQA_HEAD_DIM,
    GQA_KV_HEADS * GQA_HEAD_DIM,
    GQA_KV_HEADS * GQA_HEAD_DIM,
    MLA_Q_RANK,
    MLA_KV_RANK,
    MLA_ROPE_DIM,
    N_BRANCH * D_MODEL,
)
D_IN = sum(IN_SIZES)

kernel_name = 'hybrid_conv_gqa_mla_encoder'


def rmsnorm(x, g):
    xf = x.astype(jnp.float32)
    y = xf * lax.rsqrt(jnp.mean(xf * xf, axis=-1, keepdims=True) + NORM_EPS)
    return (y * g.astype(jnp.float32)).astype(x.dtype)


def layernorm(x, g, b):
    xf = x.astype(jnp.float32)
    mu = jnp.mean(xf, axis=-1, keepdims=True)
    xc = xf - mu
    y = xc * lax.rsqrt(jnp.mean(xc * xc, axis=-1, keepdims=True) + NORM_EPS)
    return (y * g.astype(jnp.float32) + b.astype(jnp.float32)).astype(x.dtype)


def grid_positions(n_tok):
    rows = n_tok // GRID_W
    row = jnp.repeat(jnp.arange(rows, dtype=jnp.float32), GRID_W)
    col = jnp.tile(jnp.arange(GRID_W, dtype=jnp.float32), rows)
    meta = jnp.zeros((N_META,), jnp.float32)
    return jnp.concatenate([meta, row]), jnp.concatenate([meta, col])


def apply_rope(x, pos):
    dim = x.shape[-1]
    half = dim // 2
    inv = ROPE_THETA ** (-jnp.arange(half, dtype=jnp.float32) / half)
    ang = pos[:, None] * inv[None, :]
    cos = jnp.cos(ang)[None, :, None, :].astype(x.dtype)
    sin = jnp.sin(ang)[None, :, None, :].astype(x.dtype)
    x1, x2 = x[..., :half], x[..., half:]
    return jnp.concatenate([x1 * cos - x2 * sin, x2 * cos + x1 * sin], axis=-1)


def axial_rope(x, row_pos, col_pos):
    half = x.shape[-1] // 2
    return jnp.concatenate([apply_rope(x[..., :half], row_pos),
                            apply_rope(x[..., half:], col_pos)], axis=-1)


def _attend(qb, k, v, scale):
    s = jnp.einsum('bqhgd,bkhd->bhgqk', qb, k, preferred_element_type=jnp.float32) * scale
    p = jax.nn.softmax(s, axis=-1).astype(v.dtype)
    return jnp.einsum('bhgqk,bkhe->bqhge', p, v)


def bidir_block_attention(q, k, v, scale):
    b, l, hq, dk = q.shape
    hkv, dv = k.shape[2], v.shape[-1]
    g = hq // hkv
    q = q.reshape(b, l, hkv, g, dk)
    out_meta = _attend(q[:, :N_META], k, v, scale)
    n_tok = l - N_META
    nblk = n_tok // Q_BLOCK
    qr = q[:, N_META:].reshape(b, nblk, Q_BLOCK, hkv, g, dk).transpose(1, 0, 2, 3, 4, 5)
    out_r = lax.map(lambda qb: _attend(qb, k, v, scale), qr)
    out_r = out_r.transpose(1, 0, 2, 3, 4, 5).reshape(b, n_tok, hkv, g, dv)
    out = jnp.concatenate([out_meta, out_r], axis=1)
    return out.reshape(b, l, hq * dv)


def conv_branch(u2, dw, cb, ln_g, ln_b, w_pw):
    a, gte = jnp.split(u2, 2, axis=-1)
    z = a * jax.nn.sigmoid(gte)
    z = lax.conv_general_dilated(z, dw, window_strides=(1,),
                                 padding=[(CONV_K // 2, CONV_K // 2)],
                                 dimension_numbers=('NWC', 'WIO', 'NWC'),
                                 feature_group_count=C_CONV) + cb
    z = jax.nn.silu(layernorm(z, ln_g, ln_b))
    return z @ w_pw


def gqa_branch(q, k, v, qn_g, kn_g, w_o, row_pos, col_pos):
    b, l, _ = q.shape
    q = rmsnorm(q.reshape(b, l, GQA_HEADS, GQA_HEAD_DIM), qn_g)
    k = rmsnorm(k.reshape(b, l, GQA_KV_HEADS, GQA_HEAD_DIM), kn_g)
    v = v.reshape(b, l, GQA_KV_HEADS, GQA_HEAD_DIM)
    q = axial_rope(q, row_pos, col_pos)
    k = axial_rope(k, row_pos, col_pos)
    o = bidir_block_attention(q, k, v, 1.0 / math.sqrt(GQA_HEAD_DIM))
    return o @ w_o


def mla_branch(cq, ckv, kpe, qn_g, w_uq, kvn_g, w_ukv, w_o, row_pos, col_pos):
    b, l, _ = cq.shape
    q = (rmsnorm(cq, qn_g) @ w_uq).reshape(b, l, MLA_HEADS, MLA_QK_DIM)
    q_nope, q_pe = q[..., :MLA_NOPE_DIM], q[..., MLA_NOPE_DIM:]
    q_pe = axial_rope(q_pe, row_pos, col_pos)
    kv = (rmsnorm(ckv, kvn_g) @ w_ukv).reshape(b, l, MLA_HEADS, MLA_NOPE_DIM + MLA_V_DIM)
    k_nope, v = kv[..., :MLA_NOPE_DIM], kv[..., MLA_NOPE_DIM:]
    k_pe = axial_rope(kpe.reshape(b, l, 1, MLA_ROPE_DIM), row_pos, col_pos)
    k = jnp.concatenate([k_nope, jnp.broadcast_to(k_pe, (b, l, MLA_HEADS, MLA_ROPE_DIM))], axis=-1)
    q = jnp.concatenate([q_nope, q_pe], axis=-1)
    o = bidir_block_attention(q, k, v, 1.0 / math.sqrt(MLA_QK_DIM))
    return o @ w_o


def _fwd_setup_inputs(seed: int = 0) -> dict:
    key = jax.random.key(seed)
    ks = jax.random.split(key, 24)

    def nrm(k, shape, scale):
        return jax.random.normal(k, shape, jnp.float32) * scale

    def gain(k, shape):
        return 1.0 + 0.02 * jax.random.normal(k, shape, jnp.float32)

    L = DEPTH
    return {
        'x': nrm(ks[0], (BATCH, SEQ, D_MODEL), 1.0),
        'meta_tokens': nrm(ks[1], (N_META, D_MODEL), 1.0),
        'mix_norm_g': gain(ks[2], (L, D_MODEL)),
        'w_in': nrm(ks[3], (L, D_MODEL, D_IN), D_MODEL ** -0.5),
        'conv_dw': nrm(ks[4], (L, CONV_K, 1, C_CONV), CONV_K ** -0.5),
        'conv_b': nrm(ks[5], (L, C_CONV), 0.02),
        'conv_ln_g': gain(ks[6], (L, C_CONV)),
        'conv_ln_b': nrm(ks[7], (L, C_CONV), 0.02),
        'w_conv_out': nrm(ks[8], (L, C_CONV, D_MODEL), C_CONV ** -0.5),
        'gqa_q_norm_g': gain(ks[9], (L, GQA_HEAD_DIM)),
        'gqa_k_norm_g': gain(ks[10], (L, GQA_HEAD_DIM)),
        'w_gqa_out': nrm(ks[11], (L, GQA_HEADS * GQA_HEAD_DIM, D_MODEL), (GQA_HEADS * GQA_HEAD_DIM) ** -0.5),
        'mla_q_norm_g': gain(ks[12], (L, MLA_Q_RANK)),
        'w_mla_uq': nrm(ks[13], (L, MLA_Q_RANK, MLA_HEADS * MLA_QK_DIM), MLA_Q_RANK ** -0.5),
        'mla_kv_norm_g': gain(ks[14], (L, MLA_KV_RANK)),
        'w_mla_ukv': nrm(ks[15], (L, MLA_KV_RANK, MLA_HEADS * (MLA_NOPE_DIM + MLA_V_DIM)), MLA_KV_RANK ** -0.5),
        'w_mla_out': nrm(ks[16], (L, MLA_HEADS * MLA_V_DIM, D_MODEL), (MLA_HEADS * MLA_V_DIM) ** -0.5),
        'gate_b': nrm(ks[17], (L, N_BRANCH * D_MODEL), 0.1),
        'w_out': nrm(ks[18], (L, D_MODEL, D_MODEL), 0.5 * D_MODEL ** -0.5),
        'ffn_norm_g': gain(ks[19], (L, D_MODEL)),
        'w_ffn_gate': nrm(ks[20], (L, D_MODEL, D_FF), D_MODEL ** -0.5),
        'w_ffn_up': nrm(ks[21], (L, D_MODEL, D_FF), D_MODEL ** -0.5),
        'w_ffn_down': nrm(ks[22], (L, D_FF, D_MODEL), 0.5 * D_FF ** -0.5),
        'final_norm_g': gain(ks[23], (D_MODEL,)),
    }


def _fwd_reference(x, meta_tokens, mix_norm_g, w_in, conv_dw, conv_b, conv_ln_g, conv_ln_b, w_conv_out,
              gqa_q_norm_g, gqa_k_norm_g, w_gqa_out, mla_q_norm_g, w_mla_uq, mla_kv_norm_g,
              w_mla_ukv, w_mla_out, gate_b, w_out, ffn_norm_g, w_ffn_gate, w_ffn_up, w_ffn_down,
              final_norm_g):
    b, n_tok, d = x.shape
    meta = jnp.broadcast_to(meta_tokens.astype(x.dtype)[None], (b, N_META, d))
    h = jnp.concatenate([meta, x], axis=1)
    row_pos, col_pos = grid_positions(n_tok)

    split_points = []
    acc = 0
    for s in IN_SIZES[:-1]:
        acc += s
        split_points.append(acc)

    for i in range(DEPTH):
        u = rmsnorm(h, mix_norm_g[i])
        proj = u @ w_in[i]
        (u_conv, q_g, k_g, v_g, c_q, c_kv, k_pe, gate_logits) = jnp.split(proj, split_points, axis=-1)

        y_a = conv_branch(u_conv, conv_dw[i], conv_b[i], conv_ln_g[i], conv_ln_b[i], w_conv_out[i])
        y_b = gqa_branch(q_g, k_g, v_g, gqa_q_norm_g[i], gqa_k_norm_g[i], w_gqa_out[i], row_pos, col_pos)
        y_c = mla_branch(c_q, c_kv, k_pe, mla_q_norm_g[i], w_mla_uq[i], mla_kv_norm_g[i],
                         w_mla_ukv[i], w_mla_out[i], row_pos, col_pos)

        gates = jax.nn.sigmoid(gate_logits + gate_b[i]).reshape(b, -1, N_BRANCH, d)
        merged = gates[:, :, 0] * y_a + gates[:, :, 1] * y_b + gates[:, :, 2] * y_c
        h = h + merged @ w_out[i]

        v = rmsnorm(h, ffn_norm_g[i])
        h = h + (jax.nn.silu(v @ w_ffn_gate[i]) * (v @ w_ffn_up[i])) @ w_ffn_down[i]

    h = rmsnorm(h, final_norm_g)
    return h[:, N_META:]


import jax as _jax
import jax.numpy as _jnp

TWIN_FORMAT = 'train_step'
FWD_PARAMS = ['x', 'meta_tokens', 'mix_norm_g', 'w_in', 'conv_dw', 'conv_b', 'conv_ln_g', 'conv_ln_b', 'w_conv_out', 'gqa_q_norm_g', 'gqa_k_norm_g', 'w_gqa_out', 'mla_q_norm_g', 'w_mla_uq', 'mla_kv_norm_g', 'w_mla_ukv', 'w_mla_out', 'gate_b', 'w_out', 'ffn_norm_g', 'w_ffn_gate', 'w_ffn_up', 'w_ffn_down', 'final_norm_g']
TWIN_WEIGHTS = ['meta_tokens', 'mix_norm_g', 'w_in', 'conv_dw', 'conv_b', 'conv_ln_g', 'conv_ln_b', 'w_conv_out', 'gqa_q_norm_g', 'gqa_k_norm_g', 'w_gqa_out', 'mla_q_norm_g', 'w_mla_uq', 'mla_kv_norm_g', 'w_mla_ukv', 'w_mla_out', 'gate_b', 'w_out', 'ffn_norm_g', 'w_ffn_gate', 'w_ffn_up', 'w_ffn_down', 'final_norm_g']
TWIN_DIFF_INPUT = 'x'
TWIN_INPUTS = ['x', 'meta_tokens', 'mix_norm_g', 'w_in', 'conv_dw', 'conv_b', 'conv_ln_g', 'conv_ln_b', 'w_conv_out', 'gqa_q_norm_g', 'gqa_k_norm_g', 'w_gqa_out', 'mla_q_norm_g', 'w_mla_uq', 'mla_kv_norm_g', 'w_mla_ukv', 'w_mla_out', 'gate_b', 'w_out', 'ffn_norm_g', 'w_ffn_gate', 'w_ffn_up', 'w_ffn_down', 'final_norm_g', 'loss_target', 'm_meta_tokens', 'm_mix_norm_g', 'm_w_in', 'm_conv_dw', 'm_conv_b', 'm_conv_ln_g', 'm_conv_ln_b', 'm_w_conv_out', 'm_gqa_q_norm_g', 'm_gqa_k_norm_g', 'm_w_gqa_out', 'm_mla_q_norm_g', 'm_w_mla_uq', 'm_mla_kv_norm_g', 'm_w_mla_ukv', 'm_w_mla_out', 'm_gate_b', 'm_w_out', 'm_ffn_norm_g', 'm_w_ffn_gate', 'm_w_ffn_up', 'm_w_ffn_down', 'm_final_norm_g', 'v_meta_tokens', 'v_mix_norm_g', 'v_w_in', 'v_conv_dw', 'v_conv_b', 'v_conv_ln_g', 'v_conv_ln_b', 'v_w_conv_out', 'v_gqa_q_norm_g', 'v_gqa_k_norm_g', 'v_w_gqa_out', 'v_mla_q_norm_g', 'v_w_mla_uq', 'v_mla_kv_norm_g', 'v_w_mla_ukv', 'v_w_mla_out', 'v_gate_b', 'v_w_out', 'v_ffn_norm_g', 'v_w_ffn_gate', 'v_w_ffn_up', 'v_w_ffn_down', 'v_final_norm_g']
TWIN_OUTPUTS = ['loss', 'grad_x', 'grad_meta_tokens', 'grad_mix_norm_g', 'grad_w_in', 'grad_conv_dw', 'grad_conv_b', 'grad_conv_ln_g', 'grad_conv_ln_b', 'grad_w_conv_out', 'grad_gqa_q_norm_g', 'grad_gqa_k_norm_g', 'grad_w_gqa_out', 'grad_mla_q_norm_g', 'grad_w_mla_uq', 'grad_mla_kv_norm_g', 'grad_w_mla_ukv', 'grad_w_mla_out', 'grad_gate_b', 'grad_w_out', 'grad_ffn_norm_g', 'grad_w_ffn_gate', 'grad_w_ffn_up', 'grad_w_ffn_down', 'grad_final_norm_g', 'delta_meta_tokens', 'delta_mix_norm_g', 'delta_w_in', 'delta_conv_dw', 'delta_conv_b', 'delta_conv_ln_g', 'delta_conv_ln_b', 'delta_w_conv_out', 'delta_gqa_q_norm_g', 'delta_gqa_k_norm_g', 'delta_w_gqa_out', 'delta_mla_q_norm_g', 'delta_w_mla_uq', 'delta_mla_kv_norm_g', 'delta_w_mla_ukv', 'delta_w_mla_out', 'delta_gate_b', 'delta_w_out', 'delta_ffn_norm_g', 'delta_w_ffn_gate', 'delta_w_ffn_up', 'delta_w_ffn_down', 'delta_final_norm_g', 'new_m_meta_tokens', 'new_m_mix_norm_g', 'new_m_w_in', 'new_m_conv_dw', 'new_m_conv_b', 'new_m_conv_ln_g', 'new_m_conv_ln_b', 'new_m_w_conv_out', 'new_m_gqa_q_norm_g', 'new_m_gqa_k_norm_g', 'new_m_w_gqa_out', 'new_m_mla_q_norm_g', 'new_m_w_mla_uq', 'new_m_mla_kv_norm_g', 'new_m_w_mla_ukv', 'new_m_w_mla_out', 'new_m_gate_b', 'new_m_w_out', 'new_m_ffn_norm_g', 'new_m_w_ffn_gate', 'new_m_w_ffn_up', 'new_m_w_ffn_down', 'new_m_final_norm_g', 'new_v_meta_tokens', 'new_v_mix_norm_g', 'new_v_w_in', 'new_v_conv_dw', 'new_v_conv_b', 'new_v_conv_ln_g', 'new_v_conv_ln_b', 'new_v_w_conv_out', 'new_v_gqa_q_norm_g', 'new_v_gqa_k_norm_g', 'new_v_w_gqa_out', 'new_v_mla_q_norm_g', 'new_v_w_mla_uq', 'new_v_mla_kv_norm_g', 'new_v_w_mla_ukv', 'new_v_w_mla_out', 'new_v_gate_b', 'new_v_w_out', 'new_v_ffn_norm_g', 'new_v_w_ffn_gate', 'new_v_w_ffn_up', 'new_v_w_ffn_down', 'new_v_final_norm_g']
TWIN_LEAF_KINDS = {'loss': 'loss', 'grad_x': 'grad_x', 'grad_meta_tokens': 'grad_w', 'grad_mix_norm_g': 'grad_w', 'grad_w_in': 'grad_w', 'grad_conv_dw': 'grad_w', 'grad_conv_b': 'grad_w', 'grad_conv_ln_g': 'grad_w', 'grad_conv_ln_b': 'grad_w', 'grad_w_conv_out': 'grad_w', 'grad_gqa_q_norm_g': 'grad_w', 'grad_gqa_k_norm_g': 'grad_w', 'grad_w_gqa_out': 'grad_w', 'grad_mla_q_norm_g': 'grad_w', 'grad_w_mla_uq': 'grad_w', 'grad_mla_kv_norm_g': 'grad_w', 'grad_w_mla_ukv': 'grad_w', 'grad_w_mla_out': 'grad_w', 'grad_gate_b': 'grad_w', 'grad_w_out': 'grad_w', 'grad_ffn_norm_g': 'grad_w', 'grad_w_ffn_gate': 'grad_w', 'grad_w_ffn_up': 'grad_w', 'grad_w_ffn_down': 'grad_w', 'grad_final_norm_g': 'grad_w', 'delta_meta_tokens': 'delta_w', 'delta_mix_norm_g': 'delta_w', 'delta_w_in': 'delta_w', 'delta_conv_dw': 'delta_w', 'delta_conv_b': 'delta_w', 'delta_conv_ln_g': 'delta_w', 'delta_conv_ln_b': 'delta_w', 'delta_w_conv_out': 'delta_w', 'delta_gqa_q_norm_g': 'delta_w', 'delta_gqa_k_norm_g': 'delta_w', 'delta_w_gqa_out': 'delta_w', 'delta_mla_q_norm_g': 'delta_w', 'delta_w_mla_uq': 'delta_w', 'delta_mla_kv_norm_g': 'delta_w', 'delta_w_mla_ukv': 'delta_w', 'delta_w_mla_out': 'delta_w', 'delta_gate_b': 'delta_w', 'delta_w_out': 'delta_w', 'delta_ffn_norm_g': 'delta_w', 'delta_w_ffn_gate': 'delta_w', 'delta_w_ffn_up': 'delta_w', 'delta_w_ffn_down': 'delta_w', 'delta_final_norm_g': 'delta_w', 'new_m_meta_tokens': 'new_m', 'new_m_mix_norm_g': 'new_m', 'new_m_w_in': 'new_m', 'new_m_conv_dw': 'new_m', 'new_m_conv_b': 'new_m', 'new_m_conv_ln_g': 'new_m', 'new_m_conv_ln_b': 'new_m', 'new_m_w_conv_out': 'new_m', 'new_m_gqa_q_norm_g': 'new_m', 'new_m_gqa_k_norm_g': 'new_m', 'new_m_w_gqa_out': 'new_m', 'new_m_mla_q_norm_g': 'new_m', 'new_m_w_mla_uq': 'new_m', 'new_m_mla_kv_norm_g': 'new_m', 'new_m_w_mla_ukv': 'new_m', 'new_m_w_mla_out': 'new_m', 'new_m_gate_b': 'new_m', 'new_m_w_out': 'new_m', 'new_m_ffn_norm_g': 'new_m', 'new_m_w_ffn_gate': 'new_m', 'new_m_w_ffn_up': 'new_m', 'new_m_w_ffn_down': 'new_m', 'new_m_final_norm_g': 'new_m', 'new_v_meta_tokens': 'new_v', 'new_v_mix_norm_g': 'new_v', 'new_v_w_in': 'new_v', 'new_v_conv_dw': 'new_v', 'new_v_conv_b': 'new_v', 'new_v_conv_ln_g': 'new_v', 'new_v_conv_ln_b': 'new_v', 'new_v_w_conv_out': 'new_v', 'new_v_gqa_q_norm_g': 'new_v', 'new_v_gqa_k_norm_g': 'new_v', 'new_v_w_gqa_out': 'new_v', 'new_v_mla_q_norm_g': 'new_v', 'new_v_w_mla_uq': 'new_v', 'new_v_mla_kv_norm_g': 'new_v', 'new_v_w_mla_ukv': 'new_v', 'new_v_w_mla_out': 'new_v', 'new_v_gate_b': 'new_v', 'new_v_w_out': 'new_v', 'new_v_ffn_norm_g': 'new_v', 'new_v_w_ffn_gate': 'new_v', 'new_v_w_ffn_up': 'new_v', 'new_v_w_ffn_down': 'new_v', 'new_v_final_norm_g': 'new_v'}


def _forward(args):
    return _fwd_reference(*[args[k] for k in FWD_PARAMS])


def _output_shape():
    out = _jax.eval_shape(lambda: _forward(_fwd_setup_inputs(0)))
    return out.shape, out.dtype

N_MICROBATCH = 1
ADAM_LR = 0.001
ADAM_B1 = 0.9
ADAM_B2 = 0.999
ADAM_EPS = 1e-08
ADAM_WD = 0.01
ADAM_STEP = 10
PER_EXAMPLE_BATCH_AXIS = {'x': 0, 'loss_target': 0}
SHARED_INPUTS = []
_WEIGHT_DTYPES = {'meta_tokens': _jnp.float32, 'mix_norm_g': _jnp.float32, 'w_in': _jnp.float32, 'conv_dw': _jnp.float32, 'conv_b': _jnp.float32, 'conv_ln_g': _jnp.float32, 'conv_ln_b': _jnp.float32, 'w_conv_out': _jnp.float32, 'gqa_q_norm_g': _jnp.float32, 'gqa_k_norm_g': _jnp.float32, 'w_gqa_out': _jnp.float32, 'mla_q_norm_g': _jnp.float32, 'w_mla_uq': _jnp.float32, 'mla_kv_norm_g': _jnp.float32, 'w_mla_ukv': _jnp.float32, 'w_mla_out': _jnp.float32, 'gate_b': _jnp.float32, 'w_out': _jnp.float32, 'ffn_norm_g': _jnp.float32, 'w_ffn_gate': _jnp.float32, 'w_ffn_up': _jnp.float32, 'w_ffn_down': _jnp.float32, 'final_norm_g': _jnp.float32}
MOMENT_SCALE = {'meta_tokens': 1.191290e-03, 'mix_norm_g': 1.837809e-02, 'w_in': 7.982633e-03, 'conv_dw': 2.135009e-02, 'conv_b': 4.138025e-02, 'conv_ln_g': 2.538614e-02, 'conv_ln_b': 2.259106e-02, 'w_conv_out': 1.478693e-02, 'gqa_q_norm_g': 9.062703e-03, 'gqa_k_norm_g': 9.260385e-03, 'w_gqa_out': 2.877077e-03, 'mla_q_norm_g': 5.251900e-03, 'w_mla_uq': 2.954501e-03, 'mla_kv_norm_g': 7.731923e-03, 'w_mla_ukv': 3.713012e-03, 'w_mla_out': 3.026269e-03, 'gate_b': 3.460491e-03, 'w_out': 3.046571e-02, 'ffn_norm_g': 3.614662e-02, 'w_ffn_gate': 1.547952e-02, 'w_ffn_up': 1.499087e-02, 'w_ffn_down': 4.969703e-02, 'final_norm_g': 1.598694e+01}


def _to_microbatches(a, axis):
    t = _jnp.moveaxis(a, axis, 0)
    t = t.reshape((N_MICROBATCH, t.shape[0] // N_MICROBATCH) + t.shape[1:])
    return _jnp.moveaxis(t, 1, axis + 1)


def setup_inputs(seed: int = 0) -> dict:
    inp = _fwd_setup_inputs(seed)
    key = _jax.random.fold_in(_jax.random.key(seed), 7919)
    shape, _ = _output_shape()
    out = dict(inp)
    out["loss_target"] = _jax.random.normal(_jax.random.fold_in(key, 0), shape, _jnp.float32)
    for i, name in enumerate(TWIN_WEIGHTS):
        w = inp[name].astype(_jnp.float32)
        if MOMENT_SCALE is None:
            s = _jnp.sqrt(_jnp.mean(_jnp.square(w)) + 1e-30)
        else:
            s = MOMENT_SCALE[name]
        km, kv = _jax.random.split(_jax.random.fold_in(key, i + 1))
        out[name] = w
        out["m_" + name] = s * _jax.random.normal(km, w.shape, _jnp.float32)
        out["v_" + name] = (s * s) * _jax.random.uniform(kv, w.shape, _jnp.float32, 0.5, 1.5)
    if N_MICROBATCH > 1:
        for name, axis in PER_EXAMPLE_BATCH_AXIS.items():
            out[name] = _to_microbatches(out[name], axis)
    return {'x': out['x'], 'meta_tokens': out['meta_tokens'], 'mix_norm_g': out['mix_norm_g'], 'w_in': out['w_in'], 'conv_dw': out['conv_dw'], 'conv_b': out['conv_b'], 'conv_ln_g': out['conv_ln_g'], 'conv_ln_b': out['conv_ln_b'], 'w_conv_out': out['w_conv_out'], 'gqa_q_norm_g': out['gqa_q_norm_g'], 'gqa_k_norm_g': out['gqa_k_norm_g'], 'w_gqa_out': out['w_gqa_out'], 'mla_q_norm_g': out['mla_q_norm_g'], 'w_mla_uq': out['w_mla_uq'], 'mla_kv_norm_g': out['mla_kv_norm_g'], 'w_mla_ukv': out['w_mla_ukv'], 'w_mla_out': out['w_mla_out'], 'gate_b': out['gate_b'], 'w_out': out['w_out'], 'ffn_norm_g': out['ffn_norm_g'], 'w_ffn_gate': out['w_ffn_gate'], 'w_ffn_up': out['w_ffn_up'], 'w_ffn_down': out['w_ffn_down'], 'final_norm_g': out['final_norm_g'], 'loss_target': out['loss_target'], 'm_meta_tokens': out['m_meta_tokens'], 'm_mix_norm_g': out['m_mix_norm_g'], 'm_w_in': out['m_w_in'], 'm_conv_dw': out['m_conv_dw'], 'm_conv_b': out['m_conv_b'], 'm_conv_ln_g': out['m_conv_ln_g'], 'm_conv_ln_b': out['m_conv_ln_b'], 'm_w_conv_out': out['m_w_conv_out'], 'm_gqa_q_norm_g': out['m_gqa_q_norm_g'], 'm_gqa_k_norm_g': out['m_gqa_k_norm_g'], 'm_w_gqa_out': out['m_w_gqa_out'], 'm_mla_q_norm_g': out['m_mla_q_norm_g'], 'm_w_mla_uq': out['m_w_mla_uq'], 'm_mla_kv_norm_g': out['m_mla_kv_norm_g'], 'm_w_mla_ukv': out['m_w_mla_ukv'], 'm_w_mla_out': out['m_w_mla_out'], 'm_gate_b': out['m_gate_b'], 'm_w_out': out['m_w_out'], 'm_ffn_norm_g': out['m_ffn_norm_g'], 'm_w_ffn_gate': out['m_w_ffn_gate'], 'm_w_ffn_up': out['m_w_ffn_up'], 'm_w_ffn_down': out['m_w_ffn_down'], 'm_final_norm_g': out['m_final_norm_g'], 'v_meta_tokens': out['v_meta_tokens'], 'v_mix_norm_g': out['v_mix_norm_g'], 'v_w_in': out['v_w_in'], 'v_conv_dw': out['v_conv_dw'], 'v_conv_b': out['v_conv_b'], 'v_conv_ln_g': out['v_conv_ln_g'], 'v_conv_ln_b': out['v_conv_ln_b'], 'v_w_conv_out': out['v_w_conv_out'], 'v_gqa_q_norm_g': out['v_gqa_q_norm_g'], 'v_gqa_k_norm_g': out['v_gqa_k_norm_g'], 'v_w_gqa_out': out['v_w_gqa_out'], 'v_mla_q_norm_g': out['v_mla_q_norm_g'], 'v_w_mla_uq': out['v_w_mla_uq'], 'v_mla_kv_norm_g': out['v_mla_kv_norm_g'], 'v_w_mla_ukv': out['v_w_mla_ukv'], 'v_w_mla_out': out['v_w_mla_out'], 'v_gate_b': out['v_gate_b'], 'v_w_out': out['v_w_out'], 'v_ffn_norm_g': out['v_ffn_norm_g'], 'v_w_ffn_gate': out['v_w_ffn_gate'], 'v_w_ffn_up': out['v_w_ffn_up'], 'v_w_ffn_down': out['v_w_ffn_down'], 'v_final_norm_g': out['v_final_norm_g']}


def _loss(weights, diff, rest, loss_target):
    with _jax.named_scope("forward"):
        args = {**rest, TWIN_DIFF_INPUT: diff, **{k: w.astype(_WEIGHT_DTYPES[k]) for k, w in weights.items()}}
        y = _forward(args)
    with _jax.named_scope("loss_head"):
        err = _jnp.square(y.astype(_jnp.float32) - loss_target)
        return 0.5 * _jnp.sum(_jnp.mean(err, axis=-1)) if err.ndim else 0.5 * err


def _adamw(w, g, m, v):
    m = ADAM_B1 * m + (1.0 - ADAM_B1) * g
    v = ADAM_B2 * v + (1.0 - ADAM_B2) * _jnp.square(g)
    m_hat = m / (1.0 - ADAM_B1 ** ADAM_STEP)
    v_hat = v / (1.0 - ADAM_B2 ** ADAM_STEP)
    delta = -ADAM_LR * (m_hat / (_jnp.sqrt(v_hat) + ADAM_EPS) + ADAM_WD * w)
    return delta, m, v


def reference(x, meta_tokens, mix_norm_g, w_in, conv_dw, conv_b, conv_ln_g, conv_ln_b, w_conv_out, gqa_q_norm_g, gqa_k_norm_g, w_gqa_out, mla_q_norm_g, w_mla_uq, mla_kv_norm_g, w_mla_ukv, w_mla_out, gate_b, w_out, ffn_norm_g, w_ffn_gate, w_ffn_up, w_ffn_down, final_norm_g, loss_target, m_meta_tokens, m_mix_norm_g, m_w_in, m_conv_dw, m_conv_b, m_conv_ln_g, m_conv_ln_b, m_w_conv_out, m_gqa_q_norm_g, m_gqa_k_norm_g, m_w_gqa_out, m_mla_q_norm_g, m_w_mla_uq, m_mla_kv_norm_g, m_w_mla_ukv, m_w_mla_out, m_gate_b, m_w_out, m_ffn_norm_g, m_w_ffn_gate, m_w_ffn_up, m_w_ffn_down, m_final_norm_g, v_meta_tokens, v_mix_norm_g, v_w_in, v_conv_dw, v_conv_b, v_conv_ln_g, v_conv_ln_b, v_w_conv_out, v_gqa_q_norm_g, v_gqa_k_norm_g, v_w_gqa_out, v_mla_q_norm_g, v_w_mla_uq, v_mla_kv_norm_g, v_w_mla_ukv, v_w_mla_out, v_gate_b, v_w_out, v_ffn_norm_g, v_w_ffn_gate, v_w_ffn_up, v_w_ffn_down, v_final_norm_g):
    given = dict(x=x, meta_tokens=meta_tokens, mix_norm_g=mix_norm_g, w_in=w_in, conv_dw=conv_dw, conv_b=conv_b, conv_ln_g=conv_ln_g, conv_ln_b=conv_ln_b, w_conv_out=w_conv_out, gqa_q_norm_g=gqa_q_norm_g, gqa_k_norm_g=gqa_k_norm_g, w_gqa_out=w_gqa_out, mla_q_norm_g=mla_q_norm_g, w_mla_uq=w_mla_uq, mla_kv_norm_g=mla_kv_norm_g, w_mla_ukv=w_mla_ukv, w_mla_out=w_mla_out, gate_b=gate_b, w_out=w_out, ffn_norm_g=ffn_norm_g, w_ffn_gate=w_ffn_gate, w_ffn_up=w_ffn_up, w_ffn_down=w_ffn_down, final_norm_g=final_norm_g, loss_target=loss_target, m_meta_tokens=m_meta_tokens, m_mix_norm_g=m_mix_norm_g, m_w_in=m_w_in, m_conv_dw=m_conv_dw, m_conv_b=m_conv_b, m_conv_ln_g=m_conv_ln_g, m_conv_ln_b=m_conv_ln_b, m_w_conv_out=m_w_conv_out, m_gqa_q_norm_g=m_gqa_q_norm_g, m_gqa_k_norm_g=m_gqa_k_norm_g, m_w_gqa_out=m_w_gqa_out, m_mla_q_norm_g=m_mla_q_norm_g, m_w_mla_uq=m_w_mla_uq, m_mla_kv_norm_g=m_mla_kv_norm_g, m_w_mla_ukv=m_w_mla_ukv, m_w_mla_out=m_w_mla_out, m_gate_b=m_gate_b, m_w_out=m_w_out, m_ffn_norm_g=m_ffn_norm_g, m_w_ffn_gate=m_w_ffn_gate, m_w_ffn_up=m_w_ffn_up, m_w_ffn_down=m_w_ffn_down, m_final_norm_g=m_final_norm_g, v_meta_tokens=v_meta_tokens, v_mix_norm_g=v_mix_norm_g, v_w_in=v_w_in, v_conv_dw=v_conv_dw, v_conv_b=v_conv_b, v_conv_ln_g=v_conv_ln_g, v_conv_ln_b=v_conv_ln_b, v_w_conv_out=v_w_conv_out, v_gqa_q_norm_g=v_gqa_q_norm_g, v_gqa_k_norm_g=v_gqa_k_norm_g, v_w_gqa_out=v_w_gqa_out, v_mla_q_norm_g=v_mla_q_norm_g, v_w_mla_uq=v_w_mla_uq, v_mla_kv_norm_g=v_mla_kv_norm_g, v_w_mla_ukv=v_w_mla_ukv, v_w_mla_out=v_w_mla_out, v_gate_b=v_gate_b, v_w_out=v_w_out, v_ffn_norm_g=v_ffn_norm_g, v_w_ffn_gate=v_w_ffn_gate, v_w_ffn_up=v_w_ffn_up, v_w_ffn_down=v_w_ffn_down, v_final_norm_g=v_final_norm_g)
    weights = {n: given[n] for n in TWIN_WEIGHTS}
    shared = {n: given[n] for n in SHARED_INPUTS}
    per_example = {n: given[n] for n in ['x']}
    grad_fn = _jax.value_and_grad(_loss, argnums=(0, 1))

    def one_microbatch(ex, loss_target):
        ex = dict(ex)
        diff = ex.pop(TWIN_DIFF_INPUT)
        return grad_fn(weights, diff, {**shared, **ex}, loss_target)

    if N_MICROBATCH == 1:
        loss, (grad_w, grad_x) = one_microbatch(per_example, given["loss_target"])
    else:
        def body(carry, xs):
            loss_sum, grad_sum = carry
            l_k, (gw_k, gx_k) = one_microbatch(xs[0], xs[1])
            with _jax.named_scope("update"):
                return (loss_sum + l_k, _jax.tree.map(_jnp.add, grad_sum, gw_k)), gx_k

        init = (_jnp.zeros((), _jnp.float32), _jax.tree.map(_jnp.zeros_like, weights))
        (loss, grad_w), grad_x = _jax.lax.scan(body, init, (per_example, given["loss_target"]))
    with _jax.named_scope("update"):
        delta_w, new_m, new_v = {}, {}, {}
        for n in TWIN_WEIGHTS:
            delta_w[n], new_m[n], new_v[n] = _adamw(weights[n], grad_w[n], given["m_" + n], given["v_" + n])
    return (loss, grad_x, *[grad_w[n] for n in TWIN_WEIGHTS], *[delta_w[n] for n in TWIN_WEIGHTS],
            *[new_m[n] for n in TWIN_WEIGHTS], *[new_v[n] for n in TWIN_WEIGHTS])
```

```python
import functools
import math

import jax
import jax.numpy as jnp
from jax import lax
from jax.experimental import pallas as pl
from jax.experimental.pallas import tpu as pltpu

F32 = jnp.float32
BF16 = jnp.bfloat16

N_META = 16
GRID_W = 64
ROPE_THETA = 10000.0
NORM_EPS = 1e-6
HEAD_DIM = 128
MLA_ROPE_DIM = 64
N_BRANCH = 3
N_DEV = 8
LANE = 128
VMEM_LIMIT = 56 * 1024 * 1024

ADAM_LR = 0.001
ADAM_B1 = 0.9
ADAM_B2 = 0.999
ADAM_EPS = 1e-08
ADAM_WD = 0.01
ADAM_STEP = 10

W_NAMES = ['meta_tokens', 'mix_norm_g', 'w_in', 'conv_dw', 'conv_b', 'conv_ln_g', 'conv_ln_b', 'w_conv_out',
           'gqa_q_norm_g', 'gqa_k_norm_g', 'w_gqa_out', 'mla_q_norm_g', 'w_mla_uq', 'mla_kv_norm_g', 'w_mla_ukv',
           'w_mla_out', 'gate_b', 'w_out', 'ffn_norm_g', 'w_ffn_gate', 'w_ffn_up', 'w_ffn_down', 'final_norm_g']
BIG = ['w_in', 'w_conv_out', 'w_gqa_out', 'w_mla_uq', 'w_mla_ukv', 'w_mla_out', 'w_out', 'w_ffn_gate', 'w_ffn_up',
       'w_ffn_down']
ROW_SHARDED = ('w_out', 'w_ffn_down')
SMALL = ['mix_norm_g', 'conv_b', 'conv_ln_g', 'conv_ln_b', 'gqa_q_norm_g', 'gqa_k_norm_g', 'mla_q_norm_g',
         'mla_kv_norm_g', 'gate_b', 'ffn_norm_g', 'final_norm_g']

NT = (((1,), (1,)), ((), ()))
TN = (((0,), (0,)), ((), ()))
NN = (((1,), (0,)), ((), ()))


def _tile(n, target, mult):
    best = None
    for t in range(mult, min(n, target) + 1, mult):
        if n % t == 0:
            best = t
    assert best is not None, (n, target, mult)
    return best


def _cp(n):
    return pltpu.CompilerParams(dimension_semantics=("arbitrary",) * n, vmem_limit_bytes=VMEM_LIMIT)


def _sig(x):
    return jax.nn.sigmoid(x)


def _cb(off, w):
    assert off % w == 0, (off, w)
    return off // w


def _matmul(a, b, *, mode, out_dtype, name, residual=None):
    if mode == "nn":
        (m, k), (k2, n) = a.shape, b.shape
    elif mode == "nt":
        (m, k), (n, k2) = a.shape, b.shape
    else:
        (k, m), (k2, n) = a.shape, b.shape
    assert k == k2, (a.shape, b.shape, mode)
    tm = _tile(m, 1088, 128 if mode == "tn" else 16)
    tn = _tile(n, 1024, 128)
    tk = _tile(k, 2176, 128)
    nk = k // tk
    dims = {"nn": NN, "nt": NT, "tn": TN}[mode]
    a_spec = (pl.BlockSpec((tk, tm), lambda i, j, kk: (kk, i)) if mode == "tn"
              else pl.BlockSpec((tm, tk), lambda i, j, kk: (i, kk)))
    b_spec = (pl.BlockSpec((tn, tk), lambda i, j, kk: (j, kk)) if mode == "nt"
              else pl.BlockSpec((tk, tn), lambda i, j, kk: (kk, j)))
    o_spec = pl.BlockSpec((tm, tn), lambda i, j, kk: (i, j))
    has_res = residual is not None

    def body(*refs):
        if has_res:
            a_ref, b_ref, r_ref, o_ref, acc = refs
        else:
            a_ref, b_ref, o_ref, acc = refs
        kk = pl.program_id(2)

        @pl.when(kk == 0)
        def _():
            acc[...] = jnp.zeros_like(acc)

        acc[...] += lax.dot_general(a_ref[...].astype(BF16), b_ref[...].astype(BF16), dims,
                                    preferred_element_type=F32)

        @pl.when(kk == nk - 1)
        def _():
            r = acc[...]
            if has_res:
                r = r + r_ref[...]
            o_ref[...] = r.astype(o_ref.dtype)

    ins = [a, b] + ([residual] if has_res else [])
    in_specs = [a_spec, b_spec] + ([o_spec] if has_res else [])
    return pl.pallas_call(
        body, grid=(m // tm, n // tn, nk), in_specs=in_specs, out_specs=o_spec,
        out_shape=jax.ShapeDtypeStruct((m, n), out_dtype), scratch_shapes=[pltpu.VMEM((tm, tn), F32)],
        name=name, compiler_params=_cp(3))(*ins)


def _rms_fwd(x, w, cb, g, tm, name):
    m = x.shape[0]

    def body(x_ref, g_ref, o_ref):
        xv = x_ref[...]
        r = lax.rsqrt(jnp.mean(xv * xv, axis=-1, keepdims=True) + NORM_EPS)
        o_ref[...] = (xv * r * g_ref[...]).astype(o_ref.dtype)

    return pl.pallas_call(
        body, grid=(m // tm,),
        in_specs=[pl.BlockSpec((tm, w), lambda i: (i, cb)), pl.BlockSpec((1, w), lambda i: (0, 0))],
        out_specs=pl.BlockSpec((tm, w), lambda i: (i, 0)), out_shape=jax.ShapeDtypeStruct((m, w), BF16),
        name=name, compiler_params=_cp(1))(x, g)


def _rms_bwd(x, w, cb, g, dy, tm, out_dtype, name, add=None):
    m = x.shape[0]
    has_add = add is not None

    def body(*refs):
        if has_add:
            x_ref, g_ref, dy_ref, add_ref, dx_ref, dg_ref = refs
        else:
            x_ref, g_ref, dy_ref, dx_ref, dg_ref = refs
        xv = x_ref[...]
        dyv = dy_ref[...].astype(F32)
        r = lax.rsqrt(jnp.mean(xv * xv, axis=-1, keepdims=True) + NORM_EPS)
        t = dyv * g_ref[...]
        dx = r * t - xv * (r * r * r) * jnp.mean(t * xv, axis=-1, keepdims=True)
        if has_add:
            dx = dx + add_ref[...]
        dx_ref[...] = dx.astype(dx_ref.dtype)

        @pl.when(pl.program_id(0) == 0)
        def _():
            dg_ref[...] = jnp.zeros_like(dg_ref)

        dg_ref[...] += jnp.sum(dyv * xv * r, axis=0, keepdims=True)

    row = pl.BlockSpec((tm, w), lambda i: (i, 0))
    vec = pl.BlockSpec((1, w), lambda i: (0, 0))
    ins = [x, g, dy] + ([add] if has_add else [])
    in_specs = [pl.BlockSpec((tm, w), lambda i: (i, cb)), vec, row] + ([row] if has_add else [])
    return pl.pallas_call(
        body, grid=(m // tm,), in_specs=in_specs, out_specs=(row, vec),
        out_shape=(jax.ShapeDtypeStruct((m, w), out_dtype), jax.ShapeDtypeStruct((1, w), F32)),
        name=name, compiler_params=_cp(1))(*ins)


def _ln_silu_fwd(c, lg, lb, tm):
    m, w = c.shape

    def body(c_ref, g_ref, b_ref, o_ref):
        cv = c_ref[...]
        xc = cv - jnp.mean(cv, axis=-1, keepdims=True)
        r = lax.rsqrt(jnp.mean(xc * xc, axis=-1, keepdims=True) + NORM_EPS)
        yl = xc * r * g_ref[...] + b_ref[...]
        o_ref[...] = (yl * _sig(yl)).astype(o_ref.dtype)

    row = pl.BlockSpec((tm, w), lambda i: (i, 0))
    vec = pl.BlockSpec((1, w), lambda i: (0, 0))
    return pl.pallas_call(body, grid=(m // tm,), in_specs=[row, vec, vec], out_specs=row,
                          out_shape=jax.ShapeDtypeStruct((m, w), BF16), name="ln_silu_fwd",
                          compiler_params=_cp(1))(c, lg, lb)


def _ln_silu_bwd(c, lg, lb, ds, tm):
    m, w = c.shape

    def body(c_ref, g_ref, b_ref, ds_ref, dc_ref, dg_ref, db_ref):
        cv = c_ref[...]
        xc = cv - jnp.mean(cv, axis=-1, keepdims=True)
        r = lax.rsqrt(jnp.mean(xc * xc, axis=-1, keepdims=True) + NORM_EPS)
        nv = xc * r
        yl = nv * g_ref[...] + b_ref[...]
        sg = _sig(yl)
        dyl = ds_ref[...] * (sg * (1.0 + yl * (1.0 - sg)))
        dn = dyl * g_ref[...]
        dc = r * (dn - jnp.mean(dn, axis=-1, keepdims=True) - nv * jnp.mean(dn * nv, axis=-1, keepdims=True))
        dc_ref[...] = dc

        @pl.when(pl.program_id(0) == 0)
        def _():
            dg_ref[...] = jnp.zeros_like(dg_ref)
            db_ref[...] = jnp.zeros_like(db_ref)

        dg_ref[...] += jnp.sum(dyl * nv, axis=0, keepdims=True)
        db_ref[...] += jnp.sum(dyl, axis=0, keepdims=True)

    row = pl.BlockSpec((tm, w), lambda i: (i, 0))
    vec = pl.BlockSpec((1, w), lambda i: (0, 0))
    return pl.pallas_call(
        body, grid=(m // tm,), in_specs=[row, vec, vec, row], out_specs=(row, vec, vec),
        out_shape=(jax.ShapeDtypeStruct((m, w), F32), jax.ShapeDtypeStruct((1, w), F32),
                   jax.ShapeDtypeStruct((1, w), F32)),
        name="ln_silu_bwd", compiler_params=_cp(1))(c, lg, lb, ds)


CONV_MARGIN = 16
CONV_ROWS = 128


def _conv_fwd(proj, dw, bias, *, c_conv, off_a, nb, lp, l_valid):
    kw = dw.shape[0]
    half = kw // 2
    cw = LANE
    mg = CONV_MARGIN
    assert half <= mg - 1 and lp % CONV_ROWS == 0

    def body(a_ref, g_ref, w_ref, b_ref, c_ref, zp):
        t = lax.broadcasted_iota(jnp.int32, (lp, cw), 0)
        z = jnp.where(t < l_valid, a_ref[...] * _sig(g_ref[...]), 0.0)
        zp[0:mg, :] = jnp.zeros((mg, cw), F32)
        zp[mg + lp:mg + lp + mg, :] = jnp.zeros((mg, cw), F32)
        zp[mg:mg + lp, :] = z
        for r0 in range(0, lp, CONV_ROWS):
            acc = jnp.zeros((CONV_ROWS, cw), F32) + b_ref[...]
            for k in range(kw):
                s0 = mg + r0 + k - half
                acc = acc + w_ref[k:k + 1, :] * zp[s0:s0 + CONV_ROWS, :]
            c_ref[r0:r0 + CONV_ROWS, :] = acc

    ja, jg = _cb(off_a, cw), _cb(off_a + c_conv, cw)
    return pl.pallas_call(
        body, grid=(nb, c_conv // cw),
        in_specs=[pl.BlockSpec((lp, cw), lambda b, j: (b, ja + j)), pl.BlockSpec((lp, cw), lambda b, j: (b, jg + j)),
                  pl.BlockSpec((kw, cw), lambda b, j: (0, j)), pl.BlockSpec((1, cw), lambda b, j: (0, j))],
        out_specs=pl.BlockSpec((lp, cw), lambda b, j: (b, j)),
        out_shape=jax.ShapeDtypeStruct((nb * lp, c_conv), F32),
        scratch_shapes=[pltpu.VMEM((lp + 2 * mg, cw), F32)], name="conv_fwd", compiler_params=_cp(2))(proj, proj, dw, bias)


def _conv_bwd(proj, dc, dw, *, c_conv, off_a, nb, lp, l_valid):
    kw = dw.shape[0]
    half = kw // 2
    cw = LANE
    mg = CONV_MARGIN

    def body(a_ref, g_ref, dc_ref, w_ref, da_ref, dg_ref, ddw_ref, dcb_ref, zp, dcp):
        t = lax.broadcasted_iota(jnp.int32, (lp, cw), 0)
        z = jnp.where(t < l_valid, a_ref[...] * _sig(g_ref[...]), 0.0)
        for buf in (zp, dcp):
            buf[0:mg, :] = jnp.zeros((mg, cw), F32)
            buf[mg + lp:mg + lp + mg, :] = jnp.zeros((mg, cw), F32)
        zp[mg:mg + lp, :] = z
        dcv = dc_ref[...]
        dcp[mg:mg + lp, :] = dcv

        @pl.when(pl.program_id(1) == 0)
        def _():
            ddw_ref[...] = jnp.zeros_like(ddw_ref)
            dcb_ref[...] = jnp.zeros_like(dcb_ref)

        dcb_ref[...] += jnp.sum(dcv, axis=0, keepdims=True)
        for r0 in range(0, lp, CONV_ROWS):
            acc = jnp.zeros((CONV_ROWS, cw), F32)
            for k in range(kw):
                s0 = mg + r0 - k + half
                acc = acc + w_ref[k:k + 1, :] * dcp[s0:s0 + CONV_ROWS, :]
            tc = lax.broadcasted_iota(jnp.int32, (CONV_ROWS, cw), 0) + r0
            dz = jnp.where(tc < l_valid, acc, 0.0)
            sg = _sig(g_ref[r0:r0 + CONV_ROWS, :])
            da_ref[r0:r0 + CONV_ROWS, :] = (dz * sg).astype(da_ref.dtype)
            dg_ref[r0:r0 + CONV_ROWS, :] = (dz * a_ref[r0:r0 + CONV_ROWS, :] * sg * (1.0 - sg)).astype(dg_ref.dtype)
        for k in range(kw):
            acc = jnp.zeros((CONV_ROWS, cw), F32)
            for r0 in range(0, lp, CONV_ROWS):
                s0 = mg + r0 + k - half
                acc = acc + dc_ref[r0:r0 + CONV_ROWS, :] * zp[s0:s0 + CONV_ROWS, :]
            ddw_ref[k:k + 1, :] += jnp.sum(acc, axis=0, keepdims=True)

    ja, jg = _cb(off_a, cw), _cb(off_a + c_conv, cw)
    seq = pl.BlockSpec((lp, cw), lambda j, b: (b, j))
    return pl.pallas_call(
        body, grid=(c_conv // cw, nb),
        in_specs=[pl.BlockSpec((lp, cw), lambda j, b: (b, ja + j)), pl.BlockSpec((lp, cw), lambda j, b: (b, jg + j)),
                  seq, pl.BlockSpec((kw, cw), lambda j, b: (0, j))],
        out_specs=(seq, seq, pl.BlockSpec((kw, cw), lambda j, b: (0, j)), pl.BlockSpec((1, cw), lambda j, b: (0, j))),
        out_shape=(jax.ShapeDtypeStruct((nb * lp, c_conv), BF16), jax.ShapeDtypeStruct((nb * lp, c_conv), BF16),
                   jax.ShapeDtypeStruct((kw, c_conv), F32), jax.ShapeDtypeStruct((1, c_conv), F32)),
        scratch_shapes=[pltpu.VMEM((lp + 2 * mg, cw), F32), pltpu.VMEM((lp + 2 * mg, cw), F32)],
        name="conv_bwd", compiler_params=_cp(2))(proj, proj, dc, dw)


def _swap_halves(x, half):
    fwd = pltpu.roll(x, LANE - half, axis=1)
    bwd = pltpu.roll(x, half, axis=1)
    lane = lax.broadcasted_iota(jnp.int32, x.shape, 1)
    return jnp.where((lane & (2 * half - 1)) < half, fwd, bwd)


def _rope(x, cos, sin, half):
    return x * cos + _swap_halves(x, half) * sin


def _rope_t(dy, cos, sin, half):
    return dy * cos + _swap_halves(dy * sin, half)


def _rope_tables(lp, l_valid, half):
    t = jnp.arange(lp)
    n = jnp.clip(t - N_META, 0, None)
    real = (t >= N_META) & (t < l_valid)
    row = jnp.where(real, n // GRID_W, 0).astype(F32)
    col = jnp.where(real, n % GRID_W, 0).astype(F32)
    inv = ROPE_THETA ** (-jnp.arange(half, dtype=F32) / half)
    ar, ac = row[:, None] * inv[None, :], col[:, None] * inv[None, :]
    cos = jnp.concatenate([jnp.cos(ar), jnp.cos(ar), jnp.cos(ac), jnp.cos(ac)], axis=1)
    sin = jnp.concatenate([-jnp.sin(ar), jnp.sin(ar), -jnp.sin(ac), jnp.sin(ac)], axis=1)
    padw = LANE - 4 * half
    if padw:
        cos = jnp.pad(cos, ((0, 0), (0, padw)))
        sin = jnp.pad(sin, ((0, 0), (0, padw)))
    return cos.astype(F32), sin.astype(F32)


def _gqa_prep_fwd(proj, qg, kg, cos, sin, *, off_q, hq, hkv, nb, lp, tm):
    hd = HEAD_DIM
    half = hd // 4
    wq, wkv = hq * hd, 2 * hkv * hd
    nt = lp // tm

    def body(q_ref, kv_ref, qg_ref, kg_ref, cos_ref, sin_ref, qo_ref, ko_ref, vo_ref):
        cosv, sinv = cos_ref[...], sin_ref[...]

        def norm_rope(xh, g):
            r = lax.rsqrt(jnp.mean(xh * xh, axis=-1, keepdims=True) + NORM_EPS)
            return _rope(xh * r * g, cosv, sinv, half)

        for h in range(hq):
            qo_ref[h] = norm_rope(q_ref[:, h * hd:(h + 1) * hd], qg_ref[...]).astype(BF16)
        for h in range(hkv):
            ko_ref[h] = norm_rope(kv_ref[:, h * hd:(h + 1) * hd], kg_ref[...]).astype(BF16)
            vo_ref[h] = kv_ref[:, (hkv + h) * hd:(hkv + h + 1) * hd].astype(BF16)

    jq, jkv = _cb(off_q, wq), _cb(off_q + wq, wkv)
    tab = pl.BlockSpec((tm, LANE), lambda b, i: (i, 0))
    vec = pl.BlockSpec((1, hd), lambda b, i: (0, 0))

    def heads(h):
        return pl.BlockSpec((None, h, tm, hd), lambda b, i: (b, 0, i, 0))

    return pl.pallas_call(
        body, grid=(nb, nt),
        in_specs=[pl.BlockSpec((tm, wq), lambda b, i: (b * nt + i, jq)),
                  pl.BlockSpec((tm, wkv), lambda b, i: (b * nt + i, jkv)), vec, vec, tab, tab],
        out_specs=(heads(hq), heads(hkv), heads(hkv)),
        out_shape=(jax.ShapeDtypeStruct((nb, hq, lp, hd), BF16), jax.ShapeDtypeStruct((nb, hkv, lp, hd), BF16),
                   jax.ShapeDtypeStruct((nb, hkv, lp, hd), BF16)),
        name="gqa_prep_fwd", compiler_params=_cp(2))(proj, proj, qg, kg, cos, sin)


def _gqa_prep_bwd(proj, qg, kg, cos, sin, dqh, dkh, dvh, *, off_q, hq, hkv, nb, lp, tm):
    hd = HEAD_DIM
    half = hd // 4
    wq, wkv = hq * hd, 2 * hkv * hd
    nt = lp // tm

    def body(q_ref, kv_ref, qg_ref, kg_ref, cos_ref, sin_ref, dqh_ref, dkh_ref, dvh_ref,
             dq_ref, dkv_ref, dqg_ref, dkg_ref):
        cosv, sinv = cos_ref[...], sin_ref[...]

        def back(xh, g, dyh):
            dn = _rope_t(dyh, cosv, sinv, half)
            r = lax.rsqrt(jnp.mean(xh * xh, axis=-1, keepdims=True) + NORM_EPS)
            t = dn * g
            dx = r * t - xh * (r * r * r) * jnp.mean(t * xh, axis=-1, keepdims=True)
            return dx, jnp.sum(dn * xh * r, axis=0, keepdims=True)

        first = (pl.program_id(0) == 0) & (pl.program_id(1) == 0)

        @pl.when(first)
        def _():
            dqg_ref[...] = jnp.zeros_like(dqg_ref)
            dkg_ref[...] = jnp.zeros_like(dkg_ref)

        gq = jnp.zeros((1, hd), F32)
        for h in range(hq):
            dx, dg = back(q_ref[:, h * hd:(h + 1) * hd], qg_ref[...], dqh_ref[h])
            dq_ref[:, h * hd:(h + 1) * hd] = dx.astype(dq_ref.dtype)
            gq = gq + dg
        dqg_ref[...] += gq
        gk = jnp.zeros((1, hd), F32)
        for h in range(hkv):
            dx, dg = back(kv_ref[:, h * hd:(h + 1) * hd], kg_ref[...], dkh_ref[h])
            dkv_ref[:, h * hd:(h + 1) * hd] = dx.astype(dkv_ref.dtype)
            dkv_ref[:, (hkv + h) * hd:(hkv + h + 1) * hd] = dvh_ref[h].astype(dkv_ref.dtype)
            gk = gk + dg
        dkg_ref[...] += gk

    jq, jkv = _cb(off_q, wq), _cb(off_q + wq, wkv)
    tab = pl.BlockSpec((tm, LANE), lambda b, i: (i, 0))
    vec = pl.BlockSpec((1, hd), lambda b, i: (0, 0))

    def heads(h):
        return pl.BlockSpec((None, h, tm, hd), lambda b, i: (b, 0, i, 0))

    return pl.pallas_call(
        body, grid=(nb, nt),
        in_specs=[pl.BlockSpec((tm, wq), lambda b, i: (b * nt + i, jq)),
                  pl.BlockSpec((tm, wkv), lambda b, i: (b * nt + i, jkv)), vec, vec, tab, tab,
                  heads(hq), heads(hkv), heads(hkv)],
        out_specs=(pl.BlockSpec((tm, wq), lambda b, i: (b * nt + i, 0)),
                   pl.BlockSpec((tm, wkv), lambda b, i: (b * nt + i, 0)), vec, vec),
        out_shape=(jax.ShapeDtypeStruct((nb * lp, wq), BF16), jax.ShapeDtypeStruct((nb * lp, wkv), BF16),
                   jax.ShapeDtypeStruct((1, hd), F32), jax.ShapeDtypeStruct((1, hd), F32)),
        name="gqa_prep_bwd", compiler_params=_cp(2))(proj, proj, qg, kg, cos, sin, dqh, dkh, dvh)


def _mla_prep_fwd(qm, kvm, proj, cos, sin, *, off_kpe, nh, nb, lp, tm):
    hd = HEAD_DIM
    half = MLA_ROPE_DIM // 4
    nt = lp // tm
    wh = nh * hd

    def body(qm_ref, kvm_ref, kpe_ref, cos_ref, sin_ref, qo_ref, ko_ref, vo_ref):
        cosv, sinv = cos_ref[...], sin_ref[...]
        kpe = _rope(kpe_ref[...], cosv, sinv, half).astype(BF16)
        for h in range(nh):
            qo_ref[h, :, 0:hd] = qm_ref[:, h * hd:(h + 1) * hd].astype(BF16)
            qo_ref[h, :, hd:2 * hd] = _rope(qm_ref[:, wh + h * hd:wh + (h + 1) * hd], cosv, sinv, half).astype(BF16)
            ko_ref[h, :, 0:hd] = kvm_ref[:, h * hd:(h + 1) * hd].astype(BF16)
            ko_ref[h, :, hd:2 * hd] = kpe
            vo_ref[h] = kvm_ref[:, wh + h * hd:wh + (h + 1) * hd].astype(BF16)

    tab = pl.BlockSpec((tm, LANE), lambda b, i: (i, 0))
    wide = pl.BlockSpec((tm, 2 * wh), lambda b, i: (b * nt + i, 0))
    jk = _cb(off_kpe, LANE)
    return pl.pallas_call(
        body, grid=(nb, nt),
        in_specs=[wide, wide, pl.BlockSpec((tm, LANE), lambda b, i: (b * nt + i, jk)), tab, tab],
        out_specs=(pl.BlockSpec((None, nh, tm, 2 * hd), lambda b, i: (b, 0, i, 0)),
                   pl.BlockSpec((None, nh, tm, 2 * hd), lambda b, i: (b, 0, i, 0)),
                   pl.BlockSpec((None, nh, tm, hd), lambda b, i: (b, 0, i, 0))),
        out_shape=(jax.ShapeDtypeStruct((nb, nh, lp, 2 * hd), BF16), jax.ShapeDtypeStruct((nb, nh, lp, 2 * hd), BF16),
                   jax.ShapeDtypeStruct((nb, nh, lp, hd), BF16)),
        name="mla_prep_fwd", compiler_params=_cp(2))(qm, kvm, proj, cos, sin)


def _mla_prep_bwd(dqc, dkc, dvc, cos, sin, *, nh, nb, lp, tm):
    hd = HEAD_DIM
    half = MLA_ROPE_DIM // 4
    nt = lp // tm
    wh = nh * hd

    def body(dq_ref, dk_ref, dv_ref, cos_ref, sin_ref, dqm_ref, dkvm_ref, dkpe_ref):
        cosv, sinv = cos_ref[...], sin_ref[...]
        acc = jnp.zeros((tm, hd), F32)
        for h in range(nh):
            dqm_ref[:, h * hd:(h + 1) * hd] = dq_ref[h, :, 0:hd].astype(BF16)
            dqm_ref[:, wh + h * hd:wh + (h + 1) * hd] = _rope_t(dq_ref[h, :, hd:2 * hd], cosv, sinv, half).astype(BF16)
            dkvm_ref[:, h * hd:(h + 1) * hd] = dk_ref[h, :, 0:hd].astype(BF16)
            dkvm_ref[:, wh + h * hd:wh + (h + 1) * hd] = dv_ref[h].astype(BF16)
            acc = acc + dk_ref[h, :, hd:2 * hd]
        dkpe_ref[...] = _rope_t(acc, cosv, sinv, half).astype(BF16)

    tab = pl.BlockSpec((tm, LANE), lambda b, i: (i, 0))
    wide = pl.BlockSpec((tm, 2 * wh), lambda b, i: (b * nt + i, 0))
    return pl.pallas_call(
        body, grid=(nb, nt),
        in_specs=[pl.BlockSpec((None, nh, tm, 2 * hd), lambda b, i: (b, 0, i, 0)),
                  pl.BlockSpec((None, nh, tm, 2 * hd), lambda b, i: (b, 0, i, 0)),
                  pl.BlockSpec((None, nh, tm, hd), lambda b, i: (b, 0, i, 0)), tab, tab],
        out_specs=(wide, wide, pl.BlockSpec((tm, LANE), lambda b, i: (b * nt + i, 0))),
        out_shape=(jax.ShapeDtypeStruct((nb * lp, 2 * wh), BF16), jax.ShapeDtypeStruct((nb * lp, 2 * wh), BF16),
                   jax.ShapeDtypeStruct((nb * lp, LANE), BF16)),
        name="mla_prep_bwd", compiler_params=_cp(2))(dqc, dkc, dvc, cos, sin)


def _attn_fwd(q, k, v, *, scale, l_valid, name):
    nb, hq, lp, dk = q.shape
    hkv, dv = k.shape[1], v.shape[3]
    grp = hq // hkv
    tq = _tile(lp, 544, 16)
    nq = lp // tq

    def body(q_ref, k_ref, v_ref, o_ref):
        s = lax.dot_general(q_ref[...], k_ref[...], NT, preferred_element_type=F32) * scale
        col = lax.broadcasted_iota(jnp.int32, s.shape, 1)
        s = jnp.where(col < l_valid, s, -1e30)
        p = jnp.exp(s - jnp.max(s, axis=-1, keepdims=True))
        den = jnp.sum(p, axis=-1, keepdims=True)
        o = lax.dot_general(p.astype(BF16), v_ref[...], NN, preferred_element_type=F32)
        o_ref[...] = (o / den).astype(o_ref.dtype)

    return pl.pallas_call(
        body, grid=(nb, hq, nq),
        in_specs=[pl.BlockSpec((None, None, tq, dk), lambda b, h, i: (b, h, i, 0)),
                  pl.BlockSpec((None, None, lp, dk), lambda b, h, i: (b, h // grp, 0, 0)),
                  pl.BlockSpec((None, None, lp, dv), lambda b, h, i: (b, h // grp, 0, 0))],
        out_specs=pl.BlockSpec((tq, dv), lambda b, h, i: (b * nq + i, h)),
        out_shape=jax.ShapeDtypeStruct((nb * lp, hq * dv), BF16), name=name, compiler_params=_cp(3))(q, k, v)


def _attn_bwd(q, k, v, do, *, scale, l_valid, name):
    nb, hq, lp, dk = q.shape
    hkv, dv = k.shape[1], v.shape[3]
    grp = hq // hkv
    tq = _tile(lp, 272, 16)
    nq = lp // tq

    def body(q_ref, k_ref, v_ref, do_ref, dq_ref, dk_ref, dv_ref):
        qv, kv, vv, dov = q_ref[...], k_ref[...], v_ref[...], do_ref[...]
        s = lax.dot_general(qv, kv, NT, preferred_element_type=F32) * scale
        col = lax.broadcasted_iota(jnp.int32, s.shape, 1)
        s = jnp.where(col < l_valid, s, -1e30)
        p = jnp.exp(s - jnp.max(s, axis=-1, keepdims=True))
        p = p / jnp.sum(p, axis=-1, keepdims=True)
        dp = lax.dot_general(dov, vv, NT, preferred_element_type=F32)
        ds = (p * (dp - jnp.sum(p * dp, axis=-1, keepdims=True)) * scale).astype(BF16)
        dq_ref[...] = lax.dot_general(ds, kv, NN, preferred_element_type=F32)

        @pl.when((pl.program_id(2) == 0) & (pl.program_id(3) == 0))
        def _():
            dk_ref[...] = jnp.zeros_like(dk_ref)
            dv_ref[...] = jnp.zeros_like(dv_ref)

        dk_ref[...] += lax.dot_general(ds, qv, TN, preferred_element_type=F32)
        dv_ref[...] += lax.dot_general(p.astype(BF16), dov, TN, preferred_element_type=F32)

    return pl.pallas_call(
        body, grid=(nb, hkv, grp, nq),
        in_specs=[pl.BlockSpec((None, None, tq, dk), lambda b, h, g, i: (b, h * grp + g, i, 0)),
                  pl.BlockSpec((None, None, lp, dk), lambda b, h, g, i: (b, h, 0, 0)),
                  pl.BlockSpec((None, None, lp, dv), lambda b, h, g, i: (b, h, 0, 0)),
                  pl.BlockSpec((tq, dv), lambda b, h, g, i: (b * nq + i, h * grp + g))],
        out_specs=(pl.BlockSpec((None, None, tq, dk), lambda b, h, g, i: (b, h * grp + g, i, 0)),
                   pl.BlockSpec((None, None, lp, dk), lambda b, h, g, i: (b, h, 0, 0)),
                   pl.BlockSpec((None, None, lp, dv), lambda b, h, g, i: (b, h, 0, 0))),
        out_shape=(jax.ShapeDtypeStruct((nb, hq, lp, dk), F32), jax.ShapeDtypeStruct((nb, hkv, lp, dk), F32),
                   jax.ShapeDtypeStruct((nb, hkv, lp, dv), F32)),
        name=name, compiler_params=_cp(4))(q, k, v, do)


def _gate_fwd(proj, gb, ya, yb, yc, *, d, tm):
    m = ya.shape[0]
    cw = _tile(d, 512, LANE)
    nj = d // cw

    def body(g0, g1, g2, b0, b1, b2, ya_ref, yb_ref, yc_ref, o_ref):
        o = (_sig(g0[...] + b0[...]) * ya_ref[...] + _sig(g1[...] + b1[...]) * yb_ref[...]
             + _sig(g2[...] + b2[...]) * yc_ref[...])
        o_ref[...] = o.astype(o_ref.dtype)

    def gl(br):
        return pl.BlockSpec((tm, cw), lambda j, i: (i, br * nj + j))

    def gbs(br):
        return pl.BlockSpec((1, cw), lambda j, i: (0, br * nj + j))

    blk = pl.BlockSpec((tm, cw), lambda j, i: (i, j))
    return pl.pallas_call(
        body, grid=(nj, m // tm), in_specs=[gl(0), gl(1), gl(2), gbs(0), gbs(1), gbs(2), blk, blk, blk],
        out_specs=blk, out_shape=jax.ShapeDtypeStruct((m, d), BF16), name="gate_fwd",
        compiler_params=_cp(2))(proj, proj, proj, gb, gb, gb, ya, yb, yc)


def _gate_bwd(dm, proj, gb, ya, yb, yc, *, d, tm):
    m = ya.shape[0]
    cw = _tile(d, 512, LANE)
    nj = d // cw

    def body(dm_ref, g0, g1, g2, b0, b1, b2, ya_ref, yb_ref, yc_ref,
             dya, dyb, dyc, dg0, dg1, dg2, db0, db1, db2):
        dmv = dm_ref[...]

        @pl.when(pl.program_id(1) == 0)
        def _():
            for r in (db0, db1, db2):
                r[...] = jnp.zeros_like(r)

        for g, b, y, dy, dgl, db in ((g0, b0, ya_ref, dya, dg0, db0), (g1, b1, yb_ref, dyb, dg1, db1),
                                     (g2, b2, yc_ref, dyc, dg2, db2)):
            sg = _sig(g[...] + b[...])
            dy[...] = (sg * dmv).astype(dy.dtype)
            dl = dmv * y[...] * sg * (1.0 - sg)
            dgl[...] = dl.astype(dgl.dtype)
            db[...] += jnp.sum(dl, axis=0, keepdims=True)

    def gl(br):
        return pl.BlockSpec((tm, cw), lambda j, i: (i, br * nj + j))

    def gbs(br):
        return pl.BlockSpec((1, cw), lambda j, i: (0, br * nj + j))

    blk = pl.BlockSpec((tm, cw), lambda j, i: (i, j))
    vec = pl.BlockSpec((1, cw), lambda j, i: (0, j))
    act = jax.ShapeDtypeStruct((m, d), BF16)
    vsh = jax.ShapeDtypeStruct((1, d), F32)
    return pl.pallas_call(
        body, grid=(nj, m // tm), in_specs=[blk, gl(0), gl(1), gl(2), gbs(0), gbs(1), gbs(2), blk, blk, blk],
        out_specs=(blk,) * 6 + (vec,) * 3, out_shape=(act,) * 6 + (vsh,) * 3, name="gate_bwd",
        compiler_params=_cp(2))(dm, proj, proj, proj, gb, gb, gb, ya, yb, yc)


def _swiglu_fwd(gt, up, tm):
    m, f = gt.shape
    cw = _tile(f, 512, LANE)

    def body(g_ref, u_ref, o_ref):
        g = g_ref[...]
        o_ref[...] = (g * _sig(g) * u_ref[...]).astype(o_ref.dtype)

    blk = pl.BlockSpec((tm, cw), lambda j, i: (i, j))
    return pl.pallas_call(body, grid=(f // cw, m // tm), in_specs=[blk, blk], out_specs=blk,
                          out_shape=jax.ShapeDtypeStruct((m, f), BF16), name="swiglu_fwd",
                          compiler_params=_cp(2))(gt, up)


def _swiglu_bwd(dact, gt, up, tm):
    m, f = gt.shape
    cw = _tile(f, 512, LANE)

    def body(d_ref, g_ref, u_ref, dg_ref, du_ref):
        g, d = g_ref[...], d_ref[...]
        sg = _sig(g)
        dg_ref[...] = (d * u_ref[...] * sg * (1.0 + g * (1.0 - sg))).astype(dg_ref.dtype)
        du_ref[...] = (d * g * sg).astype(du_ref.dtype)

    blk = pl.BlockSpec((tm, cw), lambda j, i: (i, j))
    sh = jax.ShapeDtypeStruct((m, f), BF16)
    return pl.pallas_call(body, grid=(f // cw, m // tm), in_specs=[blk, blk, blk], out_specs=(blk, blk),
                          out_shape=(sh, sh), name="swiglu_bwd", compiler_params=_cp(2))(dact, gt, up)


def _loss_head(h, gf, tgt, *, lp, l_valid, tm):
    m, d = h.shape
    nt = lp // tm

    def body(h_ref, g_ref, t_ref, dh_ref, dg_ref, loss_ref):
        i = pl.program_id(0)
        xv = h_ref[...]
        r = lax.rsqrt(jnp.mean(xv * xv, axis=-1, keepdims=True) + NORM_EPS)
        y = xv * r * g_ref[...]
        t = lax.broadcasted_iota(jnp.int32, (tm, d), 0) + (i % nt) * tm
        err = jnp.where((t >= N_META) & (t < l_valid), y - t_ref[...], 0.0)
        dy = err * (1.0 / d)
        tg = dy * g_ref[...]
        dh_ref[...] = r * tg - xv * (r * r * r) * jnp.mean(tg * xv, axis=-1, keepdims=True)

        @pl.when(i == 0)
        def _():
            dg_ref[...] = jnp.zeros_like(dg_ref)
            loss_ref[...] = jnp.zeros_like(loss_ref)

        dg_ref[...] += jnp.sum(dy * xv * r, axis=0, keepdims=True)
        sq = jnp.sum(jnp.sum(err * err, axis=-1, keepdims=True), axis=0, keepdims=True)
        loss_ref[...] += jnp.zeros((1, LANE), F32) + sq * (0.5 / d)

    row = pl.BlockSpec((tm, d), lambda i: (i, 0))
    vec = pl.BlockSpec((1, d), lambda i: (0, 0))
    return pl.pallas_call(
        body, grid=(m // tm,), in_specs=[row, vec, row],
        out_specs=(row, vec, pl.BlockSpec((1, LANE), lambda i: (0, 0))),
        out_shape=(jax.ShapeDtypeStruct((m, d), F32), jax.ShapeDtypeStruct((1, d), F32),
                   jax.ShapeDtypeStruct((1, LANE), F32)),
        name="loss_head", compiler_params=_cp(1))(h, gf, tgt)


def _adamw(w, g, m, v, name):
    rows, cols = w.shape
    tr = rows
    if rows % 8 == 0:
        tr = _tile(rows, max(8, (1 << 18) // cols // 8 * 8), 8)
    c1 = 1.0 / (1.0 - ADAM_B1 ** ADAM_STEP)
    c2 = 1.0 / (1.0 - ADAM_B2 ** ADAM_STEP)

    def body(w_ref, g_ref, m_ref, v_ref, d_ref, mo_ref, vo_ref):
        gv = g_ref[...]
        mn = ADAM_B1 * m_ref[...] + (1.0 - ADAM_B1) * gv
        vn = ADAM_B2 * v_ref[...] + (1.0 - ADAM_B2) * (gv * gv)
        mo_ref[...] = mn
        vo_ref[...] = vn
        d_ref[...] = -ADAM_LR * ((mn * c1) / (jnp.sqrt(vn * c2) + ADAM_EPS) + ADAM_WD * w_ref[...])

    blk = pl.BlockSpec((tr, cols), lambda i: (i, 0))
    sh = jax.ShapeDtypeStruct((rows, cols), F32)
    return pl.pallas_call(body, grid=(rows // tr,), in_specs=[blk] * 4, out_specs=(blk,) * 3, out_shape=(sh,) * 3,
                          name=name, compiler_params=_cp(1))(w, g, m, v)


MESH = pl.DeviceIdType.MESH
ANY = pl.BlockSpec(memory_space=pl.ANY)


def _place():
    return lax.axis_index("x"), lax.axis_index("y"), lax.axis_index("c")


def _other_chips(x, y):
    return [(1 - x, y), (x, 1 - y), (1 - x, 1 - y)]


def _all_gather(xs, vmem, name):
    na = len(xs)

    def body(*refs):
        x_refs, out_refs = refs[:na], refs[na:2 * na]
        send_sems, recv_sems, local_sems = refs[2 * na:]
        x, y, c = _place()
        me, sibling = (x, y, c), (x, y, 1 - c)
        chips = _other_chips(x, y)

        def blk(p, px, py, pc):
            return out_refs[p].at[4 * px + 2 * py + pc]

        def copy(k, p, block, to, src=None):
            return pltpu.make_async_remote_copy(
                src_ref=blk(p, *block) if src is None else src, dst_ref=blk(p, *block),
                send_sem=send_sems.at[k * na + p], recv_sem=recv_sems.at[k * na + p], device_id=to,
                device_id_type=MESH)

        mine = [pltpu.make_async_copy(x_refs[p], blk(p, *me), local_sems.at[p]) for p in range(na)]
        for cp in mine:
            cp.start()
        first = [copy(0, p, me, sibling, src=x_refs[p]) for p in range(na)]
        first += [copy(1 + j, p, me, (*chip, c), src=x_refs[p]) for j, chip in enumerate(chips) for p in range(na)]
        for cp in first:
            cp.start()
        passed = []
        for j, chip in enumerate(chips):
            for p in range(na):
                copy(1 + j, p, (*chip, c), me).wait_recv()
                fwd = copy(4 + j, p, (*chip, c), sibling)
                fwd.start()
                passed.append(fwd)
        for p in range(na):
            copy(0, p, sibling, me).wait_recv()
        for j, chip in enumerate(chips):
            for p in range(na):
                copy(4 + j, p, (*chip, 1 - c), me).wait_recv()
        for cp in first + passed:
            cp.wait_send()
        for cp in mine:
            cp.wait()

    spec = pl.BlockSpec(memory_space=pltpu.VMEM) if vmem else ANY
    return pl.pallas_call(
        body, out_shape=[jax.ShapeDtypeStruct((N_DEV,) + a.shape, a.dtype) for a in xs],
        in_specs=[spec] * na, out_specs=[spec] * na,
        scratch_shapes=[pltpu.SemaphoreType.DMA((7 * na,)), pltpu.SemaphoreType.DMA((7 * na,)),
                        pltpu.SemaphoreType.DMA((na,))],
        name=name)(*xs)


def _swap_with_sibling(g4s):
    na = len(g4s)

    def body(*refs):
        g_refs, out_refs = refs[:na], refs[na:2 * na]
        send_sems, recv_sems = refs[2 * na:]
        x, y, c = _place()
        cps = [pltpu.make_async_remote_copy(src_ref=g_refs[p].at[:, 1 - c], dst_ref=out_refs[p],
                                            send_sem=send_sems.at[p], recv_sem=recv_sems.at[p],
                                            device_id=(x, y, 1 - c), device_id_type=MESH) for p in range(na)]
        for cp in cps:
            cp.start()
        for cp in cps:
            cp.wait()

    return pl.pallas_call(
        body, out_shape=[jax.ShapeDtypeStruct((g.shape[0],) + g.shape[2:], g.dtype) for g in g4s],
        in_specs=[ANY] * na, out_specs=[ANY] * na,
        scratch_shapes=[pltpu.SemaphoreType.DMA((na,)), pltpu.SemaphoreType.DMA((na,))], name="rs_sibling")(*g4s)


def _scatter_to_chips(p4s):
    na = len(p4s)

    def body(*refs):
        p_refs, out_refs = refs[:na], refs[na:2 * na]
        send_sems, recv_sems = refs[2 * na:]
        x, y, c = _place()
        cps = [pltpu.make_async_remote_copy(src_ref=p_refs[p].at[2 * px + py], dst_ref=out_refs[p].at[j],
                                            send_sem=send_sems.at[j * na + p], recv_sem=recv_sems.at[j * na + p],
                                            device_id=(px, py, c), device_id_type=MESH)
               for j, (px, py) in enumerate(_other_chips(x, y)) for p in range(na)]
        for cp in cps:
            cp.start()
        for cp in cps:
            cp.wait()

    return pl.pallas_call(
        body, out_shape=[jax.ShapeDtypeStruct((3,) + a.shape[1:], a.dtype) for a in p4s],
        in_specs=[ANY] * na, out_specs=[ANY] * na,
        scratch_shapes=[pltpu.SemaphoreType.DMA((3 * na,)), pltpu.SemaphoreType.DMA((3 * na,))],
        name="rs_chips")(*p4s)


HI_MASK = -65536


def _pack_pairs_xla(a):
    half = a.shape[-1] // 2
    bits = lax.bitcast_convert_type(a.astype(BF16), jnp.uint16).astype(jnp.uint32)
    return lax.bitcast_convert_type((bits[..., half:] << 16) | bits[..., :half], F32)


def _split_pairs(words):
    wv = lax.bitcast_convert_type(words, jnp.int32)
    return lax.bitcast_convert_type(wv << 16, F32), lax.bitcast_convert_type(wv & HI_MASK, F32)


def _join_pairs(lo, hi):
    lo_b = lax.bitcast_convert_type(lo.astype(BF16).astype(F32), jnp.int32)
    hi_b = lax.bitcast_convert_type(hi.astype(BF16).astype(F32), jnp.int32)
    return lax.bitcast_convert_type((hi_b & HI_MASK) | lax.shift_right_logical(lo_b, 16), F32)


def _unpack(xp, *, name, out_rows=None, tr=None, src_fn=None, zero_fn=None):
    r, hw = xp.shape
    out_rows = out_rows or r
    tr = tr or _tile(out_rows, 512, 16)
    src = src_fn or (lambda i: i)

    def body(x_ref, o_ref):
        lo, hi = _split_pairs(x_ref[...])
        if zero_fn is not None:
            z = zero_fn(pl.program_id(0))
            lo, hi = jnp.where(z, 0.0, lo), jnp.where(z, 0.0, hi)
        o_ref[:, :hw] = lo.astype(BF16)
        o_ref[:, hw:] = hi.astype(BF16)

    return pl.pallas_call(
        body, grid=(out_rows // tr,), in_specs=[pl.BlockSpec((tr, hw), lambda i: (src(i), 0))],
        out_specs=pl.BlockSpec((tr, 2 * hw), lambda i: (i, 0)),
        out_shape=jax.ShapeDtypeStruct((out_rows, 2 * hw), BF16), name=name, compiler_params=_cp(1))(xp)


def _gather_rows(xp, *, name, out_rows, tr, src_fn):
    _, hw = xp.shape

    def body(x_ref, o_ref):
        o_ref[...] = x_ref[...]

    return pl.pallas_call(
        body, grid=(out_rows // tr,), in_specs=[pl.BlockSpec((tr, hw), lambda i: (src_fn(i), 0))],
        out_specs=pl.BlockSpec((tr, hw), lambda i: (i, 0)),
        out_shape=jax.ShapeDtypeStruct((out_rows, hw), xp.dtype), name=name, compiler_params=_cp(1))(xp)


def _matmul_tn_packed(a, b, name):
    (t, m), (t2, c) = a.shape, b.shape
    assert t == t2
    hw = c // 2
    tm = _tile(m, 1088, LANE)
    tn = _tile(hw, 512, LANE)
    tk = _tile(t, 2176, LANE)
    nk, nj = t // tk, hw // tn

    def body(a_ref, bl_ref, bh_ref, o_ref, acc_lo, acc_hi):
        kk = pl.program_id(2)

        @pl.when(kk == 0)
        def _():
            acc_lo[...] = jnp.zeros_like(acc_lo)
            acc_hi[...] = jnp.zeros_like(acc_hi)

        av = a_ref[...].astype(BF16)
        acc_lo[...] += lax.dot_general(av, bl_ref[...].astype(BF16), TN, preferred_element_type=F32)
        acc_hi[...] += lax.dot_general(av, bh_ref[...].astype(BF16), TN, preferred_element_type=F32)

        @pl.when(kk == nk - 1)
        def _():
            o_ref[...] = _join_pairs(acc_lo[...], acc_hi[...])

    return pl.pallas_call(
        body, grid=(m // tm, nj, nk),
        in_specs=[pl.BlockSpec((tk, tm), lambda i, j, kk: (kk, i)), pl.BlockSpec((tk, tn), lambda i, j, kk: (kk, j)),
                  pl.BlockSpec((tk, tn), lambda i, j, kk: (kk, nj + j))],
        out_specs=pl.BlockSpec((tm, tn), lambda i, j, kk: (i, j)), out_shape=jax.ShapeDtypeStruct((m, hw), F32),
        scratch_shapes=[pltpu.VMEM((tm, tn), F32), pltpu.VMEM((tm, tn), F32)], name=name,
        compiler_params=_cp(3))(a, b, b)


def _add_sibling(g4, recv, cidx, name):
    nchip, _, r, n = g4.shape
    tr = _tile(r, 256, 8)

    def body(c_ref, a_ref, b_ref, o_ref):
        alo, ahi = _split_pairs(a_ref[...])
        blo, bhi = _split_pairs(b_ref[...])
        o_ref[...] = _join_pairs(alo + blo, ahi + bhi)

    grid_spec = pltpu.PrefetchScalarGridSpec(
        num_scalar_prefetch=1, grid=(nchip, r // tr),
        in_specs=[pl.BlockSpec((None, None, tr, n), lambda k, i, c: (k, c[0], i, 0)),
                  pl.BlockSpec((None, tr, n), lambda k, i, c: (k, i, 0))],
        out_specs=pl.BlockSpec((None, tr, n), lambda k, i, c: (k, i, 0)))
    return pl.pallas_call(body, grid_spec=grid_spec, out_shape=jax.ShapeDtypeStruct((nchip, r, n), F32),
                          name=name, compiler_params=_cp(2))(cidx, g4, recv)


def _add_chips(p4, recv3, chip_idx, name):
    _, r, n = p4.shape
    tr = _tile(r, 256, 8)

    def body(k_ref, a_ref, b_ref, o_ref):
        lo, hi = _split_pairs(a_ref[...])
        for j in range(3):
            blo, bhi = _split_pairs(b_ref[j])
            lo, hi = lo + blo, hi + bhi
        o_ref[:, :n] = lo
        o_ref[:, n:] = hi

    grid_spec = pltpu.PrefetchScalarGridSpec(
        num_scalar_prefetch=1, grid=(r // tr,),
        in_specs=[pl.BlockSpec((None, tr, n), lambda i, k: (k[0], i, 0)),
                  pl.BlockSpec((3, tr, n), lambda i, k: (0, i, 0))],
        out_specs=pl.BlockSpec((tr, 2 * n), lambda i, k: (i, 0)))
    return pl.pallas_call(body, grid_spec=grid_spec, out_shape=jax.ShapeDtypeStruct((r, 2 * n), F32),
                          name=name, compiler_params=_cp(1))(chip_idx, p4, recv3)


def _all_reduce_small(xs):
    r, n = xs.shape

    def body(x_ref, o_ref, buf, send_sems, recv_sems):
        x, y, c = _place()
        me = 4 * x + 2 * y + c
        buf[me] = x_ref[...]
        peers = []
        for k in range(1, N_DEV):
            px = (1 - x) if (k >> 2) & 1 else x
            py = (1 - y) if (k >> 1) & 1 else y
            pc = (1 - c) if k & 1 else c
            peers.append((px, py, pc))
        sends = [pltpu.make_async_remote_copy(src_ref=x_ref, dst_ref=buf.at[me], send_sem=send_sems.at[k],
                                              recv_sem=recv_sems.at[k], device_id=peer, device_id_type=MESH)
                 for k, peer in enumerate(peers)]
        for cp in sends:
            cp.start()
        for k, (px, py, pc) in enumerate(peers):
            pltpu.make_async_remote_copy(src_ref=x_ref, dst_ref=buf.at[4 * px + 2 * py + pc],
                                         send_sem=send_sems.at[k], recv_sem=recv_sems.at[k],
                                         device_id=(px, py, pc), device_id_type=MESH).wait_recv()
        for cp in sends:
            cp.wait_send()
        acc = buf[0]
        for dv in range(1, N_DEV):
            acc = acc + buf[dv]
        o_ref[...] = acc

    vm = pl.BlockSpec(memory_space=pltpu.VMEM)
    return pl.pallas_call(
        body, out_shape=jax.ShapeDtypeStruct((r, n), F32), in_specs=[vm], out_specs=vm,
        scratch_shapes=[pltpu.VMEM((N_DEV, r, n), F32), pltpu.SemaphoreType.DMA((7,)), pltpu.SemaphoreType.DMA((7,))],
        name="all_reduce_small", compiler_params=pltpu.CompilerParams(vmem_limit_bytes=VMEM_LIMIT))(xs)


def _pack_rows(arrs, quantum):
    parts, sizes = [], []
    for a in arrs:
        f = a.reshape(-1)
        pad = (-f.shape[0]) % quantum
        if pad:
            f = jnp.pad(f, (0, pad))
        parts.append(f)
        sizes.append(f.shape[0])
    return jnp.concatenate(parts).reshape(-1, LANE), sizes


def _unpack_rows(buf, sizes, shapes):
    flat = buf.reshape(-1)
    out, off = [], 0
    for sz, sh in zip(sizes, shapes):
        n = math.prod(sh)
        out.append(flat[off:off + n].reshape(sh))
        off += sz
    return out


def kernel(x, meta_tokens, mix_norm_g, w_in, conv_dw, conv_b, conv_ln_g, conv_ln_b, w_conv_out, gqa_q_norm_g, gqa_k_norm_g, w_gqa_out, mla_q_norm_g, w_mla_uq, mla_kv_norm_g, w_mla_ukv, w_mla_out, gate_b, w_out, ffn_norm_g, w_ffn_gate, w_ffn_up, w_ffn_down, final_norm_g, loss_target, m_meta_tokens, m_mix_norm_g, m_w_in, m_conv_dw, m_conv_b, m_conv_ln_g, m_conv_ln_b, m_w_conv_out, m_gqa_q_norm_g, m_gqa_k_norm_g, m_w_gqa_out, m_mla_q_norm_g, m_w_mla_uq, m_mla_kv_norm_g, m_w_mla_ukv, m_w_mla_out, m_gate_b, m_w_out, m_ffn_norm_g, m_w_ffn_gate, m_w_ffn_up, m_w_ffn_down, m_final_norm_g, v_meta_tokens, v_mix_norm_g, v_w_in, v_conv_dw, v_conv_b, v_conv_ln_g, v_conv_ln_b, v_w_conv_out, v_gqa_q_norm_g, v_gqa_k_norm_g, v_w_gqa_out, v_mla_q_norm_g, v_w_mla_uq, v_mla_kv_norm_g, v_w_mla_ukv, v_w_mla_out, v_gate_b, v_w_out, v_ffn_norm_g, v_w_ffn_gate, v_w_ffn_up, v_w_ffn_down, v_final_norm_g):
    w = dict(meta_tokens=meta_tokens, mix_norm_g=mix_norm_g, w_in=w_in, conv_dw=conv_dw, conv_b=conv_b,
             conv_ln_g=conv_ln_g, conv_ln_b=conv_ln_b, w_conv_out=w_conv_out, gqa_q_norm_g=gqa_q_norm_g,
             gqa_k_norm_g=gqa_k_norm_g, w_gqa_out=w_gqa_out, mla_q_norm_g=mla_q_norm_g, w_mla_uq=w_mla_uq,
             mla_kv_norm_g=mla_kv_norm_g, w_mla_ukv=w_mla_ukv, w_mla_out=w_mla_out, gate_b=gate_b, w_out=w_out,
             ffn_norm_g=ffn_norm_g, w_ffn_gate=w_ffn_gate, w_ffn_up=w_ffn_up, w_ffn_down=w_ffn_down,
             final_norm_g=final_norm_g)
    mom1 = dict(zip(W_NAMES, (m_meta_tokens, m_mix_norm_g, m_w_in, m_conv_dw, m_conv_b, m_conv_ln_g, m_conv_ln_b, m_w_conv_out, m_gqa_q_norm_g, m_gqa_k_norm_g, m_w_gqa_out, m_mla_q_norm_g, m_w_mla_uq, m_mla_kv_norm_g, m_w_mla_ukv, m_w_mla_out, m_gate_b, m_w_out, m_ffn_norm_g, m_w_ffn_gate, m_w_ffn_up, m_w_ffn_down, m_final_norm_g)))
    mom2 = dict(zip(W_NAMES, (v_meta_tokens, v_mix_norm_g, v_w_in, v_conv_dw, v_conv_b, v_conv_ln_g, v_conv_ln_b, v_w_conv_out, v_gqa_q_norm_g, v_gqa_k_norm_g, v_w_gqa_out, v_mla_q_norm_g, v_w_mla_uq, v_mla_kv_norm_g, v_w_mla_ukv, v_w_mla_out, v_gate_b, v_w_out, v_ffn_norm_g, v_w_ffn_gate, v_w_ffn_up, v_w_ffn_down, v_final_norm_g)))

    nb, seq, d = x.shape
    depth = w_in.shape[0]
    c_conv = conv_b.shape[1]
    kw = conv_dw.shape[1]
    hd = HEAD_DIM
    hq = w_gqa_out.shape[1] // hd
    nh = w_mla_out.shape[1] // hd
    qr, kvr = mla_q_norm_g.shape[1], mla_kv_norm_g.shape[1]
    d_in = w_in.shape[2] * N_DEV
    hkv = (d_in - 2 * c_conv - hq * hd - qr - kvr - MLA_ROPE_DIM - N_BRANCH * d) // (2 * hd)
    l_valid = seq + N_META
    lp = -(-(l_valid + kw // 2) // LANE) * LANE
    m_tok = nb * lp
    mla_qk = hd + MLA_ROPE_DIM

    off_gate = 0
    off_conv = N_BRANCH * d
    off_q = off_conv + 2 * c_conv
    off_cq = off_q + (hq + 2 * hkv) * hd
    off_ckv = off_cq + qr
    off_kpe = off_ckv + kvr
    n_proj = off_kpe + LANE

    x_id, y_id, c_id = _place()
    me = 4 * x_id + 2 * y_id + c_id

    tm_wide = _tile(lp, 272, 16)
    tm_mid = _tile(lp, 544, 16)

    sm_buf, sm_sizes = _pack_rows([meta_tokens, conv_dw], 8 * LANE)
    sm_all = _all_gather([sm_buf], True, "ag_small")[0]
    sm_all = sm_all.reshape(N_DEV, -1)
    o0 = 0
    meta_all = sm_all[:, o0:o0 + meta_tokens.size].reshape((N_DEV,) + meta_tokens.shape)
    meta_full = jnp.transpose(meta_all, (1, 0, 2)).reshape(N_META, d)
    o0 = sm_sizes[0]
    dw_all = sm_all[:, o0:o0 + conv_dw.size].reshape((N_DEV, depth, kw, c_conv // N_DEV))
    dw_full = jnp.transpose(dw_all, (1, 2, 0, 3)).reshape(depth, kw, c_conv)

    n_head_cols = d_in - N_BRANCH * d - MLA_ROPE_DIM
    rb = MLA_ROPE_DIM
    nb_gate, nb_head = N_BRANCH * d // rb, n_head_cols // rb
    assert n_head_cols % rb == 0 and (N_BRANCH * d) % rb == 0 and n_proj == (nb_gate + nb_head + 2) * rb

    def in_src(i):
        return jnp.where(i < nb_gate, nb_head + 1 + i,
                         jnp.where(i < nb_gate + nb_head, i - nb_gate, jnp.where(i == nb_gate + nb_head, nb_head, 0)))

    def in_dst(i):
        return jnp.where(i < nb_head, nb_gate + i, jnp.where(i == nb_head, nb_gate + nb_head, i - nb_head - 1))

    def uq_src(i):
        j = i - 2 * nh
        return jnp.where(i < 2 * nh, 3 * (i // 2) + i % 2, 3 * (j // 2) + 2)

    def uq_dst(i):
        return jnp.where(i % 3 < 2, 2 * (i // 3) + i % 3, 2 * nh + 2 * (i // 3))

    def all_gather_layer(i):
        shards = [_pack_pairs_xla(w[n][i] if n in ROW_SHARDED else w[n][i].T) for n in BIG]
        gath = _all_gather(shards, False, "ag_weights")
        pk = {n: g.reshape(-1, g.shape[-1]) for n, g in zip(BIG, gath)}
        wl = {n: _unpack(pk[n], name="unpack_" + n) for n in BIG if n not in ('w_in', 'w_mla_uq', 'w_mla_ukv')}
        wl['w_in'] = _unpack(pk['w_in'], name="unpack_w_in", out_rows=n_proj, tr=rb, src_fn=in_src,
                             zero_fn=lambda i: i == nb_gate + nb_head + 1)
        wl['w_mla_uq'] = _unpack(pk['w_mla_uq'], name="unpack_w_mla_uq", out_rows=2 * nh * hd, tr=rb, src_fn=uq_src,
                                 zero_fn=lambda i: (i >= 2 * nh) & ((i - 2 * nh) % 2 == 1))
        wl['w_mla_ukv'] = _unpack(pk['w_mla_ukv'], name="unpack_w_mla_ukv", out_rows=2 * nh * hd, tr=hd,
                                  src_fn=lambda i: jnp.where(i < nh, 2 * i, 2 * (i - nh) + 1))
        return wl

    full = [all_gather_layer(i) for i in range(depth)]

    cos_g, sin_g = _rope_tables(lp, l_valid, hd // 4)
    cos_m, sin_m = _rope_tables(lp, l_valid, MLA_ROPE_DIM // 4)

    def vec2(a):
        return a.reshape(1, -1)

    meta_b = jnp.broadcast_to(meta_full[None], (nb, N_META, d))
    h = jnp.concatenate([meta_b, x, jnp.zeros((nb, lp - l_valid, d), F32)], axis=1).reshape(m_tok, d)
    tgt = jnp.pad(loss_target, ((0, 0), (N_META, lp - l_valid), (0, 0))).reshape(m_tok, d)

    saved = []
    for i in range(depth):
        s = {}
        s['h_in'] = h
        u = _rms_fwd(h, d, 0, vec2(mix_norm_g[i]), tm_wide, "mix_norm_fwd")
        wl = full[i]
        proj = _matmul(u, wl['w_in'], mode="nt", out_dtype=F32, name="mm_in")
        cc = _conv_fwd(proj, dw_full[i], vec2(conv_b[i]), c_conv=c_conv, off_a=off_conv, nb=nb, lp=lp, l_valid=l_valid)
        sc = _ln_silu_fwd(cc, vec2(conv_ln_g[i]), vec2(conv_ln_b[i]), tm_mid)
        ya = _matmul(sc, wl['w_conv_out'], mode="nt", out_dtype=F32, name="mm_conv_out")
        qh, kh, vh = _gqa_prep_fwd(proj, vec2(gqa_q_norm_g[i]), vec2(gqa_k_norm_g[i]), cos_g, sin_g,
                                   off_q=off_q, hq=hq, hkv=hkv, nb=nb, lp=lp, tm=tm_mid)
        ob = _attn_fwd(qh, kh, vh, scale=1.0 / math.sqrt(hd), l_valid=l_valid, name="gqa_attn_fwd")
        yb = _matmul(ob, wl['w_gqa_out'], mode="nt", out_dtype=F32, name="mm_gqa_out")
        cqn = _rms_fwd(proj, qr, _cb(off_cq, qr), vec2(mla_q_norm_g[i]), tm_mid, "mla_q_norm_fwd")
        ckvn = _rms_fwd(proj, kvr, _cb(off_ckv, kvr), vec2(mla_kv_norm_g[i]), tm_mid, "mla_kv_norm_fwd")
        qm = _matmul(cqn, wl['w_mla_uq'], mode="nt", out_dtype=F32, name="mm_mla_uq")
        kvm = _matmul(ckvn, wl['w_mla_ukv'], mode="nt", out_dtype=F32, name="mm_mla_ukv")
        qc, kc, vc = _mla_prep_fwd(qm, kvm, proj, cos_m, sin_m, off_kpe=off_kpe, nh=nh, nb=nb, lp=lp, tm=tm_mid)
        oc = _attn_fwd(qc, kc, vc, scale=1.0 / math.sqrt(mla_qk), l_valid=l_valid, name="mla_attn_fwd")
        yc = _matmul(oc, wl['w_mla_out'], mode="nt", out_dtype=F32, name="mm_mla_out")
        merged = _gate_fwd(proj, vec2(gate_b[i]), ya, yb, yc, d=d, tm=tm_wide)
        h2 = _matmul(merged, wl['w_out'], mode="nn", out_dtype=F32, name="mm_out", residual=h)
        vn = _rms_fwd(h2, d, 0, vec2(ffn_norm_g[i]), tm_wide, "ffn_norm_fwd")
        gt = _matmul(vn, wl['w_ffn_gate'], mode="nt", out_dtype=F32, name="mm_ffn_gate")
        up = _matmul(vn, wl['w_ffn_up'], mode="nt", out_dtype=F32, name="mm_ffn_up")
        act = _swiglu_fwd(gt, up, tm_wide)
        h = _matmul(act, wl['w_ffn_down'], mode="nn", out_dtype=F32, name="mm_ffn_down", residual=h2)
        s.update(u=u, proj=proj, cc=cc, sc=sc, ya=ya, qh=qh, kh=kh, vh=vh, ob=ob, yb=yb, cqn=cqn, ckvn=ckvn,
                 qc=qc, kc=kc, vc=vc, oc=oc, yc=yc, merged=merged, h2=h2, vn=vn, gt=gt, up=up, act=act)
        saved.append(s)

    dh, dg_final, loss_part = _loss_head(h, vec2(final_norm_g), tgt, lp=lp, l_valid=l_valid, tm=tm_wide)

    gsmall = {n: [None] * depth for n in SMALL if n != 'final_norm_g'}
    gdw = [None] * depth
    gbig = [None] * depth
    cidx = jnp.reshape(c_id, (1,)).astype(jnp.int32)
    chip_idx = jnp.reshape(2 * x_id + y_id, (1,)).astype(jnp.int32)

    def reduce_scatter_layer(gl):
        g4s = [gl[n].reshape(N_DEV // 2, 2, -1, gl[n].shape[-1]) for n in BIG]
        from_sibling = _swap_with_sibling(g4s)
        p4s = [_add_sibling(g, r, cidx, "rs_add_sibling_" + n) for n, g, r in zip(BIG, g4s, from_sibling)]
        from_chips = _scatter_to_chips(p4s)
        out = {}
        for n, p, r in zip(BIG, p4s, from_chips):
            g = _add_chips(p, r, chip_idx, "rs_add_chips_" + n)
            out[n] = g if n in ROW_SHARDED else g.T
        return out

    for i in reversed(range(depth)):
        s = saved[i]
        proj = s['proj']
        wl = full[i]
        gl = {}
        dact = _matmul(dh, wl['w_ffn_down'], mode="nt", out_dtype=F32, name="mm_ffn_down_dx")
        gl['w_ffn_down'] = _matmul_tn_packed(s['act'], dh, "mm_ffn_down_dw")
        dgt, dup = _swiglu_bwd(dact, s['gt'], s['up'], tm_wide)
        dvn = _matmul(dgt, wl['w_ffn_gate'], mode="nn", out_dtype=F32, name="mm_ffn_gate_dx")
        dvn = _matmul(dup, wl['w_ffn_up'], mode="nn", out_dtype=F32, name="mm_ffn_up_dx", residual=dvn)
        gl['w_ffn_gate'] = _matmul_tn_packed(dgt, s['vn'], "mm_ffn_gate_dw")
        gl['w_ffn_up'] = _matmul_tn_packed(dup, s['vn'], "mm_ffn_up_dw")
        dh2, gsmall['ffn_norm_g'][i] = _rms_bwd(s['h2'], d, 0, vec2(ffn_norm_g[i]), dvn, tm_wide, F32,
                                                "ffn_norm_bwd", add=dh)
        dmg = _matmul(dh2, wl['w_out'], mode="nt", out_dtype=F32, name="mm_out_dx")
        gl['w_out'] = _matmul_tn_packed(s['merged'], dh2, "mm_out_dw")
        dya, dyb, dyc, dg0, dg1, dg2, db0, db1, db2 = _gate_bwd(dmg, proj, vec2(gate_b[i]), s['ya'], s['yb'],
                                                                s['yc'], d=d, tm=tm_wide)
        gsmall['gate_b'][i] = jnp.concatenate([db0, db1, db2], axis=1)
        dsc = _matmul(dya, wl['w_conv_out'], mode="nn", out_dtype=F32, name="mm_conv_out_dx")
        gl['w_conv_out'] = _matmul_tn_packed(dya, s['sc'], "mm_conv_out_dw")
        dcc, gsmall['conv_ln_g'][i], gsmall['conv_ln_b'][i] = _ln_silu_bwd(
            s['cc'], vec2(conv_ln_g[i]), vec2(conv_ln_b[i]), dsc, tm_mid)
        da, dgc, gdw[i], gsmall['conv_b'][i] = _conv_bwd(proj, dcc, dw_full[i], c_conv=c_conv, off_a=off_conv,
                                                         nb=nb, lp=lp, l_valid=l_valid)
        dob = _matmul(dyb, wl['w_gqa_out'], mode="nn", out_dtype=BF16, name="mm_gqa_out_dx")
        gl['w_gqa_out'] = _matmul_tn_packed(dyb, s['ob'], "mm_gqa_out_dw")
        dqh, dkh, dvh = _attn_bwd(s['qh'], s['kh'], s['vh'], dob, scale=1.0 / math.sqrt(hd), l_valid=l_valid,
                                  name="gqa_attn_bwd")
        dq, dkv, gsmall['gqa_q_norm_g'][i], gsmall['gqa_k_norm_g'][i] = _gqa_prep_bwd(
            proj, vec2(gqa_q_norm_g[i]), vec2(gqa_k_norm_g[i]), cos_g, sin_g, dqh, dkh, dvh,
            off_q=off_q, hq=hq, hkv=hkv, nb=nb, lp=lp, tm=tm_mid)
        doc = _matmul(dyc, wl['w_mla_out'], mode="nn", out_dtype=BF16, name="mm_mla_out_dx")
        gl['w_mla_out'] = _matmul_tn_packed(dyc, s['oc'], "mm_mla_out_dw")
        dqc, dkc, dvc = _attn_bwd(s['qc'], s['kc'], s['vc'], doc, scale=1.0 / math.sqrt(mla_qk), l_valid=l_valid,
                                  name="mla_attn_bwd")
        dqm, dkvm, dkpe = _mla_prep_bwd(dqc, dkc, dvc, cos_m, sin_m, nh=nh, nb=nb, lp=lp, tm=tm_mid)
        dcqn = _matmul(dqm, wl['w_mla_uq'], mode="nn", out_dtype=F32, name="mm_mla_uq_dx")
        guq = _matmul_tn_packed(dqm, s['cqn'], "mm_mla_uq_dw")
        dckvn = _matmul(dkvm, wl['w_mla_ukv'], mode="nn", out_dtype=F32, name="mm_mla_ukv_dx")
        gukv = _matmul_tn_packed(dkvm, s['ckvn'], "mm_mla_ukv_dw")
        dcq, gsmall['mla_q_norm_g'][i] = _rms_bwd(proj, qr, _cb(off_cq, qr), vec2(mla_q_norm_g[i]), dcqn, tm_mid,
                                                  BF16, "mla_q_norm_bwd")
        dckv, gsmall['mla_kv_norm_g'][i] = _rms_bwd(proj, kvr, _cb(off_ckv, kvr), vec2(mla_kv_norm_g[i]), dckvn,
                                                    tm_mid, BF16, "mla_kv_norm_bwd")
        gl['w_mla_uq'] = _gather_rows(guq, name="perm_mla_uq_dw", out_rows=nh * mla_qk, tr=rb, src_fn=uq_dst)
        gl['w_mla_ukv'] = _gather_rows(gukv, name="perm_mla_ukv_dw", out_rows=2 * nh * hd, tr=hd,
                                       src_fn=lambda i: i // 2 + nh * (i % 2))
        dproj = jnp.concatenate([dg0, dg1, dg2, da, dgc, dq, dkv, dcq, dckv, dkpe], axis=1)
        du = _matmul(dproj, wl['w_in'], mode="nn", out_dtype=F32, name="mm_in_dx")
        gin = _matmul_tn_packed(dproj, s['u'], "mm_in_dw")
        gl['w_in'] = _gather_rows(gin, name="perm_in_dw", out_rows=d_in, tr=rb, src_fn=in_dst)
        dh, gsmall['mix_norm_g'][i] = _rms_bwd(s['h_in'], d, 0, vec2(mix_norm_g[i]), du, tm_wide, F32,
                                               "mix_norm_bwd", add=dh2)
        gbig[i] = reduce_scatter_layer(gl)

    dh0 = dh.reshape(nb, lp, d)
    grad_x = dh0[:, N_META:l_valid]
    gmeta_full = jnp.sum(dh0[:, :N_META], axis=0)
    grads = {n: jnp.stack([gbig[i][n] for i in range(depth)]) for n in BIG}

    small_list = [jnp.stack(gsmall[n]).reshape(w[n].shape) for n in SMALL if n != 'final_norm_g']
    small_list += [dg_final.reshape(final_norm_g.shape), gmeta_full, jnp.stack(gdw), loss_part]
    ar_buf, ar_sizes = _pack_rows(small_list, 8 * LANE)
    ar = _all_reduce_small(ar_buf)
    ar_shapes = [w[n].shape for n in SMALL] + [(N_META, d), (depth, kw, c_conv), (1, LANE)]
    ar_out = _unpack_rows(ar, ar_sizes, ar_shapes)
    for n, g in zip(SMALL, ar_out):
        grads[n] = g
    dcol = d // N_DEV
    grads['meta_tokens'] = lax.dynamic_slice(ar_out[len(SMALL)], (0, me * dcol), (N_META, dcol))
    ccol = c_conv // N_DEV
    grads['conv_dw'] = lax.dynamic_slice(ar_out[len(SMALL) + 1], (0, 0, me * ccol),
                                         (depth, kw, ccol)).reshape(conv_dw.shape)
    loss = ar_out[len(SMALL) + 2][0, 0]

    delta, new_m, new_v = {}, {}, {}
    for n in BIG:
        sh = w[n].shape
        two_d = (sh[0] * sh[1], sh[2])
        dl, mn, vn_ = _adamw(w[n].reshape(two_d), grads[n].reshape(two_d), mom1[n].reshape(two_d),
                             mom2[n].reshape(two_d), "adamw_" + n)
        delta[n], new_m[n], new_v[n] = dl.reshape(sh), mn.reshape(sh), vn_.reshape(sh)
    rest = [n for n in W_NAMES if n not in BIG]
    pw, psz = _pack_rows([w[n] for n in rest], 8 * LANE)
    pg, _ = _pack_rows([grads[n] for n in rest], 8 * LANE)
    pm, _ = _pack_rows([mom1[n] for n in rest], 8 * LANE)
    pv, _ = _pack_rows([mom2[n] for n in rest], 8 * LANE)
    dl, mn, vn_ = _adamw(pw, pg, pm, pv, "adamw_small")
    shapes = [w[n].shape for n in rest]
    for n, a, b, c in zip(rest, _unpack_rows(dl, psz, shapes), _unpack_rows(mn, psz, shapes),
                          _unpack_rows(vn_, psz, shapes)):
        delta[n], new_m[n], new_v[n] = a, b, c

    return (loss, grad_x, *[grads[n] for n in W_NAMES], *[delta[n] for n in W_NAMES],
            *[new_m[n] for n in W_NAMES], *[new_v[n] for n in W_NAMES])
```

```python
import functools
import math

import jax
import jax.numpy as jnp
from jax import lax
from jax.experimental import pallas as pl
from jax.experimental.pallas import tpu as pltpu

F32 = jnp.float32
BF16 = jnp.bfloat16

N_META = 16
GRID_W = 64
ROPE_THETA = 10000.0
NORM_EPS = 1e-6
HEAD_DIM = 128
MLA_ROPE_DIM = 64
N_BRANCH = 3
N_DEV = 8
LANE = 128
VMEM_LIMIT = 56 * 1024 * 1024

ADAM_LR = 0.001
ADAM_B1 = 0.9
ADAM_B2 = 0.999
ADAM_EPS = 1e-08
ADAM_WD = 0.01
ADAM_STEP = 10

W_NAMES = ['meta_tokens', 'mix_norm_g', 'w_in', 'conv_dw', 'conv_b', 'conv_ln_g', 'conv_ln_b', 'w_conv_out',
           'gqa_q_norm_g', 'gqa_k_norm_g', 'w_gqa_out', 'mla_q_norm_g', 'w_mla_uq', 'mla_kv_norm_g', 'w_mla_ukv',
           'w_mla_out', 'gate_b', 'w_out', 'ffn_norm_g', 'w_ffn_gate', 'w_ffn_up', 'w_ffn_down', 'final_norm_g']
BIG = ['w_in', 'w_conv_out', 'w_gqa_out', 'w_mla_uq', 'w_mla_ukv', 'w_mla_out', 'w_out', 'w_ffn_gate', 'w_ffn_up',
       'w_ffn_down']
ROW_SHARDED = ('w_out', 'w_ffn_down')
SMALL = ['mix_norm_g', 'conv_b', 'conv_ln_g', 'conv_ln_b', 'gqa_q_norm_g', 'gqa_k_norm_g', 'mla_q_norm_g',
         'mla_kv_norm_g', 'gate_b', 'ffn_norm_g', 'final_norm_g']

NT = (((1,), (1,)), ((), ()))
TN = (((0,), (0,)), ((), ()))
NN = (((1,), (0,)), ((), ()))


def _tile(n, target, mult):
    best = None
    for t in range(mult, min(n, target) + 1, mult):
        if n % t == 0:
            best = t
    assert best is not None, (n, target, mult)
    return best


def _cp(n):
    return pltpu.CompilerParams(dimension_semantics=("arbitrary",) * n, vmem_limit_bytes=VMEM_LIMIT)


def _sig(x):
    return jax.nn.sigmoid(x)


def _cb(off, w):
    assert off % w == 0, (off, w)
    return off // w


def _matmul(a, b, *, mode, out_dtype, name, residual=None, deps=()):
    if mode == "nn":
        (m, k), (k2, n) = a.shape, b.shape
    elif mode == "nt":
        (m, k), (n, k2) = a.shape, b.shape
    else:
        (k, m), (k2, n) = a.shape, b.shape
    assert k == k2, (a.shape, b.shape, mode)
    tm = _tile(m, 1088, 128 if mode == "tn" else 16)
    tn = _tile(n, 1024, 128)
    tk = _tile(k, 2176, 128)
    nk = k // tk
    dims = {"nn": NN, "nt": NT, "tn": TN}[mode]
    a_spec = (pl.BlockSpec((tk, tm), lambda i, j, kk: (kk, i)) if mode == "tn"
              else pl.BlockSpec((tm, tk), lambda i, j, kk: (i, kk)))
    b_spec = (pl.BlockSpec((tn, tk), lambda i, j, kk: (j, kk)) if mode == "nt"
              else pl.BlockSpec((tk, tn), lambda i, j, kk: (kk, j)))
    o_spec = pl.BlockSpec((tm, tn), lambda i, j, kk: (i, j))
    has_res = residual is not None

    nd = len(deps)

    def body(*refs):
        refs = refs[:len(refs) - 2 - nd] + refs[len(refs) - 2:]
        if has_res:
            a_ref, b_ref, r_ref, o_ref, acc = refs
        else:
            a_ref, b_ref, o_ref, acc = refs
        kk = pl.program_id(2)

        @pl.when(kk == 0)
        def _():
            acc[...] = jnp.zeros_like(acc)

        acc[...] += lax.dot_general(a_ref[...].astype(BF16), b_ref[...].astype(BF16), dims,
                                    preferred_element_type=F32)

        @pl.when(kk == nk - 1)
        def _():
            r = acc[...]
            if has_res:
                r = r + r_ref[...]
            o_ref[...] = r.astype(o_ref.dtype)

    ins = [a, b] + ([residual] if has_res else []) + list(deps)
    in_specs = [a_spec, b_spec] + ([o_spec] if has_res else []) + [pl.BlockSpec(memory_space=pl.ANY)] * nd
    return pl.pallas_call(
        body, grid=(m // tm, n // tn, nk), in_specs=in_specs, out_specs=o_spec,
        out_shape=jax.ShapeDtypeStruct((m, n), out_dtype), scratch_shapes=[pltpu.VMEM((tm, tn), F32)],
        name=name, compiler_params=_cp(3))(*ins)


def _rms_fwd(x, w, cb, g, tm, name):
    m = x.shape[0]

    def body(x_ref, g_ref, o_ref):
        xv = x_ref[...]
        r = lax.rsqrt(jnp.mean(xv * xv, axis=-1, keepdims=True) + NORM_EPS)
        o_ref[...] = (xv * r * g_ref[...]).astype(o_ref.dtype)

    return pl.pallas_call(
        body, grid=(m // tm,),
        in_specs=[pl.BlockSpec((tm, w), lambda i: (i, cb)), pl.BlockSpec((1, w), lambda i: (0, 0))],
        out_specs=pl.BlockSpec((tm, w), lambda i: (i, 0)), out_shape=jax.ShapeDtypeStruct((m, w), BF16),
        name=name, compiler_params=_cp(1))(x, g)


def _rms_bwd(x, w, cb, g, dy, tm, out_dtype, name, add=None):
    m = x.shape[0]
    has_add = add is not None

    def body(*refs):
        if has_add:
            x_ref, g_ref, dy_ref, add_ref, dx_ref, dg_ref = refs
        else:
            x_ref, g_ref, dy_ref, dx_ref, dg_ref = refs
        xv = x_ref[...]
        dyv = dy_ref[...].astype(F32)
        r = lax.rsqrt(jnp.mean(xv * xv, axis=-1, keepdims=True) + NORM_EPS)
        t = dyv * g_ref[...]
        dx = r * t - xv * (r * r * r) * jnp.mean(t * xv, axis=-1, keepdims=True)
        if has_add:
            dx = dx + add_ref[...]
        dx_ref[...] = dx.astype(dx_ref.dtype)

        @pl.when(pl.program_id(0) == 0)
        def _():
            dg_ref[...] = jnp.zeros_like(dg_ref)

        dg_ref[...] += jnp.sum(dyv * xv * r, axis=0, keepdims=True)

    row = pl.BlockSpec((tm, w), lambda i: (i, 0))
    vec = pl.BlockSpec((1, w), lambda i: (0, 0))
    ins = [x, g, dy] + ([add] if has_add else [])
    in_specs = [pl.BlockSpec((tm, w), lambda i: (i, cb)), vec, row] + ([row] if has_add else [])
    return pl.pallas_call(
        body, grid=(m // tm,), in_specs=in_specs, out_specs=(row, vec),
        out_shape=(jax.ShapeDtypeStruct((m, w), out_dtype), jax.ShapeDtypeStruct((1, w), F32)),
        name=name, compiler_params=_cp(1))(*ins)


def _ln_silu_fwd(c, lg, lb, tm):
    m, w = c.shape

    def body(c_ref, g_ref, b_ref, o_ref):
        cv = c_ref[...]
        xc = cv - jnp.mean(cv, axis=-1, keepdims=True)
        r = lax.rsqrt(jnp.mean(xc * xc, axis=-1, keepdims=True) + NORM_EPS)
        yl = xc * r * g_ref[...] + b_ref[...]
        o_ref[...] = (yl * _sig(yl)).astype(o_ref.dtype)

    row = pl.BlockSpec((tm, w), lambda i: (i, 0))
    vec = pl.BlockSpec((1, w), lambda i: (0, 0))
    return pl.pallas_call(body, grid=(m // tm,), in_specs=[row, vec, vec], out_specs=row,
                          out_shape=jax.ShapeDtypeStruct((m, w), BF16), name="ln_silu_fwd",
                          compiler_params=_cp(1))(c, lg, lb)


def _ln_silu_bwd(c, lg, lb, ds, tm):
    m, w = c.shape

    def body(c_ref, g_ref, b_ref, ds_ref, dc_ref, dg_ref, db_ref):
        cv = c_ref[...]
        xc = cv - jnp.mean(cv, axis=-1, keepdims=True)
        r = lax.rsqrt(jnp.mean(xc * xc, axis=-1, keepdims=True) + NORM_EPS)
        nv = xc * r
        yl = nv * g_ref[...] + b_ref[...]
        sg = _sig(yl)
        dyl = ds_ref[...] * (sg * (1.0 + yl * (1.0 - sg)))
        dn = dyl * g_ref[...]
        dc = r * (dn - jnp.mean(dn, axis=-1, keepdims=True) - nv * jnp.mean(dn * nv, axis=-1, keepdims=True))
        dc_ref[...] = dc

        @pl.when(pl.program_id(0) == 0)
        def _():
            dg_ref[...] = jnp.zeros_like(dg_ref)
            db_ref[...] = jnp.zeros_like(db_ref)

        dg_ref[...] += jnp.sum(dyl * nv, axis=0, keepdims=True)
        db_ref[...] += jnp.sum(dyl, axis=0, keepdims=True)

    row = pl.BlockSpec((tm, w), lambda i: (i, 0))
    vec = pl.BlockSpec((1, w), lambda i: (0, 0))
    return pl.pallas_call(
        body, grid=(m // tm,), in_specs=[row, vec, vec, row], out_specs=(row, vec, vec),
        out_shape=(jax.ShapeDtypeStruct((m, w), F32), jax.ShapeDtypeStruct((1, w), F32),
                   jax.ShapeDtypeStruct((1, w), F32)),
        name="ln_silu_bwd", compiler_params=_cp(1))(c, lg, lb, ds)


CONV_MARGIN = 16
CONV_ROWS = 128


def _conv_fwd(proj, dw, bias, *, c_conv, off_a, nb, lp, l_valid):
    kw = dw.shape[0]
    half = kw // 2
    cw = LANE
    mg = CONV_MARGIN
    assert half <= mg - 1 and lp % CONV_ROWS == 0

    def body(a_ref, g_ref, w_ref, b_ref, c_ref, zp):
        t = lax.broadcasted_iota(jnp.int32, (lp, cw), 0)
        z = jnp.where(t < l_valid, a_ref[...] * _sig(g_ref[...]), 0.0)
        zp[0:mg, :] = jnp.zeros((mg, cw), F32)
        zp[mg + lp:mg + lp + mg, :] = jnp.zeros((mg, cw), F32)
        zp[mg:mg + lp, :] = z
        for r0 in range(0, lp, CONV_ROWS):
            acc = jnp.zeros((CONV_ROWS, cw), F32) + b_ref[...]
            for k in range(kw):
                s0 = mg + r0 + k - half
                acc = acc + w_ref[k:k + 1, :] * zp[s0:s0 + CONV_ROWS, :]
            c_ref[r0:r0 + CONV_ROWS, :] = acc

    ja, jg = _cb(off_a, cw), _cb(off_a + c_conv, cw)
    return pl.pallas_call(
        body, grid=(nb, c_conv // cw),
        in_specs=[pl.BlockSpec((lp, cw), lambda b, j: (b, ja + j)), pl.BlockSpec((lp, cw), lambda b, j: (b, jg + j)),
                  pl.BlockSpec((kw, cw), lambda b, j: (0, j)), pl.BlockSpec((1, cw), lambda b, j: (0, j))],
        out_specs=pl.BlockSpec((lp, cw), lambda b, j: (b, j)),
        out_shape=jax.ShapeDtypeStruct((nb * lp, c_conv), F32),
        scratch_shapes=[pltpu.VMEM((lp + 2 * mg, cw), F32)], name="conv_fwd", compiler_params=_cp(2))(proj, proj, dw, bias)


def _conv_bwd(proj, dc, dw, *, c_conv, off_a, nb, lp, l_valid):
    kw = dw.shape[0]
    half = kw // 2
    cw = LANE
    mg = CONV_MARGIN

    def body(a_ref, g_ref, dc_ref, w_ref, da_ref, dg_ref, ddw_ref, dcb_ref, zp, dcp):
        t = lax.broadcasted_iota(jnp.int32, (lp, cw), 0)
        z = jnp.where(t < l_valid, a_ref[...] * _sig(g_ref[...]), 0.0)
        for buf in (zp, dcp):
            buf[0:mg, :] = jnp.zeros((mg, cw), F32)
            buf[mg + lp:mg + lp + mg, :] = jnp.zeros((mg, cw), F32)
        zp[mg:mg + lp, :] = z
        dcv = dc_ref[...]
        dcp[mg:mg + lp, :] = dcv

        @pl.when(pl.program_id(1) == 0)
        def _():
            ddw_ref[...] = jnp.zeros_like(ddw_ref)
            dcb_ref[...] = jnp.zeros_like(dcb_ref)

        dcb_ref[...] += jnp.sum(dcv, axis=0, keepdims=True)
        for r0 in range(0, lp, CONV_ROWS):
            acc = jnp.zeros((CONV_ROWS, cw), F32)
            for k in range(kw):
                s0 = mg + r0 - k + half
                acc = acc + w_ref[k:k + 1, :] * dcp[s0:s0 + CONV_ROWS, :]
            tc = lax.broadcasted_iota(jnp.int32, (CONV_ROWS, cw), 0) + r0
            dz = jnp.where(tc < l_valid, acc, 0.0)
            sg = _sig(g_ref[r0:r0 + CONV_ROWS, :])
            da_ref[r0:r0 + CONV_ROWS, :] = (dz * sg).astype(da_ref.dtype)
            dg_ref[r0:r0 + CONV_ROWS, :] = (dz * a_ref[r0:r0 + CONV_ROWS, :] * sg * (1.0 - sg)).astype(dg_ref.dtype)
        for k in range(kw):
            acc = jnp.zeros((CONV_ROWS, cw), F32)
            for r0 in range(0, lp, CONV_ROWS):
                s0 = mg + r0 + k - half
                acc = acc + dc_ref[r0:r0 + CONV_ROWS, :] * zp[s0:s0 + CONV_ROWS, :]
            ddw_ref[k:k + 1, :] += jnp.sum(acc, axis=0, keepdims=True)

    ja, jg = _cb(off_a, cw), _cb(off_a + c_conv, cw)
    seq = pl.BlockSpec((lp, cw), lambda j, b: (b, j))
    return pl.pallas_call(
        body, grid=(c_conv // cw, nb),
        in_specs=[pl.BlockSpec((lp, cw), lambda j, b: (b, ja + j)), pl.BlockSpec((lp, cw), lambda j, b: (b, jg + j)),
                  seq, pl.BlockSpec((kw, cw), lambda j, b: (0, j))],
        out_specs=(seq, seq, pl.BlockSpec((kw, cw), lambda j, b: (0, j)), pl.BlockSpec((1, cw), lambda j, b: (0, j))),
        out_shape=(jax.ShapeDtypeStruct((nb * lp, c_conv), BF16), jax.ShapeDtypeStruct((nb * lp, c_conv), BF16),
                   jax.ShapeDtypeStruct((kw, c_conv), F32), jax.ShapeDtypeStruct((1, c_conv), F32)),
        scratch_shapes=[pltpu.VMEM((lp + 2 * mg, cw), F32), pltpu.VMEM((lp + 2 * mg, cw), F32)],
        name="conv_bwd", compiler_params=_cp(2))(proj, proj, dc, dw)


def _swap_halves(x, half):
    fwd = pltpu.roll(x, LANE - half, axis=1)
    bwd = pltpu.roll(x, half, axis=1)
    lane = lax.broadcasted_iota(jnp.int32, x.shape, 1)
    return jnp.where((lane & (2 * half - 1)) < half, fwd, bwd)


def _rope(x, cos, sin, half):
    return x * cos + _swap_halves(x, half) * sin


def _rope_t(dy, cos, sin, half):
    return dy * cos + _swap_halves(dy * sin, half)


def _rope_tables(lp, l_valid, half):
    t = jnp.arange(lp)
    n = jnp.clip(t - N_META, 0, None)
    real = (t >= N_META) & (t < l_valid)
    row = jnp.where(real, n // GRID_W, 0).astype(F32)
    col = jnp.where(real, n % GRID_W, 0).astype(F32)
    inv = ROPE_THETA ** (-jnp.arange(half, dtype=F32) / half)
    ar, ac = row[:, None] * inv[None, :], col[:, None] * inv[None, :]
    cos = jnp.concatenate([jnp.cos(ar), jnp.cos(ar), jnp.cos(ac), jnp.cos(ac)], axis=1)
    sin = jnp.concatenate([-jnp.sin(ar), jnp.sin(ar), -jnp.sin(ac), jnp.sin(ac)], axis=1)
    padw = LANE - 4 * half
    if padw:
        cos = jnp.pad(cos, ((0, 0), (0, padw)))
        sin = jnp.pad(sin, ((0, 0), (0, padw)))
    return cos.astype(F32), sin.astype(F32)


def _gqa_prep_fwd(proj, qg, kg, cos, sin, *, off_q, hq, hkv, nb, lp, tm):
    hd = HEAD_DIM
    half = hd // 4
    wq, wkv = hq * hd, 2 * hkv * hd
    nt = lp // tm

    def body(q_ref, kv_ref, qg_ref, kg_ref, cos_ref, sin_ref, qo_ref, ko_ref, vo_ref):
        cosv, sinv = cos_ref[...], sin_ref[...]

        def norm_rope(xh, g):
            r = lax.rsqrt(jnp.mean(xh * xh, axis=-1, keepdims=True) + NORM_EPS)
            return _rope(xh * r * g, cosv, sinv, half)

        for h in range(hq):
            qo_ref[h] = norm_rope(q_ref[:, h * hd:(h + 1) * hd], qg_ref[...]).astype(BF16)
        for h in range(hkv):
            ko_ref[h] = norm_rope(kv_ref[:, h * hd:(h + 1) * hd], kg_ref[...]).astype(BF16)
            vo_ref[h] = kv_ref[:, (hkv + h) * hd:(hkv + h + 1) * hd].astype(BF16)

    jq, jkv = _cb(off_q, wq), _cb(off_q + wq, wkv)
    tab = pl.BlockSpec((tm, LANE), lambda b, i: (i, 0))
    vec = pl.BlockSpec((1, hd), lambda b, i: (0, 0))

    def heads(h):
        return pl.BlockSpec((None, h, tm, hd), lambda b, i: (b, 0, i, 0))

    return pl.pallas_call(
        body, grid=(nb, nt),
        in_specs=[pl.BlockSpec((tm, wq), lambda b, i: (b * nt + i, jq)),
                  pl.BlockSpec((tm, wkv), lambda b, i: (b * nt + i, jkv)), vec, vec, tab, tab],
        out_specs=(heads(hq), heads(hkv), heads(hkv)),
        out_shape=(jax.ShapeDtypeStruct((nb, hq, lp, hd), BF16), jax.ShapeDtypeStruct((nb, hkv, lp, hd), BF16),
                   jax.ShapeDtypeStruct((nb, hkv, lp, hd), BF16)),
        name="gqa_prep_fwd", compiler_params=_cp(2))(proj, proj, qg, kg, cos, sin)


def _gqa_prep_bwd(proj, qg, kg, cos, sin, dqh, dkh, dvh, *, off_q, hq, hkv, nb, lp, tm):
    hd = HEAD_DIM
    half = hd // 4
    wq, wkv = hq * hd, 2 * hkv * hd
    nt = lp // tm

    def body(q_ref, kv_ref, qg_ref, kg_ref, cos_ref, sin_ref, dqh_ref, dkh_ref, dvh_ref,
             dq_ref, dkv_ref, dqg_ref, dkg_ref):
        cosv, sinv = cos_ref[...], sin_ref[...]

        def back(xh, g, dyh):
            dn = _rope_t(dyh, cosv, sinv, half)
            r = lax.rsqrt(jnp.mean(xh * xh, axis=-1, keepdims=True) + NORM_EPS)
            t = dn * g
            dx = r * t - xh * (r * r * r) * jnp.mean(t * xh, axis=-1, keepdims=True)
            return dx, jnp.sum(dn * xh * r, axis=0, keepdims=True)

        first = (pl.program_id(0) == 0) & (pl.program_id(1) == 0)

        @pl.when(first)
        def _():
            dqg_ref[...] = jnp.zeros_like(dqg_ref)
            dkg_ref[...] = jnp.zeros_like(dkg_ref)

        gq = jnp.zeros((1, hd), F32)
        for h in range(hq):
            dx, dg = back(q_ref[:, h * hd:(h + 1) * hd], qg_ref[...], dqh_ref[h])
            dq_ref[:, h * hd:(h + 1) * hd] = dx.astype(dq_ref.dtype)
            gq = gq + dg
        dqg_ref[...] += gq
        gk = jnp.zeros((1, hd), F32)
        for h in range(hkv):
            dx, dg = back(kv_ref[:, h * hd:(h + 1) * hd], kg_ref[...], dkh_ref[h])
            dkv_ref[:, h * hd:(h + 1) * hd] = dx.astype(dkv_ref.dtype)
            dkv_ref[:, (hkv + h) * hd:(hkv + h + 1) * hd] = dvh_ref[h].astype(dkv_ref.dtype)
            gk = gk + dg
        dkg_ref[...] += gk

    jq, jkv = _cb(off_q, wq), _cb(off_q + wq, wkv)
    tab = pl.BlockSpec((tm, LANE), lambda b, i: (i, 0))
    vec = pl.BlockSpec((1, hd), lambda b, i: (0, 0))

    def heads(h):
        return pl.BlockSpec((None, h, tm, hd), lambda b, i: (b, 0, i, 0))

    return pl.pallas_call(
        body, grid=(nb, nt),
        in_specs=[pl.BlockSpec((tm, wq), lambda b, i: (b * nt + i, jq)),
                  pl.BlockSpec((tm, wkv), lambda b, i: (b * nt + i, jkv)), vec, vec, tab, tab,
                  heads(hq), heads(hkv), heads(hkv)],
        out_specs=(pl.BlockSpec((tm, wq), lambda b, i: (b * nt + i, 0)),
                   pl.BlockSpec((tm, wkv), lambda b, i: (b * nt + i, 0)), vec, vec),
        out_shape=(jax.ShapeDtypeStruct((nb * lp, wq), BF16), jax.ShapeDtypeStruct((nb * lp, wkv), BF16),
                   jax.ShapeDtypeStruct((1, hd), F32), jax.ShapeDtypeStruct((1, hd), F32)),
        name="gqa_prep_bwd", compiler_params=_cp(2))(proj, proj, qg, kg, cos, sin, dqh, dkh, dvh)


def _mla_prep_fwd(qm, kvm, proj, cos, sin, *, off_kpe, nh, nb, lp, tm):
    hd = HEAD_DIM
    half = MLA_ROPE_DIM // 4
    nt = lp // tm
    wh = nh * hd

    def body(qm_ref, kvm_ref, kpe_ref, cos_ref, sin_ref, qo_ref, ko_ref, vo_ref):
        cosv, sinv = cos_ref[...], sin_ref[...]
        kpe = _rope(kpe_ref[...], cosv, sinv, half).astype(BF16)
        for h in range(nh):
            qo_ref[h, :, 0:hd] = qm_ref[:, h * hd:(h + 1) * hd].astype(BF16)
            qo_ref[h, :, hd:2 * hd] = _rope(qm_ref[:, wh + h * hd:wh + (h + 1) * hd], cosv, sinv, half).astype(BF16)
            ko_ref[h, :, 0:hd] = kvm_ref[:, h * hd:(h + 1) * hd].astype(BF16)
            ko_ref[h, :, hd:2 * hd] = kpe
            vo_ref[h] = kvm_ref[:, wh + h * hd:wh + (h + 1) * hd].astype(BF16)

    tab = pl.BlockSpec((tm, LANE), lambda b, i: (i, 0))
    wide = pl.BlockSpec((tm, 2 * wh), lambda b, i: (b * nt + i, 0))
    jk = _cb(off_kpe, LANE)
    return pl.pallas_call(
        body, grid=(nb, nt),
        in_specs=[wide, wide, pl.BlockSpec((tm, LANE), lambda b, i: (b * nt + i, jk)), tab, tab],
        out_specs=(pl.BlockSpec((None, nh, tm, 2 * hd), lambda b, i: (b, 0, i, 0)),
                   pl.BlockSpec((None, nh, tm, 2 * hd), lambda b, i: (b, 0, i, 0)),
                   pl.BlockSpec((None, nh, tm, hd), lambda b, i: (b, 0, i, 0))),
        out_shape=(jax.ShapeDtypeStruct((nb, nh, lp, 2 * hd), BF16), jax.ShapeDtypeStruct((nb, nh, lp, 2 * hd), BF16),
                   jax.ShapeDtypeStruct((nb, nh, lp, hd), BF16)),
        name="mla_prep_fwd", compiler_params=_cp(2))(qm, kvm, proj, cos, sin)


def _mla_prep_bwd(dqc, dkc, dvc, cos, sin, *, nh, nb, lp, tm):
    hd = HEAD_DIM
    half = MLA_ROPE_DIM // 4
    nt = lp // tm
    wh = nh * hd

    def body(dq_ref, dk_ref, dv_ref, cos_ref, sin_ref, dqm_ref, dkvm_ref, dkpe_ref):
        cosv, sinv = cos_ref[...], sin_ref[...]
        acc = jnp.zeros((tm, hd), F32)
        for h in range(nh):
            dqm_ref[:, h * hd:(h + 1) * hd] = dq_ref[h, :, 0:hd].astype(BF16)
            dqm_ref[:, wh + h * hd:wh + (h + 1) * hd] = _rope_t(dq_ref[h, :, hd:2 * hd], cosv, sinv, half).astype(BF16)
            dkvm_ref[:, h * hd:(h + 1) * hd] = dk_ref[h, :, 0:hd].astype(BF16)
            dkvm_ref[:, wh + h * hd:wh + (h + 1) * hd] = dv_ref[h].astype(BF16)
            acc = acc + dk_ref[h, :, hd:2 * hd]
        dkpe_ref[...] = _rope_t(acc, cosv, sinv, half).astype(BF16)

    tab = pl.BlockSpec((tm, LANE), lambda b, i: (i, 0))
    wide = pl.BlockSpec((tm, 2 * wh), lambda b, i: (b * nt + i, 0))
    return pl.pallas_call(
        body, grid=(nb, nt),
        in_specs=[pl.BlockSpec((None, nh, tm, 2 * hd), lambda b, i: (b, 0, i, 0)),
                  pl.BlockSpec((None, nh, tm, 2 * hd), lambda b, i: (b, 0, i, 0)),
                  pl.BlockSpec((None, nh, tm, hd), lambda b, i: (b, 0, i, 0)), tab, tab],
        out_specs=(wide, wide, pl.BlockSpec((tm, LANE), lambda b, i: (b * nt + i, 0))),
        out_shape=(jax.ShapeDtypeStruct((nb * lp, 2 * wh), BF16), jax.ShapeDtypeStruct((nb * lp, 2 * wh), BF16),
                   jax.ShapeDtypeStruct((nb * lp, LANE), BF16)),
        name="mla_prep_bwd", compiler_params=_cp(2))(dqc, dkc, dvc, cos, sin)


def _attn_fwd(q, k, v, *, scale, l_valid, name):
    nb, hq, lp, dk = q.shape
    hkv, dv = k.shape[1], v.shape[3]
    grp = hq // hkv
    tq = _tile(lp, 544, 16)
    nq = lp // tq

    def body(q_ref, k_ref, v_ref, o_ref):
        s = lax.dot_general(q_ref[...], k_ref[...], NT, preferred_element_type=F32) * scale
        col = lax.broadcasted_iota(jnp.int32, s.shape, 1)
        s = jnp.where(col < l_valid, s, -1e30)
        p = jnp.exp(s - jnp.max(s, axis=-1, keepdims=True))
        den = jnp.sum(p, axis=-1, keepdims=True)
        o = lax.dot_general(p.astype(BF16), v_ref[...], NN, preferred_element_type=F32)
        o_ref[...] = (o / den).astype(o_ref.dtype)

    return pl.pallas_call(
        body, grid=(nb, hq, nq),
        in_specs=[pl.BlockSpec((None, None, tq, dk), lambda b, h, i: (b, h, i, 0)),
                  pl.BlockSpec((None, None, lp, dk), lambda b, h, i: (b, h // grp, 0, 0)),
                  pl.BlockSpec((None, None, lp, dv), lambda b, h, i: (b, h // grp, 0, 0))],
        out_specs=pl.BlockSpec((tq, dv), lambda b, h, i: (b * nq + i, h)),
        out_shape=jax.ShapeDtypeStruct((nb * lp, hq * dv), BF16), name=name, compiler_params=_cp(3))(q, k, v)


def _attn_bwd(q, k, v, do, *, scale, l_valid, name):
    nb, hq, lp, dk = q.shape
    hkv, dv = k.shape[1], v.shape[3]
    grp = hq // hkv
    tq = _tile(lp, 272, 16)
    nq = lp // tq

    def body(q_ref, k_ref, v_ref, do_ref, dq_ref, dk_ref, dv_ref):
        qv, kv, vv, dov = q_ref[...], k_ref[...], v_ref[...], do_ref[...]
        s = lax.dot_general(qv, kv, NT, preferred_element_type=F32) * scale
        col = lax.broadcasted_iota(jnp.int32, s.shape, 1)
        s = jnp.where(col < l_valid, s, -1e30)
        p = jnp.exp(s - jnp.max(s, axis=-1, keepdims=True))
        p = p / jnp.sum(p, axis=-1, keepdims=True)
        dp = lax.dot_general(dov, vv, NT, preferred_element_type=F32)
        ds = (p * (dp - jnp.sum(p * dp, axis=-1, keepdims=True)) * scale).astype(BF16)
        dq_ref[...] = lax.dot_general(ds, kv, NN, preferred_element_type=F32)

        @pl.when((pl.program_id(2) == 0) & (pl.program_id(3) == 0))
        def _():
            dk_ref[...] = jnp.zeros_like(dk_ref)
            dv_ref[...] = jnp.zeros_like(dv_ref)

        dk_ref[...] += lax.dot_general(ds, qv, TN, preferred_element_type=F32)
        dv_ref[...] += lax.dot_general(p.astype(BF16), dov, TN, preferred_element_type=F32)

    return pl.pallas_call(
        body, grid=(nb, hkv, grp, nq),
        in_specs=[pl.BlockSpec((None, None, tq, dk), lambda b, h, g, i: (b, h * grp + g, i, 0)),
                  pl.BlockSpec((None, None, lp, dk), lambda b, h, g, i: (b, h, 0, 0)),
                  pl.BlockSpec((None, None, lp, dv), lambda b, h, g, i: (b, h, 0, 0)),
                  pl.BlockSpec((tq, dv), lambda b, h, g, i: (b * nq + i, h * grp + g))],
        out_specs=(pl.BlockSpec((None, None, tq, dk), lambda b, h, g, i: (b, h * grp + g, i, 0)),
                   pl.BlockSpec((None, None, lp, dk), lambda b, h, g, i: (b, h, 0, 0)),
                   pl.BlockSpec((None, None, lp, dv), lambda b, h, g, i: (b, h, 0, 0))),
        out_shape=(jax.ShapeDtypeStruct((nb, hq, lp, dk), F32), jax.ShapeDtypeStruct((nb, hkv, lp, dk), F32),
                   jax.ShapeDtypeStruct((nb, hkv, lp, dv), F32)),
        name=name, compiler_params=_cp(4))(q, k, v, do)


def _gate_fwd(proj, gb, ya, yb, yc, *, d, tm):
    m = ya.shape[0]
    cw = _tile(d, 512, LANE)
    nj = d // cw

    def body(g0, g1, g2, b0, b1, b2, ya_ref, yb_ref, yc_ref, o_ref):
        o = (_sig(g0[...] + b0[...]) * ya_ref[...] + _sig(g1[...] + b1[...]) * yb_ref[...]
             + _sig(g2[...] + b2[...]) * yc_ref[...])
        o_ref[...] = o.astype(o_ref.dtype)

    def gl(br):
        return pl.BlockSpec((tm, cw), lambda j, i: (i, br * nj + j))

    def gbs(br):
        return pl.BlockSpec((1, cw), lambda j, i: (0, br * nj + j))

    blk = pl.BlockSpec((tm, cw), lambda j, i: (i, j))
    return pl.pallas_call(
        body, grid=(nj, m // tm), in_specs=[gl(0), gl(1), gl(2), gbs(0), gbs(1), gbs(2), blk, blk, blk],
        out_specs=blk, out_shape=jax.ShapeDtypeStruct((m, d), BF16), name="gate_fwd",
        compiler_params=_cp(2))(proj, proj, proj, gb, gb, gb, ya, yb, yc)


def _gate_bwd(dm, proj, gb, ya, yb, yc, *, d, tm):
    m = ya.shape[0]
    cw = _tile(d, 512, LANE)
    nj = d // cw

    def body(dm_ref, g0, g1, g2, b0, b1, b2, ya_ref, yb_ref, yc_ref,
             dya, dyb, dyc, dg0, dg1, dg2, db0, db1, db2):
        dmv = dm_ref[...]

        @pl.when(pl.program_id(1) == 0)
        def _():
            for r in (db0, db1, db2):
                r[...] = jnp.zeros_like(r)

        for g, b, y, dy, dgl, db in ((g0, b0, ya_ref, dya, dg0, db0), (g1, b1, yb_ref, dyb, dg1, db1),
                                     (g2, b2, yc_ref, dyc, dg2, db2)):
            sg = _sig(g[...] + b[...])
            dy[...] = (sg * dmv).astype(dy.dtype)
            dl = dmv * y[...] * sg * (1.0 - sg)
            dgl[...] = dl.astype(dgl.dtype)
            db[...] += jnp.sum(dl, axis=0, keepdims=True)

    def gl(br):
        return pl.BlockSpec((tm, cw), lambda j, i: (i, br * nj + j))

    def gbs(br):
        return pl.BlockSpec((1, cw), lambda j, i: (0, br * nj + j))

    blk = pl.BlockSpec((tm, cw), lambda j, i: (i, j))
    vec = pl.BlockSpec((1, cw), lambda j, i: (0, j))
    act = jax.ShapeDtypeStruct((m, d), BF16)
    vsh = jax.ShapeDtypeStruct((1, d), F32)
    return pl.pallas_call(
        body, grid=(nj, m // tm), in_specs=[blk, gl(0), gl(1), gl(2), gbs(0), gbs(1), gbs(2), blk, blk, blk],
        out_specs=(blk,) * 6 + (vec,) * 3, out_shape=(act,) * 6 + (vsh,) * 3, name="gate_bwd",
        compiler_params=_cp(2))(dm, proj, proj, proj, gb, gb, gb, ya, yb, yc)


def _swiglu_fwd(gt, up, tm):
    m, f = gt.shape
    cw = _tile(f, 512, LANE)

    def body(g_ref, u_ref, o_ref):
        g = g_ref[...]
        o_ref[...] = (g * _sig(g) * u_ref[...]).astype(o_ref.dtype)

    blk = pl.BlockSpec((tm, cw), lambda j, i: (i, j))
    return pl.pallas_call(body, grid=(f // cw, m // tm), in_specs=[blk, blk], out_specs=blk,
                          out_shape=jax.ShapeDtypeStruct((m, f), BF16), name="swiglu_fwd",
                          compiler_params=_cp(2))(gt, up)


def _swiglu_bwd(dact, gt, up, tm):
    m, f = gt.shape
    cw = _tile(f, 512, LANE)

    def body(d_ref, g_ref, u_ref, dg_ref, du_ref):
        g, d = g_ref[...], d_ref[...]
        sg = _sig(g)
        dg_ref[...] = (d * u_ref[...] * sg * (1.0 + g * (1.0 - sg))).astype(dg_ref.dtype)
        du_ref[...] = (d * g * sg).astype(du_ref.dtype)

    blk = pl.BlockSpec((tm, cw), lambda j, i: (i, j))
    sh = jax.ShapeDtypeStruct((m, f), BF16)
    return pl.pallas_call(body, grid=(f // cw, m // tm), in_specs=[blk, blk, blk], out_specs=(blk, blk),
                          out_shape=(sh, sh), name="swiglu_bwd", compiler_params=_cp(2))(dact, gt, up)


def _loss_head(h, gf, tgt, *, lp, l_valid, tm):
    m, d = h.shape
    nt = lp // tm

    def body(h_ref, g_ref, t_ref, dh_ref, dg_ref, loss_ref):
        i = pl.program_id(0)
        xv = h_ref[...]
        r = lax.rsqrt(jnp.mean(xv * xv, axis=-1, keepdims=True) + NORM_EPS)
        y = xv * r * g_ref[...]
        t = lax.broadcasted_iota(jnp.int32, (tm, d), 0) + (i % nt) * tm
        err = jnp.where((t >= N_META) & (t < l_valid), y - t_ref[...], 0.0)
        dy = err * (1.0 / d)
        tg = dy * g_ref[...]
        dh_ref[...] = r * tg - xv * (r * r * r) * jnp.mean(tg * xv, axis=-1, keepdims=True)

        @pl.when(i == 0)
        def _():
            dg_ref[...] = jnp.zeros_like(dg_ref)
            loss_ref[...] = jnp.zeros_like(loss_ref)

        dg_ref[...] += jnp.sum(dy * xv * r, axis=0, keepdims=True)
        sq = jnp.sum(jnp.sum(err * err, axis=-1, keepdims=True), axis=0, keepdims=True)
        loss_ref[...] += jnp.zeros((1, LANE), F32) + sq * (0.5 / d)

    row = pl.BlockSpec((tm, d), lambda i: (i, 0))
    vec = pl.BlockSpec((1, d), lambda i: (0, 0))
    return pl.pallas_call(
        body, grid=(m // tm,), in_specs=[row, vec, row],
        out_specs=(row, vec, pl.BlockSpec((1, LANE), lambda i: (0, 0))),
        out_shape=(jax.ShapeDtypeStruct((m, d), F32), jax.ShapeDtypeStruct((1, d), F32),
                   jax.ShapeDtypeStruct((1, LANE), F32)),
        name="loss_head", compiler_params=_cp(1))(h, gf, tgt)


def _adamw(w, g, m, v, name):
    rows, cols = w.shape
    tr = rows
    if rows % 8 == 0:
        tr = _tile(rows, max(8, (1 << 18) // cols // 8 * 8), 8)
    c1 = 1.0 / (1.0 - ADAM_B1 ** ADAM_STEP)
    c2 = 1.0 / (1.0 - ADAM_B2 ** ADAM_STEP)

    def body(w_ref, g_ref, m_ref, v_ref, d_ref, mo_ref, vo_ref):
        gv = g_ref[...]
        mn = ADAM_B1 * m_ref[...] + (1.0 - ADAM_B1) * gv
        vn = ADAM_B2 * v_ref[...] + (1.0 - ADAM_B2) * (gv * gv)
        mo_ref[...] = mn
        vo_ref[...] = vn
        d_ref[...] = -ADAM_LR * ((mn * c1) / (jnp.sqrt(vn * c2) + ADAM_EPS) + ADAM_WD * w_ref[...])

    blk = pl.BlockSpec((tr, cols), lambda i: (i, 0))
    sh = jax.ShapeDtypeStruct((rows, cols), F32)
    return pl.pallas_call(body, grid=(rows // tr,), in_specs=[blk] * 4, out_specs=(blk,) * 3, out_shape=(sh,) * 3,
                          name=name, compiler_params=_cp(1))(w, g, m, v)


MESH = pl.DeviceIdType.MESH
ANY = pl.BlockSpec(memory_space=pl.ANY)


def _place():
    return lax.axis_index("x"), lax.axis_index("y"), lax.axis_index("c")


def _other_chips(x, y):
    return [(1 - x, y), (x, 1 - y), (1 - x, 1 - y)]


def _all_gather(xs, vmem, name):
    na = len(xs)

    def body(*refs):
        x_refs, out_refs = refs[:na], refs[na:2 * na]
        send_sems, recv_sems, local_sems = refs[2 * na:]
        x, y, c = _place()
        me, sibling = (x, y, c), (x, y, 1 - c)
        chips = _other_chips(x, y)

        def blk(p, px, py, pc):
            return out_refs[p].at[4 * px + 2 * py + pc]

        def copy(k, p, block, to, src=None):
            return pltpu.make_async_remote_copy(
                src_ref=blk(p, *block) if src is None else src, dst_ref=blk(p, *block),
                send_sem=send_sems.at[k * na + p], recv_sem=recv_sems.at[k * na + p], device_id=to,
                device_id_type=MESH)

        mine = [pltpu.make_async_copy(x_refs[p], blk(p, *me), local_sems.at[p]) for p in range(na)]
        for cp in mine:
            cp.start()
        first = [copy(0, p, me, sibling, src=x_refs[p]) for p in range(na)]
        first += [copy(1 + j, p, me, (*chip, c), src=x_refs[p]) for j, chip in enumerate(chips) for p in range(na)]
        for cp in first:
            cp.start()
        passed = []
        for j, chip in enumerate(chips):
            for p in range(na):
                copy(1 + j, p, (*chip, c), me).wait_recv()
                fwd = copy(4 + j, p, (*chip, c), sibling)
                fwd.start()
                passed.append(fwd)
        for p in range(na):
            copy(0, p, sibling, me).wait_recv()
        for j, chip in enumerate(chips):
            for p in range(na):
                copy(4 + j, p, (*chip, 1 - c), me).wait_recv()
        for cp in first + passed:
            cp.wait_send()
        for cp in mine:
            cp.wait()

    spec = pl.BlockSpec(memory_space=pltpu.VMEM) if vmem else ANY
    return pl.pallas_call(
        body, out_shape=[jax.ShapeDtypeStruct((N_DEV,) + a.shape, a.dtype) for a in xs],
        in_specs=[spec] * na, out_specs=[spec] * na,
        scratch_shapes=[pltpu.SemaphoreType.DMA((7 * na,)), pltpu.SemaphoreType.DMA((7 * na,)),
                        pltpu.SemaphoreType.DMA((na,))],
        name=name)(*xs)


def _swap_with_sibling(g4s):
    na = len(g4s)

    def body(*refs):
        g_refs, out_refs = refs[:na], refs[na:2 * na]
        send_sems, recv_sems = refs[2 * na:]
        x, y, c = _place()
        cps = [pltpu.make_async_remote_copy(src_ref=g_refs[p].at[:, 1 - c], dst_ref=out_refs[p],
                                            send_sem=send_sems.at[p], recv_sem=recv_sems.at[p],
                                            device_id=(x, y, 1 - c), device_id_type=MESH) for p in range(na)]
        for cp in cps:
            cp.start()
        for cp in cps:
            cp.wait()

    return pl.pallas_call(
        body, out_shape=[jax.ShapeDtypeStruct((g.shape[0],) + g.shape[2:], g.dtype) for g in g4s],
        in_specs=[ANY] * na, out_specs=[ANY] * na,
        scratch_shapes=[pltpu.SemaphoreType.DMA((na,)), pltpu.SemaphoreType.DMA((na,))], name="rs_sibling")(*g4s)


HBM = pl.BlockSpec(memory_space=pltpu.HBM)
SEM = pl.BlockSpec(memory_space=pltpu.SEMAPHORE)
EFFECT = pltpu.SideEffectType.DATAFLOW_SIDE_EFFECTING


def _in_hbm(a):
    return pltpu.with_memory_space_constraint(a, pltpu.HBM)


def _ici_copies(src_refs, land_refs, send_sems, recv_sems, gather):
    na = len(src_refs)
    x, y, c = _place()
    cps = []
    for j, (px, py) in enumerate(_other_chips(x, y)):
        for p in range(na):
            src = src_refs[p] if gather else src_refs[p].at[2 * px + py]
            dst = land_refs[p].at[4 * x + 2 * y + c] if gather else land_refs[p].at[j]
            cps.append(pltpu.make_async_remote_copy(
                src_ref=src, dst_ref=dst, send_sem=send_sems.at[j * na + p], recv_sem=recv_sems.at[j * na + p],
                device_id=(px, py, c), device_id_type=MESH))
    return cps


def _ici_start(srcs, land_shapes, deps, *, gather, name):
    na, nd = len(srcs), len(deps)
    lands = [lax.empty(s, a.dtype) for s, a in zip(land_shapes, srcs)]

    def body(*refs):
        src_refs, land_refs = refs[:na], refs[na:2 * na]
        send_sems, recv_sems = refs[2 * na + nd], refs[2 * na + nd + 1]
        token = refs[-1]
        for cp in _ici_copies(src_refs, land_refs, send_sems, recv_sems, gather):
            cp.start()
        token[...] = jnp.zeros_like(token)

    out = pl.pallas_call(
        body, name=name,
        out_shape=(pltpu.SemaphoreType.DMA((3 * na,)), pltpu.SemaphoreType.DMA((3 * na,)),
                   *[pltpu.HBM(a.shape, a.dtype) for a in srcs], *[pltpu.HBM(a.shape, a.dtype) for a in lands],
                   jax.ShapeDtypeStruct((8, LANE), F32)),
        in_specs=[HBM] * (2 * na) + [ANY] * nd,
        out_specs=(SEM, SEM, *([HBM] * (2 * na)), pl.BlockSpec(memory_space=pltpu.VMEM)),
        input_output_aliases={p: 2 + p for p in range(2 * na)},
        compiler_params=pltpu.CompilerParams(has_side_effects=EFFECT),
    )(*[_in_hbm(a) for a in srcs], *[_in_hbm(a) for a in lands], *deps)
    return out[0], out[1], list(out[2:2 + na]), list(out[2 + na:2 + 2 * na]), out[-1]


def _ici_wait(send_sems, recv_sems, srcs, lands, after, *, gather, name):
    na, nd = len(srcs), len(after)

    def body(*refs):
        src_refs, land_refs = refs[:na], refs[na:2 * na]
        s_sems, r_sems = refs[2 * na], refs[2 * na + 1]
        for cp in _ici_copies(src_refs, land_refs, s_sems, r_sems, gather):
            cp.wait_send()
            cp.wait_recv()

    out = pl.pallas_call(
        body, name=name,
        out_shape=(*[pltpu.HBM(a.shape, a.dtype) for a in srcs], *[pltpu.HBM(a.shape, a.dtype) for a in lands]),
        in_specs=[HBM] * (2 * na) + [SEM, SEM] + [ANY] * nd, out_specs=tuple([HBM] * (2 * na)),
        input_output_aliases={p: p for p in range(2 * na)},
        compiler_params=pltpu.CompilerParams(has_side_effects=EFFECT),
    )(*srcs, *lands, send_sems, recv_sems, *after)
    return list(out[:na]), list(out[na:])


def _gather_to_sibling(xs, lands):
    na = len(xs)

    def body(*refs):
        x_refs, out_refs = refs[:na], refs[2 * na:3 * na]
        send_sems, recv_sems, local_sems = refs[3 * na:]
        x, y, c = _place()
        sibling = (x, y, 1 - c)
        blocks = [(x, y)] + _other_chips(x, y)
        mine = [pltpu.make_async_copy(x_refs[p], out_refs[p].at[4 * x + 2 * y + c], local_sems.at[p])
                for p in range(na)]
        for cp in mine:
            cp.start()
        cps = []
        for k, (px, py) in enumerate(blocks):
            for p in range(na):
                dst = out_refs[p].at[4 * px + 2 * py + c]
                cps.append(pltpu.make_async_remote_copy(
                    src_ref=x_refs[p] if k == 0 else dst, dst_ref=dst, send_sem=send_sems.at[k * na + p],
                    recv_sem=recv_sems.at[k * na + p], device_id=sibling, device_id_type=MESH))
        for cp in cps:
            cp.start()
        for k, (px, py) in enumerate(blocks):
            for p in range(na):
                arrives = out_refs[p].at[4 * px + 2 * py + (1 - c)]
                pltpu.make_async_remote_copy(src_ref=arrives, dst_ref=arrives, send_sem=send_sems.at[k * na + p],
                                             recv_sem=recv_sems.at[k * na + p], device_id=sibling,
                                             device_id_type=MESH).wait_recv()
        for cp in cps:
            cp.wait_send()
        for cp in mine:
            cp.wait()

    return pl.pallas_call(
        body, out_shape=[jax.ShapeDtypeStruct(a.shape, a.dtype) for a in lands],
        in_specs=[ANY] * (2 * na), out_specs=[ANY] * na, input_output_aliases={na + p: p for p in range(na)},
        scratch_shapes=[pltpu.SemaphoreType.DMA((4 * na,)), pltpu.SemaphoreType.DMA((4 * na,)),
                        pltpu.SemaphoreType.DMA((na,))],
        name="ag_sibling")(*xs, *lands)


HI_MASK = -65536


def _pack_pairs_xla(a):
    half = a.shape[-1] // 2
    bits = lax.bitcast_convert_type(a.astype(BF16), jnp.uint16).astype(jnp.uint32)
    return lax.bitcast_convert_type((bits[..., half:] << 16) | bits[..., :half], F32)


def _split_pairs(words):
    wv = lax.bitcast_convert_type(words, jnp.int32)
    return lax.bitcast_convert_type(wv << 16, F32), lax.bitcast_convert_type(wv & HI_MASK, F32)


def _join_pairs(lo, hi):
    lo_b = lax.bitcast_convert_type(lo.astype(BF16).astype(F32), jnp.int32)
    hi_b = lax.bitcast_convert_type(hi.astype(BF16).astype(F32), jnp.int32)
    return lax.bitcast_convert_type((hi_b & HI_MASK) | lax.shift_right_logical(lo_b, 16), F32)


def _unpack(xp, *, name, out_rows=None, tr=None, src_fn=None, zero_fn=None):
    r, hw = xp.shape
    out_rows = out_rows or r
    tr = tr or _tile(out_rows, 512, 16)
    src = src_fn or (lambda i: i)

    def body(x_ref, o_ref):
        lo, hi = _split_pairs(x_ref[...])
        if zero_fn is not None:
            z = zero_fn(pl.program_id(0))
            lo, hi = jnp.where(z, 0.0, lo), jnp.where(z, 0.0, hi)
        o_ref[:, :hw] = lo.astype(BF16)
        o_ref[:, hw:] = hi.astype(BF16)

    return pl.pallas_call(
        body, grid=(out_rows // tr,), in_specs=[pl.BlockSpec((tr, hw), lambda i: (src(i), 0))],
        out_specs=pl.BlockSpec((tr, 2 * hw), lambda i: (i, 0)),
        out_shape=jax.ShapeDtypeStruct((out_rows, 2 * hw), BF16), name=name, compiler_params=_cp(1))(xp)


def _gather_rows(xp, *, name, out_rows, tr, src_fn):
    _, hw = xp.shape

    def body(x_ref, o_ref):
        o_ref[...] = x_ref[...]

    return pl.pallas_call(
        body, grid=(out_rows // tr,), in_specs=[pl.BlockSpec((tr, hw), lambda i: (src_fn(i), 0))],
        out_specs=pl.BlockSpec((tr, hw), lambda i: (i, 0)),
        out_shape=jax.ShapeDtypeStruct((out_rows, hw), xp.dtype), name=name, compiler_params=_cp(1))(xp)


def _matmul_tn_packed(a, b, name):
    (t, m), (t2, c) = a.shape, b.shape
    assert t == t2
    hw = c // 2
    tm = _tile(m, 1088, LANE)
    tn = _tile(hw, 512, LANE)
    tk = _tile(t, 2176, LANE)
    nk, nj = t // tk, hw // tn

    def body(a_ref, bl_ref, bh_ref, o_ref, acc_lo, acc_hi):
        kk = pl.program_id(2)

        @pl.when(kk == 0)
        def _():
            acc_lo[...] = jnp.zeros_like(acc_lo)
            acc_hi[...] = jnp.zeros_like(acc_hi)

        av = a_ref[...].astype(BF16)
        acc_lo[...] += lax.dot_general(av, bl_ref[...].astype(BF16), TN, preferred_element_type=F32)
        acc_hi[...] += lax.dot_general(av, bh_ref[...].astype(BF16), TN, preferred_element_type=F32)

        @pl.when(kk == nk - 1)
        def _():
            o_ref[...] = _join_pairs(acc_lo[...], acc_hi[...])

    return pl.pallas_call(
        body, grid=(m // tm, nj, nk),
        in_specs=[pl.BlockSpec((tk, tm), lambda i, j, kk: (kk, i)), pl.BlockSpec((tk, tn), lambda i, j, kk: (kk, j)),
                  pl.BlockSpec((tk, tn), lambda i, j, kk: (kk, nj + j))],
        out_specs=pl.BlockSpec((tm, tn), lambda i, j, kk: (i, j)), out_shape=jax.ShapeDtypeStruct((m, hw), F32),
        scratch_shapes=[pltpu.VMEM((tm, tn), F32), pltpu.VMEM((tm, tn), F32)], name=name,
        compiler_params=_cp(3))(a, b, b)


def _add_sibling(g4, recv, cidx, name):
    nchip, _, r, n = g4.shape
    tr = _tile(r, 256, 8)

    def body(c_ref, a_ref, b_ref, o_ref):
        alo, ahi = _split_pairs(a_ref[...])
        blo, bhi = _split_pairs(b_ref[...])
        o_ref[...] = _join_pairs(alo + blo, ahi + bhi)

    grid_spec = pltpu.PrefetchScalarGridSpec(
        num_scalar_prefetch=1, grid=(nchip, r // tr),
        in_specs=[pl.BlockSpec((None, None, tr, n), lambda k, i, c: (k, c[0], i, 0)),
                  pl.BlockSpec((None, tr, n), lambda k, i, c: (k, i, 0))],
        out_specs=pl.BlockSpec((None, tr, n), lambda k, i, c: (k, i, 0)))
    return pl.pallas_call(body, grid_spec=grid_spec, out_shape=jax.ShapeDtypeStruct((nchip, r, n), F32),
                          name=name, compiler_params=_cp(2))(cidx, g4, recv)


def _add_chips(p4, recv3, chip_idx, name):
    _, r, n = p4.shape
    tr = _tile(r, 256, 8)

    def body(k_ref, a_ref, b_ref, o_ref):
        lo, hi = _split_pairs(a_ref[...])
        for j in range(3):
            blo, bhi = _split_pairs(b_ref[j])
            lo, hi = lo + blo, hi + bhi
        o_ref[:, :n] = lo
        o_ref[:, n:] = hi

    grid_spec = pltpu.PrefetchScalarGridSpec(
        num_scalar_prefetch=1, grid=(r // tr,),
        in_specs=[pl.BlockSpec((None, tr, n), lambda i, k: (k[0], i, 0)),
                  pl.BlockSpec((3, tr, n), lambda i, k: (0, i, 0))],
        out_specs=pl.BlockSpec((tr, 2 * n), lambda i, k: (i, 0)))
    return pl.pallas_call(body, grid_spec=grid_spec, out_shape=jax.ShapeDtypeStruct((r, 2 * n), F32),
                          name=name, compiler_params=_cp(1))(chip_idx, p4, recv3)


def _all_reduce_small(xs):
    r, n = xs.shape

    def body(x_ref, o_ref, buf, send_sems, recv_sems):
        x, y, c = _place()
        me = 4 * x + 2 * y + c
        buf[me] = x_ref[...]
        peers = []
        for k in range(1, N_DEV):
            px = (1 - x) if (k >> 2) & 1 else x
            py = (1 - y) if (k >> 1) & 1 else y
            pc = (1 - c) if k & 1 else c
            peers.append((px, py, pc))
        sends = [pltpu.make_async_remote_copy(src_ref=x_ref, dst_ref=buf.at[me], send_sem=send_sems.at[k],
                                              recv_sem=recv_sems.at[k], device_id=peer, device_id_type=MESH)
                 for k, peer in enumerate(peers)]
        for cp in sends:
            cp.start()
        for k, (px, py, pc) in enumerate(peers):
            pltpu.make_async_remote_copy(src_ref=x_ref, dst_ref=buf.at[4 * px + 2 * py + pc],
                                         send_sem=send_sems.at[k], recv_sem=recv_sems.at[k],
                                         device_id=(px, py, pc), device_id_type=MESH).wait_recv()
        for cp in sends:
            cp.wait_send()
        acc = buf[0]
        for dv in range(1, N_DEV):
            acc = acc + buf[dv]
        o_ref[...] = acc

    vm = pl.BlockSpec(memory_space=pltpu.VMEM)
    return pl.pallas_call(
        body, out_shape=jax.ShapeDtypeStruct((r, n), F32), in_specs=[vm], out_specs=vm,
        scratch_shapes=[pltpu.VMEM((N_DEV, r, n), F32), pltpu.SemaphoreType.DMA((7,)), pltpu.SemaphoreType.DMA((7,))],
        name="all_reduce_small", compiler_params=pltpu.CompilerParams(vmem_limit_bytes=VMEM_LIMIT))(xs)


def _pack_rows(arrs, quantum):
    parts, sizes = [], []
    for a in arrs:
        f = a.reshape(-1)
        pad = (-f.shape[0]) % quantum
        if pad:
            f = jnp.pad(f, (0, pad))
        parts.append(f)
        sizes.append(f.shape[0])
    return jnp.concatenate(parts).reshape(-1, LANE), sizes


def _unpack_rows(buf, sizes, shapes):
    flat = buf.reshape(-1)
    out, off = [], 0
    for sz, sh in zip(sizes, shapes):
        n = math.prod(sh)
        out.append(flat[off:off + n].reshape(sh))
        off += sz
    return out


def kernel(x, meta_tokens, mix_norm_g, w_in, conv_dw, conv_b, conv_ln_g, conv_ln_b, w_conv_out, gqa_q_norm_g, gqa_k_norm_g, w_gqa_out, mla_q_norm_g, w_mla_uq, mla_kv_norm_g, w_mla_ukv, w_mla_out, gate_b, w_out, ffn_norm_g, w_ffn_gate, w_ffn_up, w_ffn_down, final_norm_g, loss_target, m_meta_tokens, m_mix_norm_g, m_w_in, m_conv_dw, m_conv_b, m_conv_ln_g, m_conv_ln_b, m_w_conv_out, m_gqa_q_norm_g, m_gqa_k_norm_g, m_w_gqa_out, m_mla_q_norm_g, m_w_mla_uq, m_mla_kv_norm_g, m_w_mla_ukv, m_w_mla_out, m_gate_b, m_w_out, m_ffn_norm_g, m_w_ffn_gate, m_w_ffn_up, m_w_ffn_down, m_final_norm_g, v_meta_tokens, v_mix_norm_g, v_w_in, v_conv_dw, v_conv_b, v_conv_ln_g, v_conv_ln_b, v_w_conv_out, v_gqa_q_norm_g, v_gqa_k_norm_g, v_w_gqa_out, v_mla_q_norm_g, v_w_mla_uq, v_mla_kv_norm_g, v_w_mla_ukv, v_w_mla_out, v_gate_b, v_w_out, v_ffn_norm_g, v_w_ffn_gate, v_w_ffn_up, v_w_ffn_down, v_final_norm_g):
    w = dict(meta_tokens=meta_tokens, mix_norm_g=mix_norm_g, w_in=w_in, conv_dw=conv_dw, conv_b=conv_b,
             conv_ln_g=conv_ln_g, conv_ln_b=conv_ln_b, w_conv_out=w_conv_out, gqa_q_norm_g=gqa_q_norm_g,
             gqa_k_norm_g=gqa_k_norm_g, w_gqa_out=w_gqa_out, mla_q_norm_g=mla_q_norm_g, w_mla_uq=w_mla_uq,
             mla_kv_norm_g=mla_kv_norm_g, w_mla_ukv=w_mla_ukv, w_mla_out=w_mla_out, gate_b=gate_b, w_out=w_out,
             ffn_norm_g=ffn_norm_g, w_ffn_gate=w_ffn_gate, w_ffn_up=w_ffn_up, w_ffn_down=w_ffn_down,
             final_norm_g=final_norm_g)
    mom1 = dict(zip(W_NAMES, (m_meta_tokens, m_mix_norm_g, m_w_in, m_conv_dw, m_conv_b, m_conv_ln_g, m_conv_ln_b, m_w_conv_out, m_gqa_q_norm_g, m_gqa_k_norm_g, m_w_gqa_out, m_mla_q_norm_g, m_w_mla_uq, m_mla_kv_norm_g, m_w_mla_ukv, m_w_mla_out, m_gate_b, m_w_out, m_ffn_norm_g, m_w_ffn_gate, m_w_ffn_up, m_w_ffn_down, m_final_norm_g)))
    mom2 = dict(zip(W_NAMES, (v_meta_tokens, v_mix_norm_g, v_w_in, v_conv_dw, v_conv_b, v_conv_ln_g, v_conv_ln_b, v_w_conv_out, v_gqa_q_norm_g, v_gqa_k_norm_g, v_w_gqa_out, v_mla_q_norm_g, v_w_mla_uq, v_mla_kv_norm_g, v_w_mla_ukv, v_w_mla_out, v_gate_b, v_w_out, v_ffn_norm_g, v_w_ffn_gate, v_w_ffn_up, v_w_ffn_down, v_final_norm_g)))

    nb, seq, d = x.shape
    depth = w_in.shape[0]
    c_conv = conv_b.shape[1]
    kw = conv_dw.shape[1]
    hd = HEAD_DIM
    hq = w_gqa_out.shape[1] // hd
    nh = w_mla_out.shape[1] // hd
    qr, kvr = mla_q_norm_g.shape[1], mla_kv_norm_g.shape[1]
    d_in = w_in.shape[2] * N_DEV
    hkv = (d_in - 2 * c_conv - hq * hd - qr - kvr - MLA_ROPE_DIM - N_BRANCH * d) // (2 * hd)
    l_valid = seq + N_META
    lp = -(-(l_valid + kw // 2) // LANE) * LANE
    m_tok = nb * lp
    mla_qk = hd + MLA_ROPE_DIM

    off_gate = 0
    off_conv = N_BRANCH * d
    off_q = off_conv + 2 * c_conv
    off_cq = off_q + (hq + 2 * hkv) * hd
    off_ckv = off_cq + qr
    off_kpe = off_ckv + kvr
    n_proj = off_kpe + LANE

    x_id, y_id, c_id = _place()
    me = 4 * x_id + 2 * y_id + c_id

    tm_wide = _tile(lp, 272, 16)
    tm_mid = _tile(lp, 544, 16)

    sm_buf, sm_sizes = _pack_rows([meta_tokens, conv_dw], 8 * LANE)
    sm_all = _all_gather([sm_buf], True, "ag_small")[0]
    sm_all = sm_all.reshape(N_DEV, -1)
    o0 = 0
    meta_all = sm_all[:, o0:o0 + meta_tokens.size].reshape((N_DEV,) + meta_tokens.shape)
    meta_full = jnp.transpose(meta_all, (1, 0, 2)).reshape(N_META, d)
    o0 = sm_sizes[0]
    dw_all = sm_all[:, o0:o0 + conv_dw.size].reshape((N_DEV, depth, kw, c_conv // N_DEV))
    dw_full = jnp.transpose(dw_all, (1, 2, 0, 3)).reshape(depth, kw, c_conv)

    n_head_cols = d_in - N_BRANCH * d - MLA_ROPE_DIM
    rb = MLA_ROPE_DIM
    nb_gate, nb_head = N_BRANCH * d // rb, n_head_cols // rb
    assert n_head_cols % rb == 0 and (N_BRANCH * d) % rb == 0 and n_proj == (nb_gate + nb_head + 2) * rb

    def in_src(i):
        return jnp.where(i < nb_gate, nb_head + 1 + i,
                         jnp.where(i < nb_gate + nb_head, i - nb_gate, jnp.where(i == nb_gate + nb_head, nb_head, 0)))

    def in_dst(i):
        return jnp.where(i < nb_head, nb_gate + i, jnp.where(i == nb_head, nb_gate + nb_head, i - nb_head - 1))

    def uq_src(i):
        j = i - 2 * nh
        return jnp.where(i < 2 * nh, 3 * (i // 2) + i % 2, 3 * (j // 2) + 2)

    def uq_dst(i):
        return jnp.where(i % 3 < 2, 2 * (i // 3) + i % 3, 2 * nh + 2 * (i // 3))

    ag_pending, tokens = [], []
    for i in range(depth):
        shards = [_pack_pairs_xla(w[n][i] if n in ROW_SHARDED else w[n][i].T) for n in BIG]
        *pend, tok = _ici_start(shards, [(N_DEV,) + a.shape for a in shards], tokens[-1:], gather=True,
                                name="ag_ici_start_%d" % i)
        ag_pending.append(pend)
        tokens.append(tok)

    def all_gather_layer(i, after):
        srcs, lands = _ici_wait(*ag_pending[i], after, gather=True, name="ag_ici_wait_%d" % i)
        gath = _gather_to_sibling(srcs, lands)
        pk = {n: g.reshape(-1, g.shape[-1]) for n, g in zip(BIG, gath)}
        wl = {n: _unpack(pk[n], name="unpack_" + n) for n in BIG if n not in ('w_in', 'w_mla_uq', 'w_mla_ukv')}
        wl['w_in'] = _unpack(pk['w_in'], name="unpack_w_in", out_rows=n_proj, tr=rb, src_fn=in_src,
                             zero_fn=lambda i: i == nb_gate + nb_head + 1)
        wl['w_mla_uq'] = _unpack(pk['w_mla_uq'], name="unpack_w_mla_uq", out_rows=2 * nh * hd, tr=rb, src_fn=uq_src,
                                 zero_fn=lambda i: (i >= 2 * nh) & ((i - 2 * nh) % 2 == 1))
        wl['w_mla_ukv'] = _unpack(pk['w_mla_ukv'], name="unpack_w_mla_ukv", out_rows=2 * nh * hd, tr=hd,
                                  src_fn=lambda i: jnp.where(i < nh, 2 * i, 2 * (i - nh) + 1))
        return wl

    full = [None] * depth

    cos_g, sin_g = _rope_tables(lp, l_valid, hd // 4)
    cos_m, sin_m = _rope_tables(lp, l_valid, MLA_ROPE_DIM // 4)

    def vec2(a):
        return a.reshape(1, -1)

    meta_b = jnp.broadcast_to(meta_full[None], (nb, N_META, d))
    h = jnp.concatenate([meta_b, x, jnp.zeros((nb, lp - l_valid, d), F32)], axis=1).reshape(m_tok, d)
    tgt = jnp.pad(loss_target, ((0, 0), (N_META, lp - l_valid), (0, 0))).reshape(m_tok, d)

    saved = []
    for i in range(depth):
        s = {}
        s['h_in'] = h
        u = _rms_fwd(h, d, 0, vec2(mix_norm_g[i]), tm_wide, "mix_norm_fwd")
        wl = full[i] = all_gather_layer(i, tokens[-1:] if i == 0 else [h])
        proj = _matmul(u, wl['w_in'], mode="nt", out_dtype=F32, name="mm_in")
        cc = _conv_fwd(proj, dw_full[i], vec2(conv_b[i]), c_conv=c_conv, off_a=off_conv, nb=nb, lp=lp, l_valid=l_valid)
        sc = _ln_silu_fwd(cc, vec2(conv_ln_g[i]), vec2(conv_ln_b[i]), tm_mid)
        ya = _matmul(sc, wl['w_conv_out'], mode="nt", out_dtype=F32, name="mm_conv_out")
        qh, kh, vh = _gqa_prep_fwd(proj, vec2(gqa_q_norm_g[i]), vec2(gqa_k_norm_g[i]), cos_g, sin_g,
                                   off_q=off_q, hq=hq, hkv=hkv, nb=nb, lp=lp, tm=tm_mid)
        ob = _attn_fwd(qh, kh, vh, scale=1.0 / math.sqrt(hd), l_valid=l_valid, name="gqa_attn_fwd")
        yb = _matmul(ob, wl['w_gqa_out'], mode="nt", out_dtype=F32, name="mm_gqa_out")
        cqn = _rms_fwd(proj, qr, _cb(off_cq, qr), vec2(mla_q_norm_g[i]), tm_mid, "mla_q_norm_fwd")
        ckvn = _rms_fwd(proj, kvr, _cb(off_ckv, kvr), vec2(mla_kv_norm_g[i]), tm_mid, "mla_kv_norm_fwd")
        qm = _matmul(cqn, wl['w_mla_uq'], mode="nt", out_dtype=F32, name="mm_mla_uq")
        kvm = _matmul(ckvn, wl['w_mla_ukv'], mode="nt", out_dtype=F32, name="mm_mla_ukv")
        qc, kc, vc = _mla_prep_fwd(qm, kvm, proj, cos_m, sin_m, off_kpe=off_kpe, nh=nh, nb=nb, lp=lp, tm=tm_mid)
        oc = _attn_fwd(qc, kc, vc, scale=1.0 / math.sqrt(mla_qk), l_valid=l_valid, name="mla_attn_fwd")
        yc = _matmul(oc, wl['w_mla_out'], mode="nt", out_dtype=F32, name="mm_mla_out")
        merged = _gate_fwd(proj, vec2(gate_b[i]), ya, yb, yc, d=d, tm=tm_wide)
        h2 = _matmul(merged, wl['w_out'], mode="nn", out_dtype=F32, name="mm_out", residual=h)
        vn = _rms_fwd(h2, d, 0, vec2(ffn_norm_g[i]), tm_wide, "ffn_norm_fwd")
        gt = _matmul(vn, wl['w_ffn_gate'], mode="nt", out_dtype=F32, name="mm_ffn_gate")
        up = _matmul(vn, wl['w_ffn_up'], mode="nt", out_dtype=F32, name="mm_ffn_up")
        act = _swiglu_fwd(gt, up, tm_wide)
        h = _matmul(act, wl['w_ffn_down'], mode="nn", out_dtype=F32, name="mm_ffn_down", residual=h2)
        s.update(u=u, proj=proj, cc=cc, sc=sc, ya=ya, qh=qh, kh=kh, vh=vh, ob=ob, yb=yb, cqn=cqn, ckvn=ckvn,
                 qc=qc, kc=kc, vc=vc, oc=oc, yc=yc, merged=merged, h2=h2, vn=vn, gt=gt, up=up, act=act)
        saved.append(s)

    dh, dg_final, loss_part = _loss_head(h, vec2(final_norm_g), tgt, lp=lp, l_valid=l_valid, tm=tm_wide)

    gsmall = {n: [None] * depth for n in SMALL if n != 'final_norm_g'}
    gdw = [None] * depth
    gbig = [None] * depth
    cidx = jnp.reshape(c_id, (1,)).astype(jnp.int32)
    chip_idx = jnp.reshape(2 * x_id + y_id, (1,)).astype(jnp.int32)

    def reduce_scatter_start(gl, i):
        g4s = [gl[n].reshape(N_DEV // 2, 2, -1, gl[n].shape[-1]) for n in BIG]
        from_sibling = _swap_with_sibling(g4s)
        p4s = [_add_sibling(g, r, cidx, "rs_add_sibling_" + n) for n, g, r in zip(BIG, g4s, from_sibling)]
        *pend, tok = _ici_start(p4s, [(3,) + p.shape[1:] for p in p4s], [], gather=False,
                                name="rs_ici_start_%d" % i)
        return pend, tok

    def reduce_scatter_finish(pend, after, i):
        p4s, from_chips = _ici_wait(*pend, after, gather=False, name="rs_ici_wait_%d" % i)
        out = {}
        for n, p, r in zip(BIG, p4s, from_chips):
            g = _add_chips(p, r, chip_idx, "rs_add_chips_" + n)
            out[n] = g if n in ROW_SHARDED else g.T
        return out

    rs_pending, rs_token = None, []
    for i in reversed(range(depth)):
        s = saved[i]
        proj = s['proj']
        wl = full[i]
        gl = {}
        dact = _matmul(dh, wl['w_ffn_down'], mode="nt", out_dtype=F32, name="mm_ffn_down_dx", deps=rs_token)
        gl['w_ffn_down'] = _matmul_tn_packed(s['act'], dh, "mm_ffn_down_dw")
        dgt, dup = _swiglu_bwd(dact, s['gt'], s['up'], tm_wide)
        dvn = _matmul(dgt, wl['w_ffn_gate'], mode="nn", out_dtype=F32, name="mm_ffn_gate_dx")
        dvn = _matmul(dup, wl['w_ffn_up'], mode="nn", out_dtype=F32, name="mm_ffn_up_dx", residual=dvn)
        gl['w_ffn_gate'] = _matmul_tn_packed(dgt, s['vn'], "mm_ffn_gate_dw")
        gl['w_ffn_up'] = _matmul_tn_packed(dup, s['vn'], "mm_ffn_up_dw")
        dh2, gsmall['ffn_norm_g'][i] = _rms_bwd(s['h2'], d, 0, vec2(ffn_norm_g[i]), dvn, tm_wide, F32,
                                                "ffn_norm_bwd", add=dh)
        dmg = _matmul(dh2, wl['w_out'], mode="nt", out_dtype=F32, name="mm_out_dx")
        gl['w_out'] = _matmul_tn_packed(s['merged'], dh2, "mm_out_dw")
        dya, dyb, dyc, dg0, dg1, dg2, db0, db1, db2 = _gate_bwd(dmg, proj, vec2(gate_b[i]), s['ya'], s['yb'],
                                                                s['yc'], d=d, tm=tm_wide)
        gsmall['gate_b'][i] = jnp.concatenate([db0, db1, db2], axis=1)
        dsc = _matmul(dya, wl['w_conv_out'], mode="nn", out_dtype=F32, name="mm_conv_out_dx")
        gl['w_conv_out'] = _matmul_tn_packed(dya, s['sc'], "mm_conv_out_dw")
        dcc, gsmall['conv_ln_g'][i], gsmall['conv_ln_b'][i] = _ln_silu_bwd(
            s['cc'], vec2(conv_ln_g[i]), vec2(conv_ln_b[i]), dsc, tm_mid)
        da, dgc, gdw[i], gsmall['conv_b'][i] = _conv_bwd(proj, dcc, dw_full[i], c_conv=c_conv, off_a=off_conv,
                                                         nb=nb, lp=lp, l_valid=l_valid)
        dob = _matmul(dyb, wl['w_gqa_out'], mode="nn", out_dtype=BF16, name="mm_gqa_out_dx")
        gl['w_gqa_out'] = _matmul_tn_packed(dyb, s['ob'], "mm_gqa_out_dw")
        dqh, dkh, dvh = _attn_bwd(s['qh'], s['kh'], s['vh'], dob, scale=1.0 / math.sqrt(hd), l_valid=l_valid,
                                  name="gqa_attn_bwd")
        dq, dkv, gsmall['gqa_q_norm_g'][i], gsmall['gqa_k_norm_g'][i] = _gqa_prep_bwd(
            proj, vec2(gqa_q_norm_g[i]), vec2(gqa_k_norm_g[i]), cos_g, sin_g, dqh, dkh, dvh,
            off_q=off_q, hq=hq, hkv=hkv, nb=nb, lp=lp, tm=tm_mid)
        doc = _matmul(dyc, wl['w_mla_out'], mode="nn", out_dtype=BF16, name="mm_mla_out_dx")
        gl['w_mla_out'] = _matmul_tn_packed(dyc, s['oc'], "mm_mla_out_dw")
        dqc, dkc, dvc = _attn_bwd(s['qc'], s['kc'], s['vc'], doc, scale=1.0 / math.sqrt(mla_qk), l_valid=l_valid,
                                  name="mla_attn_bwd")
        dqm, dkvm, dkpe = _mla_prep_bwd(dqc, dkc, dvc, cos_m, sin_m, nh=nh, nb=nb, lp=lp, tm=tm_mid)
        dcqn = _matmul(dqm, wl['w_mla_uq'], mode="nn", out_dtype=F32, name="mm_mla_uq_dx")
        guq = _matmul_tn_packed(dqm, s['cqn'], "mm_mla_uq_dw")
        dckvn = _matmul(dkvm, wl['w_mla_ukv'], mode="nn", out_dtype=F32, name="mm_mla_ukv_dx")
        gukv = _matmul_tn_packed(dkvm, s['ckvn'], "mm_mla_ukv_dw")
        dcq, gsmall['mla_q_norm_g'][i] = _rms_bwd(proj, qr, _cb(off_cq, qr), vec2(mla_q_norm_g[i]), dcqn, tm_mid,
                                                  BF16, "mla_q_norm_bwd")
        dckv, gsmall['mla_kv_norm_g'][i] = _rms_bwd(proj, kvr, _cb(off_ckv, kvr), vec2(mla_kv_norm_g[i]), dckvn,
                                                    tm_mid, BF16, "mla_kv_norm_bwd")
        gl['w_mla_uq'] = _gather_rows(guq, name="perm_mla_uq_dw", out_rows=nh * mla_qk, tr=rb, src_fn=uq_dst)
        gl['w_mla_ukv'] = _gather_rows(gukv, name="perm_mla_ukv_dw", out_rows=2 * nh * hd, tr=hd,
                                       src_fn=lambda i: i // 2 + nh * (i % 2))
        dproj = jnp.concatenate([dg0, dg1, dg2, da, dgc, dq, dkv, dcq, dckv, dkpe], axis=1)
        du = _matmul(dproj, wl['w_in'], mode="nn", out_dtype=F32, name="mm_in_dx")
        gin = _matmul_tn_packed(dproj, s['u'], "mm_in_dw")
        gl['w_in'] = _gather_rows(gin, name="perm_in_dw", out_rows=d_in, tr=rb, src_fn=in_dst)
        dh, gsmall['mix_norm_g'][i] = _rms_bwd(s['h_in'], d, 0, vec2(mix_norm_g[i]), du, tm_wide, F32,
                                               "mix_norm_bwd", add=dh2)
        started, tok = reduce_scatter_start(gl, i)
        if rs_pending is not None:
            gbig[i + 1] = reduce_scatter_finish(rs_pending, [dh, tok], i + 1)
        rs_pending, rs_token = started, [tok]
    gbig[0] = reduce_scatter_finish(rs_pending, rs_token, 0)

    dh0 = dh.reshape(nb, lp, d)
    grad_x = dh0[:, N_META:l_valid]
    gmeta_full = jnp.sum(dh0[:, :N_META], axis=0)
    grads = {n: jnp.stack([gbig[i][n] for i in range(depth)]) for n in BIG}

    small_list = [jnp.stack(gsmall[n]).reshape(w[n].shape) for n in SMALL if n != 'final_norm_g']
    small_list += [dg_final.reshape(final_norm_g.shape), gmeta_full, jnp.stack(gdw), loss_part]
    ar_buf, ar_sizes = _pack_rows(small_list, 8 * LANE)
    ar = _all_reduce_small(ar_buf)
    ar_shapes = [w[n].shape for n in SMALL] + [(N_META, d), (depth, kw, c_conv), (1, LANE)]
    ar_out = _unpack_rows(ar, ar_sizes, ar_shapes)
    for n, g in zip(SMALL, ar_out):
        grads[n] = g
    dcol = d // N_DEV
    grads['meta_tokens'] = lax.dynamic_slice(ar_out[len(SMALL)], (0, me * dcol), (N_META, dcol))
    ccol = c_conv // N_DEV
    grads['conv_dw'] = lax.dynamic_slice(ar_out[len(SMALL) + 1], (0, 0, me * ccol),
                                         (depth, kw, ccol)).reshape(conv_dw.shape)
    loss = ar_out[len(SMALL) + 2][0, 0]

    delta, new_m, new_v = {}, {}, {}
    for n in BIG:
        sh = w[n].shape
        two_d = (sh[0] * sh[1], sh[2])
        dl, mn, vn_ = _adamw(w[n].reshape(two_d), grads[n].reshape(two_d), mom1[n].reshape(two_d),
                             mom2[n].reshape(two_d), "adamw_" + n)
        delta[n], new_m[n], new_v[n] = dl.reshape(sh), mn.reshape(sh), vn_.reshape(sh)
    rest = [n for n in W_NAMES if n not in BIG]
    pw, psz = _pack_rows([w[n] for n in rest], 8 * LANE)
    pg, _ = _pack_rows([grads[n] for n in rest], 8 * LANE)
    pm, _ = _pack_rows([mom1[n] for n in rest], 8 * LANE)
    pv, _ = _pack_rows([mom2[n] for n in rest], 8 * LANE)
    dl, mn, vn_ = _adamw(pw, pg, pm, pv, "adamw_small")
    shapes = [w[n].shape for n in rest]
    for n, a, b, c in zip(rest, _unpack_rows(dl, psz, shapes), _unpack_rows(mn, psz, shapes),
                          _unpack_rows(vn_, psz, shapes)):
        delta[n], new_m[n], new_v[n] = a, b, c

    return (loss, grad_x, *[grads[n] for n in W_NAMES], *[delta[n] for n in W_NAMES],
            *[new_m[n] for n in W_NAMES], *[new_v[n] for n in W_NAMES])
```

```python
import functools
import math

import jax
import jax.numpy as jnp
from jax import lax
from jax.experimental import pallas as pl
from jax.experimental.pallas import tpu as pltpu

F32 = jnp.float32
BF16 = jnp.bfloat16

N_META = 16
GRID_W = 64
ROPE_THETA = 10000.0
NORM_EPS = 1e-6
HEAD_DIM = 128
MLA_ROPE_DIM = 64
N_BRANCH = 3
N_DEV = 8
LANE = 128
VMEM_LIMIT = 56 * 1024 * 1024

ADAM_LR = 0.001
ADAM_B1 = 0.9
ADAM_B2 = 0.999
ADAM_EPS = 1e-08
ADAM_WD = 0.01
ADAM_STEP = 10

W_NAMES = ['meta_tokens', 'mix_norm_g', 'w_in', 'conv_dw', 'conv_b', 'conv_ln_g', 'conv_ln_b', 'w_conv_out',
           'gqa_q_norm_g', 'gqa_k_norm_g', 'w_gqa_out', 'mla_q_norm_g', 'w_mla_uq', 'mla_kv_norm_g', 'w_mla_ukv',
           'w_mla_out', 'gate_b', 'w_out', 'ffn_norm_g', 'w_ffn_gate', 'w_ffn_up', 'w_ffn_down', 'final_norm_g']
BIG = ['w_in', 'w_conv_out', 'w_gqa_out', 'w_mla_uq', 'w_mla_ukv', 'w_mla_out', 'w_out', 'w_ffn_gate', 'w_ffn_up',
       'w_ffn_down']
ROW_SHARDED = ('w_out', 'w_ffn_down')
SMALL = ['mix_norm_g', 'conv_b', 'conv_ln_g', 'conv_ln_b', 'gqa_q_norm_g', 'gqa_k_norm_g', 'mla_q_norm_g',
         'mla_kv_norm_g', 'gate_b', 'ffn_norm_g', 'final_norm_g']

NT = (((1,), (1,)), ((), ()))
TN = (((0,), (0,)), ((), ()))
NN = (((1,), (0,)), ((), ()))


def _tile(n, target, mult):
    best = None
    for t in range(mult, min(n, target) + 1, mult):
        if n % t == 0:
            best = t
    assert best is not None, (n, target, mult)
    return best


def _cp(n):
    return pltpu.CompilerParams(dimension_semantics=("arbitrary",) * n, vmem_limit_bytes=VMEM_LIMIT)


def _sig(x):
    return jax.nn.sigmoid(x)


def _cb(off, w):
    assert off % w == 0, (off, w)
    return off // w


def _matmul(a, b, *, mode, out_dtype, name, residual=None, deps=()):
    if mode == "nn":
        (m, k), (k2, n) = a.shape, b.shape
    elif mode == "nt":
        (m, k), (n, k2) = a.shape, b.shape
    else:
        (k, m), (k2, n) = a.shape, b.shape
    assert k == k2, (a.shape, b.shape, mode)
    tm = _tile(m, 1088, 128 if mode == "tn" else 16)
    tn = _tile(n, 1024, 128)
    tk = _tile(k, 2176, 128)
    nk = k // tk
    dims = {"nn": NN, "nt": NT, "tn": TN}[mode]
    a_spec = (pl.BlockSpec((tk, tm), lambda i, j, kk: (kk, i)) if mode == "tn"
              else pl.BlockSpec((tm, tk), lambda i, j, kk: (i, kk)))
    b_spec = (pl.BlockSpec((tn, tk), lambda i, j, kk: (j, kk)) if mode == "nt"
              else pl.BlockSpec((tk, tn), lambda i, j, kk: (kk, j)))
    o_spec = pl.BlockSpec((tm, tn), lambda i, j, kk: (i, j))
    has_res = residual is not None

    nd = len(deps)

    def body(*refs):
        refs = refs[:len(refs) - 2 - nd] + refs[len(refs) - 2:]
        if has_res:
            a_ref, b_ref, r_ref, o_ref, acc = refs
        else:
            a_ref, b_ref, o_ref, acc = refs
        kk = pl.program_id(2)

        @pl.when(kk == 0)
        def _():
            acc[...] = jnp.zeros_like(acc)

        acc[...] += lax.dot_general(a_ref[...].astype(BF16), b_ref[...].astype(BF16), dims,
                                    preferred_element_type=F32)

        @pl.when(kk == nk - 1)
        def _():
            r = acc[...]
            if has_res:
                r = r + r_ref[...]
            o_ref[...] = r.astype(o_ref.dtype)

    ins = [a, b] + ([residual] if has_res else []) + list(deps)
    in_specs = [a_spec, b_spec] + ([o_spec] if has_res else []) + [pl.BlockSpec(memory_space=pl.ANY)] * nd
    return pl.pallas_call(
        body, grid=(m // tm, n // tn, nk), in_specs=in_specs, out_specs=o_spec,
        out_shape=jax.ShapeDtypeStruct((m, n), out_dtype), scratch_shapes=[pltpu.VMEM((tm, tn), F32)],
        name=name, compiler_params=_cp(3))(*ins)


def _rms_fwd(x, w, cb, g, tm, name):
    m = x.shape[0]

    def body(x_ref, g_ref, o_ref):
        xv = x_ref[...]
        r = lax.rsqrt(jnp.mean(xv * xv, axis=-1, keepdims=True) + NORM_EPS)
        o_ref[...] = (xv * r * g_ref[...]).astype(o_ref.dtype)

    return pl.pallas_call(
        body, grid=(m // tm,),
        in_specs=[pl.BlockSpec((tm, w), lambda i: (i, cb)), pl.BlockSpec((1, w), lambda i: (0, 0))],
        out_specs=pl.BlockSpec((tm, w), lambda i: (i, 0)), out_shape=jax.ShapeDtypeStruct((m, w), BF16),
        name=name, compiler_params=_cp(1))(x, g)


def _rms_bwd(x, w, cb, g, dy, tm, out_dtype, name, add=None):
    m = x.shape[0]
    has_add = add is not None

    def body(*refs):
        if has_add:
            x_ref, g_ref, dy_ref, add_ref, dx_ref, dg_ref = refs
        else:
            x_ref, g_ref, dy_ref, dx_ref, dg_ref = refs
        xv = x_ref[...]
        dyv = dy_ref[...].astype(F32)
        r = lax.rsqrt(jnp.mean(xv * xv, axis=-1, keepdims=True) + NORM_EPS)
        t = dyv * g_ref[...]
        dx = r * t - xv * (r * r * r) * jnp.mean(t * xv, axis=-1, keepdims=True)
        if has_add:
            dx = dx + add_ref[...]
        dx_ref[...] = dx.astype(dx_ref.dtype)

        @pl.when(pl.program_id(0) == 0)
        def _():
            dg_ref[...] = jnp.zeros_like(dg_ref)

        dg_ref[...] += jnp.sum(dyv * xv * r, axis=0, keepdims=True)

    row = pl.BlockSpec((tm, w), lambda i: (i, 0))
    vec = pl.BlockSpec((1, w), lambda i: (0, 0))
    ins = [x, g, dy] + ([add] if has_add else [])
    in_specs = [pl.BlockSpec((tm, w), lambda i: (i, cb)), vec, row] + ([row] if has_add else [])
    return pl.pallas_call(
        body, grid=(m // tm,), in_specs=in_specs, out_specs=(row, vec),
        out_shape=(jax.ShapeDtypeStruct((m, w), out_dtype), jax.ShapeDtypeStruct((1, w), F32)),
        name=name, compiler_params=_cp(1))(*ins)


def _ln_silu_fwd(c, lg, lb, tm):
    m, w = c.shape

    def body(c_ref, g_ref, b_ref, o_ref):
        cv = c_ref[...]
        xc = cv - jnp.mean(cv, axis=-1, keepdims=True)
        r = lax.rsqrt(jnp.mean(xc * xc, axis=-1, keepdims=True) + NORM_EPS)
        yl = xc * r * g_ref[...] + b_ref[...]
        o_ref[...] = (yl * _sig(yl)).astype(o_ref.dtype)

    row = pl.BlockSpec((tm, w), lambda i: (i, 0))
    vec = pl.BlockSpec((1, w), lambda i: (0, 0))
    return pl.pallas_call(body, grid=(m // tm,), in_specs=[row, vec, vec], out_specs=row,
                          out_shape=jax.ShapeDtypeStruct((m, w), BF16), name="ln_silu_fwd",
                          compiler_params=_cp(1))(c, lg, lb)


def _ln_silu_bwd(c, lg, lb, ds, tm):
    m, w = c.shape

    def body(c_ref, g_ref, b_ref, ds_ref, dc_ref, dg_ref, db_ref):
        cv = c_ref[...]
        xc = cv - jnp.mean(cv, axis=-1, keepdims=True)
        r = lax.rsqrt(jnp.mean(xc * xc, axis=-1, keepdims=True) + NORM_EPS)
        nv = xc * r
        yl = nv * g_ref[...] + b_ref[...]
        sg = _sig(yl)
        dyl = ds_ref[...] * (sg * (1.0 + yl * (1.0 - sg)))
        dn = dyl * g_ref[...]
        dc = r * (dn - jnp.mean(dn, axis=-1, keepdims=True) - nv * jnp.mean(dn * nv, axis=-1, keepdims=True))
        dc_ref[...] = dc

        @pl.when(pl.program_id(0) == 0)
        def _():
            dg_ref[...] = jnp.zeros_like(dg_ref)
            db_ref[...] = jnp.zeros_like(db_ref)

        dg_ref[...] += jnp.sum(dyl * nv, axis=0, keepdims=True)
        db_ref[...] += jnp.sum(dyl, axis=0, keepdims=True)

    row = pl.BlockSpec((tm, w), lambda i: (i, 0))
    vec = pl.BlockSpec((1, w), lambda i: (0, 0))
    return pl.pallas_call(
        body, grid=(m // tm,), in_specs=[row, vec, vec, row], out_specs=(row, vec, vec),
        out_shape=(jax.ShapeDtypeStruct((m, w), F32), jax.ShapeDtypeStruct((1, w), F32),
                   jax.ShapeDtypeStruct((1, w), F32)),
        name="ln_silu_bwd", compiler_params=_cp(1))(c, lg, lb, ds)


CONV_MARGIN = 16
CONV_ROWS = 128


def _conv_fwd(proj, dw, bias, *, c_conv, off_a, nb, lp, l_valid):
    kw = dw.shape[0]
    half = kw // 2
    cw = LANE
    mg = CONV_MARGIN
    assert half <= mg - 1 and lp % CONV_ROWS == 0

    def body(a_ref, g_ref, w_ref, b_ref, c_ref, zp):
        t = lax.broadcasted_iota(jnp.int32, (lp, cw), 0)
        z = jnp.where(t < l_valid, a_ref[...] * _sig(g_ref[...]), 0.0)
        zp[0:mg, :] = jnp.zeros((mg, cw), F32)
        zp[mg + lp:mg + lp + mg, :] = jnp.zeros((mg, cw), F32)
        zp[mg:mg + lp, :] = z
        for r0 in range(0, lp, CONV_ROWS):
            acc = jnp.zeros((CONV_ROWS, cw), F32) + b_ref[...]
            for k in range(kw):
                s0 = mg + r0 + k - half
                acc = acc + w_ref[k:k + 1, :] * zp[s0:s0 + CONV_ROWS, :]
            c_ref[r0:r0 + CONV_ROWS, :] = acc

    ja, jg = _cb(off_a, cw), _cb(off_a + c_conv, cw)
    return pl.pallas_call(
        body, grid=(nb, c_conv // cw),
        in_specs=[pl.BlockSpec((lp, cw), lambda b, j: (b, ja + j)), pl.BlockSpec((lp, cw), lambda b, j: (b, jg + j)),
                  pl.BlockSpec((kw, cw), lambda b, j: (0, j)), pl.BlockSpec((1, cw), lambda b, j: (0, j))],
        out_specs=pl.BlockSpec((lp, cw), lambda b, j: (b, j)),
        out_shape=jax.ShapeDtypeStruct((nb * lp, c_conv), F32),
        scratch_shapes=[pltpu.VMEM((lp + 2 * mg, cw), F32)], name="conv_fwd", compiler_params=_cp(2))(proj, proj, dw, bias)


def _conv_bwd(proj, dc, dw, *, c_conv, off_a, nb, lp, l_valid):
    kw = dw.shape[0]
    half = kw // 2
    cw = LANE
    mg = CONV_MARGIN

    def body(a_ref, g_ref, dc_ref, w_ref, da_ref, dg_ref, ddw_ref, dcb_ref, zp, dcp):
        t = lax.broadcasted_iota(jnp.int32, (lp, cw), 0)
        z = jnp.where(t < l_valid, a_ref[...] * _sig(g_ref[...]), 0.0)
        for buf in (zp, dcp):
            buf[0:mg, :] = jnp.zeros((mg, cw), F32)
            buf[mg + lp:mg + lp + mg, :] = jnp.zeros((mg, cw), F32)
        zp[mg:mg + lp, :] = z
        dcv = dc_ref[...]
        dcp[mg:mg + lp, :] = dcv

        @pl.when(pl.program_id(1) == 0)
        def _():
            ddw_ref[...] = jnp.zeros_like(ddw_ref)
            dcb_ref[...] = jnp.zeros_like(dcb_ref)

        dcb_ref[...] += jnp.sum(dcv, axis=0, keepdims=True)
        for r0 in range(0, lp, CONV_ROWS):
            acc = jnp.zeros((CONV_ROWS, cw), F32)
            for k in range(kw):
                s0 = mg + r0 - k + half
                acc = acc + w_ref[k:k + 1, :] * dcp[s0:s0 + CONV_ROWS, :]
            tc = lax.broadcasted_iota(jnp.int32, (CONV_ROWS, cw), 0) + r0
            dz = jnp.where(tc < l_valid, acc, 0.0)
            sg = _sig(g_ref[r0:r0 + CONV_ROWS, :])
            da_ref[r0:r0 + CONV_ROWS, :] = (dz * sg).astype(da_ref.dtype)
            dg_ref[r0:r0 + CONV_ROWS, :] = (dz * a_ref[r0:r0 + CONV_ROWS, :] * sg * (1.0 - sg)).astype(dg_ref.dtype)
        for k in range(kw):
            acc = jnp.zeros((CONV_ROWS, cw), F32)
            for r0 in range(0, lp, CONV_ROWS):
                s0 = mg + r0 + k - half
                acc = acc + dc_ref[r0:r0 + CONV_ROWS, :] * zp[s0:s0 + CONV_ROWS, :]
            ddw_ref[k:k + 1, :] += jnp.sum(acc, axis=0, keepdims=True)

    ja, jg = _cb(off_a, cw), _cb(off_a + c_conv, cw)
    seq = pl.BlockSpec((lp, cw), lambda j, b: (b, j))
    return pl.pallas_call(
        body, grid=(c_conv // cw, nb),
        in_specs=[pl.BlockSpec((lp, cw), lambda j, b: (b, ja + j)), pl.BlockSpec((lp, cw), lambda j, b: (b, jg + j)),
                  seq, pl.BlockSpec((kw, cw), lambda j, b: (0, j))],
        out_specs=(seq, seq, pl.BlockSpec((kw, cw), lambda j, b: (0, j)), pl.BlockSpec((1, cw), lambda j, b: (0, j))),
        out_shape=(jax.ShapeDtypeStruct((nb * lp, c_conv), BF16), jax.ShapeDtypeStruct((nb * lp, c_conv), BF16),
                   jax.ShapeDtypeStruct((kw, c_conv), F32), jax.ShapeDtypeStruct((1, c_conv), F32)),
        scratch_shapes=[pltpu.VMEM((lp + 2 * mg, cw), F32), pltpu.VMEM((lp + 2 * mg, cw), F32)],
        name="conv_bwd", compiler_params=_cp(2))(proj, proj, dc, dw)


def _swap_halves(x, half):
    fwd = pltpu.roll(x, LANE - half, axis=1)
    bwd = pltpu.roll(x, half, axis=1)
    lane = lax.broadcasted_iota(jnp.int32, x.shape, 1)
    return jnp.where((lane & (2 * half - 1)) < half, fwd, bwd)


def _rope(x, cos, sin, half):
    return x * cos + _swap_halves(x, half) * sin


def _rope_t(dy, cos, sin, half):
    return dy * cos + _swap_halves(dy * sin, half)


def _rope_tables(lp, l_valid, half):
    t = jnp.arange(lp)
    n = jnp.clip(t - N_META, 0, None)
    real = (t >= N_META) & (t < l_valid)
    row = jnp.where(real, n // GRID_W, 0).astype(F32)
    col = jnp.where(real, n % GRID_W, 0).astype(F32)
    inv = ROPE_THETA ** (-jnp.arange(half, dtype=F32) / half)
    ar, ac = row[:, None] * inv[None, :], col[:, None] * inv[None, :]
    cos = jnp.concatenate([jnp.cos(ar), jnp.cos(ar), jnp.cos(ac), jnp.cos(ac)], axis=1)
    sin = jnp.concatenate([-jnp.sin(ar), jnp.sin(ar), -jnp.sin(ac), jnp.sin(ac)], axis=1)
    padw = LANE - 4 * half
    if padw:
        cos = jnp.pad(cos, ((0, 0), (0, padw)))
        sin = jnp.pad(sin, ((0, 0), (0, padw)))
    return cos.astype(F32), sin.astype(F32)


def _gqa_prep_fwd(proj, qg, kg, cos, sin, *, off_q, hq, hkv, nb, lp, tm):
    hd = HEAD_DIM
    half = hd // 4
    wq, wkv = hq * hd, 2 * hkv * hd
    nt = lp // tm

    def body(q_ref, kv_ref, qg_ref, kg_ref, cos_ref, sin_ref, qo_ref, ko_ref, vo_ref):
        cosv, sinv = cos_ref[...], sin_ref[...]

        def norm_rope(xh, g):
            r = lax.rsqrt(jnp.mean(xh * xh, axis=-1, keepdims=True) + NORM_EPS)
            return _rope(xh * r * g, cosv, sinv, half)

        for h in range(hq):
            qo_ref[h] = norm_rope(q_ref[:, h * hd:(h + 1) * hd], qg_ref[...]).astype(BF16)
        for h in range(hkv):
            ko_ref[h] = norm_rope(kv_ref[:, h * hd:(h + 1) * hd], kg_ref[...]).astype(BF16)
            vo_ref[h] = kv_ref[:, (hkv + h) * hd:(hkv + h + 1) * hd].astype(BF16)

    jq, jkv = _cb(off_q, wq), _cb(off_q + wq, wkv)
    tab = pl.BlockSpec((tm, LANE), lambda b, i: (i, 0))
    vec = pl.BlockSpec((1, hd), lambda b, i: (0, 0))

    def heads(h):
        return pl.BlockSpec((None, h, tm, hd), lambda b, i: (b, 0, i, 0))

    return pl.pallas_call(
        body, grid=(nb, nt),
        in_specs=[pl.BlockSpec((tm, wq), lambda b, i: (b * nt + i, jq)),
                  pl.BlockSpec((tm, wkv), lambda b, i: (b * nt + i, jkv)), vec, vec, tab, tab],
        out_specs=(heads(hq), heads(hkv), heads(hkv)),
        out_shape=(jax.ShapeDtypeStruct((nb, hq, lp, hd), BF16), jax.ShapeDtypeStruct((nb, hkv, lp, hd), BF16),
                   jax.ShapeDtypeStruct((nb, hkv, lp, hd), BF16)),
        name="gqa_prep_fwd", compiler_params=_cp(2))(proj, proj, qg, kg, cos, sin)


def _gqa_prep_bwd(proj, qg, kg, cos, sin, dqh, dkh, dvh, *, off_q, hq, hkv, nb, lp, tm):
    hd = HEAD_DIM
    half = hd // 4
    wq, wkv = hq * hd, 2 * hkv * hd
    nt = lp // tm

    def body(q_ref, kv_ref, qg_ref, kg_ref, cos_ref, sin_ref, dqh_ref, dkh_ref, dvh_ref,
             dq_ref, dkv_ref, dqg_ref, dkg_ref):
        cosv, sinv = cos_ref[...], sin_ref[...]

        def back(xh, g, dyh):
            dn = _rope_t(dyh, cosv, sinv, half)
            r = lax.rsqrt(jnp.mean(xh * xh, axis=-1, keepdims=True) + NORM_EPS)
            t = dn * g
            dx = r * t - xh * (r * r * r) * jnp.mean(t * xh, axis=-1, keepdims=True)
            return dx, jnp.sum(dn * xh * r, axis=0, keepdims=True)

        first = (pl.program_id(0) == 0) & (pl.program_id(1) == 0)

        @pl.when(first)
        def _():
            dqg_ref[...] = jnp.zeros_like(dqg_ref)
            dkg_ref[...] = jnp.zeros_like(dkg_ref)

        gq = jnp.zeros((1, hd), F32)
        for h in range(hq):
            dx, dg = back(q_ref[:, h * hd:(h + 1) * hd], qg_ref[...], dqh_ref[h])
            dq_ref[:, h * hd:(h + 1) * hd] = dx.astype(dq_ref.dtype)
            gq = gq + dg
        dqg_ref[...] += gq
        gk = jnp.zeros((1, hd), F32)
        for h in range(hkv):
            dx, dg = back(kv_ref[:, h * hd:(h + 1) * hd], kg_ref[...], dkh_ref[h])
            dkv_ref[:, h * hd:(h + 1) * hd] = dx.astype(dkv_ref.dtype)
            dkv_ref[:, (hkv + h) * hd:(hkv + h + 1) * hd] = dvh_ref[h].astype(dkv_ref.dtype)
            gk = gk + dg
        dkg_ref[...] += gk

    jq, jkv = _cb(off_q, wq), _cb(off_q + wq, wkv)
    tab = pl.BlockSpec((tm, LANE), lambda b, i: (i, 0))
    vec = pl.BlockSpec((1, hd), lambda b, i: (0, 0))

    def heads(h):
        return pl.BlockSpec((None, h, tm, hd), lambda b, i: (b, 0, i, 0))

    return pl.pallas_call(
        body, grid=(nb, nt),
        in_specs=[pl.BlockSpec((tm, wq), lambda b, i: (b * nt + i, jq)),
                  pl.BlockSpec((tm, wkv), lambda b, i: (b * nt + i, jkv)), vec, vec, tab, tab,
                  heads(hq), heads(hkv), heads(hkv)],
        out_specs=(pl.BlockSpec((tm, wq), lambda b, i: (b * nt + i, 0)),
                   pl.BlockSpec((tm, wkv), lambda b, i: (b * nt + i, 0)), vec, vec),
        out_shape=(jax.ShapeDtypeStruct((nb * lp, wq), BF16), jax.ShapeDtypeStruct((nb * lp, wkv), BF16),
                   jax.ShapeDtypeStruct((1, hd), F32), jax.ShapeDtypeStruct((1, hd), F32)),
        name="gqa_prep_bwd", compiler_params=_cp(2))(proj, proj, qg, kg, cos, sin, dqh, dkh, dvh)


def _mla_prep_fwd(qm, kvm, proj, cos, sin, *, off_kpe, nh, nb, lp, tm):
    hd = HEAD_DIM
    half = MLA_ROPE_DIM // 4
    nt = lp // tm
    wh = nh * hd

    def body(qm_ref, kvm_ref, kpe_ref, cos_ref, sin_ref, qo_ref, ko_ref, vo_ref):
        cosv, sinv = cos_ref[...], sin_ref[...]
        kpe = _rope(kpe_ref[...], cosv, sinv, half).astype(BF16)
        for h in range(nh):
            qo_ref[h, :, 0:hd] = qm_ref[:, h * hd:(h + 1) * hd].astype(BF16)
            qo_ref[h, :, hd:2 * hd] = _rope(qm_ref[:, wh + h * hd:wh + (h + 1) * hd], cosv, sinv, half).astype(BF16)
            ko_ref[h, :, 0:hd] = kvm_ref[:, h * hd:(h + 1) * hd].astype(BF16)
            ko_ref[h, :, hd:2 * hd] = kpe
            vo_ref[h] = kvm_ref[:, wh + h * hd:wh + (h + 1) * hd].astype(BF16)

    tab = pl.BlockSpec((tm, LANE), lambda b, i: (i, 0))
    wide = pl.BlockSpec((tm, 2 * wh), lambda b, i: (b * nt + i, 0))
    jk = _cb(off_kpe, LANE)
    return pl.pallas_call(
        body, grid=(nb, nt),
        in_specs=[wide, wide, pl.BlockSpec((tm, LANE), lambda b, i: (b * nt + i, jk)), tab, tab],
        out_specs=(pl.BlockSpec((None, nh, tm, 2 * hd), lambda b, i: (b, 0, i, 0)),
                   pl.BlockSpec((None, nh, tm, 2 * hd), lambda b, i: (b, 0, i, 0)),
                   pl.BlockSpec((None, nh, tm, hd), lambda b, i: (b, 0, i, 0))),
        out_shape=(jax.ShapeDtypeStruct((nb, nh, lp, 2 * hd), BF16), jax.ShapeDtypeStruct((nb, nh, lp, 2 * hd), BF16),
                   jax.ShapeDtypeStruct((nb, nh, lp, hd), BF16)),
        name="mla_prep_fwd", compiler_params=_cp(2))(qm, kvm, proj, cos, sin)


def _mla_prep_bwd(dqc, dkc, dvc, cos, sin, *, nh, nb, lp, tm):
    hd = HEAD_DIM
    half = MLA_ROPE_DIM // 4
    nt = lp // tm
    wh = nh * hd

    def body(dq_ref, dk_ref, dv_ref, cos_ref, sin_ref, dqm_ref, dkvm_ref, dkpe_ref):
        cosv, sinv = cos_ref[...], sin_ref[...]
        acc = jnp.zeros((tm, hd), F32)
        for h in range(nh):
            dqm_ref[:, h * hd:(h + 1) * hd] = dq_ref[h, :, 0:hd].astype(BF16)
            dqm_ref[:, wh + h * hd:wh + (h + 1) * hd] = _rope_t(dq_ref[h, :, hd:2 * hd], cosv, sinv, half).astype(BF16)
            dkvm_ref[:, h * hd:(h + 1) * hd] = dk_ref[h, :, 0:hd].astype(BF16)
            dkvm_ref[:, wh + h * hd:wh + (h + 1) * hd] = dv_ref[h].astype(BF16)
            acc = acc + dk_ref[h, :, hd:2 * hd]
        dkpe_ref[...] = _rope_t(acc, cosv, sinv, half).astype(BF16)

    tab = pl.BlockSpec((tm, LANE), lambda b, i: (i, 0))
    wide = pl.BlockSpec((tm, 2 * wh), lambda b, i: (b * nt + i, 0))
    return pl.pallas_call(
        body, grid=(nb, nt),
        in_specs=[pl.BlockSpec((None, nh, tm, 2 * hd), lambda b, i: (b, 0, i, 0)),
                  pl.BlockSpec((None, nh, tm, 2 * hd), lambda b, i: (b, 0, i, 0)),
                  pl.BlockSpec((None, nh, tm, hd), lambda b, i: (b, 0, i, 0)), tab, tab],
        out_specs=(wide, wide, pl.BlockSpec((tm, LANE), lambda b, i: (b * nt + i, 0))),
        out_shape=(jax.ShapeDtypeStruct((nb * lp, 2 * wh), BF16), jax.ShapeDtypeStruct((nb * lp, 2 * wh), BF16),
                   jax.ShapeDtypeStruct((nb * lp, LANE), BF16)),
        name="mla_prep_bwd", compiler_params=_cp(2))(dqc, dkc, dvc, cos, sin)


def _attn_fwd(q, k, v, *, scale, l_valid, name):
    nb, hq, lp, dk = q.shape
    hkv, dv = k.shape[1], v.shape[3]
    grp = hq // hkv
    tq = _tile(lp, 544, 16)
    nq = lp // tq

    def body(q_ref, k_ref, v_ref, o_ref):
        s = lax.dot_general(q_ref[...], k_ref[...], NT, preferred_element_type=F32) * scale
        col = lax.broadcasted_iota(jnp.int32, s.shape, 1)
        s = jnp.where(col < l_valid, s, -1e30)
        p = jnp.exp(s - jnp.max(s, axis=-1, keepdims=True))
        den = jnp.sum(p, axis=-1, keepdims=True)
        o = lax.dot_general(p.astype(BF16), v_ref[...], NN, preferred_element_type=F32)
        o_ref[...] = (o / den).astype(o_ref.dtype)

    return pl.pallas_call(
        body, grid=(nb, hq, nq),
        in_specs=[pl.BlockSpec((None, None, tq, dk), lambda b, h, i: (b, h, i, 0)),
                  pl.BlockSpec((None, None, lp, dk), lambda b, h, i: (b, h // grp, 0, 0)),
                  pl.BlockSpec((None, None, lp, dv), lambda b, h, i: (b, h // grp, 0, 0))],
        out_specs=pl.BlockSpec((tq, dv), lambda b, h, i: (b * nq + i, h)),
        out_shape=jax.ShapeDtypeStruct((nb * lp, hq * dv), BF16), name=name, compiler_params=_cp(3))(q, k, v)


def _attn_bwd(q, k, v, do, *, scale, l_valid, name):
    nb, hq, lp, dk = q.shape
    hkv, dv = k.shape[1], v.shape[3]
    grp = hq // hkv
    tq = _tile(lp, 272, 16)
    nq = lp // tq

    def body(q_ref, k_ref, v_ref, do_ref, dq_ref, dk_ref, dv_ref):
        qv, kv, vv, dov = q_ref[...], k_ref[...], v_ref[...], do_ref[...]
        s = lax.dot_general(qv, kv, NT, preferred_element_type=F32) * scale
        col = lax.broadcasted_iota(jnp.int32, s.shape, 1)
        s = jnp.where(col < l_valid, s, -1e30)
        p = jnp.exp(s - jnp.max(s, axis=-1, keepdims=True))
        p = p / jnp.sum(p, axis=-1, keepdims=True)
        dp = lax.dot_general(dov, vv, NT, preferred_element_type=F32)
        ds = (p * (dp - jnp.sum(p * dp, axis=-1, keepdims=True)) * scale).astype(BF16)
        dq_ref[...] = lax.dot_general(ds, kv, NN, preferred_element_type=F32)

        @pl.when((pl.program_id(2) == 0) & (pl.program_id(3) == 0))
        def _():
            dk_ref[...] = jnp.zeros_like(dk_ref)
            dv_ref[...] = jnp.zeros_like(dv_ref)

        dk_ref[...] += lax.dot_general(ds, qv, TN, preferred_element_type=F32)
        dv_ref[...] += lax.dot_general(p.astype(BF16), dov, TN, preferred_element_type=F32)

    return pl.pallas_call(
        body, grid=(nb, hkv, grp, nq),
        in_specs=[pl.BlockSpec((None, None, tq, dk), lambda b, h, g, i: (b, h * grp + g, i, 0)),
                  pl.BlockSpec((None, None, lp, dk), lambda b, h, g, i: (b, h, 0, 0)),
                  pl.BlockSpec((None, None, lp, dv), lambda b, h, g, i: (b, h, 0, 0)),
                  pl.BlockSpec((tq, dv), lambda b, h, g, i: (b * nq + i, h * grp + g))],
        out_specs=(pl.BlockSpec((None, None, tq, dk), lambda b, h, g, i: (b, h * grp + g, i, 0)),
                   pl.BlockSpec((None, None, lp, dk), lambda b, h, g, i: (b, h, 0, 0)),
                   pl.BlockSpec((None, None, lp, dv), lambda b, h, g, i: (b, h, 0, 0))),
        out_shape=(jax.ShapeDtypeStruct((nb, hq, lp, dk), F32), jax.ShapeDtypeStruct((nb, hkv, lp, dk), F32),
                   jax.ShapeDtypeStruct((nb, hkv, lp, dv), F32)),
        name=name, compiler_params=_cp(4))(q, k, v, do)


def _gate_fwd(proj, gb, ya, yb, yc, *, d, tm):
    m = ya.shape[0]
    cw = _tile(d, 512, LANE)
    nj = d // cw

    def body(g0, g1, g2, b0, b1, b2, ya_ref, yb_ref, yc_ref, o_ref):
        o = (_sig(g0[...] + b0[...]) * ya_ref[...] + _sig(g1[...] + b1[...]) * yb_ref[...]
             + _sig(g2[...] + b2[...]) * yc_ref[...])
        o_ref[...] = o.astype(o_ref.dtype)

    def gl(br):
        return pl.BlockSpec((tm, cw), lambda j, i: (i, br * nj + j))

    def gbs(br):
        return pl.BlockSpec((1, cw), lambda j, i: (0, br * nj + j))

    blk = pl.BlockSpec((tm, cw), lambda j, i: (i, j))
    return pl.pallas_call(
        body, grid=(nj, m // tm), in_specs=[gl(0), gl(1), gl(2), gbs(0), gbs(1), gbs(2), blk, blk, blk],
        out_specs=blk, out_shape=jax.ShapeDtypeStruct((m, d), BF16), name="gate_fwd",
        compiler_params=_cp(2))(proj, proj, proj, gb, gb, gb, ya, yb, yc)


def _gate_bwd(dm, proj, gb, ya, yb, yc, *, d, tm):
    m = ya.shape[0]
    cw = _tile(d, 512, LANE)
    nj = d // cw

    def body(dm_ref, g0, g1, g2, b0, b1, b2, ya_ref, yb_ref, yc_ref,
             dya, dyb, dyc, dg0, dg1, dg2, db0, db1, db2):
        dmv = dm_ref[...]

        @pl.when(pl.program_id(1) == 0)
        def _():
            for r in (db0, db1, db2):
                r[...] = jnp.zeros_like(r)

        for g, b, y, dy, dgl, db in ((g0, b0, ya_ref, dya, dg0, db0), (g1, b1, yb_ref, dyb, dg1, db1),
                                     (g2, b2, yc_ref, dyc, dg2, db2)):
            sg = _sig(g[...] + b[...])
            dy[...] = (sg * dmv).astype(dy.dtype)
            dl = dmv * y[...] * sg * (1.0 - sg)
            dgl[...] = dl.astype(dgl.dtype)
            db[...] += jnp.sum(dl, axis=0, keepdims=True)

    def gl(br):
        return pl.BlockSpec((tm, cw), lambda j, i: (i, br * nj + j))

    def gbs(br):
        return pl.BlockSpec((1, cw), lambda j, i: (0, br * nj + j))

    blk = pl.BlockSpec((tm, cw), lambda j, i: (i, j))
    vec = pl.BlockSpec((1, cw), lambda j, i: (0, j))
    act = jax.ShapeDtypeStruct((m, d), BF16)
    vsh = jax.ShapeDtypeStruct((1, d), F32)
    return pl.pallas_call(
        body, grid=(nj, m // tm), in_specs=[blk, gl(0), gl(1), gl(2), gbs(0), gbs(1), gbs(2), blk, blk, blk],
        out_specs=(blk,) * 6 + (vec,) * 3, out_shape=(act,) * 6 + (vsh,) * 3, name="gate_bwd",
        compiler_params=_cp(2))(dm, proj, proj, proj, gb, gb, gb, ya, yb, yc)


def _swiglu_fwd(gt, up, tm):
    m, f = gt.shape
    cw = _tile(f, 512, LANE)

    def body(g_ref, u_ref, o_ref):
        g = g_ref[...]
        o_ref[...] = (g * _sig(g) * u_ref[...]).astype(o_ref.dtype)

    blk = pl.BlockSpec((tm, cw), lambda j, i: (i, j))
    return pl.pallas_call(body, grid=(f // cw, m // tm), in_specs=[blk, blk], out_specs=blk,
                          out_shape=jax.ShapeDtypeStruct((m, f), BF16), name="swiglu_fwd",
                          compiler_params=_cp(2))(gt, up)


def _swiglu_bwd(dact, gt, up, tm):
    m, f = gt.shape
    cw = _tile(f, 512, LANE)

    def body(d_ref, g_ref, u_ref, dg_ref, du_ref):
        g, d = g_ref[...], d_ref[...]
        sg = _sig(g)
        dg_ref[...] = (d * u_ref[...] * sg * (1.0 + g * (1.0 - sg))).astype(dg_ref.dtype)
        du_ref[...] = (d * g * sg).astype(du_ref.dtype)

    blk = pl.BlockSpec((tm, cw), lambda j, i: (i, j))
    sh = jax.ShapeDtypeStruct((m, f), BF16)
    return pl.pallas_call(body, grid=(f // cw, m // tm), in_specs=[blk, blk, blk], out_specs=(blk, blk),
                          out_shape=(sh, sh), name="swiglu_bwd", compiler_params=_cp(2))(dact, gt, up)


def _loss_head(h, gf, tgt, *, lp, l_valid, tm):
    m, d = h.shape
    nt = lp // tm

    def body(h_ref, g_ref, t_ref, dh_ref, dg_ref, loss_ref):
        i = pl.program_id(0)
        xv = h_ref[...]
        r = lax.rsqrt(jnp.mean(xv * xv, axis=-1, keepdims=True) + NORM_EPS)
        y = xv * r * g_ref[...]
        t = lax.broadcasted_iota(jnp.int32, (tm, d), 0) + (i % nt) * tm
        err = jnp.where((t >= N_META) & (t < l_valid), y - t_ref[...], 0.0)
        dy = err * (1.0 / d)
        tg = dy * g_ref[...]
        dh_ref[...] = r * tg - xv * (r * r * r) * jnp.mean(tg * xv, axis=-1, keepdims=True)

        @pl.when(i == 0)
        def _():
            dg_ref[...] = jnp.zeros_like(dg_ref)
            loss_ref[...] = jnp.zeros_like(loss_ref)

        dg_ref[...] += jnp.sum(dy * xv * r, axis=0, keepdims=True)
        sq = jnp.sum(jnp.sum(err * err, axis=-1, keepdims=True), axis=0, keepdims=True)
        loss_ref[...] += jnp.zeros((1, LANE), F32) + sq * (0.5 / d)

    row = pl.BlockSpec((tm, d), lambda i: (i, 0))
    vec = pl.BlockSpec((1, d), lambda i: (0, 0))
    return pl.pallas_call(
        body, grid=(m // tm,), in_specs=[row, vec, row],
        out_specs=(row, vec, pl.BlockSpec((1, LANE), lambda i: (0, 0))),
        out_shape=(jax.ShapeDtypeStruct((m, d), F32), jax.ShapeDtypeStruct((1, d), F32),
                   jax.ShapeDtypeStruct((1, LANE), F32)),
        name="loss_head", compiler_params=_cp(1))(h, gf, tgt)


def _adamw(w, g, m, v, name):
    rows, cols = w.shape
    tr = rows
    if rows % 8 == 0:
        tr = _tile(rows, max(8, (1 << 18) // cols // 8 * 8), 8)
    c1 = 1.0 / (1.0 - ADAM_B1 ** ADAM_STEP)
    c2 = 1.0 / (1.0 - ADAM_B2 ** ADAM_STEP)

    def body(w_ref, g_ref, m_ref, v_ref, d_ref, mo_ref, vo_ref):
        gv = g_ref[...]
        mn = ADAM_B1 * m_ref[...] + (1.0 - ADAM_B1) * gv
        vn = ADAM_B2 * v_ref[...] + (1.0 - ADAM_B2) * (gv * gv)
        mo_ref[...] = mn
        vo_ref[...] = vn
        d_ref[...] = -ADAM_LR * ((mn * c1) / (jnp.sqrt(vn * c2) + ADAM_EPS) + ADAM_WD * w_ref[...])

    blk = pl.BlockSpec((tr, cols), lambda i: (i, 0))
    sh = jax.ShapeDtypeStruct((rows, cols), F32)
    return pl.pallas_call(body, grid=(rows // tr,), in_specs=[blk] * 4, out_specs=(blk,) * 3, out_shape=(sh,) * 3,
                          name=name, compiler_params=_cp(1))(w, g, m, v)


MESH = pl.DeviceIdType.MESH
ANY = pl.BlockSpec(memory_space=pl.ANY)


def _place():
    return lax.axis_index("x"), lax.axis_index("y"), lax.axis_index("c")


def _other_chips(x, y):
    return [(1 - x, y), (x, 1 - y), (1 - x, 1 - y)]


def _all_gather(xs, vmem, name):
    na = len(xs)

    def body(*refs):
        x_refs, out_refs = refs[:na], refs[na:2 * na]
        send_sems, recv_sems, local_sems = refs[2 * na:]
        x, y, c = _place()
        me, sibling = (x, y, c), (x, y, 1 - c)
        chips = _other_chips(x, y)

        def blk(p, px, py, pc):
            return out_refs[p].at[4 * px + 2 * py + pc]

        def copy(k, p, block, to, src=None):
            return pltpu.make_async_remote_copy(
                src_ref=blk(p, *block) if src is None else src, dst_ref=blk(p, *block),
                send_sem=send_sems.at[k * na + p], recv_sem=recv_sems.at[k * na + p], device_id=to,
                device_id_type=MESH)

        mine = [pltpu.make_async_copy(x_refs[p], blk(p, *me), local_sems.at[p]) for p in range(na)]
        for cp in mine:
            cp.start()
        first = [copy(0, p, me, sibling, src=x_refs[p]) for p in range(na)]
        first += [copy(1 + j, p, me, (*chip, c), src=x_refs[p]) for j, chip in enumerate(chips) for p in range(na)]
        for cp in first:
            cp.start()
        passed = []
        for j, chip in enumerate(chips):
            for p in range(na):
                copy(1 + j, p, (*chip, c), me).wait_recv()
                fwd = copy(4 + j, p, (*chip, c), sibling)
                fwd.start()
                passed.append(fwd)
        for p in range(na):
            copy(0, p, sibling, me).wait_recv()
        for j, chip in enumerate(chips):
            for p in range(na):
                copy(4 + j, p, (*chip, 1 - c), me).wait_recv()
        for cp in first + passed:
            cp.wait_send()
        for cp in mine:
            cp.wait()

    spec = pl.BlockSpec(memory_space=pltpu.VMEM) if vmem else ANY
    return pl.pallas_call(
        body, out_shape=[jax.ShapeDtypeStruct((N_DEV,) + a.shape, a.dtype) for a in xs],
        in_specs=[spec] * na, out_specs=[spec] * na,
        scratch_shapes=[pltpu.SemaphoreType.DMA((7 * na,)), pltpu.SemaphoreType.DMA((7 * na,)),
                        pltpu.SemaphoreType.DMA((na,))],
        name=name)(*xs)


def _swap_with_sibling(g4s):
    na = len(g4s)

    def body(*refs):
        g_refs, out_refs = refs[:na], refs[na:2 * na]
        send_sems, recv_sems = refs[2 * na:]
        x, y, c = _place()
        cps = [pltpu.make_async_remote_copy(src_ref=g_refs[p].at[:, 1 - c], dst_ref=out_refs[p],
                                            send_sem=send_sems.at[p], recv_sem=recv_sems.at[p],
                                            device_id=(x, y, 1 - c), device_id_type=MESH) for p in range(na)]
        for cp in cps:
            cp.start()
        for cp in cps:
            cp.wait()

    return pl.pallas_call(
        body, out_shape=[jax.ShapeDtypeStruct((g.shape[0],) + g.shape[2:], g.dtype) for g in g4s],
        in_specs=[ANY] * na, out_specs=[ANY] * na,
        scratch_shapes=[pltpu.SemaphoreType.DMA((na,)), pltpu.SemaphoreType.DMA((na,))], name="rs_sibling")(*g4s)


HBM = pl.BlockSpec(memory_space=pltpu.HBM)
SEM = pl.BlockSpec(memory_space=pltpu.SEMAPHORE)
EFFECT = pltpu.SideEffectType.DATAFLOW_SIDE_EFFECTING


def _in_hbm(a):
    return pltpu.with_memory_space_constraint(a, pltpu.HBM)


def _ici_copies(src_refs, land_refs, send_sems, recv_sems, gather):
    na = len(src_refs)
    x, y, c = _place()
    cps = []
    for j, (px, py) in enumerate(_other_chips(x, y)):
        for p in range(na):
            src = src_refs[p].at[4 * x + 2 * y + c] if gather else src_refs[p].at[2 * px + py]
            dst = src if gather else land_refs[p].at[j]
            cps.append(pltpu.make_async_remote_copy(
                src_ref=src, dst_ref=dst, send_sem=send_sems.at[j * na + p], recv_sem=recv_sems.at[j * na + p],
                device_id=(px, py, c), device_id_type=MESH))
    return cps


def _ici_start(srcs, land_shapes, deps, *, gather, name):
    na, nd = len(srcs), len(deps)
    lands = [lax.empty(s, a.dtype) for s, a in zip(land_shapes, srcs)]
    nb = na + len(lands)

    def body(*refs):
        src_refs, land_refs = refs[:na], refs[na:nb]
        send_sems, recv_sems = refs[nb + nd], refs[nb + nd + 1]
        token = refs[-1]
        for cp in _ici_copies(src_refs, land_refs, send_sems, recv_sems, gather):
            cp.start()
        token[...] = jnp.zeros_like(token)

    bufs = list(srcs) + lands
    out = pl.pallas_call(
        body, name=name,
        out_shape=(pltpu.SemaphoreType.DMA((3 * na,)), pltpu.SemaphoreType.DMA((3 * na,)),
                   *[pltpu.HBM(a.shape, a.dtype) for a in bufs], jax.ShapeDtypeStruct((8, LANE), F32)),
        in_specs=[HBM] * nb + [ANY] * nd,
        out_specs=(SEM, SEM, *([HBM] * nb), pl.BlockSpec(memory_space=pltpu.VMEM)),
        input_output_aliases={p: 2 + p for p in range(nb)},
        compiler_params=pltpu.CompilerParams(has_side_effects=EFFECT),
    )(*[_in_hbm(a) for a in bufs], *deps)
    return out[0], out[1], list(out[2:2 + na]), list(out[2 + na:2 + nb]), out[-1]


def _ici_wait(send_sems, recv_sems, srcs, lands, after, *, gather, name):
    na, nd = len(srcs), len(after)
    nb = na + len(lands)

    def body(*refs):
        src_refs, land_refs = refs[:na], refs[na:nb]
        s_sems, r_sems = refs[nb], refs[nb + 1]
        for cp in _ici_copies(src_refs, land_refs, s_sems, r_sems, gather):
            cp.wait_send()
            cp.wait_recv()

    bufs = list(srcs) + list(lands)
    out = pl.pallas_call(
        body, name=name, out_shape=tuple(pltpu.HBM(a.shape, a.dtype) for a in bufs),
        in_specs=[HBM] * nb + [SEM, SEM] + [ANY] * nd, out_specs=tuple([HBM] * nb),
        input_output_aliases={p: p for p in range(nb)},
        compiler_params=pltpu.CompilerParams(has_side_effects=EFFECT),
    )(*bufs, send_sems, recv_sems, *after)
    return list(out[:na]), list(out[na:])


def _gather_to_sibling(gs):
    na = len(gs)

    def body(*refs):
        out_refs = refs[na:2 * na]
        send_sems, recv_sems = refs[2 * na:]
        x, y, c = _place()
        sibling = (x, y, 1 - c)
        blocks = [(x, y)] + _other_chips(x, y)

        def copy(k, p, core):
            px, py = blocks[k]
            blk = out_refs[p].at[4 * px + 2 * py + core]
            return pltpu.make_async_remote_copy(src_ref=blk, dst_ref=blk, send_sem=send_sems.at[k * na + p],
                                                recv_sem=recv_sems.at[k * na + p], device_id=sibling,
                                                device_id_type=MESH)

        sends = [copy(k, p, c) for k in range(4) for p in range(na)]
        for cp in sends:
            cp.start()
        for k in range(4):
            for p in range(na):
                copy(k, p, 1 - c).wait_recv()
        for cp in sends:
            cp.wait_send()

    return pl.pallas_call(
        body, out_shape=[jax.ShapeDtypeStruct(a.shape, a.dtype) for a in gs],
        in_specs=[ANY] * na, out_specs=[ANY] * na, input_output_aliases={p: p for p in range(na)},
        scratch_shapes=[pltpu.SemaphoreType.DMA((4 * na,)), pltpu.SemaphoreType.DMA((4 * na,))],
        name="ag_sibling")(*gs)


HI_MASK = -65536


def _pack_pairs_xla(a):
    half = a.shape[-1] // 2
    bits = lax.bitcast_convert_type(a.astype(BF16), jnp.uint16).astype(jnp.uint32)
    return lax.bitcast_convert_type((bits[..., half:] << 16) | bits[..., :half], F32)


def _split_pairs(words):
    wv = lax.bitcast_convert_type(words, jnp.int32)
    return lax.bitcast_convert_type(wv << 16, F32), lax.bitcast_convert_type(wv & HI_MASK, F32)


def _join_pairs(lo, hi):
    lo_b = lax.bitcast_convert_type(lo.astype(BF16).astype(F32), jnp.int32)
    hi_b = lax.bitcast_convert_type(hi.astype(BF16).astype(F32), jnp.int32)
    return lax.bitcast_convert_type((hi_b & HI_MASK) | lax.shift_right_logical(lo_b, 16), F32)


def _unpack(xp, *, name, out_rows=None, tr=None, src_fn=None, zero_fn=None):
    r, hw = xp.shape
    out_rows = out_rows or r
    tr = tr or _tile(out_rows, 512, 16)
    src = src_fn or (lambda i: i)

    def body(x_ref, o_ref):
        lo, hi = _split_pairs(x_ref[...])
        if zero_fn is not None:
            z = zero_fn(pl.program_id(0))
            lo, hi = jnp.where(z, 0.0, lo), jnp.where(z, 0.0, hi)
        o_ref[:, :hw] = lo.astype(BF16)
        o_ref[:, hw:] = hi.astype(BF16)

    return pl.pallas_call(
        body, grid=(out_rows // tr,), in_specs=[pl.BlockSpec((tr, hw), lambda i: (src(i), 0))],
        out_specs=pl.BlockSpec((tr, 2 * hw), lambda i: (i, 0)),
        out_shape=jax.ShapeDtypeStruct((out_rows, 2 * hw), BF16), name=name, compiler_params=_cp(1))(xp)


def _gather_rows(xp, *, name, out_rows, tr, src_fn):
    _, hw = xp.shape

    def body(x_ref, o_ref):
        o_ref[...] = x_ref[...]

    return pl.pallas_call(
        body, grid=(out_rows // tr,), in_specs=[pl.BlockSpec((tr, hw), lambda i: (src_fn(i), 0))],
        out_specs=pl.BlockSpec((tr, hw), lambda i: (i, 0)),
        out_shape=jax.ShapeDtypeStruct((out_rows, hw), xp.dtype), name=name, compiler_params=_cp(1))(xp)


def _matmul_tn_packed(a, b, name):
    (t, m), (t2, c) = a.shape, b.shape
    assert t == t2
    hw = c // 2
    tm = _tile(m, 1088, LANE)
    tn = _tile(hw, 512, LANE)
    tk = _tile(t, 2176, LANE)
    nk, nj = t // tk, hw // tn

    def body(a_ref, bl_ref, bh_ref, o_ref, acc_lo, acc_hi):
        kk = pl.program_id(2)

        @pl.when(kk == 0)
        def _():
            acc_lo[...] = jnp.zeros_like(acc_lo)
            acc_hi[...] = jnp.zeros_like(acc_hi)

        av = a_ref[...].astype(BF16)
        acc_lo[...] += lax.dot_general(av, bl_ref[...].astype(BF16), TN, preferred_element_type=F32)
        acc_hi[...] += lax.dot_general(av, bh_ref[...].astype(BF16), TN, preferred_element_type=F32)

        @pl.when(kk == nk - 1)
        def _():
            o_ref[...] = _join_pairs(acc_lo[...], acc_hi[...])

    return pl.pallas_call(
        body, grid=(m // tm, nj, nk),
        in_specs=[pl.BlockSpec((tk, tm), lambda i, j, kk: (kk, i)), pl.BlockSpec((tk, tn), lambda i, j, kk: (kk, j)),
                  pl.BlockSpec((tk, tn), lambda i, j, kk: (kk, nj + j))],
        out_specs=pl.BlockSpec((tm, tn), lambda i, j, kk: (i, j)), out_shape=jax.ShapeDtypeStruct((m, hw), F32),
        scratch_shapes=[pltpu.VMEM((tm, tn), F32), pltpu.VMEM((tm, tn), F32)], name=name,
        compiler_params=_cp(3))(a, b, b)


def _add_sibling(g4, recv, cidx, name):
    nchip, _, r, n = g4.shape
    tr = _tile(r, 256, 8)

    def body(c_ref, a_ref, b_ref, o_ref):
        alo, ahi = _split_pairs(a_ref[...])
        blo, bhi = _split_pairs(b_ref[...])
        o_ref[...] = _join_pairs(alo + blo, ahi + bhi)

    grid_spec = pltpu.PrefetchScalarGridSpec(
        num_scalar_prefetch=1, grid=(nchip, r // tr),
        in_specs=[pl.BlockSpec((None, None, tr, n), lambda k, i, c: (k, c[0], i, 0)),
                  pl.BlockSpec((None, tr, n), lambda k, i, c: (k, i, 0))],
        out_specs=pl.BlockSpec((None, tr, n), lambda k, i, c: (k, i, 0)))
    return pl.pallas_call(body, grid_spec=grid_spec, out_shape=jax.ShapeDtypeStruct((nchip, r, n), F32),
                          name=name, compiler_params=_cp(2))(cidx, g4, recv)


def _add_chips(p4, recv3, chip_idx, name):
    _, r, n = p4.shape
    tr = _tile(r, 256, 8)

    def body(k_ref, a_ref, b_ref, o_ref):
        lo, hi = _split_pairs(a_ref[...])
        for j in range(3):
            blo, bhi = _split_pairs(b_ref[j])
            lo, hi = lo + blo, hi + bhi
        o_ref[:, :n] = lo
        o_ref[:, n:] = hi

    grid_spec = pltpu.PrefetchScalarGridSpec(
        num_scalar_prefetch=1, grid=(r // tr,),
        in_specs=[pl.BlockSpec((None, tr, n), lambda i, k: (k[0], i, 0)),
                  pl.BlockSpec((3, tr, n), lambda i, k: (0, i, 0))],
        out_specs=pl.BlockSpec((tr, 2 * n), lambda i, k: (i, 0)))
    return pl.pallas_call(body, grid_spec=grid_spec, out_shape=jax.ShapeDtypeStruct((r, 2 * n), F32),
                          name=name, compiler_params=_cp(1))(chip_idx, p4, recv3)


def _all_reduce_small(xs):
    r, n = xs.shape

    def body(x_ref, o_ref, buf, send_sems, recv_sems):
        x, y, c = _place()
        me = 4 * x + 2 * y + c
        buf[me] = x_ref[...]
        peers = []
        for k in range(1, N_DEV):
            px = (1 - x) if (k >> 2) & 1 else x
            py = (1 - y) if (k >> 1) & 1 else y
            pc = (1 - c) if k & 1 else c
            peers.append((px, py, pc))
        sends = [pltpu.make_async_remote_copy(src_ref=x_ref, dst_ref=buf.at[me], send_sem=send_sems.at[k],
                                              recv_sem=recv_sems.at[k], device_id=peer, device_id_type=MESH)
                 for k, peer in enumerate(peers)]
        for cp in sends:
            cp.start()
        for k, (px, py, pc) in enumerate(peers):
            pltpu.make_async_remote_copy(src_ref=x_ref, dst_ref=buf.at[4 * px + 2 * py + pc],
                                         send_sem=send_sems.at[k], recv_sem=recv_sems.at[k],
                                         device_id=(px, py, pc), device_id_type=MESH).wait_recv()
        for cp in sends:
            cp.wait_send()
        acc = buf[0]
        for dv in range(1, N_DEV):
            acc = acc + buf[dv]
        o_ref[...] = acc

    vm = pl.BlockSpec(memory_space=pltpu.VMEM)
    return pl.pallas_call(
        body, out_shape=jax.ShapeDtypeStruct((r, n), F32), in_specs=[vm], out_specs=vm,
        scratch_shapes=[pltpu.VMEM((N_DEV, r, n), F32), pltpu.SemaphoreType.DMA((7,)), pltpu.SemaphoreType.DMA((7,))],
        name="all_reduce_small", compiler_params=pltpu.CompilerParams(vmem_limit_bytes=VMEM_LIMIT))(xs)


def _pack_rows(arrs, quantum):
    parts, sizes = [], []
    for a in arrs:
        f = a.reshape(-1)
        pad = (-f.shape[0]) % quantum
        if pad:
            f = jnp.pad(f, (0, pad))
        parts.append(f)
        sizes.append(f.shape[0])
    return jnp.concatenate(parts).reshape(-1, LANE), sizes


def _unpack_rows(buf, sizes, shapes):
    flat = buf.reshape(-1)
    out, off = [], 0
    for sz, sh in zip(sizes, shapes):
        n = math.prod(sh)
        out.append(flat[off:off + n].reshape(sh))
        off += sz
    return out


def kernel(x, meta_tokens, mix_norm_g, w_in, conv_dw, conv_b, conv_ln_g, conv_ln_b, w_conv_out, gqa_q_norm_g, gqa_k_norm_g, w_gqa_out, mla_q_norm_g, w_mla_uq, mla_kv_norm_g, w_mla_ukv, w_mla_out, gate_b, w_out, ffn_norm_g, w_ffn_gate, w_ffn_up, w_ffn_down, final_norm_g, loss_target, m_meta_tokens, m_mix_norm_g, m_w_in, m_conv_dw, m_conv_b, m_conv_ln_g, m_conv_ln_b, m_w_conv_out, m_gqa_q_norm_g, m_gqa_k_norm_g, m_w_gqa_out, m_mla_q_norm_g, m_w_mla_uq, m_mla_kv_norm_g, m_w_mla_ukv, m_w_mla_out, m_gate_b, m_w_out, m_ffn_norm_g, m_w_ffn_gate, m_w_ffn_up, m_w_ffn_down, m_final_norm_g, v_meta_tokens, v_mix_norm_g, v_w_in, v_conv_dw, v_conv_b, v_conv_ln_g, v_conv_ln_b, v_w_conv_out, v_gqa_q_norm_g, v_gqa_k_norm_g, v_w_gqa_out, v_mla_q_norm_g, v_w_mla_uq, v_mla_kv_norm_g, v_w_mla_ukv, v_w_mla_out, v_gate_b, v_w_out, v_ffn_norm_g, v_w_ffn_gate, v_w_ffn_up, v_w_ffn_down, v_final_norm_g):
    w = dict(meta_tokens=meta_tokens, mix_norm_g=mix_norm_g, w_in=w_in, conv_dw=conv_dw, conv_b=conv_b,
             conv_ln_g=conv_ln_g, conv_ln_b=conv_ln_b, w_conv_out=w_conv_out, gqa_q_norm_g=gqa_q_norm_g,
             gqa_k_norm_g=gqa_k_norm_g, w_gqa_out=w_gqa_out, mla_q_norm_g=mla_q_norm_g, w_mla_uq=w_mla_uq,
             mla_kv_norm_g=mla_kv_norm_g, w_mla_ukv=w_mla_ukv, w_mla_out=w_mla_out, gate_b=gate_b, w_out=w_out,
             ffn_norm_g=ffn_norm_g, w_ffn_gate=w_ffn_gate, w_ffn_up=w_ffn_up, w_ffn_down=w_ffn_down,
             final_norm_g=final_norm_g)
    mom1 = dict(zip(W_NAMES, (m_meta_tokens, m_mix_norm_g, m_w_in, m_conv_dw, m_conv_b, m_conv_ln_g, m_conv_ln_b, m_w_conv_out, m_gqa_q_norm_g, m_gqa_k_norm_g, m_w_gqa_out, m_mla_q_norm_g, m_w_mla_uq, m_mla_kv_norm_g, m_w_mla_ukv, m_w_mla_out, m_gate_b, m_w_out, m_ffn_norm_g, m_w_ffn_gate, m_w_ffn_up, m_w_ffn_down, m_final_norm_g)))
    mom2 = dict(zip(W_NAMES, (v_meta_tokens, v_mix_norm_g, v_w_in, v_conv_dw, v_conv_b, v_conv_ln_g, v_conv_ln_b, v_w_conv_out, v_gqa_q_norm_g, v_gqa_k_norm_g, v_w_gqa_out, v_mla_q_norm_g, v_w_mla_uq, v_mla_kv_norm_g, v_w_mla_ukv, v_w_mla_out, v_gate_b, v_w_out, v_ffn_norm_g, v_w_ffn_gate, v_w_ffn_up, v_w_ffn_down, v_final_norm_g)))

    nb, seq, d = x.shape
    depth = w_in.shape[0]
    c_conv = conv_b.shape[1]
    kw = conv_dw.shape[1]
    hd = HEAD_DIM
    hq = w_gqa_out.shape[1] // hd
    nh = w_mla_out.shape[1] // hd
    qr, kvr = mla_q_norm_g.shape[1], mla_kv_norm_g.shape[1]
    d_in = w_in.shape[2] * N_DEV
    hkv = (d_in - 2 * c_conv - hq * hd - qr - kvr - MLA_ROPE_DIM - N_BRANCH * d) // (2 * hd)
    l_valid = seq + N_META
    lp = -(-(l_valid + kw // 2) // LANE) * LANE
    m_tok = nb * lp
    mla_qk = hd + MLA_ROPE_DIM

    off_gate = 0
    off_conv = N_BRANCH * d
    off_q = off_conv + 2 * c_conv
    off_cq = off_q + (hq + 2 * hkv) * hd
    off_ckv = off_cq + qr
    off_kpe = off_ckv + kvr
    n_proj = off_kpe + LANE

    x_id, y_id, c_id = _place()
    me = 4 * x_id + 2 * y_id + c_id

    tm_wide = _tile(lp, 272, 16)
    tm_mid = _tile(lp, 544, 16)

    sm_buf, sm_sizes = _pack_rows([meta_tokens, conv_dw], 8 * LANE)
    sm_all = _all_gather([sm_buf], True, "ag_small")[0]
    sm_all = sm_all.reshape(N_DEV, -1)
    o0 = 0
    meta_all = sm_all[:, o0:o0 + meta_tokens.size].reshape((N_DEV,) + meta_tokens.shape)
    meta_full = jnp.transpose(meta_all, (1, 0, 2)).reshape(N_META, d)
    o0 = sm_sizes[0]
    dw_all = sm_all[:, o0:o0 + conv_dw.size].reshape((N_DEV, depth, kw, c_conv // N_DEV))
    dw_full = jnp.transpose(dw_all, (1, 2, 0, 3)).reshape(depth, kw, c_conv)

    n_head_cols = d_in - N_BRANCH * d - MLA_ROPE_DIM
    rb = MLA_ROPE_DIM
    nb_gate, nb_head = N_BRANCH * d // rb, n_head_cols // rb
    assert n_head_cols % rb == 0 and (N_BRANCH * d) % rb == 0 and n_proj == (nb_gate + nb_head + 2) * rb

    def in_src(i):
        return jnp.where(i < nb_gate, nb_head + 1 + i,
                         jnp.where(i < nb_gate + nb_head, i - nb_gate, jnp.where(i == nb_gate + nb_head, nb_head, 0)))

    def in_dst(i):
        return jnp.where(i < nb_head, nb_gate + i, jnp.where(i == nb_head, nb_gate + nb_head, i - nb_head - 1))

    def uq_src(i):
        j = i - 2 * nh
        return jnp.where(i < 2 * nh, 3 * (i // 2) + i % 2, 3 * (j // 2) + 2)

    def uq_dst(i):
        return jnp.where(i % 3 < 2, 2 * (i // 3) + i % 3, 2 * nh + 2 * (i // 3))

    ag_pending, tokens = [], []
    for i in range(depth):
        shards = [_pack_pairs_xla(w[n][i] if n in ROW_SHARDED else w[n][i].T) for n in BIG]
        bufs = [lax.dynamic_update_slice(lax.empty((N_DEV,) + a.shape, F32), a[None], (me, 0, 0)) for a in shards]
        *pend, tok = _ici_start(bufs, [], tokens[-1:], gather=True, name="ag_ici_start_%d" % i)
        ag_pending.append(pend)
        tokens.append(tok)

    def all_gather_layer(i, after):
        bufs, _ = _ici_wait(*ag_pending[i], after, gather=True, name="ag_ici_wait_%d" % i)
        gath = _gather_to_sibling(bufs)
        pk = {n: g.reshape(-1, g.shape[-1]) for n, g in zip(BIG, gath)}
        wl = {n: _unpack(pk[n], name="unpack_" + n) for n in BIG if n not in ('w_in', 'w_mla_uq', 'w_mla_ukv')}
        wl['w_in'] = _unpack(pk['w_in'], name="unpack_w_in", out_rows=n_proj, tr=rb, src_fn=in_src,
                             zero_fn=lambda i: i == nb_gate + nb_head + 1)
        wl['w_mla_uq'] = _unpack(pk['w_mla_uq'], name="unpack_w_mla_uq", out_rows=2 * nh * hd, tr=rb, src_fn=uq_src,
                                 zero_fn=lambda i: (i >= 2 * nh) & ((i - 2 * nh) % 2 == 1))
        wl['w_mla_ukv'] = _unpack(pk['w_mla_ukv'], name="unpack_w_mla_ukv", out_rows=2 * nh * hd, tr=hd,
                                  src_fn=lambda i: jnp.where(i < nh, 2 * i, 2 * (i - nh) + 1))
        return wl

    full = [None] * depth

    cos_g, sin_g = _rope_tables(lp, l_valid, hd // 4)
    cos_m, sin_m = _rope_tables(lp, l_valid, MLA_ROPE_DIM // 4)

    def vec2(a):
        return a.reshape(1, -1)

    meta_b = jnp.broadcast_to(meta_full[None], (nb, N_META, d))
    h = jnp.concatenate([meta_b, x, jnp.zeros((nb, lp - l_valid, d), F32)], axis=1).reshape(m_tok, d)
    tgt = jnp.pad(loss_target, ((0, 0), (N_META, lp - l_valid), (0, 0))).reshape(m_tok, d)

    saved = []
    for i in range(depth):
        s = {}
        s['h_in'] = h
        u = _rms_fwd(h, d, 0, vec2(mix_norm_g[i]), tm_wide, "mix_norm_fwd")
        wl = full[i] = all_gather_layer(i, tokens[-1:] if i == 0 else [h])
        proj = _matmul(u, wl['w_in'], mode="nt", out_dtype=F32, name="mm_in")
        cc = _conv_fwd(proj, dw_full[i], vec2(conv_b[i]), c_conv=c_conv, off_a=off_conv, nb=nb, lp=lp, l_valid=l_valid)
        sc = _ln_silu_fwd(cc, vec2(conv_ln_g[i]), vec2(conv_ln_b[i]), tm_mid)
        ya = _matmul(sc, wl['w_conv_out'], mode="nt", out_dtype=F32, name="mm_conv_out")
        qh, kh, vh = _gqa_prep_fwd(proj, vec2(gqa_q_norm_g[i]), vec2(gqa_k_norm_g[i]), cos_g, sin_g,
                                   off_q=off_q, hq=hq, hkv=hkv, nb=nb, lp=lp, tm=tm_mid)
        ob = _attn_fwd(qh, kh, vh, scale=1.0 / math.sqrt(hd), l_valid=l_valid, name="gqa_attn_fwd")
        yb = _matmul(ob, wl['w_gqa_out'], mode="nt", out_dtype=F32, name="mm_gqa_out")
        cqn = _rms_fwd(proj, qr, _cb(off_cq, qr), vec2(mla_q_norm_g[i]), tm_mid, "mla_q_norm_fwd")
        ckvn = _rms_fwd(proj, kvr, _cb(off_ckv, kvr), vec2(mla_kv_norm_g[i]), tm_mid, "mla_kv_norm_fwd")
        qm = _matmul(cqn, wl['w_mla_uq'], mode="nt", out_dtype=F32, name="mm_mla_uq")
        kvm = _matmul(ckvn, wl['w_mla_ukv'], mode="nt", out_dtype=F32, name="mm_mla_ukv")
        qc, kc, vc = _mla_prep_fwd(qm, kvm, proj, cos_m, sin_m, off_kpe=off_kpe, nh=nh, nb=nb, lp=lp, tm=tm_mid)
        oc = _attn_fwd(qc, kc, vc, scale=1.0 / math.sqrt(mla_qk), l_valid=l_valid, name="mla_attn_fwd")
        yc = _matmul(oc, wl['w_mla_out'], mode="nt", out_dtype=F32, name="mm_mla_out")
        merged = _gate_fwd(proj, vec2(gate_b[i]), ya, yb, yc, d=d, tm=tm_wide)
        h2 = _matmul(merged, wl['w_out'], mode="nn", out_dtype=F32, name="mm_out", residual=h)
        vn = _rms_fwd(h2, d, 0, vec2(ffn_norm_g[i]), tm_wide, "ffn_norm_fwd")
        gt = _matmul(vn, wl['w_ffn_gate'], mode="nt", out_dtype=F32, name="mm_ffn_gate")
        up = _matmul(vn, wl['w_ffn_up'], mode="nt", out_dtype=F32, name="mm_ffn_up")
        act = _swiglu_fwd(gt, up, tm_wide)
        h = _matmul(act, wl['w_ffn_down'], mode="nn", out_dtype=F32, name="mm_ffn_down", residual=h2)
        s.update(u=u, proj=proj, cc=cc, sc=sc, ya=ya, qh=qh, kh=kh, vh=vh, ob=ob, yb=yb, cqn=cqn, ckvn=ckvn,
                 qc=qc, kc=kc, vc=vc, oc=oc, yc=yc, merged=merged, h2=h2, vn=vn, gt=gt, up=up, act=act)
        saved.append(s)

    dh, dg_final, loss_part = _loss_head(h, vec2(final_norm_g), tgt, lp=lp, l_valid=l_valid, tm=tm_wide)

    gsmall = {n: [None] * depth for n in SMALL if n != 'final_norm_g'}
    gdw = [None] * depth
    gbig = [None] * depth
    cidx = jnp.reshape(c_id, (1,)).astype(jnp.int32)
    chip_idx = jnp.reshape(2 * x_id + y_id, (1,)).astype(jnp.int32)

    def reduce_scatter_start(gl, i):
        g4s = [gl[n].reshape(N_DEV // 2, 2, -1, gl[n].shape[-1]) for n in BIG]
        from_sibling = _swap_with_sibling(g4s)
        p4s = [_add_sibling(g, r, cidx, "rs_add_sibling_" + n) for n, g, r in zip(BIG, g4s, from_sibling)]
        *pend, tok = _ici_start(p4s, [(3,) + p.shape[1:] for p in p4s], [], gather=False,
                                name="rs_ici_start_%d" % i)
        return pend, tok

    def reduce_scatter_finish(pend, after, i):
        p4s, from_chips = _ici_wait(*pend, after, gather=False, name="rs_ici_wait_%d" % i)
        out = {}
        for n, p, r in zip(BIG, p4s, from_chips):
            g = _add_chips(p, r, chip_idx, "rs_add_chips_" + n)
            out[n] = g if n in ROW_SHARDED else g.T
        return out

    rs_pending, rs_token = None, []
    for i in reversed(range(depth)):
        s = saved[i]
        proj = s['proj']
        wl = full[i]
        gl = {}
        dact = _matmul(dh, wl['w_ffn_down'], mode="nt", out_dtype=F32, name="mm_ffn_down_dx", deps=rs_token)
        gl['w_ffn_down'] = _matmul_tn_packed(s['act'], dh, "mm_ffn_down_dw")
        dgt, dup = _swiglu_bwd(dact, s['gt'], s['up'], tm_wide)
        dvn = _matmul(dgt, wl['w_ffn_gate'], mode="nn", out_dtype=F32, name="mm_ffn_gate_dx")
        dvn = _matmul(dup, wl['w_ffn_up'], mode="nn", out_dtype=F32, name="mm_ffn_up_dx", residual=dvn)
        gl['w_ffn_gate'] = _matmul_tn_packed(dgt, s['vn'], "mm_ffn_gate_dw")
        gl['w_ffn_up'] = _matmul_tn_packed(dup, s['vn'], "mm_ffn_up_dw")
        dh2, gsmall['ffn_norm_g'][i] = _rms_bwd(s['h2'], d, 0, vec2(ffn_norm_g[i]), dvn, tm_wide, F32,
                                                "ffn_norm_bwd", add=dh)
        dmg = _matmul(dh2, wl['w_out'], mode="nt", out_dtype=F32, name="mm_out_dx")
        gl['w_out'] = _matmul_tn_packed(s['merged'], dh2, "mm_out_dw")
        dya, dyb, dyc, dg0, dg1, dg2, db0, db1, db2 = _gate_bwd(dmg, proj, vec2(gate_b[i]), s['ya'], s['yb'],
                                                                s['yc'], d=d, tm=tm_wide)
        gsmall['gate_b'][i] = jnp.concatenate([db0, db1, db2], axis=1)
        dsc = _matmul(dya, wl['w_conv_out'], mode="nn", out_dtype=F32, name="mm_conv_out_dx")
        gl['w_conv_out'] = _matmul_tn_packed(dya, s['sc'], "mm_conv_out_dw")
        dcc, gsmall['conv_ln_g'][i], gsmall['conv_ln_b'][i] = _ln_silu_bwd(
            s['cc'], vec2(conv_ln_g[i]), vec2(conv_ln_b[i]), dsc, tm_mid)
        da, dgc, gdw[i], gsmall['conv_b'][i] = _conv_bwd(proj, dcc, dw_full[i], c_conv=c_conv, off_a=off_conv,
                                                         nb=nb, lp=lp, l_valid=l_valid)
        dob = _matmul(dyb, wl['w_gqa_out'], mode="nn", out_dtype=BF16, name="mm_gqa_out_dx")
        gl['w_gqa_out'] = _matmul_tn_packed(dyb, s['ob'], "mm_gqa_out_dw")
        dqh, dkh, dvh = _attn_bwd(s['qh'], s['kh'], s['vh'], dob, scale=1.0 / math.sqrt(hd), l_valid=l_valid,
                                  name="gqa_attn_bwd")
        dq, dkv, gsmall['gqa_q_norm_g'][i], gsmall['gqa_k_norm_g'][i] = _gqa_prep_bwd(
            proj, vec2(gqa_q_norm_g[i]), vec2(gqa_k_norm_g[i]), cos_g, sin_g, dqh, dkh, dvh,
            off_q=off_q, hq=hq, hkv=hkv, nb=nb, lp=lp, tm=tm_mid)
        doc = _matmul(dyc, wl['w_mla_out'], mode="nn", out_dtype=BF16, name="mm_mla_out_dx")
        gl['w_mla_out'] = _matmul_tn_packed(dyc, s['oc'], "mm_mla_out_dw")
        dqc, dkc, dvc = _attn_bwd(s['qc'], s['kc'], s['vc'], doc, scale=1.0 / math.sqrt(mla_qk), l_valid=l_valid,
                                  name="mla_attn_bwd")
        dqm, dkvm, dkpe = _mla_prep_bwd(dqc, dkc, dvc, cos_m, sin_m, nh=nh, nb=nb, lp=lp, tm=tm_mid)
        dcqn = _matmul(dqm, wl['w_mla_uq'], mode="nn", out_dtype=F32, name="mm_mla_uq_dx")
        guq = _matmul_tn_packed(dqm, s['cqn'], "mm_mla_uq_dw")
        dckvn = _matmul(dkvm, wl['w_mla_ukv'], mode="nn", out_dtype=F32, name="mm_mla_ukv_dx")
        gukv = _matmul_tn_packed(dkvm, s['ckvn'], "mm_mla_ukv_dw")
        dcq, gsmall['mla_q_norm_g'][i] = _rms_bwd(proj, qr, _cb(off_cq, qr), vec2(mla_q_norm_g[i]), dcqn, tm_mid,
                                                  BF16, "mla_q_norm_bwd")
        dckv, gsmall['mla_kv_norm_g'][i] = _rms_bwd(proj, kvr, _cb(off_ckv, kvr), vec2(mla_kv_norm_g[i]), dckvn,
                                                    tm_mid, BF16, "mla_kv_norm_bwd")
        gl['w_mla_uq'] = _gather_rows(guq, name="perm_mla_uq_dw", out_rows=nh * mla_qk, tr=rb, src_fn=uq_dst)
        gl['w_mla_ukv'] = _gather_rows(gukv, name="perm_mla_ukv_dw", out_rows=2 * nh * hd, tr=hd,
                                       src_fn=lambda i: i // 2 + nh * (i % 2))
        dproj = jnp.concatenate([dg0, dg1, dg2, da, dgc, dq, dkv, dcq, dckv, dkpe], axis=1)
        du = _matmul(dproj, wl['w_in'], mode="nn", out_dtype=F32, name="mm_in_dx")
        gin = _matmul_tn_packed(dproj, s['u'], "mm_in_dw")
        gl['w_in'] = _gather_rows(gin, name="perm_in_dw", out_rows=d_in, tr=rb, src_fn=in_dst)
        dh, gsmall['mix_norm_g'][i] = _rms_bwd(s['h_in'], d, 0, vec2(mix_norm_g[i]), du, tm_wide, F32,
                                               "mix_norm_bwd", add=dh2)
        started, tok = reduce_scatter_start(gl, i)
        if rs_pending is not None:
            gbig[i + 1] = reduce_scatter_finish(rs_pending, [dh, tok], i + 1)
        rs_pending, rs_token = started, [tok]
    gbig[0] = reduce_scatter_finish(rs_pending, rs_token, 0)

    dh0 = dh.reshape(nb, lp, d)
    grad_x = dh0[:, N_META:l_valid]
    gmeta_full = jnp.sum(dh0[:, :N_META], axis=0)
    grads = {n: jnp.stack([gbig[i][n] for i in range(depth)]) for n in BIG}

    small_list = [jnp.stack(gsmall[n]).reshape(w[n].shape) for n in SMALL if n != 'final_norm_g']
    small_list += [dg_final.reshape(final_norm_g.shape), gmeta_full, jnp.stack(gdw), loss_part]
    ar_buf, ar_sizes = _pack_rows(small_list, 8 * LANE)
    ar = _all_reduce_small(ar_buf)
    ar_shapes = [w[n].shape for n in SMALL] + [(N_META, d), (depth, kw, c_conv), (1, LANE)]
    ar_out = _unpack_rows(ar, ar_sizes, ar_shapes)
    for n, g in zip(SMALL, ar_out):
        grads[n] = g
    dcol = d // N_DEV
    grads['meta_tokens'] = lax.dynamic_slice(ar_out[len(SMALL)], (0, me * dcol), (N_META, dcol))
    ccol = c_conv // N_DEV
    grads['conv_dw'] = lax.dynamic_slice(ar_out[len(SMALL) + 1], (0, 0, me * ccol),
                                         (depth, kw, ccol)).reshape(conv_dw.shape)
    loss = ar_out[len(SMALL) + 2][0, 0]

    delta, new_m, new_v = {}, {}, {}
    for n in BIG:
        sh = w[n].shape
        two_d = (sh[0] * sh[1], sh[2])
        dl, mn, vn_ = _adamw(w[n].reshape(two_d), grads[n].reshape(two_d), mom1[n].reshape(two_d),
                             mom2[n].reshape(two_d), "adamw_" + n)
        delta[n], new_m[n], new_v[n] = dl.reshape(sh), mn.reshape(sh), vn_.reshape(sh)
    rest = [n for n in W_NAMES if n not in BIG]
    pw, psz = _pack_rows([w[n] for n in rest], 8 * LANE)
    pg, _ = _pack_rows([grads[n] for n in rest], 8 * LANE)
    pm, _ = _pack_rows([mom1[n] for n in rest], 8 * LANE)
    pv, _ = _pack_rows([mom2[n] for n in rest], 8 * LANE)
    dl, mn, vn_ = _adamw(pw, pg, pm, pv, "adamw_small")
    shapes = [w[n].shape for n in rest]
    for n, a, b, c in zip(rest, _unpack_rows(dl, psz, shapes), _unpack_rows(mn, psz, shapes),
                          _unpack_rows(vn_, psz, shapes)):
        delta[n], new_m[n], new_v[n] = a, b, c

    return (loss, grad_x, *[grads[n] for n in W_NAMES], *[delta[n] for n in W_NAMES],
            *[new_m[n] for n in W_NAMES], *[new_v[n] for n in W_NAMES])
```

```python
import functools
import math

import jax
import jax.numpy as jnp
from jax import lax
from jax.experimental import pallas as pl
from jax.experimental.pallas import tpu as pltpu

F32 = jnp.float32
BF16 = jnp.bfloat16

N_META = 16
GRID_W = 64
ROPE_THETA = 10000.0
NORM_EPS = 1e-6
HEAD_DIM = 128
MLA_ROPE_DIM = 64
N_BRANCH = 3
N_DEV = 8
LANE = 128
VMEM_LIMIT = 56 * 1024 * 1024

ADAM_LR = 0.001
ADAM_B1 = 0.9
ADAM_B2 = 0.999
ADAM_EPS = 1e-08
ADAM_WD = 0.01
ADAM_STEP = 10

W_NAMES = ['meta_tokens', 'mix_norm_g', 'w_in', 'conv_dw', 'conv_b', 'conv_ln_g', 'conv_ln_b', 'w_conv_out',
           'gqa_q_norm_g', 'gqa_k_norm_g', 'w_gqa_out', 'mla_q_norm_g', 'w_mla_uq', 'mla_kv_norm_g', 'w_mla_ukv',
           'w_mla_out', 'gate_b', 'w_out', 'ffn_norm_g', 'w_ffn_gate', 'w_ffn_up', 'w_ffn_down', 'final_norm_g']
BIG = ['w_in', 'w_conv_out', 'w_gqa_out', 'w_mla_uq', 'w_mla_ukv', 'w_mla_out', 'w_out', 'w_ffn_gate', 'w_ffn_up',
       'w_ffn_down']
ROW_SHARDED = ('w_out', 'w_ffn_down')
SMALL = ['mix_norm_g', 'conv_b', 'conv_ln_g', 'conv_ln_b', 'gqa_q_norm_g', 'gqa_k_norm_g', 'mla_q_norm_g',
         'mla_kv_norm_g', 'gate_b', 'ffn_norm_g', 'final_norm_g']

NT = (((1,), (1,)), ((), ()))
TN = (((0,), (0,)), ((), ()))
NN = (((1,), (0,)), ((), ()))


def _tile(n, target, mult):
    best = None
    for t in range(mult, min(n, target) + 1, mult):
        if n % t == 0:
            best = t
    assert best is not None, (n, target, mult)
    return best


def _cp(n):
    return pltpu.CompilerParams(dimension_semantics=("arbitrary",) * n, vmem_limit_bytes=VMEM_LIMIT)


def _sig(x):
    return jax.nn.sigmoid(x)


def _cb(off, w):
    assert off % w == 0, (off, w)
    return off // w


def _matmul(a, b, *, mode, out_dtype, name, residual=None, deps=()):
    if mode == "nn":
        (m, k), (k2, n) = a.shape, b.shape
    elif mode == "nt":
        (m, k), (n, k2) = a.shape, b.shape
    else:
        (k, m), (k2, n) = a.shape, b.shape
    assert k == k2, (a.shape, b.shape, mode)
    tm = _tile(m, 1088, 128 if mode == "tn" else 16)
    tn = _tile(n, 1024, 128)
    tk = _tile(k, 2176, 128)
    nk = k // tk
    dims = {"nn": NN, "nt": NT, "tn": TN}[mode]
    a_spec = (pl.BlockSpec((tk, tm), lambda i, j, kk: (kk, i)) if mode == "tn"
              else pl.BlockSpec((tm, tk), lambda i, j, kk: (i, kk)))
    b_spec = (pl.BlockSpec((tn, tk), lambda i, j, kk: (j, kk)) if mode == "nt"
              else pl.BlockSpec((tk, tn), lambda i, j, kk: (kk, j)))
    o_spec = pl.BlockSpec((tm, tn), lambda i, j, kk: (i, j))
    has_res = residual is not None

    nd = len(deps)

    def body(*refs):
        refs = refs[:len(refs) - 2 - nd] + refs[len(refs) - 2:]
        if has_res:
            a_ref, b_ref, r_ref, o_ref, acc = refs
        else:
            a_ref, b_ref, o_ref, acc = refs
        kk = pl.program_id(2)

        @pl.when(kk == 0)
        def _():
            acc[...] = jnp.zeros_like(acc)

        acc[...] += lax.dot_general(a_ref[...].astype(BF16), b_ref[...].astype(BF16), dims,
                                    preferred_element_type=F32)

        @pl.when(kk == nk - 1)
        def _():
            r = acc[...]
            if has_res:
                r = r + r_ref[...]
            o_ref[...] = r.astype(o_ref.dtype)

    ins = [a, b] + ([residual] if has_res else []) + list(deps)
    in_specs = [a_spec, b_spec] + ([o_spec] if has_res else []) + [pl.BlockSpec(memory_space=pl.ANY)] * nd
    return pl.pallas_call(
        body, grid=(m // tm, n // tn, nk), in_specs=in_specs, out_specs=o_spec,
        out_shape=jax.ShapeDtypeStruct((m, n), out_dtype), scratch_shapes=[pltpu.VMEM((tm, tn), F32)],
        name=name, compiler_params=_cp(3))(*ins)


def _rms_fwd(x, w, cb, g, tm, name):
    m = x.shape[0]

    def body(x_ref, g_ref, o_ref):
        xv = x_ref[...]
        r = lax.rsqrt(jnp.mean(xv * xv, axis=-1, keepdims=True) + NORM_EPS)
        o_ref[...] = (xv * r * g_ref[...]).astype(o_ref.dtype)

    return pl.pallas_call(
        body, grid=(m // tm,),
        in_specs=[pl.BlockSpec((tm, w), lambda i: (i, cb)), pl.BlockSpec((1, w), lambda i: (0, 0))],
        out_specs=pl.BlockSpec((tm, w), lambda i: (i, 0)), out_shape=jax.ShapeDtypeStruct((m, w), BF16),
        name=name, compiler_params=_cp(1))(x, g)


def _rms_bwd(x, w, cb, g, dy, tm, out_dtype, name, add=None):
    m = x.shape[0]
    has_add = add is not None

    def body(*refs):
        if has_add:
            x_ref, g_ref, dy_ref, add_ref, dx_ref, dg_ref = refs
        else:
            x_ref, g_ref, dy_ref, dx_ref, dg_ref = refs
        xv = x_ref[...]
        dyv = dy_ref[...].astype(F32)
        r = lax.rsqrt(jnp.mean(xv * xv, axis=-1, keepdims=True) + NORM_EPS)
        t = dyv * g_ref[...]
        dx = r * t - xv * (r * r * r) * jnp.mean(t * xv, axis=-1, keepdims=True)
        if has_add:
            dx = dx + add_ref[...]
        dx_ref[...] = dx.astype(dx_ref.dtype)

        @pl.when(pl.program_id(0) == 0)
        def _():
            dg_ref[...] = jnp.zeros_like(dg_ref)

        dg_ref[...] += jnp.sum(dyv * xv * r, axis=0, keepdims=True)

    row = pl.BlockSpec((tm, w), lambda i: (i, 0))
    vec = pl.BlockSpec((1, w), lambda i: (0, 0))
    ins = [x, g, dy] + ([add] if has_add else [])
    in_specs = [pl.BlockSpec((tm, w), lambda i: (i, cb)), vec, row] + ([row] if has_add else [])
    return pl.pallas_call(
        body, grid=(m // tm,), in_specs=in_specs, out_specs=(row, vec),
        out_shape=(jax.ShapeDtypeStruct((m, w), out_dtype), jax.ShapeDtypeStruct((1, w), F32)),
        name=name, compiler_params=_cp(1))(*ins)


def _ln_silu_fwd(c, lg, lb, tm):
    m, w = c.shape

    def body(c_ref, g_ref, b_ref, o_ref):
        cv = c_ref[...]
        xc = cv - jnp.mean(cv, axis=-1, keepdims=True)
        r = lax.rsqrt(jnp.mean(xc * xc, axis=-1, keepdims=True) + NORM_EPS)
        yl = xc * r * g_ref[...] + b_ref[...]
        o_ref[...] = (yl * _sig(yl)).astype(o_ref.dtype)

    row = pl.BlockSpec((tm, w), lambda i: (i, 0))
    vec = pl.BlockSpec((1, w), lambda i: (0, 0))
    return pl.pallas_call(body, grid=(m // tm,), in_specs=[row, vec, vec], out_specs=row,
                          out_shape=jax.ShapeDtypeStruct((m, w), BF16), name="ln_silu_fwd",
                          compiler_params=_cp(1))(c, lg, lb)


def _ln_silu_bwd(c, lg, lb, ds, tm):
    m, w = c.shape

    def body(c_ref, g_ref, b_ref, ds_ref, dc_ref, dg_ref, db_ref):
        cv = c_ref[...]
        xc = cv - jnp.mean(cv, axis=-1, keepdims=True)
        r = lax.rsqrt(jnp.mean(xc * xc, axis=-1, keepdims=True) + NORM_EPS)
        nv = xc * r
        yl = nv * g_ref[...] + b_ref[...]
        sg = _sig(yl)
        dyl = ds_ref[...] * (sg * (1.0 + yl * (1.0 - sg)))
        dn = dyl * g_ref[...]
        dc = r * (dn - jnp.mean(dn, axis=-1, keepdims=True) - nv * jnp.mean(dn * nv, axis=-1, keepdims=True))
        dc_ref[...] = dc

        @pl.when(pl.program_id(0) == 0)
        def _():
            dg_ref[...] = jnp.zeros_like(dg_ref)
            db_ref[...] = jnp.zeros_like(db_ref)

        dg_ref[...] += jnp.sum(dyl * nv, axis=0, keepdims=True)
        db_ref[...] += jnp.sum(dyl, axis=0, keepdims=True)

    row = pl.BlockSpec((tm, w), lambda i: (i, 0))
    vec = pl.BlockSpec((1, w), lambda i: (0, 0))
    return pl.pallas_call(
        body, grid=(m // tm,), in_specs=[row, vec, vec, row], out_specs=(row, vec, vec),
        out_shape=(jax.ShapeDtypeStruct((m, w), F32), jax.ShapeDtypeStruct((1, w), F32),
                   jax.ShapeDtypeStruct((1, w), F32)),
        name="ln_silu_bwd", compiler_params=_cp(1))(c, lg, lb, ds)


CONV_MARGIN = 16
CONV_ROWS = 128


def _conv_fwd(proj, dw, bias, *, c_conv, off_a, nb, lp, l_valid):
    kw = dw.shape[0]
    half = kw // 2
    cw = LANE
    mg = CONV_MARGIN
    assert half <= mg - 1 and lp % CONV_ROWS == 0

    def body(a_ref, g_ref, w_ref, b_ref, c_ref, zp):
        t = lax.broadcasted_iota(jnp.int32, (lp, cw), 0)
        z = jnp.where(t < l_valid, a_ref[...] * _sig(g_ref[...]), 0.0)
        zp[0:mg, :] = jnp.zeros((mg, cw), F32)
        zp[mg + lp:mg + lp + mg, :] = jnp.zeros((mg, cw), F32)
        zp[mg:mg + lp, :] = z
        for r0 in range(0, lp, CONV_ROWS):
            acc = jnp.zeros((CONV_ROWS, cw), F32) + b_ref[...]
            for k in range(kw):
                s0 = mg + r0 + k - half
                acc = acc + w_ref[k:k + 1, :] * zp[s0:s0 + CONV_ROWS, :]
            c_ref[r0:r0 + CONV_ROWS, :] = acc

    ja, jg = _cb(off_a, cw), _cb(off_a + c_conv, cw)
    return pl.pallas_call(
        body, grid=(nb, c_conv // cw),
        in_specs=[pl.BlockSpec((lp, cw), lambda b, j: (b, ja + j)), pl.BlockSpec((lp, cw), lambda b, j: (b, jg + j)),
                  pl.BlockSpec((kw, cw), lambda b, j: (0, j)), pl.BlockSpec((1, cw), lambda b, j: (0, j))],
        out_specs=pl.BlockSpec((lp, cw), lambda b, j: (b, j)),
        out_shape=jax.ShapeDtypeStruct((nb * lp, c_conv), F32),
        scratch_shapes=[pltpu.VMEM((lp + 2 * mg, cw), F32)], name="conv_fwd", compiler_params=_cp(2))(proj, proj, dw, bias)


def _conv_bwd(proj, dc, dw, *, c_conv, off_a, nb, lp, l_valid):
    kw = dw.shape[0]
    half = kw // 2
    cw = LANE
    mg = CONV_MARGIN

    def body(a_ref, g_ref, dc_ref, w_ref, da_ref, dg_ref, ddw_ref, dcb_ref, zp, dcp):
        t = lax.broadcasted_iota(jnp.int32, (lp, cw), 0)
        z = jnp.where(t < l_valid, a_ref[...] * _sig(g_ref[...]), 0.0)
        for buf in (zp, dcp):
            buf[0:mg, :] = jnp.zeros((mg, cw), F32)
            buf[mg + lp:mg + lp + mg, :] = jnp.zeros((mg, cw), F32)
        zp[mg:mg + lp, :] = z
        dcv = dc_ref[...]
        dcp[mg:mg + lp, :] = dcv

        @pl.when(pl.program_id(1) == 0)
        def _():
            ddw_ref[...] = jnp.zeros_like(ddw_ref)
            dcb_ref[...] = jnp.zeros_like(dcb_ref)

        dcb_ref[...] += jnp.sum(dcv, axis=0, keepdims=True)
        for r0 in range(0, lp, CONV_ROWS):
            acc = jnp.zeros((CONV_ROWS, cw), F32)
            for k in range(kw):
                s0 = mg + r0 - k + half
                acc = acc + w_ref[k:k + 1, :] * dcp[s0:s0 + CONV_ROWS, :]
            tc = lax.broadcasted_iota(jnp.int32, (CONV_ROWS, cw), 0) + r0
            dz = jnp.where(tc < l_valid, acc, 0.0)
            sg = _sig(g_ref[r0:r0 + CONV_ROWS, :])
            da_ref[r0:r0 + CONV_ROWS, :] = (dz * sg).astype(da_ref.dtype)
            dg_ref[r0:r0 + CONV_ROWS, :] = (dz * a_ref[r0:r0 + CONV_ROWS, :] * sg * (1.0 - sg)).astype(dg_ref.dtype)
        for k in range(kw):
            acc = jnp.zeros((CONV_ROWS, cw), F32)
            for r0 in range(0, lp, CONV_ROWS):
                s0 = mg + r0 + k - half
                acc = acc + dc_ref[r0:r0 + CONV_ROWS, :] * zp[s0:s0 + CONV_ROWS, :]
            ddw_ref[k:k + 1, :] += jnp.sum(acc, axis=0, keepdims=True)

    ja, jg = _cb(off_a, cw), _cb(off_a + c_conv, cw)
    seq = pl.BlockSpec((lp, cw), lambda j, b: (b, j))
    return pl.pallas_call(
        body, grid=(c_conv // cw, nb),
        in_specs=[pl.BlockSpec((lp, cw), lambda j, b: (b, ja + j)), pl.BlockSpec((lp, cw), lambda j, b: (b, jg + j)),
                  seq, pl.BlockSpec((kw, cw), lambda j, b: (0, j))],
        out_specs=(seq, seq, pl.BlockSpec((kw, cw), lambda j, b: (0, j)), pl.BlockSpec((1, cw), lambda j, b: (0, j))),
        out_shape=(jax.ShapeDtypeStruct((nb * lp, c_conv), BF16), jax.ShapeDtypeStruct((nb * lp, c_conv), BF16),
                   jax.ShapeDtypeStruct((kw, c_conv), F32), jax.ShapeDtypeStruct((1, c_conv), F32)),
        scratch_shapes=[pltpu.VMEM((lp + 2 * mg, cw), F32), pltpu.VMEM((lp + 2 * mg, cw), F32)],
        name="conv_bwd", compiler_params=_cp(2))(proj, proj, dc, dw)


def _swap_halves(x, half):
    fwd = pltpu.roll(x, LANE - half, axis=1)
    bwd = pltpu.roll(x, half, axis=1)
    lane = lax.broadcasted_iota(jnp.int32, x.shape, 1)
    return jnp.where((lane & (2 * half - 1)) < half, fwd, bwd)


def _rope(x, cos, sin, half):
    return x * cos + _swap_halves(x, half) * sin


def _rope_t(dy, cos, sin, half):
    return dy * cos + _swap_halves(dy * sin, half)


def _rope_tables(lp, l_valid, half):
    t = jnp.arange(lp)
    n = jnp.clip(t - N_META, 0, None)
    real = (t >= N_META) & (t < l_valid)
    row = jnp.where(real, n // GRID_W, 0).astype(F32)
    col = jnp.where(real, n % GRID_W, 0).astype(F32)
    inv = ROPE_THETA ** (-jnp.arange(half, dtype=F32) / half)
    ar, ac = row[:, None] * inv[None, :], col[:, None] * inv[None, :]
    cos = jnp.concatenate([jnp.cos(ar), jnp.cos(ar), jnp.cos(ac), jnp.cos(ac)], axis=1)
    sin = jnp.concatenate([-jnp.sin(ar), jnp.sin(ar), -jnp.sin(ac), jnp.sin(ac)], axis=1)
    padw = LANE - 4 * half
    if padw:
        cos = jnp.pad(cos, ((0, 0), (0, padw)))
        sin = jnp.pad(sin, ((0, 0), (0, padw)))
    return cos.astype(F32), sin.astype(F32)


def _gqa_prep_fwd(proj, qg, kg, cos, sin, *, off_q, hq, hkv, nb, lp, tm, q_scale):
    hd = HEAD_DIM
    half = hd // 4
    wq, wkv = hq * hd, 2 * hkv * hd
    nt = lp // tm

    def body(q_ref, kv_ref, qg_ref, kg_ref, cos_ref, sin_ref, qo_ref, ko_ref, vo_ref):
        cosv, sinv = cos_ref[...], sin_ref[...]

        def norm_rope(xh, g):
            r = lax.rsqrt(jnp.mean(xh * xh, axis=-1, keepdims=True) + NORM_EPS)
            return _rope(xh * r * g, cosv, sinv, half)

        for h in range(hq):
            qo_ref[h] = (norm_rope(q_ref[:, h * hd:(h + 1) * hd], qg_ref[...]) * q_scale).astype(BF16)
        for h in range(hkv):
            ko_ref[h] = norm_rope(kv_ref[:, h * hd:(h + 1) * hd], kg_ref[...]).astype(BF16)
            vo_ref[h] = kv_ref[:, (hkv + h) * hd:(hkv + h + 1) * hd].astype(BF16)

    jq, jkv = _cb(off_q, wq), _cb(off_q + wq, wkv)
    tab = pl.BlockSpec((tm, LANE), lambda b, i: (i, 0))
    vec = pl.BlockSpec((1, hd), lambda b, i: (0, 0))

    def heads(h):
        return pl.BlockSpec((None, h, tm, hd), lambda b, i: (b, 0, i, 0))

    return pl.pallas_call(
        body, grid=(nb, nt),
        in_specs=[pl.BlockSpec((tm, wq), lambda b, i: (b * nt + i, jq)),
                  pl.BlockSpec((tm, wkv), lambda b, i: (b * nt + i, jkv)), vec, vec, tab, tab],
        out_specs=(heads(hq), heads(hkv), heads(hkv)),
        out_shape=(jax.ShapeDtypeStruct((nb, hq, lp, hd), BF16), jax.ShapeDtypeStruct((nb, hkv, lp, hd), BF16),
                   jax.ShapeDtypeStruct((nb, hkv, lp, hd), BF16)),
        name="gqa_prep_fwd", compiler_params=_cp(2))(proj, proj, qg, kg, cos, sin)


def _gqa_prep_bwd(proj, qg, kg, cos, sin, dqh, dkh, dvh, *, off_q, hq, hkv, nb, lp, tm, q_scale):
    hd = HEAD_DIM
    half = hd // 4
    wq, wkv = hq * hd, 2 * hkv * hd
    nt = lp // tm

    def body(q_ref, kv_ref, qg_ref, kg_ref, cos_ref, sin_ref, dqh_ref, dkh_ref, dvh_ref,
             dq_ref, dkv_ref, dqg_ref, dkg_ref):
        cosv, sinv = cos_ref[...], sin_ref[...]

        def back(xh, g, dyh):
            dn = _rope_t(dyh, cosv, sinv, half)
            r = lax.rsqrt(jnp.mean(xh * xh, axis=-1, keepdims=True) + NORM_EPS)
            t = dn * g
            dx = r * t - xh * (r * r * r) * jnp.mean(t * xh, axis=-1, keepdims=True)
            return dx, jnp.sum(dn * xh * r, axis=0, keepdims=True)

        first = (pl.program_id(0) == 0) & (pl.program_id(1) == 0)

        @pl.when(first)
        def _():
            dqg_ref[...] = jnp.zeros_like(dqg_ref)
            dkg_ref[...] = jnp.zeros_like(dkg_ref)

        gq = jnp.zeros((1, hd), F32)
        for h in range(hq):
            dx, dg = back(q_ref[:, h * hd:(h + 1) * hd], qg_ref[...], dqh_ref[h] * q_scale)
            dq_ref[:, h * hd:(h + 1) * hd] = dx.astype(dq_ref.dtype)
            gq = gq + dg
        dqg_ref[...] += gq
        gk = jnp.zeros((1, hd), F32)
        for h in range(hkv):
            dx, dg = back(kv_ref[:, h * hd:(h + 1) * hd], kg_ref[...], dkh_ref[h])
            dkv_ref[:, h * hd:(h + 1) * hd] = dx.astype(dkv_ref.dtype)
            dkv_ref[:, (hkv + h) * hd:(hkv + h + 1) * hd] = dvh_ref[h].astype(dkv_ref.dtype)
            gk = gk + dg
        dkg_ref[...] += gk

    jq, jkv = _cb(off_q, wq), _cb(off_q + wq, wkv)
    tab = pl.BlockSpec((tm, LANE), lambda b, i: (i, 0))
    vec = pl.BlockSpec((1, hd), lambda b, i: (0, 0))

    def heads(h):
        return pl.BlockSpec((None, h, tm, hd), lambda b, i: (b, 0, i, 0))

    return pl.pallas_call(
        body, grid=(nb, nt),
        in_specs=[pl.BlockSpec((tm, wq), lambda b, i: (b * nt + i, jq)),
                  pl.BlockSpec((tm, wkv), lambda b, i: (b * nt + i, jkv)), vec, vec, tab, tab,
                  heads(hq), heads(hkv), heads(hkv)],
        out_specs=(pl.BlockSpec((tm, wq), lambda b, i: (b * nt + i, 0)),
                   pl.BlockSpec((tm, wkv), lambda b, i: (b * nt + i, 0)), vec, vec),
        out_shape=(jax.ShapeDtypeStruct((nb * lp, wq), BF16), jax.ShapeDtypeStruct((nb * lp, wkv), BF16),
                   jax.ShapeDtypeStruct((1, hd), F32), jax.ShapeDtypeStruct((1, hd), F32)),
        name="gqa_prep_bwd", compiler_params=_cp(2))(proj, proj, qg, kg, cos, sin, dqh, dkh, dvh)


def _mla_prep_fwd(qm, kvm, proj, cos, sin, *, off_kpe, nh, nb, lp, tm, q_scale):
    hd = HEAD_DIM
    half = MLA_ROPE_DIM // 4
    nt = lp // tm
    wh = nh * hd

    def body(qm_ref, kvm_ref, kpe_ref, cos_ref, sin_ref, qo_ref, ko_ref, vo_ref):
        cosv, sinv = cos_ref[...], sin_ref[...]
        kpe = _rope(kpe_ref[...], cosv, sinv, half).astype(BF16)
        for h in range(nh):
            qo_ref[h, :, 0:hd] = (qm_ref[:, h * hd:(h + 1) * hd] * q_scale).astype(BF16)
            qo_ref[h, :, hd:2 * hd] = (_rope(qm_ref[:, wh + h * hd:wh + (h + 1) * hd], cosv, sinv, half)
                                       * q_scale).astype(BF16)
            ko_ref[h, :, 0:hd] = kvm_ref[:, h * hd:(h + 1) * hd].astype(BF16)
            ko_ref[h, :, hd:2 * hd] = kpe
            vo_ref[h] = kvm_ref[:, wh + h * hd:wh + (h + 1) * hd].astype(BF16)

    tab = pl.BlockSpec((tm, LANE), lambda b, i: (i, 0))
    wide = pl.BlockSpec((tm, 2 * wh), lambda b, i: (b * nt + i, 0))
    jk = _cb(off_kpe, LANE)
    return pl.pallas_call(
        body, grid=(nb, nt),
        in_specs=[wide, wide, pl.BlockSpec((tm, LANE), lambda b, i: (b * nt + i, jk)), tab, tab],
        out_specs=(pl.BlockSpec((None, nh, tm, 2 * hd), lambda b, i: (b, 0, i, 0)),
                   pl.BlockSpec((None, nh, tm, 2 * hd), lambda b, i: (b, 0, i, 0)),
                   pl.BlockSpec((None, nh, tm, hd), lambda b, i: (b, 0, i, 0))),
        out_shape=(jax.ShapeDtypeStruct((nb, nh, lp, 2 * hd), BF16), jax.ShapeDtypeStruct((nb, nh, lp, 2 * hd), BF16),
                   jax.ShapeDtypeStruct((nb, nh, lp, hd), BF16)),
        name="mla_prep_fwd", compiler_params=_cp(2))(qm, kvm, proj, cos, sin)


def _mla_prep_bwd(dqc, dkc, dvc, cos, sin, *, nh, nb, lp, tm, q_scale):
    hd = HEAD_DIM
    half = MLA_ROPE_DIM // 4
    nt = lp // tm
    wh = nh * hd

    def body(dq_ref, dk_ref, dv_ref, cos_ref, sin_ref, dqm_ref, dkvm_ref, dkpe_ref):
        cosv, sinv = cos_ref[...], sin_ref[...]
        acc = jnp.zeros((tm, hd), F32)
        for h in range(nh):
            dqm_ref[:, h * hd:(h + 1) * hd] = (dq_ref[h, :, 0:hd] * q_scale).astype(BF16)
            dqm_ref[:, wh + h * hd:wh + (h + 1) * hd] = (_rope_t(dq_ref[h, :, hd:2 * hd], cosv, sinv, half)
                                                         * q_scale).astype(BF16)
            dkvm_ref[:, h * hd:(h + 1) * hd] = dk_ref[h, :, 0:hd].astype(BF16)
            dkvm_ref[:, wh + h * hd:wh + (h + 1) * hd] = dv_ref[h].astype(BF16)
            acc = acc + dk_ref[h, :, hd:2 * hd]
        dkpe_ref[...] = _rope_t(acc, cosv, sinv, half).astype(BF16)

    tab = pl.BlockSpec((tm, LANE), lambda b, i: (i, 0))
    wide = pl.BlockSpec((tm, 2 * wh), lambda b, i: (b * nt + i, 0))
    return pl.pallas_call(
        body, grid=(nb, nt),
        in_specs=[pl.BlockSpec((None, nh, tm, 2 * hd), lambda b, i: (b, 0, i, 0)),
                  pl.BlockSpec((None, nh, tm, 2 * hd), lambda b, i: (b, 0, i, 0)),
                  pl.BlockSpec((None, nh, tm, hd), lambda b, i: (b, 0, i, 0)), tab, tab],
        out_specs=(wide, wide, pl.BlockSpec((tm, LANE), lambda b, i: (b * nt + i, 0))),
        out_shape=(jax.ShapeDtypeStruct((nb * lp, 2 * wh), BF16), jax.ShapeDtypeStruct((nb * lp, 2 * wh), BF16),
                   jax.ShapeDtypeStruct((nb * lp, LANE), BF16)),
        name="mla_prep_bwd", compiler_params=_cp(2))(dqc, dkc, dvc, cos, sin)


def _softmax_parts(s, lp, l_valid):
    tw = -(-(lp - l_valid) // LANE) * LANE
    main, tail = s[:, :lp - tw], s[:, lp - tw:]
    col = lax.broadcasted_iota(jnp.int32, tail.shape, 1) + (lp - tw)
    tail = jnp.where(col < l_valid, tail, -1e30)
    m = jnp.maximum(jnp.max(main, axis=-1, keepdims=True), jnp.max(tail, axis=-1, keepdims=True))
    pm, pt = jnp.exp(main - m), jnp.exp(tail - m)
    den = jnp.sum(pm, axis=-1, keepdims=True) + jnp.sum(pt, axis=-1, keepdims=True)
    return pm, pt, den, tw


def _attn_fwd(q, k, v, *, l_valid, name):
    nb, hq, lp, dk = q.shape
    hkv, dv = k.shape[1], v.shape[3]
    grp = hq // hkv
    tq = _tile(lp, 1088, 64)
    sub = tq // 4
    nq = lp // tq

    def body(q_ref, k_ref, v_ref, o_ref, p_s):
        for r0 in range(0, tq, sub):
            rows = slice(r0, r0 + sub)
            s = lax.dot_general(q_ref[rows, :], k_ref[...], NT, preferred_element_type=F32)
            pm, pt, den, tw = _softmax_parts(s, lp, l_valid)
            p_s[rows, :lp - tw] = pm.astype(BF16)
            p_s[rows, lp - tw:] = pt.astype(BF16)
            o = lax.dot_general(p_s[rows, :], v_ref[...], NN, preferred_element_type=F32)
            o_ref[rows, :] = (o / den).astype(o_ref.dtype)

    return pl.pallas_call(
        body, grid=(nb, hq, nq),
        in_specs=[pl.BlockSpec((None, None, tq, dk), lambda b, h, i: (b, h, i, 0)),
                  pl.BlockSpec((None, None, lp, dk), lambda b, h, i: (b, h // grp, 0, 0)),
                  pl.BlockSpec((None, None, lp, dv), lambda b, h, i: (b, h // grp, 0, 0))],
        out_specs=pl.BlockSpec((tq, dv), lambda b, h, i: (b * nq + i, h)),
        out_shape=jax.ShapeDtypeStruct((nb * lp, hq * dv), BF16), scratch_shapes=[pltpu.VMEM((tq, lp), BF16)],
        name=name, compiler_params=_cp(3))(q, k, v)


def _attn_bwd(q, k, v, do, *, l_valid, name):
    nb, hq, lp, dk = q.shape
    hkv, dv = k.shape[1], v.shape[3]
    grp = hq // hkv
    tq = _tile(lp, 1088, 64)
    sub = tq // 4
    nq = lp // tq

    def body(q_ref, k_ref, v_ref, do_ref, dq_ref, dk_ref, dv_ref, p_s, ds_s):
        for r0 in range(0, tq, sub):
            rows = slice(r0, r0 + sub)
            s = lax.dot_general(q_ref[rows, :], k_ref[...], NT, preferred_element_type=F32)
            pm, pt, den, tw = _softmax_parts(s, lp, l_valid)
            inv = 1.0 / den
            pm, pt = pm * inv, pt * inv
            dp = lax.dot_general(do_ref[rows, :], v_ref[...], NT, preferred_element_type=F32)
            dpm, dpt = dp[:, :lp - tw], dp[:, lp - tw:]
            dd = jnp.sum(pm * dpm, axis=-1, keepdims=True) + jnp.sum(pt * dpt, axis=-1, keepdims=True)
            p_s[rows, :lp - tw] = pm.astype(BF16)
            p_s[rows, lp - tw:] = pt.astype(BF16)
            ds_s[rows, :lp - tw] = (pm * (dpm - dd)).astype(BF16)
            ds_s[rows, lp - tw:] = (pt * (dpt - dd)).astype(BF16)
            dq_ref[rows, :] = lax.dot_general(ds_s[rows, :], k_ref[...], NN, preferred_element_type=F32)

        @pl.when((pl.program_id(2) == 0) & (pl.program_id(3) == 0))
        def _():
            dk_ref[...] = jnp.zeros_like(dk_ref)
            dv_ref[...] = jnp.zeros_like(dv_ref)

        dk_ref[...] += lax.dot_general(ds_s[...], q_ref[...], TN, preferred_element_type=F32)
        dv_ref[...] += lax.dot_general(p_s[...], do_ref[...], TN, preferred_element_type=F32)

    return pl.pallas_call(
        body, grid=(nb, hkv, grp, nq), scratch_shapes=[pltpu.VMEM((tq, lp), BF16), pltpu.VMEM((tq, lp), BF16)],
        in_specs=[pl.BlockSpec((None, None, tq, dk), lambda b, h, g, i: (b, h * grp + g, i, 0)),
                  pl.BlockSpec((None, None, lp, dk), lambda b, h, g, i: (b, h, 0, 0)),
                  pl.BlockSpec((None, None, lp, dv), lambda b, h, g, i: (b, h, 0, 0)),
                  pl.BlockSpec((tq, dv), lambda b, h, g, i: (b * nq + i, h * grp + g))],
        out_specs=(pl.BlockSpec((None, None, tq, dk), lambda b, h, g, i: (b, h * grp + g, i, 0)),
                   pl.BlockSpec((None, None, lp, dk), lambda b, h, g, i: (b, h, 0, 0)),
                   pl.BlockSpec((None, None, lp, dv), lambda b, h, g, i: (b, h, 0, 0))),
        out_shape=(jax.ShapeDtypeStruct((nb, hq, lp, dk), F32), jax.ShapeDtypeStruct((nb, hkv, lp, dk), F32),
                   jax.ShapeDtypeStruct((nb, hkv, lp, dv), F32)),
        name=name, compiler_params=_cp(4))(q, k, v, do)


def _gate_fwd(proj, gb, ya, yb, yc, *, d, tm):
    m = ya.shape[0]
    cw = _tile(d, 512, LANE)
    nj = d // cw

    def body(g0, g1, g2, b0, b1, b2, ya_ref, yb_ref, yc_ref, o_ref):
        o = (_sig(g0[...] + b0[...]) * ya_ref[...] + _sig(g1[...] + b1[...]) * yb_ref[...]
             + _sig(g2[...] + b2[...]) * yc_ref[...])
        o_ref[...] = o.astype(o_ref.dtype)

    def gl(br):
        return pl.BlockSpec((tm, cw), lambda j, i: (i, br * nj + j))

    def gbs(br):
        return pl.BlockSpec((1, cw), lambda j, i: (0, br * nj + j))

    blk = pl.BlockSpec((tm, cw), lambda j, i: (i, j))
    return pl.pallas_call(
        body, grid=(nj, m // tm), in_specs=[gl(0), gl(1), gl(2), gbs(0), gbs(1), gbs(2), blk, blk, blk],
        out_specs=blk, out_shape=jax.ShapeDtypeStruct((m, d), BF16), name="gate_fwd",
        compiler_params=_cp(2))(proj, proj, proj, gb, gb, gb, ya, yb, yc)


def _gate_bwd(dm, proj, gb, ya, yb, yc, *, d, tm):
    m = ya.shape[0]
    cw = _tile(d, 512, LANE)
    nj = d // cw

    def body(dm_ref, g0, g1, g2, b0, b1, b2, ya_ref, yb_ref, yc_ref,
             dya, dyb, dyc, dg0, dg1, dg2, db0, db1, db2):
        dmv = dm_ref[...]

        @pl.when(pl.program_id(1) == 0)
        def _():
            for r in (db0, db1, db2):
                r[...] = jnp.zeros_like(r)

        for g, b, y, dy, dgl, db in ((g0, b0, ya_ref, dya, dg0, db0), (g1, b1, yb_ref, dyb, dg1, db1),
                                     (g2, b2, yc_ref, dyc, dg2, db2)):
            sg = _sig(g[...] + b[...])
            dy[...] = (sg * dmv).astype(dy.dtype)
            dl = dmv * y[...] * sg * (1.0 - sg)
            dgl[...] = dl.astype(dgl.dtype)
            db[...] += jnp.sum(dl, axis=0, keepdims=True)

    def gl(br):
        return pl.BlockSpec((tm, cw), lambda j, i: (i, br * nj + j))

    def gbs(br):
        return pl.BlockSpec((1, cw), lambda j, i: (0, br * nj + j))

    blk = pl.BlockSpec((tm, cw), lambda j, i: (i, j))
    vec = pl.BlockSpec((1, cw), lambda j, i: (0, j))
    act = jax.ShapeDtypeStruct((m, d), BF16)
    vsh = jax.ShapeDtypeStruct((1, d), F32)
    return pl.pallas_call(
        body, grid=(nj, m // tm), in_specs=[blk, gl(0), gl(1), gl(2), gbs(0), gbs(1), gbs(2), blk, blk, blk],
        out_specs=(blk,) * 6 + (vec,) * 3, out_shape=(act,) * 6 + (vsh,) * 3, name="gate_bwd",
        compiler_params=_cp(2))(dm, proj, proj, proj, gb, gb, gb, ya, yb, yc)


def _swiglu_fwd(gt, up, tm):
    m, f = gt.shape
    cw = _tile(f, 512, LANE)

    def body(g_ref, u_ref, o_ref):
        g = g_ref[...]
        o_ref[...] = (g * _sig(g) * u_ref[...]).astype(o_ref.dtype)

    blk = pl.BlockSpec((tm, cw), lambda j, i: (i, j))
    return pl.pallas_call(body, grid=(f // cw, m // tm), in_specs=[blk, blk], out_specs=blk,
                          out_shape=jax.ShapeDtypeStruct((m, f), BF16), name="swiglu_fwd",
                          compiler_params=_cp(2))(gt, up)


def _swiglu_bwd(dact, gt, up, tm):
    m, f = gt.shape
    cw = _tile(f, 512, LANE)

    def body(d_ref, g_ref, u_ref, dg_ref, du_ref):
        g, d = g_ref[...], d_ref[...]
        sg = _sig(g)
        dg_ref[...] = (d * u_ref[...] * sg * (1.0 + g * (1.0 - sg))).astype(dg_ref.dtype)
        du_ref[...] = (d * g * sg).astype(du_ref.dtype)

    blk = pl.BlockSpec((tm, cw), lambda j, i: (i, j))
    sh = jax.ShapeDtypeStruct((m, f), BF16)
    return pl.pallas_call(body, grid=(f // cw, m // tm), in_specs=[blk, blk, blk], out_specs=(blk, blk),
                          out_shape=(sh, sh), name="swiglu_bwd", compiler_params=_cp(2))(dact, gt, up)


def _loss_head(h, gf, tgt, *, lp, l_valid, tm):
    m, d = h.shape
    nt = lp // tm

    def body(h_ref, g_ref, t_ref, dh_ref, dg_ref, loss_ref):
        i = pl.program_id(0)
        xv = h_ref[...]
        r = lax.rsqrt(jnp.mean(xv * xv, axis=-1, keepdims=True) + NORM_EPS)
        y = xv * r * g_ref[...]
        t = lax.broadcasted_iota(jnp.int32, (tm, d), 0) + (i % nt) * tm
        err = jnp.where((t >= N_META) & (t < l_valid), y - t_ref[...], 0.0)
        dy = err * (1.0 / d)
        tg = dy * g_ref[...]
        dh_ref[...] = r * tg - xv * (r * r * r) * jnp.mean(tg * xv, axis=-1, keepdims=True)

        @pl.when(i == 0)
        def _():
            dg_ref[...] = jnp.zeros_like(dg_ref)
            loss_ref[...] = jnp.zeros_like(loss_ref)

        dg_ref[...] += jnp.sum(dy * xv * r, axis=0, keepdims=True)
        sq = jnp.sum(jnp.sum(err * err, axis=-1, keepdims=True), axis=0, keepdims=True)
        loss_ref[...] += jnp.zeros((1, LANE), F32) + sq * (0.5 / d)

    row = pl.BlockSpec((tm, d), lambda i: (i, 0))
    vec = pl.BlockSpec((1, d), lambda i: (0, 0))
    return pl.pallas_call(
        body, grid=(m // tm,), in_specs=[row, vec, row],
        out_specs=(row, vec, pl.BlockSpec((1, LANE), lambda i: (0, 0))),
        out_shape=(jax.ShapeDtypeStruct((m, d), F32), jax.ShapeDtypeStruct((1, d), F32),
                   jax.ShapeDtypeStruct((1, LANE), F32)),
        name="loss_head", compiler_params=_cp(1))(h, gf, tgt)


def _adamw(w, g, m, v, name):
    rows, cols = w.shape
    tr = rows
    if rows % 8 == 0:
        tr = _tile(rows, max(8, (1 << 18) // cols // 8 * 8), 8)
    c1 = 1.0 / (1.0 - ADAM_B1 ** ADAM_STEP)
    c2 = 1.0 / (1.0 - ADAM_B2 ** ADAM_STEP)

    def body(w_ref, g_ref, m_ref, v_ref, d_ref, mo_ref, vo_ref):
        gv = g_ref[...]
        mn = ADAM_B1 * m_ref[...] + (1.0 - ADAM_B1) * gv
        vn = ADAM_B2 * v_ref[...] + (1.0 - ADAM_B2) * (gv * gv)
        mo_ref[...] = mn
        vo_ref[...] = vn
        d_ref[...] = -ADAM_LR * ((mn * c1) / (jnp.sqrt(vn * c2) + ADAM_EPS) + ADAM_WD * w_ref[...])

    blk = pl.BlockSpec((tr, cols), lambda i: (i, 0))
    sh = jax.ShapeDtypeStruct((rows, cols), F32)
    return pl.pallas_call(body, grid=(rows // tr,), in_specs=[blk] * 4, out_specs=(blk,) * 3, out_shape=(sh,) * 3,
                          name=name, compiler_params=_cp(1))(w, g, m, v)


MESH = pl.DeviceIdType.MESH
ANY = pl.BlockSpec(memory_space=pl.ANY)


def _place():
    return lax.axis_index("x"), lax.axis_index("y"), lax.axis_index("c")


def _other_chips(x, y):
    return [(1 - x, y), (x, 1 - y), (1 - x, 1 - y)]


def _all_gather(xs, vmem, name):
    na = len(xs)

    def body(*refs):
        x_refs, out_refs = refs[:na], refs[na:2 * na]
        send_sems, recv_sems, local_sems = refs[2 * na:]
        x, y, c = _place()
        me, sibling = (x, y, c), (x, y, 1 - c)
        chips = _other_chips(x, y)

        def blk(p, px, py, pc):
            return out_refs[p].at[4 * px + 2 * py + pc]

        def copy(k, p, block, to, src=None):
            return pltpu.make_async_remote_copy(
                src_ref=blk(p, *block) if src is None else src, dst_ref=blk(p, *block),
                send_sem=send_sems.at[k * na + p], recv_sem=recv_sems.at[k * na + p], device_id=to,
                device_id_type=MESH)

        mine = [pltpu.make_async_copy(x_refs[p], blk(p, *me), local_sems.at[p]) for p in range(na)]
        for cp in mine:
            cp.start()
        first = [copy(0, p, me, sibling, src=x_refs[p]) for p in range(na)]
        first += [copy(1 + j, p, me, (*chip, c), src=x_refs[p]) for j, chip in enumerate(chips) for p in range(na)]
        for cp in first:
            cp.start()
        passed = []
        for j, chip in enumerate(chips):
            for p in range(na):
                copy(1 + j, p, (*chip, c), me).wait_recv()
                fwd = copy(4 + j, p, (*chip, c), sibling)
                fwd.start()
                passed.append(fwd)
        for p in range(na):
            copy(0, p, sibling, me).wait_recv()
        for j, chip in enumerate(chips):
            for p in range(na):
                copy(4 + j, p, (*chip, 1 - c), me).wait_recv()
        for cp in first + passed:
            cp.wait_send()
        for cp in mine:
            cp.wait()

    spec = pl.BlockSpec(memory_space=pltpu.VMEM) if vmem else ANY
    return pl.pallas_call(
        body, out_shape=[jax.ShapeDtypeStruct((N_DEV,) + a.shape, a.dtype) for a in xs],
        in_specs=[spec] * na, out_specs=[spec] * na,
        scratch_shapes=[pltpu.SemaphoreType.DMA((7 * na,)), pltpu.SemaphoreType.DMA((7 * na,)),
                        pltpu.SemaphoreType.DMA((na,))],
        name=name)(*xs)


def _swap_with_sibling(g4s):
    na = len(g4s)

    def body(*refs):
        g_refs, out_refs = refs[:na], refs[na:2 * na]
        send_sems, recv_sems = refs[2 * na:]
        x, y, c = _place()
        cps = [pltpu.make_async_remote_copy(src_ref=g_refs[p].at[:, 1 - c], dst_ref=out_refs[p],
                                            send_sem=send_sems.at[p], recv_sem=recv_sems.at[p],
                                            device_id=(x, y, 1 - c), device_id_type=MESH) for p in range(na)]
        for cp in cps:
            cp.start()
        for cp in cps:
            cp.wait()

    return pl.pallas_call(
        body, out_shape=[jax.ShapeDtypeStruct((g.shape[0],) + g.shape[2:], g.dtype) for g in g4s],
        in_specs=[ANY] * na, out_specs=[ANY] * na,
        scratch_shapes=[pltpu.SemaphoreType.DMA((na,)), pltpu.SemaphoreType.DMA((na,))], name="rs_sibling")(*g4s)


HBM = pl.BlockSpec(memory_space=pltpu.HBM)
SEM = pl.BlockSpec(memory_space=pltpu.SEMAPHORE)
EFFECT = pltpu.SideEffectType.DATAFLOW_SIDE_EFFECTING


def _in_hbm(a):
    return pltpu.with_memory_space_constraint(a, pltpu.HBM)


def _ici_copies(src_refs, land_refs, send_sems, recv_sems, gather):
    na = len(src_refs)
    x, y, c = _place()
    cps = []
    for j, (px, py) in enumerate(_other_chips(x, y)):
        for p in range(na):
            src = src_refs[p].at[4 * x + 2 * y + c] if gather else src_refs[p].at[2 * px + py]
            dst = src if gather else land_refs[p].at[j]
            cps.append(pltpu.make_async_remote_copy(
                src_ref=src, dst_ref=dst, send_sem=send_sems.at[j * na + p], recv_sem=recv_sems.at[j * na + p],
                device_id=(px, py, c), device_id_type=MESH))
    return cps


def _ici_start(srcs, land_shapes, deps, *, gather, name):
    na, nd = len(srcs), len(deps)
    lands = [lax.empty(s, a.dtype) for s, a in zip(land_shapes, srcs)]
    nb = na + len(lands)

    def body(*refs):
        src_refs, land_refs = refs[:na], refs[na:nb]
        send_sems, recv_sems = refs[nb + nd], refs[nb + nd + 1]
        token = refs[-1]
        for cp in _ici_copies(src_refs, land_refs, send_sems, recv_sems, gather):
            cp.start()
        token[...] = jnp.zeros_like(token)

    bufs = list(srcs) + lands
    out = pl.pallas_call(
        body, name=name,
        out_shape=(pltpu.SemaphoreType.DMA((3 * na,)), pltpu.SemaphoreType.DMA((3 * na,)),
                   *[pltpu.HBM(a.shape, a.dtype) for a in bufs], jax.ShapeDtypeStruct((8, LANE), F32)),
        in_specs=[HBM] * nb + [ANY] * nd,
        out_specs=(SEM, SEM, *([HBM] * nb), pl.BlockSpec(memory_space=pltpu.VMEM)),
        input_output_aliases={p: 2 + p for p in range(nb)},
        compiler_params=pltpu.CompilerParams(has_side_effects=EFFECT),
    )(*[_in_hbm(a) for a in bufs], *deps)
    return out[0], out[1], list(out[2:2 + na]), list(out[2 + na:2 + nb]), out[-1]


def _ici_wait(send_sems, recv_sems, srcs, lands, after, *, gather, name):
    na, nd = len(srcs), len(after)
    nb = na + len(lands)

    def body(*refs):
        src_refs, land_refs = refs[:na], refs[na:nb]
        s_sems, r_sems = refs[nb], refs[nb + 1]
        for cp in _ici_copies(src_refs, land_refs, s_sems, r_sems, gather):
            cp.wait_send()
            cp.wait_recv()

    bufs = list(srcs) + list(lands)
    out = pl.pallas_call(
        body, name=name, out_shape=tuple(pltpu.HBM(a.shape, a.dtype) for a in bufs),
        in_specs=[HBM] * nb + [SEM, SEM] + [ANY] * nd, out_specs=tuple([HBM] * nb),
        input_output_aliases={p: p for p in range(nb)},
        compiler_params=pltpu.CompilerParams(has_side_effects=EFFECT),
    )(*bufs, send_sems, recv_sems, *after)
    return list(out[:na]), list(out[na:])


def _gather_to_sibling(gs):
    na = len(gs)

    def body(*refs):
        out_refs = refs[na:2 * na]
        send_sems, recv_sems = refs[2 * na:]
        x, y, c = _place()
        sibling = (x, y, 1 - c)
        blocks = [(x, y)] + _other_chips(x, y)

        def copy(k, p, core):
            px, py = blocks[k]
            blk = out_refs[p].at[4 * px + 2 * py + core]
            return pltpu.make_async_remote_copy(src_ref=blk, dst_ref=blk, send_sem=send_sems.at[k * na + p],
                                                recv_sem=recv_sems.at[k * na + p], device_id=sibling,
                                                device_id_type=MESH)

        sends = [copy(k, p, c) for k in range(4) for p in range(na)]
        for cp in sends:
            cp.start()
        for k in range(4):
            for p in range(na):
                copy(k, p, 1 - c).wait_recv()
        for cp in sends:
            cp.wait_send()

    return pl.pallas_call(
        body, out_shape=[jax.ShapeDtypeStruct(a.shape, a.dtype) for a in gs],
        in_specs=[ANY] * na, out_specs=[ANY] * na, input_output_aliases={p: p for p in range(na)},
        scratch_shapes=[pltpu.SemaphoreType.DMA((4 * na,)), pltpu.SemaphoreType.DMA((4 * na,))],
        name="ag_sibling")(*gs)


HI_MASK = -65536


def _pack_pairs_xla(a):
    half = a.shape[-1] // 2
    bits = lax.bitcast_convert_type(a.astype(BF16), jnp.uint16).astype(jnp.uint32)
    return lax.bitcast_convert_type((bits[..., half:] << 16) | bits[..., :half], F32)


def _split_pairs(words):
    wv = lax.bitcast_convert_type(words, jnp.int32)
    return lax.bitcast_convert_type(wv << 16, F32), lax.bitcast_convert_type(wv & HI_MASK, F32)


def _join_pairs(lo, hi):
    lo_b = lax.bitcast_convert_type(lo.astype(BF16).astype(F32), jnp.int32)
    hi_b = lax.bitcast_convert_type(hi.astype(BF16).astype(F32), jnp.int32)
    return lax.bitcast_convert_type((hi_b & HI_MASK) | lax.shift_right_logical(lo_b, 16), F32)


def _unpack(xp, *, name, out_rows=None, tr=None, src_fn=None, zero_fn=None):
    r, hw = xp.shape
    out_rows = out_rows or r
    tr = tr or _tile(out_rows, 512, 16)
    src = src_fn or (lambda i: i)

    def body(x_ref, o_ref):
        lo, hi = _split_pairs(x_ref[...])
        if zero_fn is not None:
            z = zero_fn(pl.program_id(0))
            lo, hi = jnp.where(z, 0.0, lo), jnp.where(z, 0.0, hi)
        o_ref[:, :hw] = lo.astype(BF16)
        o_ref[:, hw:] = hi.astype(BF16)

    return pl.pallas_call(
        body, grid=(out_rows // tr,), in_specs=[pl.BlockSpec((tr, hw), lambda i: (src(i), 0))],
        out_specs=pl.BlockSpec((tr, 2 * hw), lambda i: (i, 0)),
        out_shape=jax.ShapeDtypeStruct((out_rows, 2 * hw), BF16), name=name, compiler_params=_cp(1))(xp)


def _gather_rows(xp, *, name, out_rows, tr, src_fn):
    _, hw = xp.shape

    def body(x_ref, o_ref):
        o_ref[...] = x_ref[...]

    return pl.pallas_call(
        body, grid=(out_rows // tr,), in_specs=[pl.BlockSpec((tr, hw), lambda i: (src_fn(i), 0))],
        out_specs=pl.BlockSpec((tr, hw), lambda i: (i, 0)),
        out_shape=jax.ShapeDtypeStruct((out_rows, hw), xp.dtype), name=name, compiler_params=_cp(1))(xp)


def _matmul_tn_packed(a, b, name):
    (t, m), (t2, c) = a.shape, b.shape
    assert t == t2
    hw = c // 2
    tm = _tile(m, 1088, LANE)
    tn = _tile(hw, 512, LANE)
    tk = _tile(t, 2176, LANE)
    nk, nj = t // tk, hw // tn

    def body(a_ref, bl_ref, bh_ref, o_ref, acc_lo, acc_hi):
        kk = pl.program_id(2)

        @pl.when(kk == 0)
        def _():
            acc_lo[...] = jnp.zeros_like(acc_lo)
            acc_hi[...] = jnp.zeros_like(acc_hi)

        av = a_ref[...].astype(BF16)
        acc_lo[...] += lax.dot_general(av, bl_ref[...].astype(BF16), TN, preferred_element_type=F32)
        acc_hi[...] += lax.dot_general(av, bh_ref[...].astype(BF16), TN, preferred_element_type=F32)

        @pl.when(kk == nk - 1)
        def _():
            o_ref[...] = _join_pairs(acc_lo[...], acc_hi[...])

    return pl.pallas_call(
        body, grid=(m // tm, nj, nk),
        in_specs=[pl.BlockSpec((tk, tm), lambda i, j, kk: (kk, i)), pl.BlockSpec((tk, tn), lambda i, j, kk: (kk, j)),
                  pl.BlockSpec((tk, tn), lambda i, j, kk: (kk, nj + j))],
        out_specs=pl.BlockSpec((tm, tn), lambda i, j, kk: (i, j)), out_shape=jax.ShapeDtypeStruct((m, hw), F32),
        scratch_shapes=[pltpu.VMEM((tm, tn), F32), pltpu.VMEM((tm, tn), F32)], name=name,
        compiler_params=_cp(3))(a, b, b)


def _add_sibling(g4, recv, cidx, name):
    nchip, _, r, n = g4.shape
    tr = _tile(r, 256, 8)

    def body(c_ref, a_ref, b_ref, o_ref):
        alo, ahi = _split_pairs(a_ref[...])
        blo, bhi = _split_pairs(b_ref[...])
        o_ref[...] = _join_pairs(alo + blo, ahi + bhi)

    grid_spec = pltpu.PrefetchScalarGridSpec(
        num_scalar_prefetch=1, grid=(nchip, r // tr),
        in_specs=[pl.BlockSpec((None, None, tr, n), lambda k, i, c: (k, c[0], i, 0)),
                  pl.BlockSpec((None, tr, n), lambda k, i, c: (k, i, 0))],
        out_specs=pl.BlockSpec((None, tr, n), lambda k, i, c: (k, i, 0)))
    return pl.pallas_call(body, grid_spec=grid_spec, out_shape=jax.ShapeDtypeStruct((nchip, r, n), F32),
                          name=name, compiler_params=_cp(2))(cidx, g4, recv)


def _add_chips(p4, recv3, chip_idx, name):
    _, r, n = p4.shape
    tr = _tile(r, 256, 8)

    def body(k_ref, a_ref, b_ref, o_ref):
        lo, hi = _split_pairs(a_ref[...])
        for j in range(3):
            blo, bhi = _split_pairs(b_ref[j])
            lo, hi = lo + blo, hi + bhi
        o_ref[:, :n] = lo
        o_ref[:, n:] = hi

    grid_spec = pltpu.PrefetchScalarGridSpec(
        num_scalar_prefetch=1, grid=(r // tr,),
        in_specs=[pl.BlockSpec((None, tr, n), lambda i, k: (k[0], i, 0)),
                  pl.BlockSpec((3, tr, n), lambda i, k: (0, i, 0))],
        out_specs=pl.BlockSpec((tr, 2 * n), lambda i, k: (i, 0)))
    return pl.pallas_call(body, grid_spec=grid_spec, out_shape=jax.ShapeDtypeStruct((r, 2 * n), F32),
                          name=name, compiler_params=_cp(1))(chip_idx, p4, recv3)


def _all_reduce_small(xs):
    r, n = xs.shape

    def body(x_ref, o_ref, buf, send_sems, recv_sems):
        x, y, c = _place()
        me = 4 * x + 2 * y + c
        buf[me] = x_ref[...]
        peers = []
        for k in range(1, N_DEV):
            px = (1 - x) if (k >> 2) & 1 else x
            py = (1 - y) if (k >> 1) & 1 else y
            pc = (1 - c) if k & 1 else c
            peers.append((px, py, pc))
        sends = [pltpu.make_async_remote_copy(src_ref=x_ref, dst_ref=buf.at[me], send_sem=send_sems.at[k],
                                              recv_sem=recv_sems.at[k], device_id=peer, device_id_type=MESH)
                 for k, peer in enumerate(peers)]
        for cp in sends:
            cp.start()
        for k, (px, py, pc) in enumerate(peers):
            pltpu.make_async_remote_copy(src_ref=x_ref, dst_ref=buf.at[4 * px + 2 * py + pc],
                                         send_sem=send_sems.at[k], recv_sem=recv_sems.at[k],
                                         device_id=(px, py, pc), device_id_type=MESH).wait_recv()
        for cp in sends:
            cp.wait_send()
        acc = buf[0]
        for dv in range(1, N_DEV):
            acc = acc + buf[dv]
        o_ref[...] = acc

    vm = pl.BlockSpec(memory_space=pltpu.VMEM)
    return pl.pallas_call(
        body, out_shape=jax.ShapeDtypeStruct((r, n), F32), in_specs=[vm], out_specs=vm,
        scratch_shapes=[pltpu.VMEM((N_DEV, r, n), F32), pltpu.SemaphoreType.DMA((7,)), pltpu.SemaphoreType.DMA((7,))],
        name="all_reduce_small", compiler_params=pltpu.CompilerParams(vmem_limit_bytes=VMEM_LIMIT))(xs)


def _pack_rows(arrs, quantum):
    parts, sizes = [], []
    for a in arrs:
        f = a.reshape(-1)
        pad = (-f.shape[0]) % quantum
        if pad:
            f = jnp.pad(f, (0, pad))
        parts.append(f)
        sizes.append(f.shape[0])
    return jnp.concatenate(parts).reshape(-1, LANE), sizes


def _unpack_rows(buf, sizes, shapes):
    flat = buf.reshape(-1)
    out, off = [], 0
    for sz, sh in zip(sizes, shapes):
        n = math.prod(sh)
        out.append(flat[off:off + n].reshape(sh))
        off += sz
    return out


def kernel(x, meta_tokens, mix_norm_g, w_in, conv_dw, conv_b, conv_ln_g, conv_ln_b, w_conv_out, gqa_q_norm_g, gqa_k_norm_g, w_gqa_out, mla_q_norm_g, w_mla_uq, mla_kv_norm_g, w_mla_ukv, w_mla_out, gate_b, w_out, ffn_norm_g, w_ffn_gate, w_ffn_up, w_ffn_down, final_norm_g, loss_target, m_meta_tokens, m_mix_norm_g, m_w_in, m_conv_dw, m_conv_b, m_conv_ln_g, m_conv_ln_b, m_w_conv_out, m_gqa_q_norm_g, m_gqa_k_norm_g, m_w_gqa_out, m_mla_q_norm_g, m_w_mla_uq, m_mla_kv_norm_g, m_w_mla_ukv, m_w_mla_out, m_gate_b, m_w_out, m_ffn_norm_g, m_w_ffn_gate, m_w_ffn_up, m_w_ffn_down, m_final_norm_g, v_meta_tokens, v_mix_norm_g, v_w_in, v_conv_dw, v_conv_b, v_conv_ln_g, v_conv_ln_b, v_w_conv_out, v_gqa_q_norm_g, v_gqa_k_norm_g, v_w_gqa_out, v_mla_q_norm_g, v_w_mla_uq, v_mla_kv_norm_g, v_w_mla_ukv, v_w_mla_out, v_gate_b, v_w_out, v_ffn_norm_g, v_w_ffn_gate, v_w_ffn_up, v_w_ffn_down, v_final_norm_g):
    w = dict(meta_tokens=meta_tokens, mix_norm_g=mix_norm_g, w_in=w_in, conv_dw=conv_dw, conv_b=conv_b,
             conv_ln_g=conv_ln_g, conv_ln_b=conv_ln_b, w_conv_out=w_conv_out, gqa_q_norm_g=gqa_q_norm_g,
             gqa_k_norm_g=gqa_k_norm_g, w_gqa_out=w_gqa_out, mla_q_norm_g=mla_q_norm_g, w_mla_uq=w_mla_uq,
             mla_kv_norm_g=mla_kv_norm_g, w_mla_ukv=w_mla_ukv, w_mla_out=w_mla_out, gate_b=gate_b, w_out=w_out,
             ffn_norm_g=ffn_norm_g, w_ffn_gate=w_ffn_gate, w_ffn_up=w_ffn_up, w_ffn_down=w_ffn_down,
             final_norm_g=final_norm_g)
    mom1 = dict(zip(W_NAMES, (m_meta_tokens, m_mix_norm_g, m_w_in, m_conv_dw, m_conv_b, m_conv_ln_g, m_conv_ln_b, m_w_conv_out, m_gqa_q_norm_g, m_gqa_k_norm_g, m_w_gqa_out, m_mla_q_norm_g, m_w_mla_uq, m_mla_kv_norm_g, m_w_mla_ukv, m_w_mla_out, m_gate_b, m_w_out, m_ffn_norm_g, m_w_ffn_gate, m_w_ffn_up, m_w_ffn_down, m_final_norm_g)))
    mom2 = dict(zip(W_NAMES, (v_meta_tokens, v_mix_norm_g, v_w_in, v_conv_dw, v_conv_b, v_conv_ln_g, v_conv_ln_b, v_w_conv_out, v_gqa_q_norm_g, v_gqa_k_norm_g, v_w_gqa_out, v_mla_q_norm_g, v_w_mla_uq, v_mla_kv_norm_g, v_w_mla_ukv, v_w_mla_out, v_gate_b, v_w_out, v_ffn_norm_g, v_w_ffn_gate, v_w_ffn_up, v_w_ffn_down, v_final_norm_g)))

    nb, seq, d = x.shape
    depth = w_in.shape[0]
    c_conv = conv_b.shape[1]
    kw = conv_dw.shape[1]
    hd = HEAD_DIM
    hq = w_gqa_out.shape[1] // hd
    nh = w_mla_out.shape[1] // hd
    qr, kvr = mla_q_norm_g.shape[1], mla_kv_norm_g.shape[1]
    d_in = w_in.shape[2] * N_DEV
    hkv = (d_in - 2 * c_conv - hq * hd - qr - kvr - MLA_ROPE_DIM - N_BRANCH * d) // (2 * hd)
    l_valid = seq + N_META
    lp = -(-(l_valid + kw // 2) // LANE) * LANE
    m_tok = nb * lp
    mla_qk = hd + MLA_ROPE_DIM
    gqa_scale, mla_scale = 1.0 / math.sqrt(hd), 1.0 / math.sqrt(mla_qk)

    off_gate = 0
    off_conv = N_BRANCH * d
    off_q = off_conv + 2 * c_conv
    off_cq = off_q + (hq + 2 * hkv) * hd
    off_ckv = off_cq + qr
    off_kpe = off_ckv + kvr
    n_proj = off_kpe + LANE

    x_id, y_id, c_id = _place()
    me = 4 * x_id + 2 * y_id + c_id

    tm_wide = _tile(lp, 272, 16)
    tm_mid = _tile(lp, 544, 16)

    sm_buf, sm_sizes = _pack_rows([meta_tokens, conv_dw], 8 * LANE)
    sm_all = _all_gather([sm_buf], True, "ag_small")[0]
    sm_all = sm_all.reshape(N_DEV, -1)
    o0 = 0
    meta_all = sm_all[:, o0:o0 + meta_tokens.size].reshape((N_DEV,) + meta_tokens.shape)
    meta_full = jnp.transpose(meta_all, (1, 0, 2)).reshape(N_META, d)
    o0 = sm_sizes[0]
    dw_all = sm_all[:, o0:o0 + conv_dw.size].reshape((N_DEV, depth, kw, c_conv // N_DEV))
    dw_full = jnp.transpose(dw_all, (1, 2, 0, 3)).reshape(depth, kw, c_conv)

    n_head_cols = d_in - N_BRANCH * d - MLA_ROPE_DIM
    rb = MLA_ROPE_DIM
    nb_gate, nb_head = N_BRANCH * d // rb, n_head_cols // rb
    assert n_head_cols % rb == 0 and (N_BRANCH * d) % rb == 0 and n_proj == (nb_gate + nb_head + 2) * rb

    def in_src(i):
        return jnp.where(i < nb_gate, nb_head + 1 + i,
                         jnp.where(i < nb_gate + nb_head, i - nb_gate, jnp.where(i == nb_gate + nb_head, nb_head, 0)))

    def in_dst(i):
        return jnp.where(i < nb_head, nb_gate + i, jnp.where(i == nb_head, nb_gate + nb_head, i - nb_head - 1))

    def uq_src(i):
        j = i - 2 * nh
        return jnp.where(i < 2 * nh, 3 * (i // 2) + i % 2, 3 * (j // 2) + 2)

    def uq_dst(i):
        return jnp.where(i % 3 < 2, 2 * (i // 3) + i % 3, 2 * nh + 2 * (i // 3))

    ag_pending, tokens = [], []
    for i in range(depth):
        shards = [_pack_pairs_xla(w[n][i] if n in ROW_SHARDED else w[n][i].T) for n in BIG]
        bufs = [lax.dynamic_update_slice(lax.empty((N_DEV,) + a.shape, F32), a[None], (me, 0, 0)) for a in shards]
        *pend, tok = _ici_start(bufs, [], tokens[-1:], gather=True, name="ag_ici_start_%d" % i)
        ag_pending.append(pend)
        tokens.append(tok)

    def all_gather_layer(i, after):
        bufs, _ = _ici_wait(*ag_pending[i], after, gather=True, name="ag_ici_wait_%d" % i)
        gath = _gather_to_sibling(bufs)
        pk = {n: g.reshape(-1, g.shape[-1]) for n, g in zip(BIG, gath)}
        wl = {n: _unpack(pk[n], name="unpack_" + n) for n in BIG if n not in ('w_in', 'w_mla_uq', 'w_mla_ukv')}
        wl['w_in'] = _unpack(pk['w_in'], name="unpack_w_in", out_rows=n_proj, tr=rb, src_fn=in_src,
                             zero_fn=lambda i: i == nb_gate + nb_head + 1)
        wl['w_mla_uq'] = _unpack(pk['w_mla_uq'], name="unpack_w_mla_uq", out_rows=2 * nh * hd, tr=rb, src_fn=uq_src,
                                 zero_fn=lambda i: (i >= 2 * nh) & ((i - 2 * nh) % 2 == 1))
        wl['w_mla_ukv'] = _unpack(pk['w_mla_ukv'], name="unpack_w_mla_ukv", out_rows=2 * nh * hd, tr=hd,
                                  src_fn=lambda i: jnp.where(i < nh, 2 * i, 2 * (i - nh) + 1))
        return wl

    full = [None] * depth

    cos_g, sin_g = _rope_tables(lp, l_valid, hd // 4)
    cos_m, sin_m = _rope_tables(lp, l_valid, MLA_ROPE_DIM // 4)

    def vec2(a):
        return a.reshape(1, -1)

    meta_b = jnp.broadcast_to(meta_full[None], (nb, N_META, d))
    h = jnp.concatenate([meta_b, x, jnp.zeros((nb, lp - l_valid, d), F32)], axis=1).reshape(m_tok, d)
    tgt = jnp.pad(loss_target, ((0, 0), (N_META, lp - l_valid), (0, 0))).reshape(m_tok, d)

    saved = []
    for i in range(depth):
        s = {}
        s['h_in'] = h
        u = _rms_fwd(h, d, 0, vec2(mix_norm_g[i]), tm_wide, "mix_norm_fwd")
        wl = full[i] = all_gather_layer(i, tokens[-1:] if i == 0 else [h])
        proj = _matmul(u, wl['w_in'], mode="nt", out_dtype=F32, name="mm_in")
        cc = _conv_fwd(proj, dw_full[i], vec2(conv_b[i]), c_conv=c_conv, off_a=off_conv, nb=nb, lp=lp, l_valid=l_valid)
        sc = _ln_silu_fwd(cc, vec2(conv_ln_g[i]), vec2(conv_ln_b[i]), tm_mid)
        ya = _matmul(sc, wl['w_conv_out'], mode="nt", out_dtype=F32, name="mm_conv_out")
        qh, kh, vh = _gqa_prep_fwd(proj, vec2(gqa_q_norm_g[i]), vec2(gqa_k_norm_g[i]), cos_g, sin_g,
                                   off_q=off_q, hq=hq, hkv=hkv, nb=nb, lp=lp, tm=tm_mid, q_scale=gqa_scale)
        ob = _attn_fwd(qh, kh, vh, l_valid=l_valid, name="gqa_attn_fwd")
        yb = _matmul(ob, wl['w_gqa_out'], mode="nt", out_dtype=F32, name="mm_gqa_out")
        cqn = _rms_fwd(proj, qr, _cb(off_cq, qr), vec2(mla_q_norm_g[i]), tm_mid, "mla_q_norm_fwd")
        ckvn = _rms_fwd(proj, kvr, _cb(off_ckv, kvr), vec2(mla_kv_norm_g[i]), tm_mid, "mla_kv_norm_fwd")
        qm = _matmul(cqn, wl['w_mla_uq'], mode="nt", out_dtype=F32, name="mm_mla_uq")
        kvm = _matmul(ckvn, wl['w_mla_ukv'], mode="nt", out_dtype=F32, name="mm_mla_ukv")
        qc, kc, vc = _mla_prep_fwd(qm, kvm, proj, cos_m, sin_m, off_kpe=off_kpe, nh=nh, nb=nb, lp=lp, tm=tm_mid,
                                   q_scale=mla_scale)
        oc = _attn_fwd(qc, kc, vc, l_valid=l_valid, name="mla_attn_fwd")
        yc = _matmul(oc, wl['w_mla_out'], mode="nt", out_dtype=F32, name="mm_mla_out")
        merged = _gate_fwd(proj, vec2(gate_b[i]), ya, yb, yc, d=d, tm=tm_wide)
        h2 = _matmul(merged, wl['w_out'], mode="nn", out_dtype=F32, name="mm_out", residual=h)
        vn = _rms_fwd(h2, d, 0, vec2(ffn_norm_g[i]), tm_wide, "ffn_norm_fwd")
        gt = _matmul(vn, wl['w_ffn_gate'], mode="nt", out_dtype=F32, name="mm_ffn_gate")
        up = _matmul(vn, wl['w_ffn_up'], mode="nt", out_dtype=F32, name="mm_ffn_up")
        act = _swiglu_fwd(gt, up, tm_wide)
        h = _matmul(act, wl['w_ffn_down'], mode="nn", out_dtype=F32, name="mm_ffn_down", residual=h2)
        s.update(u=u, proj=proj, cc=cc, sc=sc, ya=ya, qh=qh, kh=kh, vh=vh, ob=ob, yb=yb, cqn=cqn, ckvn=ckvn,
                 qc=qc, kc=kc, vc=vc, oc=oc, yc=yc, merged=merged, h2=h2, vn=vn, gt=gt, up=up, act=act)
        saved.append(s)

    dh, dg_final, loss_part = _loss_head(h, vec2(final_norm_g), tgt, lp=lp, l_valid=l_valid, tm=tm_wide)

    gsmall = {n: [None] * depth for n in SMALL if n != 'final_norm_g'}
    gdw = [None] * depth
    gbig = [None] * depth
    cidx = jnp.reshape(c_id, (1,)).astype(jnp.int32)
    chip_idx = jnp.reshape(2 * x_id + y_id, (1,)).astype(jnp.int32)

    def reduce_scatter_start(gl, i):
        g4s = [gl[n].reshape(N_DEV // 2, 2, -1, gl[n].shape[-1]) for n in BIG]
        from_sibling = _swap_with_sibling(g4s)
        p4s = [_add_sibling(g, r, cidx, "rs_add_sibling_" + n) for n, g, r in zip(BIG, g4s, from_sibling)]
        *pend, tok = _ici_start(p4s, [(3,) + p.shape[1:] for p in p4s], [], gather=False,
                                name="rs_ici_start_%d" % i)
        return pend, tok

    def reduce_scatter_finish(pend, after, i):
        p4s, from_chips = _ici_wait(*pend, after, gather=False, name="rs_ici_wait_%d" % i)
        out = {}
        for n, p, r in zip(BIG, p4s, from_chips):
            g = _add_chips(p, r, chip_idx, "rs_add_chips_" + n)
            out[n] = g if n in ROW_SHARDED else g.T
        return out

    rs_pending, rs_token = None, []
    for i in reversed(range(depth)):
        s = saved[i]
        proj = s['proj']
        wl = full[i]
        gl = {}
        dact = _matmul(dh, wl['w_ffn_down'], mode="nt", out_dtype=F32, name="mm_ffn_down_dx", deps=rs_token)
        gl['w_ffn_down'] = _matmul_tn_packed(s['act'], dh, "mm_ffn_down_dw")
        dgt, dup = _swiglu_bwd(dact, s['gt'], s['up'], tm_wide)
        dvn = _matmul(dgt, wl['w_ffn_gate'], mode="nn", out_dtype=F32, name="mm_ffn_gate_dx")
        dvn = _matmul(dup, wl['w_ffn_up'], mode="nn", out_dtype=F32, name="mm_ffn_up_dx", residual=dvn)
        gl['w_ffn_gate'] = _matmul_tn_packed(dgt, s['vn'], "mm_ffn_gate_dw")
        gl['w_ffn_up'] = _matmul_tn_packed(dup, s['vn'], "mm_ffn_up_dw")
        dh2, gsmall['ffn_norm_g'][i] = _rms_bwd(s['h2'], d, 0, vec2(ffn_norm_g[i]), dvn, tm_wide, F32,
                                                "ffn_norm_bwd", add=dh)
        dmg = _matmul(dh2, wl['w_out'], mode="nt", out_dtype=F32, name="mm_out_dx")
        gl['w_out'] = _matmul_tn_packed(s['merged'], dh2, "mm_out_dw")
        dya, dyb, dyc, dg0, dg1, dg2, db0, db1, db2 = _gate_bwd(dmg, proj, vec2(gate_b[i]), s['ya'], s['yb'],
                                                                s['yc'], d=d, tm=tm_wide)
        gsmall['gate_b'][i] = jnp.concatenate([db0, db1, db2], axis=1)
        dsc = _matmul(dya, wl['w_conv_out'], mode="nn", out_dtype=F32, name="mm_conv_out_dx")
        gl['w_conv_out'] = _matmul_tn_packed(dya, s['sc'], "mm_conv_out_dw")
        dcc, gsmall['conv_ln_g'][i], gsmall['conv_ln_b'][i] = _ln_silu_bwd(
            s['cc'], vec2(conv_ln_g[i]), vec2(conv_ln_b[i]), dsc, tm_mid)
        da, dgc, gdw[i], gsmall['conv_b'][i] = _conv_bwd(proj, dcc, dw_full[i], c_conv=c_conv, off_a=off_conv,
                                                         nb=nb, lp=lp, l_valid=l_valid)
        dob = _matmul(dyb, wl['w_gqa_out'], mode="nn", out_dtype=BF16, name="mm_gqa_out_dx")
        gl['w_gqa_out'] = _matmul_tn_packed(dyb, s['ob'], "mm_gqa_out_dw")
        dqh, dkh, dvh = _attn_bwd(s['qh'], s['kh'], s['vh'], dob, l_valid=l_valid, name="gqa_attn_bwd")
        dq, dkv, gsmall['gqa_q_norm_g'][i], gsmall['gqa_k_norm_g'][i] = _gqa_prep_bwd(
            proj, vec2(gqa_q_norm_g[i]), vec2(gqa_k_norm_g[i]), cos_g, sin_g, dqh, dkh, dvh,
            off_q=off_q, hq=hq, hkv=hkv, nb=nb, lp=lp, tm=tm_mid, q_scale=gqa_scale)
        doc = _matmul(dyc, wl['w_mla_out'], mode="nn", out_dtype=BF16, name="mm_mla_out_dx")
        gl['w_mla_out'] = _matmul_tn_packed(dyc, s['oc'], "mm_mla_out_dw")
        dqc, dkc, dvc = _attn_bwd(s['qc'], s['kc'], s['vc'], doc, l_valid=l_valid, name="mla_attn_bwd")
        dqm, dkvm, dkpe = _mla_prep_bwd(dqc, dkc, dvc, cos_m, sin_m, nh=nh, nb=nb, lp=lp, tm=tm_mid,
                                        q_scale=mla_scale)
        dcqn = _matmul(dqm, wl['w_mla_uq'], mode="nn", out_dtype=F32, name="mm_mla_uq_dx")
        guq = _matmul_tn_packed(dqm, s['cqn'], "mm_mla_uq_dw")
        dckvn = _matmul(dkvm, wl['w_mla_ukv'], mode="nn", out_dtype=F32, name="mm_mla_ukv_dx")
        gukv = _matmul_tn_packed(dkvm, s['ckvn'], "mm_mla_ukv_dw")
        dcq, gsmall['mla_q_norm_g'][i] = _rms_bwd(proj, qr, _cb(off_cq, qr), vec2(mla_q_norm_g[i]), dcqn, tm_mid,
                                                  BF16, "mla_q_norm_bwd")
        dckv, gsmall['mla_kv_norm_g'][i] = _rms_bwd(proj, kvr, _cb(off_ckv, kvr), vec2(mla_kv_norm_g[i]), dckvn,
                                                    tm_mid, BF16, "mla_kv_norm_bwd")
        gl['w_mla_uq'] = _gather_rows(guq, name="perm_mla_uq_dw", out_rows=nh * mla_qk, tr=rb, src_fn=uq_dst)
        gl['w_mla_ukv'] = _gather_rows(gukv, name="perm_mla_ukv_dw", out_rows=2 * nh * hd, tr=hd,
                                       src_fn=lambda i: i // 2 + nh * (i % 2))
        dproj = jnp.concatenate([dg0, dg1, dg2, da, dgc, dq, dkv, dcq, dckv, dkpe], axis=1)
        du = _matmul(dproj, wl['w_in'], mode="nn", out_dtype=F32, name="mm_in_dx")
        gin = _matmul_tn_packed(dproj, s['u'], "mm_in_dw")
        gl['w_in'] = _gather_rows(gin, name="perm_in_dw", out_rows=d_in, tr=rb, src_fn=in_dst)
        dh, gsmall['mix_norm_g'][i] = _rms_bwd(s['h_in'], d, 0, vec2(mix_norm_g[i]), du, tm_wide, F32,
                                               "mix_norm_bwd", add=dh2)
        started, tok = reduce_scatter_start(gl, i)
        if rs_pending is not None:
            gbig[i + 1] = reduce_scatter_finish(rs_pending, [dh, tok], i + 1)
        rs_pending, rs_token = started, [tok]
    gbig[0] = reduce_scatter_finish(rs_pending, rs_token, 0)

    dh0 = dh.reshape(nb, lp, d)
    grad_x = dh0[:, N_META:l_valid]
    gmeta_full = jnp.sum(dh0[:, :N_META], axis=0)
    grads = {n: jnp.stack([gbig[i][n] for i in range(depth)]) for n in BIG}

    small_list = [jnp.stack(gsmall[n]).reshape(w[n].shape) for n in SMALL if n != 'final_norm_g']
    small_list += [dg_final.reshape(final_norm_g.shape), gmeta_full, jnp.stack(gdw), loss_part]
    ar_buf, ar_sizes = _pack_rows(small_list, 8 * LANE)
    ar = _all_reduce_small(ar_buf)
    ar_shapes = [w[n].shape for n in SMALL] + [(N_META, d), (depth, kw, c_conv), (1, LANE)]
    ar_out = _unpack_rows(ar, ar_sizes, ar_shapes)
    for n, g in zip(SMALL, ar_out):
        grads[n] = g
    dcol = d // N_DEV
    grads['meta_tokens'] = lax.dynamic_slice(ar_out[len(SMALL)], (0, me * dcol), (N_META, dcol))
    ccol = c_conv // N_DEV
    grads['conv_dw'] = lax.dynamic_slice(ar_out[len(SMALL) + 1], (0, 0, me * ccol),
                                         (depth, kw, ccol)).reshape(conv_dw.shape)
    loss = ar_out[len(SMALL) + 2][0, 0]

    delta, new_m, new_v = {}, {}, {}
    for n in BIG:
        sh = w[n].shape
        two_d = (sh[0] * sh[1], sh[2])
        dl, mn, vn_ = _adamw(w[n].reshape(two_d), grads[n].reshape(two_d), mom1[n].reshape(two_d),
                             mom2[n].reshape(two_d), "adamw_" + n)
        delta[n], new_m[n], new_v[n] = dl.reshape(sh), mn.reshape(sh), vn_.reshape(sh)
    rest = [n for n in W_NAMES if n not in BIG]
    pw, psz = _pack_rows([w[n] for n in rest], 8 * LANE)
    pg, _ = _pack_rows([grads[n] for n in rest], 8 * LANE)
    pm, _ = _pack_rows([mom1[n] for n in rest], 8 * LANE)
    pv, _ = _pack_rows([mom2[n] for n in rest], 8 * LANE)
    dl, mn, vn_ = _adamw(pw, pg, pm, pv, "adamw_small")
    shapes = [w[n].shape for n in rest]
    for n, a, b, c in zip(rest, _unpack_rows(dl, psz, shapes), _unpack_rows(mn, psz, shapes),
                          _unpack_rows(vn_, psz, shapes)):
        delta[n], new_m[n], new_v[n] = a, b, c

    return (loss, grad_x, *[grads[n] for n in W_NAMES], *[delta[n] for n in W_NAMES],
            *[new_m[n] for n in W_NAMES], *[new_v[n] for n in W_NAMES])
```

```python
import functools
import math

import jax
import jax.numpy as jnp
from jax import lax
from jax.experimental import pallas as pl
from jax.experimental.pallas import tpu as pltpu

F32 = jnp.float32
BF16 = jnp.bfloat16

N_META = 16
GRID_W = 64
ROPE_THETA = 10000.0
NORM_EPS = 1e-6
HEAD_DIM = 128
MLA_ROPE_DIM = 64
N_BRANCH = 3
N_DEV = 8
LANE = 128
VMEM_LIMIT = 56 * 1024 * 1024

ADAM_LR = 0.001
ADAM_B1 = 0.9
ADAM_B2 = 0.999
ADAM_EPS = 1e-08
ADAM_WD = 0.01
ADAM_STEP = 10

W_NAMES = ['meta_tokens', 'mix_norm_g', 'w_in', 'conv_dw', 'conv_b', 'conv_ln_g', 'conv_ln_b', 'w_conv_out',
           'gqa_q_norm_g', 'gqa_k_norm_g', 'w_gqa_out', 'mla_q_norm_g', 'w_mla_uq', 'mla_kv_norm_g', 'w_mla_ukv',
           'w_mla_out', 'gate_b', 'w_out', 'ffn_norm_g', 'w_ffn_gate', 'w_ffn_up', 'w_ffn_down', 'final_norm_g']
BIG = ['w_in', 'w_conv_out', 'w_gqa_out', 'w_mla_uq', 'w_mla_ukv', 'w_mla_out', 'w_out', 'w_ffn_gate', 'w_ffn_up',
       'w_ffn_down']
ROW_SHARDED = ('w_out', 'w_ffn_down')
GROUPS = (('w_in', 'w_conv_out', 'w_gqa_out', 'w_mla_uq', 'w_mla_ukv', 'w_mla_out', 'w_out'),
          ('w_ffn_gate', 'w_ffn_up', 'w_ffn_down'))
SMALL = ['mix_norm_g', 'conv_b', 'conv_ln_g', 'conv_ln_b', 'gqa_q_norm_g', 'gqa_k_norm_g', 'mla_q_norm_g',
         'mla_kv_norm_g', 'gate_b', 'ffn_norm_g', 'final_norm_g']

NT = (((1,), (1,)), ((), ()))
TN = (((0,), (0,)), ((), ()))
NN = (((1,), (0,)), ((), ()))


def _tile(n, target, mult):
    best = None
    for t in range(mult, min(n, target) + 1, mult):
        if n % t == 0:
            best = t
    assert best is not None, (n, target, mult)
    return best


def _cp(n):
    return pltpu.CompilerParams(dimension_semantics=("arbitrary",) * n, vmem_limit_bytes=VMEM_LIMIT)


def _sig(x):
    return jax.nn.sigmoid(x)


def _cb(off, w):
    assert off % w == 0, (off, w)
    return off // w


def _matmul(a, b, *, mode, out_dtype, name, residual=None, deps=()):
    if mode == "nn":
        (m, k), (k2, n) = a.shape, b.shape
    elif mode == "nt":
        (m, k), (n, k2) = a.shape, b.shape
    else:
        (k, m), (k2, n) = a.shape, b.shape
    assert k == k2, (a.shape, b.shape, mode)
    tm = _tile(m, 1088, 128 if mode == "tn" else 16)
    tn = _tile(n, 1024, 128)
    tk = _tile(k, 2176, 128)
    nk = k // tk
    dims = {"nn": NN, "nt": NT, "tn": TN}[mode]
    a_spec = (pl.BlockSpec((tk, tm), lambda i, j, kk: (kk, i)) if mode == "tn"
              else pl.BlockSpec((tm, tk), lambda i, j, kk: (i, kk)))
    b_spec = (pl.BlockSpec((tn, tk), lambda i, j, kk: (j, kk)) if mode == "nt"
              else pl.BlockSpec((tk, tn), lambda i, j, kk: (kk, j)))
    o_spec = pl.BlockSpec((tm, tn), lambda i, j, kk: (i, j))
    has_res = residual is not None

    nd = len(deps)

    def body(*refs):
        refs = refs[:len(refs) - 2 - nd] + refs[len(refs) - 2:]
        if has_res:
            a_ref, b_ref, r_ref, o_ref, acc = refs
        else:
            a_ref, b_ref, o_ref, acc = refs
        kk = pl.program_id(2)

        @pl.when(kk == 0)
        def _():
            acc[...] = jnp.zeros_like(acc)

        acc[...] += lax.dot_general(a_ref[...].astype(BF16), b_ref[...].astype(BF16), dims,
                                    preferred_element_type=F32)

        @pl.when(kk == nk - 1)
        def _():
            r = acc[...]
            if has_res:
                r = r + r_ref[...]
            o_ref[...] = r.astype(o_ref.dtype)

    ins = [a, b] + ([residual] if has_res else []) + list(deps)
    in_specs = [a_spec, b_spec] + ([o_spec] if has_res else []) + [pl.BlockSpec(memory_space=pl.ANY)] * nd
    return pl.pallas_call(
        body, grid=(m // tm, n // tn, nk), in_specs=in_specs, out_specs=o_spec,
        out_shape=jax.ShapeDtypeStruct((m, n), out_dtype), scratch_shapes=[pltpu.VMEM((tm, tn), F32)],
        name=name, compiler_params=_cp(3))(*ins)


def _rms_fwd(x, w, cb, g, tm, name):
    m = x.shape[0]

    def body(x_ref, g_ref, o_ref):
        xv = x_ref[...]
        r = lax.rsqrt(jnp.mean(xv * xv, axis=-1, keepdims=True) + NORM_EPS)
        o_ref[...] = (xv * r * g_ref[...]).astype(o_ref.dtype)

    return pl.pallas_call(
        body, grid=(m // tm,),
        in_specs=[pl.BlockSpec((tm, w), lambda i: (i, cb)), pl.BlockSpec((1, w), lambda i: (0, 0))],
        out_specs=pl.BlockSpec((tm, w), lambda i: (i, 0)), out_shape=jax.ShapeDtypeStruct((m, w), BF16),
        name=name, compiler_params=_cp(1))(x, g)


def _rms_bwd(x, w, cb, g, dy, tm, out_dtype, name, add=None):
    m = x.shape[0]
    has_add = add is not None

    def body(*refs):
        if has_add:
            x_ref, g_ref, dy_ref, add_ref, dx_ref, dg_ref = refs
        else:
            x_ref, g_ref, dy_ref, dx_ref, dg_ref = refs
        xv = x_ref[...]
        dyv = dy_ref[...].astype(F32)
        r = lax.rsqrt(jnp.mean(xv * xv, axis=-1, keepdims=True) + NORM_EPS)
        t = dyv * g_ref[...]
        dx = r * t - xv * (r * r * r) * jnp.mean(t * xv, axis=-1, keepdims=True)
        if has_add:
            dx = dx + add_ref[...]
        dx_ref[...] = dx.astype(dx_ref.dtype)

        @pl.when(pl.program_id(0) == 0)
        def _():
            dg_ref[...] = jnp.zeros_like(dg_ref)

        dg_ref[...] += jnp.sum(dyv * xv * r, axis=0, keepdims=True)

    row = pl.BlockSpec((tm, w), lambda i: (i, 0))
    vec = pl.BlockSpec((1, w), lambda i: (0, 0))
    ins = [x, g, dy] + ([add] if has_add else [])
    in_specs = [pl.BlockSpec((tm, w), lambda i: (i, cb)), vec, row] + ([row] if has_add else [])
    return pl.pallas_call(
        body, grid=(m // tm,), in_specs=in_specs, out_specs=(row, vec),
        out_shape=(jax.ShapeDtypeStruct((m, w), out_dtype), jax.ShapeDtypeStruct((1, w), F32)),
        name=name, compiler_params=_cp(1))(*ins)


def _ln_silu_fwd(c, lg, lb, tm):
    m, w = c.shape

    def body(c_ref, g_ref, b_ref, o_ref):
        cv = c_ref[...]
        xc = cv - jnp.mean(cv, axis=-1, keepdims=True)
        r = lax.rsqrt(jnp.mean(xc * xc, axis=-1, keepdims=True) + NORM_EPS)
        yl = xc * r * g_ref[...] + b_ref[...]
        o_ref[...] = (yl * _sig(yl)).astype(o_ref.dtype)

    row = pl.BlockSpec((tm, w), lambda i: (i, 0))
    vec = pl.BlockSpec((1, w), lambda i: (0, 0))
    return pl.pallas_call(body, grid=(m // tm,), in_specs=[row, vec, vec], out_specs=row,
                          out_shape=jax.ShapeDtypeStruct((m, w), BF16), name="ln_silu_fwd",
                          compiler_params=_cp(1))(c, lg, lb)


def _ln_silu_bwd(c, lg, lb, ds, tm):
    m, w = c.shape

    def body(c_ref, g_ref, b_ref, ds_ref, dc_ref, dg_ref, db_ref):
        cv = c_ref[...]
        xc = cv - jnp.mean(cv, axis=-1, keepdims=True)
        r = lax.rsqrt(jnp.mean(xc * xc, axis=-1, keepdims=True) + NORM_EPS)
        nv = xc * r
        yl = nv * g_ref[...] + b_ref[...]
        sg = _sig(yl)
        dyl = ds_ref[...] * (sg * (1.0 + yl * (1.0 - sg)))
        dn = dyl * g_ref[...]
        dc = r * (dn - jnp.mean(dn, axis=-1, keepdims=True) - nv * jnp.mean(dn * nv, axis=-1, keepdims=True))
        dc_ref[...] = dc

        @pl.when(pl.program_id(0) == 0)
        def _():
            dg_ref[...] = jnp.zeros_like(dg_ref)
            db_ref[...] = jnp.zeros_like(db_ref)

        dg_ref[...] += jnp.sum(dyl * nv, axis=0, keepdims=True)
        db_ref[...] += jnp.sum(dyl, axis=0, keepdims=True)

    row = pl.BlockSpec((tm, w), lambda i: (i, 0))
    vec = pl.BlockSpec((1, w), lambda i: (0, 0))
    return pl.pallas_call(
        body, grid=(m // tm,), in_specs=[row, vec, vec, row], out_specs=(row, vec, vec),
        out_shape=(jax.ShapeDtypeStruct((m, w), F32), jax.ShapeDtypeStruct((1, w), F32),
                   jax.ShapeDtypeStruct((1, w), F32)),
        name="ln_silu_bwd", compiler_params=_cp(1))(c, lg, lb, ds)


CONV_MARGIN = 16
CONV_ROWS = 128


def _conv_fwd(proj, dw, bias, *, c_conv, off_a, nb, lp, l_valid):
    kw = dw.shape[0]
    half = kw // 2
    cw = LANE
    mg = CONV_MARGIN
    assert half <= mg - 1 and lp % CONV_ROWS == 0

    def body(a_ref, g_ref, w_ref, b_ref, c_ref, zp):
        t = lax.broadcasted_iota(jnp.int32, (lp, cw), 0)
        z = jnp.where(t < l_valid, a_ref[...] * _sig(g_ref[...]), 0.0)
        zp[0:mg, :] = jnp.zeros((mg, cw), F32)
        zp[mg + lp:mg + lp + mg, :] = jnp.zeros((mg, cw), F32)
        zp[mg:mg + lp, :] = z
        for r0 in range(0, lp, CONV_ROWS):
            acc = jnp.zeros((CONV_ROWS, cw), F32) + b_ref[...]
            for k in range(kw):
                s0 = mg + r0 + k - half
                acc = acc + w_ref[k:k + 1, :] * zp[s0:s0 + CONV_ROWS, :]
            c_ref[r0:r0 + CONV_ROWS, :] = acc

    ja, jg = _cb(off_a, cw), _cb(off_a + c_conv, cw)
    return pl.pallas_call(
        body, grid=(nb, c_conv // cw),
        in_specs=[pl.BlockSpec((lp, cw), lambda b, j: (b, ja + j)), pl.BlockSpec((lp, cw), lambda b, j: (b, jg + j)),
                  pl.BlockSpec((kw, cw), lambda b, j: (0, j)), pl.BlockSpec((1, cw), lambda b, j: (0, j))],
        out_specs=pl.BlockSpec((lp, cw), lambda b, j: (b, j)),
        out_shape=jax.ShapeDtypeStruct((nb * lp, c_conv), F32),
        scratch_shapes=[pltpu.VMEM((lp + 2 * mg, cw), F32)], name="conv_fwd", compiler_params=_cp(2))(proj, proj, dw, bias)


def _conv_bwd(proj, dc, dw, *, c_conv, off_a, nb, lp, l_valid):
    kw = dw.shape[0]
    half = kw // 2
    cw = LANE
    mg = CONV_MARGIN

    def body(a_ref, g_ref, dc_ref, w_ref, da_ref, dg_ref, ddw_ref, dcb_ref, zp, dcp):
        t = lax.broadcasted_iota(jnp.int32, (lp, cw), 0)
        z = jnp.where(t < l_valid, a_ref[...] * _sig(g_ref[...]), 0.0)
        for buf in (zp, dcp):
            buf[0:mg, :] = jnp.zeros((mg, cw), F32)
            buf[mg + lp:mg + lp + mg, :] = jnp.zeros((mg, cw), F32)
        zp[mg:mg + lp, :] = z
        dcv = dc_ref[...]
        dcp[mg:mg + lp, :] = dcv

        @pl.when(pl.program_id(1) == 0)
        def _():
            ddw_ref[...] = jnp.zeros_like(ddw_ref)
            dcb_ref[...] = jnp.zeros_like(dcb_ref)

        dcb_ref[...] += jnp.sum(dcv, axis=0, keepdims=True)
        for r0 in range(0, lp, CONV_ROWS):
            acc = jnp.zeros((CONV_ROWS, cw), F32)
            for k in range(kw):
                s0 = mg + r0 - k + half
                acc = acc + w_ref[k:k + 1, :] * dcp[s0:s0 + CONV_ROWS, :]
            tc = lax.broadcasted_iota(jnp.int32, (CONV_ROWS, cw), 0) + r0
            dz = jnp.where(tc < l_valid, acc, 0.0)
            sg = _sig(g_ref[r0:r0 + CONV_ROWS, :])
            da_ref[r0:r0 + CONV_ROWS, :] = (dz * sg).astype(da_ref.dtype)
            dg_ref[r0:r0 + CONV_ROWS, :] = (dz * a_ref[r0:r0 + CONV_ROWS, :] * sg * (1.0 - sg)).astype(dg_ref.dtype)
        for k in range(kw):
            acc = jnp.zeros((CONV_ROWS, cw), F32)
            for r0 in range(0, lp, CONV_ROWS):
                s0 = mg + r0 + k - half
                acc = acc + dc_ref[r0:r0 + CONV_ROWS, :] * zp[s0:s0 + CONV_ROWS, :]
            ddw_ref[k:k + 1, :] += jnp.sum(acc, axis=0, keepdims=True)

    ja, jg = _cb(off_a, cw), _cb(off_a + c_conv, cw)
    seq = pl.BlockSpec((lp, cw), lambda j, b: (b, j))
    return pl.pallas_call(
        body, grid=(c_conv // cw, nb),
        in_specs=[pl.BlockSpec((lp, cw), lambda j, b: (b, ja + j)), pl.BlockSpec((lp, cw), lambda j, b: (b, jg + j)),
                  seq, pl.BlockSpec((kw, cw), lambda j, b: (0, j))],
        out_specs=(seq, seq, pl.BlockSpec((kw, cw), lambda j, b: (0, j)), pl.BlockSpec((1, cw), lambda j, b: (0, j))),
        out_shape=(jax.ShapeDtypeStruct((nb * lp, c_conv), BF16), jax.ShapeDtypeStruct((nb * lp, c_conv), BF16),
                   jax.ShapeDtypeStruct((kw, c_conv), F32), jax.ShapeDtypeStruct((1, c_conv), F32)),
        scratch_shapes=[pltpu.VMEM((lp + 2 * mg, cw), F32), pltpu.VMEM((lp + 2 * mg, cw), F32)],
        name="conv_bwd", compiler_params=_cp(2))(proj, proj, dc, dw)


def _swap_halves(x, half):
    fwd = pltpu.roll(x, LANE - half, axis=1)
    bwd = pltpu.roll(x, half, axis=1)
    lane = lax.broadcasted_iota(jnp.int32, x.shape, 1)
    return jnp.where((lane & (2 * half - 1)) < half, fwd, bwd)


def _rope(x, cos, sin, half):
    return x * cos + _swap_halves(x, half) * sin


def _rope_t(dy, cos, sin, half):
    return dy * cos + _swap_halves(dy * sin, half)


def _rope_tables(lp, l_valid, half):
    t = jnp.arange(lp)
    n = jnp.clip(t - N_META, 0, None)
    real = (t >= N_META) & (t < l_valid)
    row = jnp.where(real, n // GRID_W, 0).astype(F32)
    col = jnp.where(real, n % GRID_W, 0).astype(F32)
    inv = ROPE_THETA ** (-jnp.arange(half, dtype=F32) / half)
    ar, ac = row[:, None] * inv[None, :], col[:, None] * inv[None, :]
    cos = jnp.concatenate([jnp.cos(ar), jnp.cos(ar), jnp.cos(ac), jnp.cos(ac)], axis=1)
    sin = jnp.concatenate([-jnp.sin(ar), jnp.sin(ar), -jnp.sin(ac), jnp.sin(ac)], axis=1)
    padw = LANE - 4 * half
    if padw:
        cos = jnp.pad(cos, ((0, 0), (0, padw)))
        sin = jnp.pad(sin, ((0, 0), (0, padw)))
    return cos.astype(F32), sin.astype(F32)


def _gqa_prep_fwd(proj, qg, kg, cos, sin, *, off_q, hq, hkv, nb, lp, tm, q_scale):
    hd = HEAD_DIM
    half = hd // 4
    wq, wkv = hq * hd, 2 * hkv * hd
    nt = lp // tm

    def body(q_ref, kv_ref, qg_ref, kg_ref, cos_ref, sin_ref, qo_ref, ko_ref, vo_ref):
        cosv, sinv = cos_ref[...], sin_ref[...]

        def norm_rope(xh, g):
            r = lax.rsqrt(jnp.mean(xh * xh, axis=-1, keepdims=True) + NORM_EPS)
            return _rope(xh * r * g, cosv, sinv, half)

        for h in range(hq):
            qo_ref[h] = (norm_rope(q_ref[:, h * hd:(h + 1) * hd], qg_ref[...]) * q_scale).astype(BF16)
        for h in range(hkv):
            ko_ref[h] = norm_rope(kv_ref[:, h * hd:(h + 1) * hd], kg_ref[...]).astype(BF16)
            vo_ref[h] = kv_ref[:, (hkv + h) * hd:(hkv + h + 1) * hd].astype(BF16)

    jq, jkv = _cb(off_q, wq), _cb(off_q + wq, wkv)
    tab = pl.BlockSpec((tm, LANE), lambda b, i: (i, 0))
    vec = pl.BlockSpec((1, hd), lambda b, i: (0, 0))

    def heads(h):
        return pl.BlockSpec((None, h, tm, hd), lambda b, i: (b, 0, i, 0))

    return pl.pallas_call(
        body, grid=(nb, nt),
        in_specs=[pl.BlockSpec((tm, wq), lambda b, i: (b * nt + i, jq)),
                  pl.BlockSpec((tm, wkv), lambda b, i: (b * nt + i, jkv)), vec, vec, tab, tab],
        out_specs=(heads(hq), heads(hkv), heads(hkv)),
        out_shape=(jax.ShapeDtypeStruct((nb, hq, lp, hd), BF16), jax.ShapeDtypeStruct((nb, hkv, lp, hd), BF16),
                   jax.ShapeDtypeStruct((nb, hkv, lp, hd), BF16)),
        name="gqa_prep_fwd", compiler_params=_cp(2))(proj, proj, qg, kg, cos, sin)


def _gqa_prep_bwd(proj, qg, kg, cos, sin, dqh, dkh, dvh, *, off_q, hq, hkv, nb, lp, tm, q_scale):
    hd = HEAD_DIM
    half = hd // 4
    wq, wkv = hq * hd, 2 * hkv * hd
    nt = lp // tm

    def body(q_ref, kv_ref, qg_ref, kg_ref, cos_ref, sin_ref, dqh_ref, dkh_ref, dvh_ref,
             dq_ref, dkv_ref, dqg_ref, dkg_ref):
        cosv, sinv = cos_ref[...], sin_ref[...]

        def back(xh, g, dyh):
            dn = _rope_t(dyh, cosv, sinv, half)
            r = lax.rsqrt(jnp.mean(xh * xh, axis=-1, keepdims=True) + NORM_EPS)
            t = dn * g
            dx = r * t - xh * (r * r * r) * jnp.mean(t * xh, axis=-1, keepdims=True)
            return dx, jnp.sum(dn * xh * r, axis=0, keepdims=True)

        first = (pl.program_id(0) == 0) & (pl.program_id(1) == 0)

        @pl.when(first)
        def _():
            dqg_ref[...] = jnp.zeros_like(dqg_ref)
            dkg_ref[...] = jnp.zeros_like(dkg_ref)

        gq = jnp.zeros((1, hd), F32)
        for h in range(hq):
            dx, dg = back(q_ref[:, h * hd:(h + 1) * hd], qg_ref[...], dqh_ref[h] * q_scale)
            dq_ref[:, h * hd:(h + 1) * hd] = dx.astype(dq_ref.dtype)
            gq = gq + dg
        dqg_ref[...] += gq
        gk = jnp.zeros((1, hd), F32)
        for h in range(hkv):
            dx, dg = back(kv_ref[:, h * hd:(h + 1) * hd], kg_ref[...], dkh_ref[h])
            dkv_ref[:, h * hd:(h + 1) * hd] = dx.astype(dkv_ref.dtype)
            dkv_ref[:, (hkv + h) * hd:(hkv + h + 1) * hd] = dvh_ref[h].astype(dkv_ref.dtype)
            gk = gk + dg
        dkg_ref[...] += gk

    jq, jkv = _cb(off_q, wq), _cb(off_q + wq, wkv)
    tab = pl.BlockSpec((tm, LANE), lambda b, i: (i, 0))
    vec = pl.BlockSpec((1, hd), lambda b, i: (0, 0))

    def heads(h):
        return pl.BlockSpec((None, h, tm, hd), lambda b, i: (b, 0, i, 0))

    return pl.pallas_call(
        body, grid=(nb, nt),
        in_specs=[pl.BlockSpec((tm, wq), lambda b, i: (b * nt + i, jq)),
                  pl.BlockSpec((tm, wkv), lambda b, i: (b * nt + i, jkv)), vec, vec, tab, tab,
                  heads(hq), heads(hkv), heads(hkv)],
        out_specs=(pl.BlockSpec((tm, wq), lambda b, i: (b * nt + i, 0)),
                   pl.BlockSpec((tm, wkv), lambda b, i: (b * nt + i, 0)), vec, vec),
        out_shape=(jax.ShapeDtypeStruct((nb * lp, wq), BF16), jax.ShapeDtypeStruct((nb * lp, wkv), BF16),
                   jax.ShapeDtypeStruct((1, hd), F32), jax.ShapeDtypeStruct((1, hd), F32)),
        name="gqa_prep_bwd", compiler_params=_cp(2))(proj, proj, qg, kg, cos, sin, dqh, dkh, dvh)


def _mla_prep_fwd(qm, kvm, proj, cos, sin, *, off_kpe, nh, nb, lp, tm, q_scale):
    hd = HEAD_DIM
    half = MLA_ROPE_DIM // 4
    nt = lp // tm
    wh = nh * hd

    def body(qm_ref, kvm_ref, kpe_ref, cos_ref, sin_ref, qo_ref, ko_ref, vo_ref):
        cosv, sinv = cos_ref[...], sin_ref[...]
        kpe = _rope(kpe_ref[...], cosv, sinv, half).astype(BF16)
        for h in range(nh):
            qo_ref[h, :, 0:hd] = (qm_ref[:, h * hd:(h + 1) * hd] * q_scale).astype(BF16)
            qo_ref[h, :, hd:2 * hd] = (_rope(qm_ref[:, wh + h * hd:wh + (h + 1) * hd], cosv, sinv, half)
                                       * q_scale).astype(BF16)
            ko_ref[h, :, 0:hd] = kvm_ref[:, h * hd:(h + 1) * hd].astype(BF16)
            ko_ref[h, :, hd:2 * hd] = kpe
            vo_ref[h] = kvm_ref[:, wh + h * hd:wh + (h + 1) * hd].astype(BF16)

    tab = pl.BlockSpec((tm, LANE), lambda b, i: (i, 0))
    wide = pl.BlockSpec((tm, 2 * wh), lambda b, i: (b * nt + i, 0))
    jk = _cb(off_kpe, LANE)
    return pl.pallas_call(
        body, grid=(nb, nt),
        in_specs=[wide, wide, pl.BlockSpec((tm, LANE), lambda b, i: (b * nt + i, jk)), tab, tab],
        out_specs=(pl.BlockSpec((None, nh, tm, 2 * hd), lambda b, i: (b, 0, i, 0)),
                   pl.BlockSpec((None, nh, tm, 2 * hd), lambda b, i: (b, 0, i, 0)),
                   pl.BlockSpec((None, nh, tm, hd), lambda b, i: (b, 0, i, 0))),
        out_shape=(jax.ShapeDtypeStruct((nb, nh, lp, 2 * hd), BF16), jax.ShapeDtypeStruct((nb, nh, lp, 2 * hd), BF16),
                   jax.ShapeDtypeStruct((nb, nh, lp, hd), BF16)),
        name="mla_prep_fwd", compiler_params=_cp(2))(qm, kvm, proj, cos, sin)


def _mla_prep_bwd(dqc, dkc, dvc, cos, sin, *, nh, nb, lp, tm, q_scale):
    hd = HEAD_DIM
    half = MLA_ROPE_DIM // 4
    nt = lp // tm
    wh = nh * hd

    def body(dq_ref, dk_ref, dv_ref, cos_ref, sin_ref, dqm_ref, dkvm_ref, dkpe_ref):
        cosv, sinv = cos_ref[...], sin_ref[...]
        acc = jnp.zeros((tm, hd), F32)
        for h in range(nh):
            dqm_ref[:, h * hd:(h + 1) * hd] = (dq_ref[h, :, 0:hd] * q_scale).astype(BF16)
            dqm_ref[:, wh + h * hd:wh + (h + 1) * hd] = (_rope_t(dq_ref[h, :, hd:2 * hd], cosv, sinv, half)
                                                         * q_scale).astype(BF16)
            dkvm_ref[:, h * hd:(h + 1) * hd] = dk_ref[h, :, 0:hd].astype(BF16)
            dkvm_ref[:, wh + h * hd:wh + (h + 1) * hd] = dv_ref[h].astype(BF16)
            acc = acc + dk_ref[h, :, hd:2 * hd]
        dkpe_ref[...] = _rope_t(acc, cosv, sinv, half).astype(BF16)

    tab = pl.BlockSpec((tm, LANE), lambda b, i: (i, 0))
    wide = pl.BlockSpec((tm, 2 * wh), lambda b, i: (b * nt + i, 0))
    return pl.pallas_call(
        body, grid=(nb, nt),
        in_specs=[pl.BlockSpec((None, nh, tm, 2 * hd), lambda b, i: (b, 0, i, 0)),
                  pl.BlockSpec((None, nh, tm, 2 * hd), lambda b, i: (b, 0, i, 0)),
                  pl.BlockSpec((None, nh, tm, hd), lambda b, i: (b, 0, i, 0)), tab, tab],
        out_specs=(wide, wide, pl.BlockSpec((tm, LANE), lambda b, i: (b * nt + i, 0))),
        out_shape=(jax.ShapeDtypeStruct((nb * lp, 2 * wh), BF16), jax.ShapeDtypeStruct((nb * lp, 2 * wh), BF16),
                   jax.ShapeDtypeStruct((nb * lp, LANE), BF16)),
        name="mla_prep_bwd", compiler_params=_cp(2))(dqc, dkc, dvc, cos, sin)


def _softmax_parts(s, lp, l_valid):
    tw = -(-(lp - l_valid) // LANE) * LANE
    main, tail = s[:, :lp - tw], s[:, lp - tw:]
    col = lax.broadcasted_iota(jnp.int32, tail.shape, 1) + (lp - tw)
    tail = jnp.where(col < l_valid, tail, -1e30)
    m = jnp.maximum(jnp.max(main, axis=-1, keepdims=True), jnp.max(tail, axis=-1, keepdims=True))
    pm, pt = jnp.exp(main - m), jnp.exp(tail - m)
    den = jnp.sum(pm, axis=-1, keepdims=True) + jnp.sum(pt, axis=-1, keepdims=True)
    return pm, pt, den, tw


def _attn_fwd(q, k, v, *, l_valid, name):
    nb, hq, lp, dk = q.shape
    hkv, dv = k.shape[1], v.shape[3]
    grp = hq // hkv
    tq = _tile(lp, 1088, 64)
    sub = tq // 4
    nq = lp // tq

    def body(q_ref, k_ref, v_ref, o_ref, p_s):
        for r0 in range(0, tq, sub):
            rows = slice(r0, r0 + sub)
            s = lax.dot_general(q_ref[rows, :], k_ref[...], NT, preferred_element_type=F32)
            pm, pt, den, tw = _softmax_parts(s, lp, l_valid)
            p_s[rows, :lp - tw] = pm.astype(BF16)
            p_s[rows, lp - tw:] = pt.astype(BF16)
            o = lax.dot_general(p_s[rows, :], v_ref[...], NN, preferred_element_type=F32)
            o_ref[rows, :] = (o / den).astype(o_ref.dtype)

    return pl.pallas_call(
        body, grid=(nb, hq, nq),
        in_specs=[pl.BlockSpec((None, None, tq, dk), lambda b, h, i: (b, h, i, 0)),
                  pl.BlockSpec((None, None, lp, dk), lambda b, h, i: (b, h // grp, 0, 0)),
                  pl.BlockSpec((None, None, lp, dv), lambda b, h, i: (b, h // grp, 0, 0))],
        out_specs=pl.BlockSpec((tq, dv), lambda b, h, i: (b * nq + i, h)),
        out_shape=jax.ShapeDtypeStruct((nb * lp, hq * dv), BF16), scratch_shapes=[pltpu.VMEM((tq, lp), BF16)],
        name=name, compiler_params=_cp(3))(q, k, v)


def _attn_bwd(q, k, v, do, *, l_valid, name):
    nb, hq, lp, dk = q.shape
    hkv, dv = k.shape[1], v.shape[3]
    grp = hq // hkv
    tq = _tile(lp, 1088, 64)
    sub = tq // 4
    nq = lp // tq

    def body(q_ref, k_ref, v_ref, do_ref, dq_ref, dk_ref, dv_ref, p_s, ds_s):
        for r0 in range(0, tq, sub):
            rows = slice(r0, r0 + sub)
            s = lax.dot_general(q_ref[rows, :], k_ref[...], NT, preferred_element_type=F32)
            pm, pt, den, tw = _softmax_parts(s, lp, l_valid)
            inv = 1.0 / den
            pm, pt = pm * inv, pt * inv
            dp = lax.dot_general(do_ref[rows, :], v_ref[...], NT, preferred_element_type=F32)
            dpm, dpt = dp[:, :lp - tw], dp[:, lp - tw:]
            dd = jnp.sum(pm * dpm, axis=-1, keepdims=True) + jnp.sum(pt * dpt, axis=-1, keepdims=True)
            p_s[rows, :lp - tw] = pm.astype(BF16)
            p_s[rows, lp - tw:] = pt.astype(BF16)
            ds_s[rows, :lp - tw] = (pm * (dpm - dd)).astype(BF16)
            ds_s[rows, lp - tw:] = (pt * (dpt - dd)).astype(BF16)
            dq_ref[rows, :] = lax.dot_general(ds_s[rows, :], k_ref[...], NN, preferred_element_type=F32)

        @pl.when((pl.program_id(2) == 0) & (pl.program_id(3) == 0))
        def _():
            dk_ref[...] = jnp.zeros_like(dk_ref)
            dv_ref[...] = jnp.zeros_like(dv_ref)

        dk_ref[...] += lax.dot_general(ds_s[...], q_ref[...], TN, preferred_element_type=F32)
        dv_ref[...] += lax.dot_general(p_s[...], do_ref[...], TN, preferred_element_type=F32)

    return pl.pallas_call(
        body, grid=(nb, hkv, grp, nq), scratch_shapes=[pltpu.VMEM((tq, lp), BF16), pltpu.VMEM((tq, lp), BF16)],
        in_specs=[pl.BlockSpec((None, None, tq, dk), lambda b, h, g, i: (b, h * grp + g, i, 0)),
                  pl.BlockSpec((None, None, lp, dk), lambda b, h, g, i: (b, h, 0, 0)),
                  pl.BlockSpec((None, None, lp, dv), lambda b, h, g, i: (b, h, 0, 0)),
                  pl.BlockSpec((tq, dv), lambda b, h, g, i: (b * nq + i, h * grp + g))],
        out_specs=(pl.BlockSpec((None, None, tq, dk), lambda b, h, g, i: (b, h * grp + g, i, 0)),
                   pl.BlockSpec((None, None, lp, dk), lambda b, h, g, i: (b, h, 0, 0)),
                   pl.BlockSpec((None, None, lp, dv), lambda b, h, g, i: (b, h, 0, 0))),
        out_shape=(jax.ShapeDtypeStruct((nb, hq, lp, dk), F32), jax.ShapeDtypeStruct((nb, hkv, lp, dk), F32),
                   jax.ShapeDtypeStruct((nb, hkv, lp, dv), F32)),
        name=name, compiler_params=_cp(4))(q, k, v, do)


def _gate_fwd(proj, gb, ya, yb, yc, *, d, tm):
    m = ya.shape[0]
    cw = _tile(d, 512, LANE)
    nj = d // cw

    def body(g0, g1, g2, b0, b1, b2, ya_ref, yb_ref, yc_ref, o_ref):
        o = (_sig(g0[...] + b0[...]) * ya_ref[...] + _sig(g1[...] + b1[...]) * yb_ref[...]
             + _sig(g2[...] + b2[...]) * yc_ref[...])
        o_ref[...] = o.astype(o_ref.dtype)

    def gl(br):
        return pl.BlockSpec((tm, cw), lambda j, i: (i, br * nj + j))

    def gbs(br):
        return pl.BlockSpec((1, cw), lambda j, i: (0, br * nj + j))

    blk = pl.BlockSpec((tm, cw), lambda j, i: (i, j))
    return pl.pallas_call(
        body, grid=(nj, m // tm), in_specs=[gl(0), gl(1), gl(2), gbs(0), gbs(1), gbs(2), blk, blk, blk],
        out_specs=blk, out_shape=jax.ShapeDtypeStruct((m, d), BF16), name="gate_fwd",
        compiler_params=_cp(2))(proj, proj, proj, gb, gb, gb, ya, yb, yc)


def _gate_bwd(dm, proj, gb, ya, yb, yc, *, d, tm):
    m = ya.shape[0]
    cw = _tile(d, 512, LANE)
    nj = d // cw

    def body(dm_ref, g0, g1, g2, b0, b1, b2, ya_ref, yb_ref, yc_ref,
             dya, dyb, dyc, dg0, dg1, dg2, db0, db1, db2):
        dmv = dm_ref[...]

        @pl.when(pl.program_id(1) == 0)
        def _():
            for r in (db0, db1, db2):
                r[...] = jnp.zeros_like(r)

        for g, b, y, dy, dgl, db in ((g0, b0, ya_ref, dya, dg0, db0), (g1, b1, yb_ref, dyb, dg1, db1),
                                     (g2, b2, yc_ref, dyc, dg2, db2)):
            sg = _sig(g[...] + b[...])
            dy[...] = (sg * dmv).astype(dy.dtype)
            dl = dmv * y[...] * sg * (1.0 - sg)
            dgl[...] = dl.astype(dgl.dtype)
            db[...] += jnp.sum(dl, axis=0, keepdims=True)

    def gl(br):
        return pl.BlockSpec((tm, cw), lambda j, i: (i, br * nj + j))

    def gbs(br):
        return pl.BlockSpec((1, cw), lambda j, i: (0, br * nj + j))

    blk = pl.BlockSpec((tm, cw), lambda j, i: (i, j))
    vec = pl.BlockSpec((1, cw), lambda j, i: (0, j))
    act = jax.ShapeDtypeStruct((m, d), BF16)
    vsh = jax.ShapeDtypeStruct((1, d), F32)
    return pl.pallas_call(
        body, grid=(nj, m // tm), in_specs=[blk, gl(0), gl(1), gl(2), gbs(0), gbs(1), gbs(2), blk, blk, blk],
        out_specs=(blk,) * 6 + (vec,) * 3, out_shape=(act,) * 6 + (vsh,) * 3, name="gate_bwd",
        compiler_params=_cp(2))(dm, proj, proj, proj, gb, gb, gb, ya, yb, yc)


def _swiglu_fwd(gt, up, tm):
    m, f = gt.shape
    cw = _tile(f, 512, LANE)

    def body(g_ref, u_ref, o_ref):
        g = g_ref[...]
        o_ref[...] = (g * _sig(g) * u_ref[...]).astype(o_ref.dtype)

    blk = pl.BlockSpec((tm, cw), lambda j, i: (i, j))
    return pl.pallas_call(body, grid=(f // cw, m // tm), in_specs=[blk, blk], out_specs=blk,
                          out_shape=jax.ShapeDtypeStruct((m, f), BF16), name="swiglu_fwd",
                          compiler_params=_cp(2))(gt, up)


def _swiglu_bwd(dact, gt, up, tm):
    m, f = gt.shape
    cw = _tile(f, 512, LANE)

    def body(d_ref, g_ref, u_ref, dg_ref, du_ref):
        g, d = g_ref[...], d_ref[...]
        sg = _sig(g)
        dg_ref[...] = (d * u_ref[...] * sg * (1.0 + g * (1.0 - sg))).astype(dg_ref.dtype)
        du_ref[...] = (d * g * sg).astype(du_ref.dtype)

    blk = pl.BlockSpec((tm, cw), lambda j, i: (i, j))
    sh = jax.ShapeDtypeStruct((m, f), BF16)
    return pl.pallas_call(body, grid=(f // cw, m // tm), in_specs=[blk, blk, blk], out_specs=(blk, blk),
                          out_shape=(sh, sh), name="swiglu_bwd", compiler_params=_cp(2))(dact, gt, up)


def _loss_head(h, gf, tgt, *, lp, l_valid, tm):
    m, d = h.shape
    nt = lp // tm

    def body(h_ref, g_ref, t_ref, dh_ref, dg_ref, loss_ref):
        i = pl.program_id(0)
        xv = h_ref[...]
        r = lax.rsqrt(jnp.mean(xv * xv, axis=-1, keepdims=True) + NORM_EPS)
        y = xv * r * g_ref[...]
        t = lax.broadcasted_iota(jnp.int32, (tm, d), 0) + (i % nt) * tm
        err = jnp.where((t >= N_META) & (t < l_valid), y - t_ref[...], 0.0)
        dy = err * (1.0 / d)
        tg = dy * g_ref[...]
        dh_ref[...] = r * tg - xv * (r * r * r) * jnp.mean(tg * xv, axis=-1, keepdims=True)

        @pl.when(i == 0)
        def _():
            dg_ref[...] = jnp.zeros_like(dg_ref)
            loss_ref[...] = jnp.zeros_like(loss_ref)

        dg_ref[...] += jnp.sum(dy * xv * r, axis=0, keepdims=True)
        sq = jnp.sum(jnp.sum(err * err, axis=-1, keepdims=True), axis=0, keepdims=True)
        loss_ref[...] += jnp.zeros((1, LANE), F32) + sq * (0.5 / d)

    row = pl.BlockSpec((tm, d), lambda i: (i, 0))
    vec = pl.BlockSpec((1, d), lambda i: (0, 0))
    return pl.pallas_call(
        body, grid=(m // tm,), in_specs=[row, vec, row],
        out_specs=(row, vec, pl.BlockSpec((1, LANE), lambda i: (0, 0))),
        out_shape=(jax.ShapeDtypeStruct((m, d), F32), jax.ShapeDtypeStruct((1, d), F32),
                   jax.ShapeDtypeStruct((1, LANE), F32)),
        name="loss_head", compiler_params=_cp(1))(h, gf, tgt)


def _adamw(w, g, m, v, name):
    rows, cols = w.shape
    tr = rows
    if rows % 8 == 0:
        tr = _tile(rows, max(8, (1 << 18) // cols // 8 * 8), 8)
    c1 = 1.0 / (1.0 - ADAM_B1 ** ADAM_STEP)
    c2 = 1.0 / (1.0 - ADAM_B2 ** ADAM_STEP)

    def body(w_ref, g_ref, m_ref, v_ref, d_ref, mo_ref, vo_ref):
        gv = g_ref[...]
        mn = ADAM_B1 * m_ref[...] + (1.0 - ADAM_B1) * gv
        vn = ADAM_B2 * v_ref[...] + (1.0 - ADAM_B2) * (gv * gv)
        mo_ref[...] = mn
        vo_ref[...] = vn
        d_ref[...] = -ADAM_LR * ((mn * c1) / (jnp.sqrt(vn * c2) + ADAM_EPS) + ADAM_WD * w_ref[...])

    blk = pl.BlockSpec((tr, cols), lambda i: (i, 0))
    sh = jax.ShapeDtypeStruct((rows, cols), F32)
    return pl.pallas_call(body, grid=(rows // tr,), in_specs=[blk] * 4, out_specs=(blk,) * 3, out_shape=(sh,) * 3,
                          name=name, compiler_params=_cp(1))(w, g, m, v)


MESH = pl.DeviceIdType.MESH
ANY = pl.BlockSpec(memory_space=pl.ANY)


def _place():
    return lax.axis_index("x"), lax.axis_index("y"), lax.axis_index("c")


def _other_chips(x, y):
    return [(1 - x, y), (x, 1 - y), (1 - x, 1 - y)]


def _all_gather(xs, vmem, name):
    na = len(xs)

    def body(*refs):
        x_refs, out_refs = refs[:na], refs[na:2 * na]
        send_sems, recv_sems, local_sems = refs[2 * na:]
        x, y, c = _place()
        me, sibling = (x, y, c), (x, y, 1 - c)
        chips = _other_chips(x, y)

        def blk(p, px, py, pc):
            return out_refs[p].at[4 * px + 2 * py + pc]

        def copy(k, p, block, to, src=None):
            return pltpu.make_async_remote_copy(
                src_ref=blk(p, *block) if src is None else src, dst_ref=blk(p, *block),
                send_sem=send_sems.at[k * na + p], recv_sem=recv_sems.at[k * na + p], device_id=to,
                device_id_type=MESH)

        mine = [pltpu.make_async_copy(x_refs[p], blk(p, *me), local_sems.at[p]) for p in range(na)]
        for cp in mine:
            cp.start()
        first = [copy(0, p, me, sibling, src=x_refs[p]) for p in range(na)]
        first += [copy(1 + j, p, me, (*chip, c), src=x_refs[p]) for j, chip in enumerate(chips) for p in range(na)]
        for cp in first:
            cp.start()
        passed = []
        for j, chip in enumerate(chips):
            for p in range(na):
                copy(1 + j, p, (*chip, c), me).wait_recv()
                fwd = copy(4 + j, p, (*chip, c), sibling)
                fwd.start()
                passed.append(fwd)
        for p in range(na):
            copy(0, p, sibling, me).wait_recv()
        for j, chip in enumerate(chips):
            for p in range(na):
                copy(4 + j, p, (*chip, 1 - c), me).wait_recv()
        for cp in first + passed:
            cp.wait_send()
        for cp in mine:
            cp.wait()

    spec = pl.BlockSpec(memory_space=pltpu.VMEM) if vmem else ANY
    return pl.pallas_call(
        body, out_shape=[jax.ShapeDtypeStruct((N_DEV,) + a.shape, a.dtype) for a in xs],
        in_specs=[spec] * na, out_specs=[spec] * na,
        scratch_shapes=[pltpu.SemaphoreType.DMA((7 * na,)), pltpu.SemaphoreType.DMA((7 * na,)),
                        pltpu.SemaphoreType.DMA((na,))],
        name=name)(*xs)


def _swap_with_sibling(g4s):
    na = len(g4s)

    def body(*refs):
        g_refs, out_refs = refs[:na], refs[na:2 * na]
        send_sems, recv_sems = refs[2 * na:]
        x, y, c = _place()
        cps = [pltpu.make_async_remote_copy(src_ref=g_refs[p].at[:, 1 - c], dst_ref=out_refs[p],
                                            send_sem=send_sems.at[p], recv_sem=recv_sems.at[p],
                                            device_id=(x, y, 1 - c), device_id_type=MESH) for p in range(na)]
        for cp in cps:
            cp.start()
        for cp in cps:
            cp.wait()

    return pl.pallas_call(
        body, out_shape=[jax.ShapeDtypeStruct((g.shape[0],) + g.shape[2:], g.dtype) for g in g4s],
        in_specs=[ANY] * na, out_specs=[ANY] * na,
        scratch_shapes=[pltpu.SemaphoreType.DMA((na,)), pltpu.SemaphoreType.DMA((na,))], name="rs_sibling")(*g4s)


HBM = pl.BlockSpec(memory_space=pltpu.HBM)
SEM = pl.BlockSpec(memory_space=pltpu.SEMAPHORE)
EFFECT = pltpu.SideEffectType.DATAFLOW_SIDE_EFFECTING


def _in_hbm(a):
    return pltpu.with_memory_space_constraint(a, pltpu.HBM)


def _ici_copies(src_refs, land_refs, send_sems, recv_sems, gather):
    na = len(src_refs)
    x, y, c = _place()
    cps = []
    for j, (px, py) in enumerate(_other_chips(x, y)):
        for p in range(na):
            src = src_refs[p].at[4 * x + 2 * y + c] if gather else src_refs[p].at[2 * px + py]
            dst = src if gather else land_refs[p].at[j]
            cps.append(pltpu.make_async_remote_copy(
                src_ref=src, dst_ref=dst, send_sem=send_sems.at[j * na + p], recv_sem=recv_sems.at[j * na + p],
                device_id=(px, py, c), device_id_type=MESH))
    return cps


def _ici_start(srcs, land_shapes, deps, *, gather, name):
    na, nd = len(srcs), len(deps)
    lands = [lax.empty(s, a.dtype) for s, a in zip(land_shapes, srcs)]
    nb = na + len(lands)

    def body(*refs):
        src_refs, land_refs = refs[:na], refs[na:nb]
        send_sems, recv_sems = refs[nb + nd], refs[nb + nd + 1]
        token = refs[-1]
        for cp in _ici_copies(src_refs, land_refs, send_sems, recv_sems, gather):
            cp.start()
        token[...] = jnp.zeros_like(token)

    bufs = list(srcs) + lands
    out = pl.pallas_call(
        body, name=name,
        out_shape=(pltpu.SemaphoreType.DMA((3 * na,)), pltpu.SemaphoreType.DMA((3 * na,)),
                   *[pltpu.HBM(a.shape, a.dtype) for a in bufs], jax.ShapeDtypeStruct((8, LANE), F32)),
        in_specs=[HBM] * nb + [ANY] * nd,
        out_specs=(SEM, SEM, *([HBM] * nb), pl.BlockSpec(memory_space=pltpu.VMEM)),
        input_output_aliases={p: 2 + p for p in range(nb)},
        compiler_params=pltpu.CompilerParams(has_side_effects=EFFECT),
    )(*[_in_hbm(a) for a in bufs], *deps)
    return out[0], out[1], list(out[2:2 + na]), list(out[2 + na:2 + nb]), out[-1]


def _ici_wait(send_sems, recv_sems, srcs, lands, after, *, gather, name):
    na, nd = len(srcs), len(after)
    nb = na + len(lands)

    def body(*refs):
        src_refs, land_refs = refs[:na], refs[na:nb]
        s_sems, r_sems = refs[nb], refs[nb + 1]
        for cp in _ici_copies(src_refs, land_refs, s_sems, r_sems, gather):
            cp.wait_send()
            cp.wait_recv()

    bufs = list(srcs) + list(lands)
    out = pl.pallas_call(
        body, name=name, out_shape=tuple(pltpu.HBM(a.shape, a.dtype) for a in bufs),
        in_specs=[HBM] * nb + [SEM, SEM] + [ANY] * nd, out_specs=tuple([HBM] * nb),
        input_output_aliases={p: p for p in range(nb)},
        compiler_params=pltpu.CompilerParams(has_side_effects=EFFECT),
    )(*bufs, send_sems, recv_sems, *after)
    return list(out[:na]), list(out[na:])


def _gather_to_sibling(gs):
    na = len(gs)

    def body(*refs):
        out_refs = refs[na:2 * na]
        send_sems, recv_sems = refs[2 * na:]
        x, y, c = _place()
        sibling = (x, y, 1 - c)
        blocks = [(x, y)] + _other_chips(x, y)

        def copy(k, p, core):
            px, py = blocks[k]
            blk = out_refs[p].at[4 * px + 2 * py + core]
            return pltpu.make_async_remote_copy(src_ref=blk, dst_ref=blk, send_sem=send_sems.at[k * na + p],
                                                recv_sem=recv_sems.at[k * na + p], device_id=sibling,
                                                device_id_type=MESH)

        sends = [copy(k, p, c) for k in range(4) for p in range(na)]
        for cp in sends:
            cp.start()
        for k in range(4):
            for p in range(na):
                copy(k, p, 1 - c).wait_recv()
        for cp in sends:
            cp.wait_send()

    return pl.pallas_call(
        body, out_shape=[jax.ShapeDtypeStruct(a.shape, a.dtype) for a in gs],
        in_specs=[ANY] * na, out_specs=[ANY] * na, input_output_aliases={p: p for p in range(na)},
        scratch_shapes=[pltpu.SemaphoreType.DMA((4 * na,)), pltpu.SemaphoreType.DMA((4 * na,))],
        name="ag_sibling")(*gs)


HI_MASK = -65536


def _pack_pairs_xla(a):
    half = a.shape[-1] // 2
    bits = lax.bitcast_convert_type(a.astype(BF16), jnp.uint16).astype(jnp.uint32)
    return lax.bitcast_convert_type((bits[..., half:] << 16) | bits[..., :half], F32)


def _split_pairs(words):
    wv = lax.bitcast_convert_type(words, jnp.int32)
    return lax.bitcast_convert_type(wv << 16, F32), lax.bitcast_convert_type(wv & HI_MASK, F32)


def _join_pairs(lo, hi):
    lo_b = lax.bitcast_convert_type(lo.astype(BF16).astype(F32), jnp.int32)
    hi_b = lax.bitcast_convert_type(hi.astype(BF16).astype(F32), jnp.int32)
    return lax.bitcast_convert_type((hi_b & HI_MASK) | lax.shift_right_logical(lo_b, 16), F32)


def _unpack(xp, *, name, out_rows=None, tr=None, src_fn=None, zero_fn=None):
    r, hw = xp.shape
    out_rows = out_rows or r
    tr = tr or _tile(out_rows, 512, 16)
    src = src_fn or (lambda i: i)
    k = _blocks_per_step(out_rows // tr) if src_fn is not None else 1

    def body(*refs):
        o_ref = refs[k]
        for j in range(k):
            lo, hi = _split_pairs(refs[j][...])
            if zero_fn is not None:
                z = zero_fn(pl.program_id(0) * k + j)
                lo, hi = jnp.where(z, 0.0, lo), jnp.where(z, 0.0, hi)
            o_ref[j * tr:(j + 1) * tr, :hw] = lo.astype(BF16)
            o_ref[j * tr:(j + 1) * tr, hw:] = hi.astype(BF16)

    return pl.pallas_call(
        body, grid=(out_rows // (k * tr),),
        in_specs=[pl.BlockSpec((tr, hw), lambda i, j=j: (src(i * k + j), 0)) for j in range(k)],
        out_specs=pl.BlockSpec((k * tr, 2 * hw), lambda i: (i, 0)),
        out_shape=jax.ShapeDtypeStruct((out_rows, 2 * hw), BF16), name=name, compiler_params=_cp(1))(*([xp] * k))


def _blocks_per_step(n_blocks):
    return max(k for k in range(1, 17) if n_blocks % k == 0)


def _gather_rows(xp, *, name, out_rows, tr, src_fn):
    _, hw = xp.shape
    k = _blocks_per_step(out_rows // tr)

    def body(*refs):
        for j in range(k):
            refs[k][j * tr:(j + 1) * tr, :] = refs[j][...]

    return pl.pallas_call(
        body, grid=(out_rows // (k * tr),),
        in_specs=[pl.BlockSpec((tr, hw), lambda i, j=j: (src_fn(i * k + j), 0)) for j in range(k)],
        out_specs=pl.BlockSpec((k * tr, hw), lambda i: (i, 0)),
        out_shape=jax.ShapeDtypeStruct((out_rows, hw), xp.dtype), name=name, compiler_params=_cp(1))(*([xp] * k))


def _matmul_tn_packed(a, b, name):
    (t, m), (t2, c) = a.shape, b.shape
    assert t == t2
    hw = c // 2
    tm = _tile(m, 1088, LANE)
    tn = _tile(hw, 512, LANE)
    tk = _tile(t, 2176, LANE)
    nk, nj = t // tk, hw // tn

    def body(a_ref, bl_ref, bh_ref, o_ref, acc_lo, acc_hi):
        kk = pl.program_id(2)

        @pl.when(kk == 0)
        def _():
            acc_lo[...] = jnp.zeros_like(acc_lo)
            acc_hi[...] = jnp.zeros_like(acc_hi)

        av = a_ref[...].astype(BF16)
        acc_lo[...] += lax.dot_general(av, bl_ref[...].astype(BF16), TN, preferred_element_type=F32)
        acc_hi[...] += lax.dot_general(av, bh_ref[...].astype(BF16), TN, preferred_element_type=F32)

        @pl.when(kk == nk - 1)
        def _():
            o_ref[...] = _join_pairs(acc_lo[...], acc_hi[...])

    return pl.pallas_call(
        body, grid=(m // tm, nj, nk),
        in_specs=[pl.BlockSpec((tk, tm), lambda i, j, kk: (kk, i)), pl.BlockSpec((tk, tn), lambda i, j, kk: (kk, j)),
                  pl.BlockSpec((tk, tn), lambda i, j, kk: (kk, nj + j))],
        out_specs=pl.BlockSpec((tm, tn), lambda i, j, kk: (i, j)), out_shape=jax.ShapeDtypeStruct((m, hw), F32),
        scratch_shapes=[pltpu.VMEM((tm, tn), F32), pltpu.VMEM((tm, tn), F32)], name=name,
        compiler_params=_cp(3))(a, b, b)


def _add_sibling(g4, recv, cidx, name):
    nchip, _, r, n = g4.shape
    tr = _tile(r, 256, 8)

    def body(c_ref, a_ref, b_ref, o_ref):
        alo, ahi = _split_pairs(a_ref[...])
        blo, bhi = _split_pairs(b_ref[...])
        o_ref[...] = _join_pairs(alo + blo, ahi + bhi)

    grid_spec = pltpu.PrefetchScalarGridSpec(
        num_scalar_prefetch=1, grid=(nchip, r // tr),
        in_specs=[pl.BlockSpec((None, None, tr, n), lambda k, i, c: (k, c[0], i, 0)),
                  pl.BlockSpec((None, tr, n), lambda k, i, c: (k, i, 0))],
        out_specs=pl.BlockSpec((None, tr, n), lambda k, i, c: (k, i, 0)))
    return pl.pallas_call(body, grid_spec=grid_spec, out_shape=jax.ShapeDtypeStruct((nchip, r, n), F32),
                          name=name, compiler_params=_cp(2))(cidx, g4, recv)


def _add_chips(p4, recv3, chip_idx, name):
    _, r, n = p4.shape
    tr = _tile(r, 256, 8)

    def body(k_ref, a_ref, b_ref, o_ref):
        lo, hi = _split_pairs(a_ref[...])
        for j in range(3):
            blo, bhi = _split_pairs(b_ref[j])
            lo, hi = lo + blo, hi + bhi
        o_ref[:, :n] = lo
        o_ref[:, n:] = hi

    grid_spec = pltpu.PrefetchScalarGridSpec(
        num_scalar_prefetch=1, grid=(r // tr,),
        in_specs=[pl.BlockSpec((None, tr, n), lambda i, k: (k[0], i, 0)),
                  pl.BlockSpec((3, tr, n), lambda i, k: (0, i, 0))],
        out_specs=pl.BlockSpec((tr, 2 * n), lambda i, k: (i, 0)))
    return pl.pallas_call(body, grid_spec=grid_spec, out_shape=jax.ShapeDtypeStruct((r, 2 * n), F32),
                          name=name, compiler_params=_cp(1))(chip_idx, p4, recv3)


def _all_reduce_small(xs):
    r, n = xs.shape

    def body(x_ref, o_ref, buf, send_sems, recv_sems):
        x, y, c = _place()
        me = 4 * x + 2 * y + c
        buf[me] = x_ref[...]
        peers = []
        for k in range(1, N_DEV):
            px = (1 - x) if (k >> 2) & 1 else x
            py = (1 - y) if (k >> 1) & 1 else y
            pc = (1 - c) if k & 1 else c
            peers.append((px, py, pc))
        sends = [pltpu.make_async_remote_copy(src_ref=x_ref, dst_ref=buf.at[me], send_sem=send_sems.at[k],
                                              recv_sem=recv_sems.at[k], device_id=peer, device_id_type=MESH)
                 for k, peer in enumerate(peers)]
        for cp in sends:
            cp.start()
        for k, (px, py, pc) in enumerate(peers):
            pltpu.make_async_remote_copy(src_ref=x_ref, dst_ref=buf.at[4 * px + 2 * py + pc],
                                         send_sem=send_sems.at[k], recv_sem=recv_sems.at[k],
                                         device_id=(px, py, pc), device_id_type=MESH).wait_recv()
        for cp in sends:
            cp.wait_send()
        acc = buf[0]
        for dv in range(1, N_DEV):
            acc = acc + buf[dv]
        o_ref[...] = acc

    vm = pl.BlockSpec(memory_space=pltpu.VMEM)
    return pl.pallas_call(
        body, out_shape=jax.ShapeDtypeStruct((r, n), F32), in_specs=[vm], out_specs=vm,
        scratch_shapes=[pltpu.VMEM((N_DEV, r, n), F32), pltpu.SemaphoreType.DMA((7,)), pltpu.SemaphoreType.DMA((7,))],
        name="all_reduce_small", compiler_params=pltpu.CompilerParams(vmem_limit_bytes=VMEM_LIMIT))(xs)


def _pack_rows(arrs, quantum):
    parts, sizes = [], []
    for a in arrs:
        f = a.reshape(-1)
        pad = (-f.shape[0]) % quantum
        if pad:
            f = jnp.pad(f, (0, pad))
        parts.append(f)
        sizes.append(f.shape[0])
    return jnp.concatenate(parts).reshape(-1, LANE), sizes


def _unpack_rows(buf, sizes, shapes):
    flat = buf.reshape(-1)
    out, off = [], 0
    for sz, sh in zip(sizes, shapes):
        n = math.prod(sh)
        out.append(flat[off:off + n].reshape(sh))
        off += sz
    return out


def kernel(x, meta_tokens, mix_norm_g, w_in, conv_dw, conv_b, conv_ln_g, conv_ln_b, w_conv_out, gqa_q_norm_g, gqa_k_norm_g, w_gqa_out, mla_q_norm_g, w_mla_uq, mla_kv_norm_g, w_mla_ukv, w_mla_out, gate_b, w_out, ffn_norm_g, w_ffn_gate, w_ffn_up, w_ffn_down, final_norm_g, loss_target, m_meta_tokens, m_mix_norm_g, m_w_in, m_conv_dw, m_conv_b, m_conv_ln_g, m_conv_ln_b, m_w_conv_out, m_gqa_q_norm_g, m_gqa_k_norm_g, m_w_gqa_out, m_mla_q_norm_g, m_w_mla_uq, m_mla_kv_norm_g, m_w_mla_ukv, m_w_mla_out, m_gate_b, m_w_out, m_ffn_norm_g, m_w_ffn_gate, m_w_ffn_up, m_w_ffn_down, m_final_norm_g, v_meta_tokens, v_mix_norm_g, v_w_in, v_conv_dw, v_conv_b, v_conv_ln_g, v_conv_ln_b, v_w_conv_out, v_gqa_q_norm_g, v_gqa_k_norm_g, v_w_gqa_out, v_mla_q_norm_g, v_w_mla_uq, v_mla_kv_norm_g, v_w_mla_ukv, v_w_mla_out, v_gate_b, v_w_out, v_ffn_norm_g, v_w_ffn_gate, v_w_ffn_up, v_w_ffn_down, v_final_norm_g):
    w = dict(meta_tokens=meta_tokens, mix_norm_g=mix_norm_g, w_in=w_in, conv_dw=conv_dw, conv_b=conv_b,
             conv_ln_g=conv_ln_g, conv_ln_b=conv_ln_b, w_conv_out=w_conv_out, gqa_q_norm_g=gqa_q_norm_g,
             gqa_k_norm_g=gqa_k_norm_g, w_gqa_out=w_gqa_out, mla_q_norm_g=mla_q_norm_g, w_mla_uq=w_mla_uq,
             mla_kv_norm_g=mla_kv_norm_g, w_mla_ukv=w_mla_ukv, w_mla_out=w_mla_out, gate_b=gate_b, w_out=w_out,
             ffn_norm_g=ffn_norm_g, w_ffn_gate=w_ffn_gate, w_ffn_up=w_ffn_up, w_ffn_down=w_ffn_down,
             final_norm_g=final_norm_g)
    mom1 = dict(zip(W_NAMES, (m_meta_tokens, m_mix_norm_g, m_w_in, m_conv_dw, m_conv_b, m_conv_ln_g, m_conv_ln_b, m_w_conv_out, m_gqa_q_norm_g, m_gqa_k_norm_g, m_w_gqa_out, m_mla_q_norm_g, m_w_mla_uq, m_mla_kv_norm_g, m_w_mla_ukv, m_w_mla_out, m_gate_b, m_w_out, m_ffn_norm_g, m_w_ffn_gate, m_w_ffn_up, m_w_ffn_down, m_final_norm_g)))
    mom2 = dict(zip(W_NAMES, (v_meta_tokens, v_mix_norm_g, v_w_in, v_conv_dw, v_conv_b, v_conv_ln_g, v_conv_ln_b, v_w_conv_out, v_gqa_q_norm_g, v_gqa_k_norm_g, v_w_gqa_out, v_mla_q_norm_g, v_w_mla_uq, v_mla_kv_norm_g, v_w_mla_ukv, v_w_mla_out, v_gate_b, v_w_out, v_ffn_norm_g, v_w_ffn_gate, v_w_ffn_up, v_w_ffn_down, v_final_norm_g)))

    nb, seq, d = x.shape
    depth = w_in.shape[0]
    c_conv = conv_b.shape[1]
    kw = conv_dw.shape[1]
    hd = HEAD_DIM
    hq = w_gqa_out.shape[1] // hd
    nh = w_mla_out.shape[1] // hd
    qr, kvr = mla_q_norm_g.shape[1], mla_kv_norm_g.shape[1]
    d_in = w_in.shape[2] * N_DEV
    hkv = (d_in - 2 * c_conv - hq * hd - qr - kvr - MLA_ROPE_DIM - N_BRANCH * d) // (2 * hd)
    l_valid = seq + N_META
    lp = -(-(l_valid + kw // 2) // LANE) * LANE
    m_tok = nb * lp
    mla_qk = hd + MLA_ROPE_DIM
    gqa_scale, mla_scale = 1.0 / math.sqrt(hd), 1.0 / math.sqrt(mla_qk)

    off_gate = 0
    off_conv = N_BRANCH * d
    off_q = off_conv + 2 * c_conv
    off_cq = off_q + (hq + 2 * hkv) * hd
    off_ckv = off_cq + qr
    off_kpe = off_ckv + kvr
    n_proj = off_kpe + LANE

    x_id, y_id, c_id = _place()
    me = 4 * x_id + 2 * y_id + c_id

    tm_wide = _tile(lp, 272, 16)
    tm_mid = _tile(lp, 544, 16)

    sm_buf, sm_sizes = _pack_rows([meta_tokens, conv_dw], 8 * LANE)
    sm_all = _all_gather([sm_buf], True, "ag_small")[0]
    sm_all = sm_all.reshape(N_DEV, -1)
    o0 = 0
    meta_all = sm_all[:, o0:o0 + meta_tokens.size].reshape((N_DEV,) + meta_tokens.shape)
    meta_full = jnp.transpose(meta_all, (1, 0, 2)).reshape(N_META, d)
    o0 = sm_sizes[0]
    dw_all = sm_all[:, o0:o0 + conv_dw.size].reshape((N_DEV, depth, kw, c_conv // N_DEV))
    dw_full = jnp.transpose(dw_all, (1, 2, 0, 3)).reshape(depth, kw, c_conv)

    n_head_cols = d_in - N_BRANCH * d - MLA_ROPE_DIM
    rb = MLA_ROPE_DIM
    nb_gate, nb_head = N_BRANCH * d // rb, n_head_cols // rb
    assert n_head_cols % rb == 0 and (N_BRANCH * d) % rb == 0 and n_proj == (nb_gate + nb_head + 2) * rb

    def in_src(i):
        return jnp.where(i < nb_gate, nb_head + 1 + i,
                         jnp.where(i < nb_gate + nb_head, i - nb_gate, jnp.where(i == nb_gate + nb_head, nb_head, 0)))

    def in_dst(i):
        return jnp.where(i < nb_head, nb_gate + i, jnp.where(i == nb_head, nb_gate + nb_head, i - nb_head - 1))

    def uq_src(i):
        j = i - 2 * nh
        return jnp.where(i < 2 * nh, 3 * (i // 2) + i % 2, 3 * (j // 2) + 2)

    def uq_dst(i):
        return jnp.where(i % 3 < 2, 2 * (i // 3) + i % 3, 2 * nh + 2 * (i // 3))

    ag_pending, tokens = {}, []
    for i in range(depth):
        for gi, names in enumerate(GROUPS):
            shards = [_pack_pairs_xla(w[n][i] if n in ROW_SHARDED else w[n][i].T) for n in names]
            bufs = [lax.dynamic_update_slice(lax.empty((N_DEV,) + a.shape, F32), a[None], (me, 0, 0))
                    for a in shards]
            *pend, tok = _ici_start(bufs, [], tokens[-1:], gather=True, name="ag_ici_start_%d_%d" % (i, gi))
            ag_pending[i, gi] = pend
            tokens.append(tok)

    def all_gather_group(i, gi, after):
        bufs, _ = _ici_wait(*ag_pending[i, gi], after, gather=True, name="ag_ici_wait_%d_%d" % (i, gi))
        gath = _gather_to_sibling(bufs)
        wl = {}
        for n, g in zip(GROUPS[gi], gath):
            pk = g.reshape(-1, g.shape[-1])
            if n == 'w_in':
                wl[n] = _unpack(pk, name="unpack_w_in", out_rows=n_proj, tr=rb, src_fn=in_src,
                                zero_fn=lambda i: i == nb_gate + nb_head + 1)
            elif n == 'w_mla_uq':
                wl[n] = _unpack(pk, name="unpack_w_mla_uq", out_rows=2 * nh * hd, tr=rb, src_fn=uq_src,
                                zero_fn=lambda i: (i >= 2 * nh) & ((i - 2 * nh) % 2 == 1))
            elif n == 'w_mla_ukv':
                wl[n] = _unpack(pk, name="unpack_w_mla_ukv", out_rows=2 * nh * hd, tr=hd,
                                src_fn=lambda i: jnp.where(i < nh, 2 * i, 2 * (i - nh) + 1))
            else:
                wl[n] = _unpack(pk, name="unpack_" + n)
        return wl

    full = [None] * depth

    cos_g, sin_g = _rope_tables(lp, l_valid, hd // 4)
    cos_m, sin_m = _rope_tables(lp, l_valid, MLA_ROPE_DIM // 4)

    def vec2(a):
        return a.reshape(1, -1)

    meta_b = jnp.broadcast_to(meta_full[None], (nb, N_META, d))
    h = jnp.concatenate([meta_b, x, jnp.zeros((nb, lp - l_valid, d), F32)], axis=1).reshape(m_tok, d)
    tgt = jnp.pad(loss_target, ((0, 0), (N_META, lp - l_valid), (0, 0))).reshape(m_tok, d)

    saved = []
    for i in range(depth):
        s = {}
        s['h_in'] = h
        u = _rms_fwd(h, d, 0, vec2(mix_norm_g[i]), tm_wide, "mix_norm_fwd")
        wl = full[i] = all_gather_group(i, 0, tokens[-1:] if i == 0 else [h])
        proj = _matmul(u, wl['w_in'], mode="nt", out_dtype=F32, name="mm_in")
        cc = _conv_fwd(proj, dw_full[i], vec2(conv_b[i]), c_conv=c_conv, off_a=off_conv, nb=nb, lp=lp, l_valid=l_valid)
        sc = _ln_silu_fwd(cc, vec2(conv_ln_g[i]), vec2(conv_ln_b[i]), tm_mid)
        ya = _matmul(sc, wl['w_conv_out'], mode="nt", out_dtype=F32, name="mm_conv_out")
        qh, kh, vh = _gqa_prep_fwd(proj, vec2(gqa_q_norm_g[i]), vec2(gqa_k_norm_g[i]), cos_g, sin_g,
                                   off_q=off_q, hq=hq, hkv=hkv, nb=nb, lp=lp, tm=tm_mid, q_scale=gqa_scale)
        ob = _attn_fwd(qh, kh, vh, l_valid=l_valid, name="gqa_attn_fwd")
        yb = _matmul(ob, wl['w_gqa_out'], mode="nt", out_dtype=F32, name="mm_gqa_out")
        cqn = _rms_fwd(proj, qr, _cb(off_cq, qr), vec2(mla_q_norm_g[i]), tm_mid, "mla_q_norm_fwd")
        ckvn = _rms_fwd(proj, kvr, _cb(off_ckv, kvr), vec2(mla_kv_norm_g[i]), tm_mid, "mla_kv_norm_fwd")
        qm = _matmul(cqn, wl['w_mla_uq'], mode="nt", out_dtype=F32, name="mm_mla_uq")
        kvm = _matmul(ckvn, wl['w_mla_ukv'], mode="nt", out_dtype=F32, name="mm_mla_ukv")
        qc, kc, vc = _mla_prep_fwd(qm, kvm, proj, cos_m, sin_m, off_kpe=off_kpe, nh=nh, nb=nb, lp=lp, tm=tm_mid,
                                   q_scale=mla_scale)
        oc = _attn_fwd(qc, kc, vc, l_valid=l_valid, name="mla_attn_fwd")
        yc = _matmul(oc, wl['w_mla_out'], mode="nt", out_dtype=F32, name="mm_mla_out")
        merged = _gate_fwd(proj, vec2(gate_b[i]), ya, yb, yc, d=d, tm=tm_wide)
        h2 = _matmul(merged, wl['w_out'], mode="nn", out_dtype=F32, name="mm_out", residual=h)
        vn = _rms_fwd(h2, d, 0, vec2(ffn_norm_g[i]), tm_wide, "ffn_norm_fwd")
        wl.update(all_gather_group(i, 1, [h2]))
        gt = _matmul(vn, wl['w_ffn_gate'], mode="nt", out_dtype=F32, name="mm_ffn_gate")
        up = _matmul(vn, wl['w_ffn_up'], mode="nt", out_dtype=F32, name="mm_ffn_up")
        act = _swiglu_fwd(gt, up, tm_wide)
        h = _matmul(act, wl['w_ffn_down'], mode="nn", out_dtype=F32, name="mm_ffn_down", residual=h2)
        s.update(u=u, proj=proj, cc=cc, sc=sc, ya=ya, qh=qh, kh=kh, vh=vh, ob=ob, yb=yb, cqn=cqn, ckvn=ckvn,
                 qc=qc, kc=kc, vc=vc, oc=oc, yc=yc, merged=merged, h2=h2, vn=vn, gt=gt, up=up, act=act)
        saved.append(s)

    dh, dg_final, loss_part = _loss_head(h, vec2(final_norm_g), tgt, lp=lp, l_valid=l_valid, tm=tm_wide)

    gsmall = {n: [None] * depth for n in SMALL if n != 'final_norm_g'}
    gdw = [None] * depth
    gbig = [{} for _ in range(depth)]
    cidx = jnp.reshape(c_id, (1,)).astype(jnp.int32)
    chip_idx = jnp.reshape(2 * x_id + y_id, (1,)).astype(jnp.int32)

    def reduce_scatter_start(gl, i, gi):
        names = GROUPS[gi]
        g4s = [gl[n].reshape(N_DEV // 2, 2, -1, gl[n].shape[-1]) for n in names]
        from_sibling = _swap_with_sibling(g4s)
        p4s = [_add_sibling(g, r, cidx, "rs_add_sibling_" + n) for n, g, r in zip(names, g4s, from_sibling)]
        *pend, tok = _ici_start(p4s, [(3,) + p.shape[1:] for p in p4s], [], gather=False,
                                name="rs_ici_start_%d_%d" % (i, gi))
        return pend, tok

    def reduce_scatter_finish(pend, after, i, gi):
        p4s, from_chips = _ici_wait(*pend, after, gather=False, name="rs_ici_wait_%d_%d" % (i, gi))
        out = {}
        for n, p, r in zip(GROUPS[gi], p4s, from_chips):
            g = _add_chips(p, r, chip_idx, "rs_add_chips_" + n)
            out[n] = g if n in ROW_SHARDED else g.T
        return out

    rs_pending = [None, None]
    rs_token = []
    for i in reversed(range(depth)):
        s = saved[i]
        proj = s['proj']
        wl = full[i]
        gl = {}
        dact = _matmul(dh, wl['w_ffn_down'], mode="nt", out_dtype=F32, name="mm_ffn_down_dx", deps=rs_token)
        gl['w_ffn_down'] = _matmul_tn_packed(s['act'], dh, "mm_ffn_down_dw")
        dgt, dup = _swiglu_bwd(dact, s['gt'], s['up'], tm_wide)
        dvn = _matmul(dgt, wl['w_ffn_gate'], mode="nn", out_dtype=F32, name="mm_ffn_gate_dx")
        dvn = _matmul(dup, wl['w_ffn_up'], mode="nn", out_dtype=F32, name="mm_ffn_up_dx", residual=dvn)
        gl['w_ffn_gate'] = _matmul_tn_packed(dgt, s['vn'], "mm_ffn_gate_dw")
        gl['w_ffn_up'] = _matmul_tn_packed(dup, s['vn'], "mm_ffn_up_dw")
        dh2, gsmall['ffn_norm_g'][i] = _rms_bwd(s['h2'], d, 0, vec2(ffn_norm_g[i]), dvn, tm_wide, F32,
                                                "ffn_norm_bwd", add=dh)
        started, tok = reduce_scatter_start(gl, i, 1)
        if rs_pending[1] is not None:
            gbig[i + 1].update(reduce_scatter_finish(rs_pending[1], [dh2, tok], i + 1, 1))
        rs_pending[1] = started
        dmg = _matmul(dh2, wl['w_out'], mode="nt", out_dtype=F32, name="mm_out_dx", deps=[tok])
        gl['w_out'] = _matmul_tn_packed(s['merged'], dh2, "mm_out_dw")
        dya, dyb, dyc, dg0, dg1, dg2, db0, db1, db2 = _gate_bwd(dmg, proj, vec2(gate_b[i]), s['ya'], s['yb'],
                                                                s['yc'], d=d, tm=tm_wide)
        gsmall['gate_b'][i] = jnp.concatenate([db0, db1, db2], axis=1)
        dsc = _matmul(dya, wl['w_conv_out'], mode="nn", out_dtype=F32, name="mm_conv_out_dx")
        gl['w_conv_out'] = _matmul_tn_packed(dya, s['sc'], "mm_conv_out_dw")
        dcc, gsmall['conv_ln_g'][i], gsmall['conv_ln_b'][i] = _ln_silu_bwd(
            s['cc'], vec2(conv_ln_g[i]), vec2(conv_ln_b[i]), dsc, tm_mid)
        da, dgc, gdw[i], gsmall['conv_b'][i] = _conv_bwd(proj, dcc, dw_full[i], c_conv=c_conv, off_a=off_conv,
                                                         nb=nb, lp=lp, l_valid=l_valid)
        dob = _matmul(dyb, wl['w_gqa_out'], mode="nn", out_dtype=BF16, name="mm_gqa_out_dx")
        gl['w_gqa_out'] = _matmul_tn_packed(dyb, s['ob'], "mm_gqa_out_dw")
        dqh, dkh, dvh = _attn_bwd(s['qh'], s['kh'], s['vh'], dob, l_valid=l_valid, name="gqa_attn_bwd")
        dq, dkv, gsmall['gqa_q_norm_g'][i], gsmall['gqa_k_norm_g'][i] = _gqa_prep_bwd(
            proj, vec2(gqa_q_norm_g[i]), vec2(gqa_k_norm_g[i]), cos_g, sin_g, dqh, dkh, dvh,
            off_q=off_q, hq=hq, hkv=hkv, nb=nb, lp=lp, tm=tm_mid, q_scale=gqa_scale)
        doc = _matmul(dyc, wl['w_mla_out'], mode="nn", out_dtype=BF16, name="mm_mla_out_dx")
        gl['w_mla_out'] = _matmul_tn_packed(dyc, s['oc'], "mm_mla_out_dw")
        dqc, dkc, dvc = _attn_bwd(s['qc'], s['kc'], s['vc'], doc, l_valid=l_valid, name="mla_attn_bwd")
        dqm, dkvm, dkpe = _mla_prep_bwd(dqc, dkc, dvc, cos_m, sin_m, nh=nh, nb=nb, lp=lp, tm=tm_mid,
                                        q_scale=mla_scale)
        dcqn = _matmul(dqm, wl['w_mla_uq'], mode="nn", out_dtype=F32, name="mm_mla_uq_dx")
        guq = _matmul_tn_packed(dqm, s['cqn'], "mm_mla_uq_dw")
        dckvn = _matmul(dkvm, wl['w_mla_ukv'], mode="nn", out_dtype=F32, name="mm_mla_ukv_dx")
        gukv = _matmul_tn_packed(dkvm, s['ckvn'], "mm_mla_ukv_dw")
        dcq, gsmall['mla_q_norm_g'][i] = _rms_bwd(proj, qr, _cb(off_cq, qr), vec2(mla_q_norm_g[i]), dcqn, tm_mid,
                                                  BF16, "mla_q_norm_bwd")
        dckv, gsmall['mla_kv_norm_g'][i] = _rms_bwd(proj, kvr, _cb(off_ckv, kvr), vec2(mla_kv_norm_g[i]), dckvn,
                                                    tm_mid, BF16, "mla_kv_norm_bwd")
        gl['w_mla_uq'] = _gather_rows(guq, name="perm_mla_uq_dw", out_rows=nh * mla_qk, tr=rb, src_fn=uq_dst)
        gl['w_mla_ukv'] = _gather_rows(gukv, name="perm_mla_ukv_dw", out_rows=2 * nh * hd, tr=hd,
                                       src_fn=lambda i: i // 2 + nh * (i % 2))
        dproj = jnp.concatenate([dg0, dg1, dg2, da, dgc, dq, dkv, dcq, dckv, dkpe], axis=1)
        du = _matmul(dproj, wl['w_in'], mode="nn", out_dtype=F32, name="mm_in_dx")
        gin = _matmul_tn_packed(dproj, s['u'], "mm_in_dw")
        gl['w_in'] = _gather_rows(gin, name="perm_in_dw", out_rows=d_in, tr=rb, src_fn=in_dst)
        dh, gsmall['mix_norm_g'][i] = _rms_bwd(s['h_in'], d, 0, vec2(mix_norm_g[i]), du, tm_wide, F32,
                                               "mix_norm_bwd", add=dh2)
        started, tok = reduce_scatter_start(gl, i, 0)
        if rs_pending[0] is not None:
            gbig[i + 1].update(reduce_scatter_finish(rs_pending[0], [dh, tok], i + 1, 0))
        rs_pending[0], rs_token = started, [tok]

    def adamw_big(names):
        for n in names:
            sh = w[n].shape
            two_d = (sh[0] * sh[1], sh[2])
            grads[n] = jnp.stack([gbig[i][n] for i in range(depth)])
            dl, mn, vn_ = _adamw(w[n].reshape(two_d), grads[n].reshape(two_d), mom1[n].reshape(two_d),
                                 mom2[n].reshape(two_d), "adamw_" + n)
            delta[n], new_m[n], new_v[n] = dl.reshape(sh), mn.reshape(sh), vn_.reshape(sh)

    grads, delta, new_m, new_v = {}, {}, {}, {}
    gbig[0].update(reduce_scatter_finish(rs_pending[1], rs_token, 0, 1))
    adamw_big(GROUPS[1])
    gbig[0].update(reduce_scatter_finish(rs_pending[0], [delta[GROUPS[1][-1]]], 0, 0))
    adamw_big(GROUPS[0])

    dh0 = dh.reshape(nb, lp, d)
    grad_x = dh0[:, N_META:l_valid]
    gmeta_full = jnp.sum(dh0[:, :N_META], axis=0)

    small_list = [jnp.stack(gsmall[n]).reshape(w[n].shape) for n in SMALL if n != 'final_norm_g']
    small_list += [dg_final.reshape(final_norm_g.shape), gmeta_full, jnp.stack(gdw), loss_part]
    ar_buf, ar_sizes = _pack_rows(small_list, 8 * LANE)
    ar = _all_reduce_small(ar_buf)
    ar_shapes = [w[n].shape for n in SMALL] + [(N_META, d), (depth, kw, c_conv), (1, LANE)]
    ar_out = _unpack_rows(ar, ar_sizes, ar_shapes)
    for n, g in zip(SMALL, ar_out):
        grads[n] = g
    dcol = d // N_DEV
    grads['meta_tokens'] = lax.dynamic_slice(ar_out[len(SMALL)], (0, me * dcol), (N_META, dcol))
    ccol = c_conv // N_DEV
    grads['conv_dw'] = lax.dynamic_slice(ar_out[len(SMALL) + 1], (0, 0, me * ccol),
                                         (depth, kw, ccol)).reshape(conv_dw.shape)
    loss = ar_out[len(SMALL) + 2][0, 0]

    rest = [n for n in W_NAMES if n not in BIG]
    pw, psz = _pack_rows([w[n] for n in rest], 8 * LANE)
    pg, _ = _pack_rows([grads[n] for n in rest], 8 * LANE)
    pm, _ = _pack_rows([mom1[n] for n in rest], 8 * LANE)
    pv, _ = _pack_rows([mom2[n] for n in rest], 8 * LANE)
    dl, mn, vn_ = _adamw(pw, pg, pm, pv, "adamw_small")
    shapes = [w[n].shape for n in rest]
    for n, a, b, c in zip(rest, _unpack_rows(dl, psz, shapes), _unpack_rows(mn, psz, shapes),
                          _unpack_rows(vn_, psz, shapes)):
        delta[n], new_m[n], new_v[n] = a, b, c

    return (loss, grad_x, *[grads[n] for n in W_NAMES], *[delta[n] for n in W_NAMES],
            *[new_m[n] for n in W_NAMES], *[new_v[n] for n in W_NAMES])
```

```python
import functools
import math

import jax
import jax.numpy as jnp
from jax import lax
from jax.experimental import pallas as pl
from jax.experimental.pallas import tpu as pltpu

F32 = jnp.float32
BF16 = jnp.bfloat16

N_META = 16
GRID_W = 64
ROPE_THETA = 10000.0
NORM_EPS = 1e-6
HEAD_DIM = 128
MLA_ROPE_DIM = 64
N_BRANCH = 3
N_DEV = 8
LANE = 128
VMEM_LIMIT = 56 * 1024 * 1024

ADAM_LR = 0.001
ADAM_B1 = 0.9
ADAM_B2 = 0.999
ADAM_EPS = 1e-08
ADAM_WD = 0.01
ADAM_STEP = 10

W_NAMES = ['meta_tokens', 'mix_norm_g', 'w_in', 'conv_dw', 'conv_b', 'conv_ln_g', 'conv_ln_b', 'w_conv_out',
           'gqa_q_norm_g', 'gqa_k_norm_g', 'w_gqa_out', 'mla_q_norm_g', 'w_mla_uq', 'mla_kv_norm_g', 'w_mla_ukv',
           'w_mla_out', 'gate_b', 'w_out', 'ffn_norm_g', 'w_ffn_gate', 'w_ffn_up', 'w_ffn_down', 'final_norm_g']
BIG = ['w_in', 'w_conv_out', 'w_gqa_out', 'w_mla_uq', 'w_mla_ukv', 'w_mla_out', 'w_out', 'w_ffn_gate', 'w_ffn_up',
       'w_ffn_down']
ROW_SHARDED = ('w_out', 'w_ffn_down')
GROUPS = (('w_in', 'w_conv_out', 'w_gqa_out', 'w_mla_uq', 'w_mla_ukv', 'w_mla_out', 'w_out'),
          ('w_ffn_gate', 'w_ffn_up', 'w_ffn_down'))
SMALL = ['mix_norm_g', 'conv_b', 'conv_ln_g', 'conv_ln_b', 'gqa_q_norm_g', 'gqa_k_norm_g', 'mla_q_norm_g',
         'mla_kv_norm_g', 'gate_b', 'ffn_norm_g', 'final_norm_g']

NT = (((1,), (1,)), ((), ()))
TN = (((0,), (0,)), ((), ()))
NN = (((1,), (0,)), ((), ()))


def _tile(n, target, mult):
    best = None
    for t in range(mult, min(n, target) + 1, mult):
        if n % t == 0:
            best = t
    assert best is not None, (n, target, mult)
    return best


def _cp(n):
    return pltpu.CompilerParams(dimension_semantics=("arbitrary",) * n, vmem_limit_bytes=VMEM_LIMIT)


def _sig(x):
    return jax.nn.sigmoid(x)


def _cb(off, w):
    assert off % w == 0, (off, w)
    return off // w


def _matmul(a, b, *, mode, out_dtype, name, residual=None, deps=()):
    if mode == "nn":
        (m, k), (k2, n) = a.shape, b.shape
    elif mode == "nt":
        (m, k), (n, k2) = a.shape, b.shape
    else:
        (k, m), (k2, n) = a.shape, b.shape
    assert k == k2, (a.shape, b.shape, mode)
    tm = _tile(m, 1088, 128 if mode == "tn" else 16)
    tn = _tile(n, 1024, 128)
    tk = _tile(k, 2176, 128)
    nk = k // tk
    dims = {"nn": NN, "nt": NT, "tn": TN}[mode]
    a_spec = (pl.BlockSpec((tk, tm), lambda i, j, kk: (kk, i)) if mode == "tn"
              else pl.BlockSpec((tm, tk), lambda i, j, kk: (i, kk)))
    b_spec = (pl.BlockSpec((tn, tk), lambda i, j, kk: (j, kk)) if mode == "nt"
              else pl.BlockSpec((tk, tn), lambda i, j, kk: (kk, j)))
    o_spec = pl.BlockSpec((tm, tn), lambda i, j, kk: (i, j))
    has_res = residual is not None

    nd = len(deps)

    def body(*refs):
        refs = refs[:len(refs) - 2 - nd] + refs[len(refs) - 2:]
        if has_res:
            a_ref, b_ref, r_ref, o_ref, acc = refs
        else:
            a_ref, b_ref, o_ref, acc = refs
        kk = pl.program_id(2)

        @pl.when(kk == 0)
        def _():
            acc[...] = jnp.zeros_like(acc)

        acc[...] += lax.dot_general(a_ref[...].astype(BF16), b_ref[...].astype(BF16), dims,
                                    preferred_element_type=F32)

        @pl.when(kk == nk - 1)
        def _():
            r = acc[...]
            if has_res:
                r = r + r_ref[...]
            o_ref[...] = r.astype(o_ref.dtype)

    ins = [a, b] + ([residual] if has_res else []) + list(deps)
    in_specs = [a_spec, b_spec] + ([o_spec] if has_res else []) + [pl.BlockSpec(memory_space=pl.ANY)] * nd
    return pl.pallas_call(
        body, grid=(m // tm, n // tn, nk), in_specs=in_specs, out_specs=o_spec,
        out_shape=jax.ShapeDtypeStruct((m, n), out_dtype), scratch_shapes=[pltpu.VMEM((tm, tn), F32)],
        name=name, compiler_params=_cp(3))(*ins)


def _rms_fwd(x, w, cb, g, tm, name):
    m = x.shape[0]

    def body(x_ref, g_ref, o_ref):
        xv = x_ref[...]
        r = lax.rsqrt(jnp.mean(xv * xv, axis=-1, keepdims=True) + NORM_EPS)
        o_ref[...] = (xv * r * g_ref[...]).astype(o_ref.dtype)

    return pl.pallas_call(
        body, grid=(m // tm,),
        in_specs=[pl.BlockSpec((tm, w), lambda i: (i, cb)), pl.BlockSpec((1, w), lambda i: (0, 0))],
        out_specs=pl.BlockSpec((tm, w), lambda i: (i, 0)), out_shape=jax.ShapeDtypeStruct((m, w), BF16),
        name=name, compiler_params=_cp(1))(x, g)


def _rms_bwd(x, w, cb, g, dy, tm, out_dtype, name, add=None):
    m = x.shape[0]
    has_add = add is not None

    def body(*refs):
        if has_add:
            x_ref, g_ref, dy_ref, add_ref, dx_ref, dg_ref = refs
        else:
            x_ref, g_ref, dy_ref, dx_ref, dg_ref = refs
        xv = x_ref[...]
        dyv = dy_ref[...].astype(F32)
        r = lax.rsqrt(jnp.mean(xv * xv, axis=-1, keepdims=True) + NORM_EPS)
        t = dyv * g_ref[...]
        dx = r * t - xv * (r * r * r) * jnp.mean(t * xv, axis=-1, keepdims=True)
        if has_add:
            dx = dx + add_ref[...]
        dx_ref[...] = dx.astype(dx_ref.dtype)

        @pl.when(pl.program_id(0) == 0)
        def _():
            dg_ref[...] = jnp.zeros_like(dg_ref)

        dg_ref[...] += jnp.sum(dyv * xv * r, axis=0, keepdims=True)

    row = pl.BlockSpec((tm, w), lambda i: (i, 0))
    vec = pl.BlockSpec((1, w), lambda i: (0, 0))
    ins = [x, g, dy] + ([add] if has_add else [])
    in_specs = [pl.BlockSpec((tm, w), lambda i: (i, cb)), vec, row] + ([row] if has_add else [])
    return pl.pallas_call(
        body, grid=(m // tm,), in_specs=in_specs, out_specs=(row, vec),
        out_shape=(jax.ShapeDtypeStruct((m, w), out_dtype), jax.ShapeDtypeStruct((1, w), F32)),
        name=name, compiler_params=_cp(1))(*ins)


def _ln_silu_fwd(c, lg, lb, tm):
    m, w = c.shape

    def body(c_ref, g_ref, b_ref, o_ref):
        cv = c_ref[...]
        xc = cv - jnp.mean(cv, axis=-1, keepdims=True)
        r = lax.rsqrt(jnp.mean(xc * xc, axis=-1, keepdims=True) + NORM_EPS)
        yl = xc * r * g_ref[...] + b_ref[...]
        o_ref[...] = (yl * _sig(yl)).astype(o_ref.dtype)

    row = pl.BlockSpec((tm, w), lambda i: (i, 0))
    vec = pl.BlockSpec((1, w), lambda i: (0, 0))
    return pl.pallas_call(body, grid=(m // tm,), in_specs=[row, vec, vec], out_specs=row,
                          out_shape=jax.ShapeDtypeStruct((m, w), BF16), name="ln_silu_fwd",
                          compiler_params=_cp(1))(c, lg, lb)


def _ln_silu_bwd(c, lg, lb, ds, tm):
    m, w = c.shape

    def body(c_ref, g_ref, b_ref, ds_ref, dc_ref, dg_ref, db_ref):
        cv = c_ref[...]
        xc = cv - jnp.mean(cv, axis=-1, keepdims=True)
        r = lax.rsqrt(jnp.mean(xc * xc, axis=-1, keepdims=True) + NORM_EPS)
        nv = xc * r
        yl = nv * g_ref[...] + b_ref[...]
        sg = _sig(yl)
        dyl = ds_ref[...] * (sg * (1.0 + yl * (1.0 - sg)))
        dn = dyl * g_ref[...]
        dc = r * (dn - jnp.mean(dn, axis=-1, keepdims=True) - nv * jnp.mean(dn * nv, axis=-1, keepdims=True))
        dc_ref[...] = dc

        @pl.when(pl.program_id(0) == 0)
        def _():
            dg_ref[...] = jnp.zeros_like(dg_ref)
            db_ref[...] = jnp.zeros_like(db_ref)

        dg_ref[...] += jnp.sum(dyl * nv, axis=0, keepdims=True)
        db_ref[...] += jnp.sum(dyl, axis=0, keepdims=True)

    row = pl.BlockSpec((tm, w), lambda i: (i, 0))
    vec = pl.BlockSpec((1, w), lambda i: (0, 0))
    return pl.pallas_call(
        body, grid=(m // tm,), in_specs=[row, vec, vec, row], out_specs=(row, vec, vec),
        out_shape=(jax.ShapeDtypeStruct((m, w), F32), jax.ShapeDtypeStruct((1, w), F32),
                   jax.ShapeDtypeStruct((1, w), F32)),
        name="ln_silu_bwd", compiler_params=_cp(1))(c, lg, lb, ds)


CONV_MARGIN = 16
CONV_ROWS = 128


def _conv_fwd(proj, dw, bias, *, c_conv, off_a, nb, lp, l_valid):
    kw = dw.shape[0]
    half = kw // 2
    cw = LANE
    mg = CONV_MARGIN
    assert half <= mg - 1 and lp % CONV_ROWS == 0

    def body(a_ref, g_ref, w_ref, b_ref, c_ref, zp):
        t = lax.broadcasted_iota(jnp.int32, (lp, cw), 0)
        z = jnp.where(t < l_valid, a_ref[...] * _sig(g_ref[...]), 0.0)
        zp[0:mg, :] = jnp.zeros((mg, cw), F32)
        zp[mg + lp:mg + lp + mg, :] = jnp.zeros((mg, cw), F32)
        zp[mg:mg + lp, :] = z
        for r0 in range(0, lp, CONV_ROWS):
            acc = jnp.zeros((CONV_ROWS, cw), F32) + b_ref[...]
            for k in range(kw):
                s0 = mg + r0 + k - half
                acc = acc + w_ref[k:k + 1, :] * zp[s0:s0 + CONV_ROWS, :]
            c_ref[r0:r0 + CONV_ROWS, :] = acc

    ja, jg = _cb(off_a, cw), _cb(off_a + c_conv, cw)
    return pl.pallas_call(
        body, grid=(nb, c_conv // cw),
        in_specs=[pl.BlockSpec((lp, cw), lambda b, j: (b, ja + j)), pl.BlockSpec((lp, cw), lambda b, j: (b, jg + j)),
                  pl.BlockSpec((kw, cw), lambda b, j: (0, j)), pl.BlockSpec((1, cw), lambda b, j: (0, j))],
        out_specs=pl.BlockSpec((lp, cw), lambda b, j: (b, j)),
        out_shape=jax.ShapeDtypeStruct((nb * lp, c_conv), F32),
        scratch_shapes=[pltpu.VMEM((lp + 2 * mg, cw), F32)], name="conv_fwd", compiler_params=_cp(2))(proj, proj, dw, bias)


def _conv_bwd(proj, dc, dw, *, c_conv, off_a, nb, lp, l_valid):
    kw = dw.shape[0]
    half = kw // 2
    cw = LANE
    mg = CONV_MARGIN

    def body(a_ref, g_ref, dc_ref, w_ref, da_ref, dg_ref, ddw_ref, dcb_ref, zp, dcp):
        t = lax.broadcasted_iota(jnp.int32, (lp, cw), 0)
        z = jnp.where(t < l_valid, a_ref[...] * _sig(g_ref[...]), 0.0)
        for buf in (zp, dcp):
            buf[0:mg, :] = jnp.zeros((mg, cw), F32)
            buf[mg + lp:mg + lp + mg, :] = jnp.zeros((mg, cw), F32)
        zp[mg:mg + lp, :] = z
        dcv = dc_ref[...]
        dcp[mg:mg + lp, :] = dcv

        @pl.when(pl.program_id(1) == 0)
        def _():
            ddw_ref[...] = jnp.zeros_like(ddw_ref)
            dcb_ref[...] = jnp.zeros_like(dcb_ref)

        dcb_ref[...] += jnp.sum(dcv, axis=0, keepdims=True)
        for r0 in range(0, lp, CONV_ROWS):
            acc = jnp.zeros((CONV_ROWS, cw), F32)
            for k in range(kw):
                s0 = mg + r0 - k + half
                acc = acc + w_ref[k:k + 1, :] * dcp[s0:s0 + CONV_ROWS, :]
            tc = lax.broadcasted_iota(jnp.int32, (CONV_ROWS, cw), 0) + r0
            dz = jnp.where(tc < l_valid, acc, 0.0)
            sg = _sig(g_ref[r0:r0 + CONV_ROWS, :])
            da_ref[r0:r0 + CONV_ROWS, :] = (dz * sg).astype(da_ref.dtype)
            dg_ref[r0:r0 + CONV_ROWS, :] = (dz * a_ref[r0:r0 + CONV_ROWS, :] * sg * (1.0 - sg)).astype(dg_ref.dtype)
        for k in range(kw):
            acc = jnp.zeros((CONV_ROWS, cw), F32)
            for r0 in range(0, lp, CONV_ROWS):
                s0 = mg + r0 + k - half
                acc = acc + dc_ref[r0:r0 + CONV_ROWS, :] * zp[s0:s0 + CONV_ROWS, :]
            ddw_ref[k:k + 1, :] += jnp.sum(acc, axis=0, keepdims=True)

    ja, jg = _cb(off_a, cw), _cb(off_a + c_conv, cw)
    seq = pl.BlockSpec((lp, cw), lambda j, b: (b, j))
    return pl.pallas_call(
        body, grid=(c_conv // cw, nb),
        in_specs=[pl.BlockSpec((lp, cw), lambda j, b: (b, ja + j)), pl.BlockSpec((lp, cw), lambda j, b: (b, jg + j)),
                  seq, pl.BlockSpec((kw, cw), lambda j, b: (0, j))],
        out_specs=(seq, seq, pl.BlockSpec((kw, cw), lambda j, b: (0, j)), pl.BlockSpec((1, cw), lambda j, b: (0, j))),
        out_shape=(jax.ShapeDtypeStruct((nb * lp, c_conv), BF16), jax.ShapeDtypeStruct((nb * lp, c_conv), BF16),
                   jax.ShapeDtypeStruct((kw, c_conv), F32), jax.ShapeDtypeStruct((1, c_conv), F32)),
        scratch_shapes=[pltpu.VMEM((lp + 2 * mg, cw), F32), pltpu.VMEM((lp + 2 * mg, cw), F32)],
        name="conv_bwd", compiler_params=_cp(2))(proj, proj, dc, dw)


def _swap_halves(x, half):
    fwd = pltpu.roll(x, LANE - half, axis=1)
    bwd = pltpu.roll(x, half, axis=1)
    lane = lax.broadcasted_iota(jnp.int32, x.shape, 1)
    return jnp.where((lane & (2 * half - 1)) < half, fwd, bwd)


def _rope(x, cos, sin, half):
    return x * cos + _swap_halves(x, half) * sin


def _rope_t(dy, cos, sin, half):
    return dy * cos + _swap_halves(dy * sin, half)


def _rope_tables(lp, l_valid, half):
    t = jnp.arange(lp)
    n = jnp.clip(t - N_META, 0, None)
    real = (t >= N_META) & (t < l_valid)
    row = jnp.where(real, n // GRID_W, 0).astype(F32)
    col = jnp.where(real, n % GRID_W, 0).astype(F32)
    inv = ROPE_THETA ** (-jnp.arange(half, dtype=F32) / half)
    ar, ac = row[:, None] * inv[None, :], col[:, None] * inv[None, :]
    cos = jnp.concatenate([jnp.cos(ar), jnp.cos(ar), jnp.cos(ac), jnp.cos(ac)], axis=1)
    sin = jnp.concatenate([-jnp.sin(ar), jnp.sin(ar), -jnp.sin(ac), jnp.sin(ac)], axis=1)
    padw = LANE - 4 * half
    if padw:
        cos = jnp.pad(cos, ((0, 0), (0, padw)))
        sin = jnp.pad(sin, ((0, 0), (0, padw)))
    return cos.astype(F32), sin.astype(F32)


def _gqa_prep_fwd(proj, qg, kg, cos, sin, *, off_q, hq, hkv, nb, lp, tm, q_scale):
    hd = HEAD_DIM
    half = hd // 4
    wq, wkv = hq * hd, 2 * hkv * hd
    nt = lp // tm

    def body(q_ref, kv_ref, qg_ref, kg_ref, cos_ref, sin_ref, qo_ref, ko_ref, vo_ref):
        cosv, sinv = cos_ref[...], sin_ref[...]

        def norm_rope(xh, g):
            r = lax.rsqrt(jnp.mean(xh * xh, axis=-1, keepdims=True) + NORM_EPS)
            return _rope(xh * r * g, cosv, sinv, half)

        for h in range(hq):
            qo_ref[h] = (norm_rope(q_ref[:, h * hd:(h + 1) * hd], qg_ref[...]) * q_scale).astype(BF16)
        for h in range(hkv):
            ko_ref[h] = norm_rope(kv_ref[:, h * hd:(h + 1) * hd], kg_ref[...]).astype(BF16)
            vo_ref[h] = kv_ref[:, (hkv + h) * hd:(hkv + h + 1) * hd].astype(BF16)

    jq, jkv = _cb(off_q, wq), _cb(off_q + wq, wkv)
    tab = pl.BlockSpec((tm, LANE), lambda b, i: (i, 0))
    vec = pl.BlockSpec((1, hd), lambda b, i: (0, 0))

    def heads(h):
        return pl.BlockSpec((None, h, tm, hd), lambda b, i: (b, 0, i, 0))

    return pl.pallas_call(
        body, grid=(nb, nt),
        in_specs=[pl.BlockSpec((tm, wq), lambda b, i: (b * nt + i, jq)),
                  pl.BlockSpec((tm, wkv), lambda b, i: (b * nt + i, jkv)), vec, vec, tab, tab],
        out_specs=(heads(hq), heads(hkv), heads(hkv)),
        out_shape=(jax.ShapeDtypeStruct((nb, hq, lp, hd), BF16), jax.ShapeDtypeStruct((nb, hkv, lp, hd), BF16),
                   jax.ShapeDtypeStruct((nb, hkv, lp, hd), BF16)),
        name="gqa_prep_fwd", compiler_params=_cp(2))(proj, proj, qg, kg, cos, sin)


def _gqa_prep_bwd(proj, qg, kg, cos, sin, dqh, dkh, dvh, *, off_q, hq, hkv, nb, lp, tm, q_scale):
    hd = HEAD_DIM
    half = hd // 4
    wq, wkv = hq * hd, 2 * hkv * hd
    nt = lp // tm

    def body(q_ref, kv_ref, qg_ref, kg_ref, cos_ref, sin_ref, dqh_ref, dkh_ref, dvh_ref,
             dq_ref, dkv_ref, dqg_ref, dkg_ref):
        cosv, sinv = cos_ref[...], sin_ref[...]

        def back(xh, g, dyh):
            dn = _rope_t(dyh, cosv, sinv, half)
            r = lax.rsqrt(jnp.mean(xh * xh, axis=-1, keepdims=True) + NORM_EPS)
            t = dn * g
            dx = r * t - xh * (r * r * r) * jnp.mean(t * xh, axis=-1, keepdims=True)
            return dx, jnp.sum(dn * xh * r, axis=0, keepdims=True)

        first = (pl.program_id(0) == 0) & (pl.program_id(1) == 0)

        @pl.when(first)
        def _():
            dqg_ref[...] = jnp.zeros_like(dqg_ref)
            dkg_ref[...] = jnp.zeros_like(dkg_ref)

        gq = jnp.zeros((1, hd), F32)
        for h in range(hq):
            dx, dg = back(q_ref[:, h * hd:(h + 1) * hd], qg_ref[...], dqh_ref[h] * q_scale)
            dq_ref[:, h * hd:(h + 1) * hd] = dx.astype(dq_ref.dtype)
            gq = gq + dg
        dqg_ref[...] += gq
        gk = jnp.zeros((1, hd), F32)
        for h in range(hkv):
            dx, dg = back(kv_ref[:, h * hd:(h + 1) * hd], kg_ref[...], dkh_ref[h])
            dkv_ref[:, h * hd:(h + 1) * hd] = dx.astype(dkv_ref.dtype)
            dkv_ref[:, (hkv + h) * hd:(hkv + h + 1) * hd] = dvh_ref[h].astype(dkv_ref.dtype)
            gk = gk + dg
        dkg_ref[...] += gk

    jq, jkv = _cb(off_q, wq), _cb(off_q + wq, wkv)
    tab = pl.BlockSpec((tm, LANE), lambda b, i: (i, 0))
    vec = pl.BlockSpec((1, hd), lambda b, i: (0, 0))

    def heads(h):
        return pl.BlockSpec((None, h, tm, hd), lambda b, i: (b, 0, i, 0))

    return pl.pallas_call(
        body, grid=(nb, nt),
        in_specs=[pl.BlockSpec((tm, wq), lambda b, i: (b * nt + i, jq)),
                  pl.BlockSpec((tm, wkv), lambda b, i: (b * nt + i, jkv)), vec, vec, tab, tab,
                  heads(hq), heads(hkv), heads(hkv)],
        out_specs=(pl.BlockSpec((tm, wq), lambda b, i: (b * nt + i, 0)),
                   pl.BlockSpec((tm, wkv), lambda b, i: (b * nt + i, 0)), vec, vec),
        out_shape=(jax.ShapeDtypeStruct((nb * lp, wq), BF16), jax.ShapeDtypeStruct((nb * lp, wkv), BF16),
                   jax.ShapeDtypeStruct((1, hd), F32), jax.ShapeDtypeStruct((1, hd), F32)),
        name="gqa_prep_bwd", compiler_params=_cp(2))(proj, proj, qg, kg, cos, sin, dqh, dkh, dvh)


def _mla_prep_fwd(qm, kvm, proj, cos, sin, *, off_kpe, nh, nb, lp, tm, q_scale):
    hd = HEAD_DIM
    half = MLA_ROPE_DIM // 4
    nt = lp // tm
    wh = nh * hd

    def body(qm_ref, kvm_ref, kpe_ref, cos_ref, sin_ref, qo_ref, ko_ref, vo_ref):
        cosv, sinv = cos_ref[...], sin_ref[...]
        kpe = _rope(kpe_ref[...], cosv, sinv, half).astype(BF16)
        for h in range(nh):
            qo_ref[h, :, 0:hd] = (qm_ref[:, h * hd:(h + 1) * hd] * q_scale).astype(BF16)
            qo_ref[h, :, hd:2 * hd] = (_rope(qm_ref[:, wh + h * hd:wh + (h + 1) * hd], cosv, sinv, half)
                                       * q_scale).astype(BF16)
            ko_ref[h, :, 0:hd] = kvm_ref[:, h * hd:(h + 1) * hd].astype(BF16)
            ko_ref[h, :, hd:2 * hd] = kpe
            vo_ref[h] = kvm_ref[:, wh + h * hd:wh + (h + 1) * hd].astype(BF16)

    tab = pl.BlockSpec((tm, LANE), lambda b, i: (i, 0))
    wide = pl.BlockSpec((tm, 2 * wh), lambda b, i: (b * nt + i, 0))
    jk = _cb(off_kpe, LANE)
    return pl.pallas_call(
        body, grid=(nb, nt),
        in_specs=[wide, wide, pl.BlockSpec((tm, LANE), lambda b, i: (b * nt + i, jk)), tab, tab],
        out_specs=(pl.BlockSpec((None, nh, tm, 2 * hd), lambda b, i: (b, 0, i, 0)),
                   pl.BlockSpec((None, nh, tm, 2 * hd), lambda b, i: (b, 0, i, 0)),
                   pl.BlockSpec((None, nh, tm, hd), lambda b, i: (b, 0, i, 0))),
        out_shape=(jax.ShapeDtypeStruct((nb, nh, lp, 2 * hd), BF16), jax.ShapeDtypeStruct((nb, nh, lp, 2 * hd), BF16),
                   jax.ShapeDtypeStruct((nb, nh, lp, hd), BF16)),
        name="mla_prep_fwd", compiler_params=_cp(2))(qm, kvm, proj, cos, sin)


def _mla_prep_bwd(dqc, dkc, dvc, cos, sin, *, nh, nb, lp, tm, q_scale):
    hd = HEAD_DIM
    half = MLA_ROPE_DIM // 4
    nt = lp // tm
    wh = nh * hd

    def body(dq_ref, dk_ref, dv_ref, cos_ref, sin_ref, dqm_ref, dkvm_ref, dkpe_ref):
        cosv, sinv = cos_ref[...], sin_ref[...]
        acc = jnp.zeros((tm, hd), F32)
        for h in range(nh):
            dqm_ref[:, h * hd:(h + 1) * hd] = (dq_ref[h, :, 0:hd] * q_scale).astype(BF16)
            dqm_ref[:, wh + h * hd:wh + (h + 1) * hd] = (_rope_t(dq_ref[h, :, hd:2 * hd], cosv, sinv, half)
                                                         * q_scale).astype(BF16)
            dkvm_ref[:, h * hd:(h + 1) * hd] = dk_ref[h, :, 0:hd].astype(BF16)
            dkvm_ref[:, wh + h * hd:wh + (h + 1) * hd] = dv_ref[h].astype(BF16)
            acc = acc + dk_ref[h, :, hd:2 * hd]
        dkpe_ref[...] = _rope_t(acc, cosv, sinv, half).astype(BF16)

    tab = pl.BlockSpec((tm, LANE), lambda b, i: (i, 0))
    wide = pl.BlockSpec((tm, 2 * wh), lambda b, i: (b * nt + i, 0))
    return pl.pallas_call(
        body, grid=(nb, nt),
        in_specs=[pl.BlockSpec((None, nh, tm, 2 * hd), lambda b, i: (b, 0, i, 0)),
                  pl.BlockSpec((None, nh, tm, 2 * hd), lambda b, i: (b, 0, i, 0)),
                  pl.BlockSpec((None, nh, tm, hd), lambda b, i: (b, 0, i, 0)), tab, tab],
        out_specs=(wide, wide, pl.BlockSpec((tm, LANE), lambda b, i: (b * nt + i, 0))),
        out_shape=(jax.ShapeDtypeStruct((nb * lp, 2 * wh), BF16), jax.ShapeDtypeStruct((nb * lp, 2 * wh), BF16),
                   jax.ShapeDtypeStruct((nb * lp, LANE), BF16)),
        name="mla_prep_bwd", compiler_params=_cp(2))(dqc, dkc, dvc, cos, sin)


def _softmax_parts(s, lp, l_valid):
    tw = -(-(lp - l_valid) // LANE) * LANE
    main, tail = s[:, :lp - tw], s[:, lp - tw:]
    col = lax.broadcasted_iota(jnp.int32, tail.shape, 1) + (lp - tw)
    tail = jnp.where(col < l_valid, tail, -1e30)
    m = jnp.maximum(jnp.max(main, axis=-1, keepdims=True), jnp.max(tail, axis=-1, keepdims=True))
    pm, pt = jnp.exp(main - m), jnp.exp(tail - m)
    den = jnp.sum(pm, axis=-1, keepdims=True) + jnp.sum(pt, axis=-1, keepdims=True)
    return pm, pt, den, tw


def _attn_fwd(q, k, v, *, l_valid, name):
    nb, hq, lp, dk = q.shape
    hkv, dv = k.shape[1], v.shape[3]
    grp = hq // hkv
    tq = _tile(lp, 1088, 64)
    sub = tq // 4
    nq = lp // tq

    def body(q_ref, k_ref, v_ref, o_ref, p_s):
        for r0 in range(0, tq, sub):
            rows = slice(r0, r0 + sub)
            s = lax.dot_general(q_ref[rows, :], k_ref[...], NT, preferred_element_type=F32)
            pm, pt, den, tw = _softmax_parts(s, lp, l_valid)
            p_s[rows, :lp - tw] = pm.astype(BF16)
            p_s[rows, lp - tw:] = pt.astype(BF16)
            o = lax.dot_general(p_s[rows, :], v_ref[...], NN, preferred_element_type=F32)
            o_ref[rows, :] = (o / den).astype(o_ref.dtype)

    return pl.pallas_call(
        body, grid=(nb, hq, nq),
        in_specs=[pl.BlockSpec((None, None, tq, dk), lambda b, h, i: (b, h, i, 0)),
                  pl.BlockSpec((None, None, lp, dk), lambda b, h, i: (b, h // grp, 0, 0)),
                  pl.BlockSpec((None, None, lp, dv), lambda b, h, i: (b, h // grp, 0, 0))],
        out_specs=pl.BlockSpec((tq, dv), lambda b, h, i: (b * nq + i, h)),
        out_shape=jax.ShapeDtypeStruct((nb * lp, hq * dv), BF16), scratch_shapes=[pltpu.VMEM((tq, lp), BF16)],
        name=name, compiler_params=_cp(3))(q, k, v)


def _attn_bwd(q, k, v, do, *, l_valid, name):
    nb, hq, lp, dk = q.shape
    hkv, dv = k.shape[1], v.shape[3]
    grp = hq // hkv
    tq = _tile(lp, 1088, 64)
    sub = tq // 4
    nq = lp // tq

    def body(q_ref, k_ref, v_ref, do_ref, dq_ref, dk_ref, dv_ref, p_s, ds_s):
        for r0 in range(0, tq, sub):
            rows = slice(r0, r0 + sub)
            s = lax.dot_general(q_ref[rows, :], k_ref[...], NT, preferred_element_type=F32)
            pm, pt, den, tw = _softmax_parts(s, lp, l_valid)
            inv = 1.0 / den
            pm, pt = pm * inv, pt * inv
            dp = lax.dot_general(do_ref[rows, :], v_ref[...], NT, preferred_element_type=F32)
            dpm, dpt = dp[:, :lp - tw], dp[:, lp - tw:]
            dd = jnp.sum(pm * dpm, axis=-1, keepdims=True) + jnp.sum(pt * dpt, axis=-1, keepdims=True)
            p_s[rows, :lp - tw] = pm.astype(BF16)
            p_s[rows, lp - tw:] = pt.astype(BF16)
            ds_s[rows, :lp - tw] = (pm * (dpm - dd)).astype(BF16)
            ds_s[rows, lp - tw:] = (pt * (dpt - dd)).astype(BF16)
            dq_ref[rows, :] = lax.dot_general(ds_s[rows, :], k_ref[...], NN, preferred_element_type=F32)

        @pl.when((pl.program_id(2) == 0) & (pl.program_id(3) == 0))
        def _():
            dk_ref[...] = jnp.zeros_like(dk_ref)
            dv_ref[...] = jnp.zeros_like(dv_ref)

        dk_ref[...] += lax.dot_general(ds_s[...], q_ref[...], TN, preferred_element_type=F32)
        dv_ref[...] += lax.dot_general(p_s[...], do_ref[...], TN, preferred_element_type=F32)

    return pl.pallas_call(
        body, grid=(nb, hkv, grp, nq), scratch_shapes=[pltpu.VMEM((tq, lp), BF16), pltpu.VMEM((tq, lp), BF16)],
        in_specs=[pl.BlockSpec((None, None, tq, dk), lambda b, h, g, i: (b, h * grp + g, i, 0)),
                  pl.BlockSpec((None, None, lp, dk), lambda b, h, g, i: (b, h, 0, 0)),
                  pl.BlockSpec((None, None, lp, dv), lambda b, h, g, i: (b, h, 0, 0)),
                  pl.BlockSpec((tq, dv), lambda b, h, g, i: (b * nq + i, h * grp + g))],
        out_specs=(pl.BlockSpec((None, None, tq, dk), lambda b, h, g, i: (b, h * grp + g, i, 0)),
                   pl.BlockSpec((None, None, lp, dk), lambda b, h, g, i: (b, h, 0, 0)),
                   pl.BlockSpec((None, None, lp, dv), lambda b, h, g, i: (b, h, 0, 0))),
        out_shape=(jax.ShapeDtypeStruct((nb, hq, lp, dk), F32), jax.ShapeDtypeStruct((nb, hkv, lp, dk), F32),
                   jax.ShapeDtypeStruct((nb, hkv, lp, dv), F32)),
        name=name, compiler_params=_cp(4))(q, k, v, do)


def _gate_fwd(proj, gb, ya, yb, yc, *, d, tm):
    m = ya.shape[0]
    cw = _tile(d, 512, LANE)
    nj = d // cw

    def body(g0, g1, g2, b0, b1, b2, ya_ref, yb_ref, yc_ref, o_ref):
        o = (_sig(g0[...] + b0[...]) * ya_ref[...] + _sig(g1[...] + b1[...]) * yb_ref[...]
             + _sig(g2[...] + b2[...]) * yc_ref[...])
        o_ref[...] = o.astype(o_ref.dtype)

    def gl(br):
        return pl.BlockSpec((tm, cw), lambda j, i: (i, br * nj + j))

    def gbs(br):
        return pl.BlockSpec((1, cw), lambda j, i: (0, br * nj + j))

    blk = pl.BlockSpec((tm, cw), lambda j, i: (i, j))
    return pl.pallas_call(
        body, grid=(nj, m // tm), in_specs=[gl(0), gl(1), gl(2), gbs(0), gbs(1), gbs(2), blk, blk, blk],
        out_specs=blk, out_shape=jax.ShapeDtypeStruct((m, d), BF16), name="gate_fwd",
        compiler_params=_cp(2))(proj, proj, proj, gb, gb, gb, ya, yb, yc)


def _gate_bwd(dm, proj, gb, ya, yb, yc, *, d, tm):
    m = ya.shape[0]
    cw = _tile(d, 512, LANE)
    nj = d // cw

    def body(dm_ref, g0, g1, g2, b0, b1, b2, ya_ref, yb_ref, yc_ref,
             dya, dyb, dyc, dg0, dg1, dg2, db0, db1, db2):
        dmv = dm_ref[...]

        @pl.when(pl.program_id(1) == 0)
        def _():
            for r in (db0, db1, db2):
                r[...] = jnp.zeros_like(r)

        for g, b, y, dy, dgl, db in ((g0, b0, ya_ref, dya, dg0, db0), (g1, b1, yb_ref, dyb, dg1, db1),
                                     (g2, b2, yc_ref, dyc, dg2, db2)):
            sg = _sig(g[...] + b[...])
            dy[...] = (sg * dmv).astype(dy.dtype)
            dl = dmv * y[...] * sg * (1.0 - sg)
            dgl[...] = dl.astype(dgl.dtype)
            db[...] += jnp.sum(dl, axis=0, keepdims=True)

    def gl(br):
        return pl.BlockSpec((tm, cw), lambda j, i: (i, br * nj + j))

    def gbs(br):
        return pl.BlockSpec((1, cw), lambda j, i: (0, br * nj + j))

    blk = pl.BlockSpec((tm, cw), lambda j, i: (i, j))
    vec = pl.BlockSpec((1, cw), lambda j, i: (0, j))
    act = jax.ShapeDtypeStruct((m, d), BF16)
    vsh = jax.ShapeDtypeStruct((1, d), F32)
    return pl.pallas_call(
        body, grid=(nj, m // tm), in_specs=[blk, gl(0), gl(1), gl(2), gbs(0), gbs(1), gbs(2), blk, blk, blk],
        out_specs=(blk,) * 6 + (vec,) * 3, out_shape=(act,) * 6 + (vsh,) * 3, name="gate_bwd",
        compiler_params=_cp(2))(dm, proj, proj, proj, gb, gb, gb, ya, yb, yc)


def _ffn_up_fwd(vn, wg_t, wu_t):
    m, d = vn.shape
    f = wg_t.shape[0]
    tm = _tile(m, 1088, 16)
    tn = _tile(f, 512, LANE)

    def body(v_ref, wg_ref, wu_ref, g_ref, u_ref, a_ref):
        v = v_ref[...]
        g = lax.dot_general(v, wg_ref[...], NT, preferred_element_type=F32)
        u = lax.dot_general(v, wu_ref[...], NT, preferred_element_type=F32)
        g_ref[...] = g
        u_ref[...] = u
        a_ref[...] = (g * _sig(g) * u).astype(a_ref.dtype)

    row = pl.BlockSpec((tm, d), lambda i, j: (i, 0))
    wsp = pl.BlockSpec((tn, d), lambda i, j: (j, 0))
    blk = pl.BlockSpec((tm, tn), lambda i, j: (i, j))
    return pl.pallas_call(
        body, grid=(m // tm, f // tn), in_specs=[row, wsp, wsp], out_specs=(blk, blk, blk),
        out_shape=(jax.ShapeDtypeStruct((m, f), F32), jax.ShapeDtypeStruct((m, f), F32),
                   jax.ShapeDtypeStruct((m, f), BF16)),
        name="ffn_up_fwd", compiler_params=_cp(2))(vn, wg_t, wu_t)


def _ffn_down_bwd(dh, wd, gt, up, deps):
    m, d = dh.shape
    f = wd.shape[0]
    tm = _tile(m, 1088, 16)
    tn = _tile(f, 512, LANE)
    nd = len(deps)

    def body(*refs):
        dh_ref, wd_ref, g_ref, u_ref = refs[:4]
        dg_ref, du_ref = refs[4 + nd:]
        da = lax.dot_general(dh_ref[...].astype(BF16), wd_ref[...], NT, preferred_element_type=F32)
        g = g_ref[...]
        sg = _sig(g)
        dg_ref[...] = (da * u_ref[...] * sg * (1.0 + g * (1.0 - sg))).astype(dg_ref.dtype)
        du_ref[...] = (da * g * sg).astype(du_ref.dtype)

    row = pl.BlockSpec((tm, d), lambda i, j: (i, 0))
    wsp = pl.BlockSpec((tn, d), lambda i, j: (j, 0))
    blk = pl.BlockSpec((tm, tn), lambda i, j: (i, j))
    sh = jax.ShapeDtypeStruct((m, f), BF16)
    return pl.pallas_call(
        body, grid=(m // tm, f // tn), in_specs=[row, wsp, blk, blk] + [pl.BlockSpec(memory_space=pl.ANY)] * nd,
        out_specs=(blk, blk), out_shape=(sh, sh), name="ffn_down_bwd", compiler_params=_cp(2))(dh, wd, gt, up, *deps)


def _loss_head(h, gf, tgt, *, lp, l_valid, tm):
    m, d = h.shape
    nt = lp // tm

    def body(h_ref, g_ref, t_ref, dh_ref, dg_ref, loss_ref):
        i = pl.program_id(0)
        xv = h_ref[...]
        r = lax.rsqrt(jnp.mean(xv * xv, axis=-1, keepdims=True) + NORM_EPS)
        y = xv * r * g_ref[...]
        t = lax.broadcasted_iota(jnp.int32, (tm, d), 0) + (i % nt) * tm
        err = jnp.where((t >= N_META) & (t < l_valid), y - t_ref[...], 0.0)
        dy = err * (1.0 / d)
        tg = dy * g_ref[...]
        dh_ref[...] = r * tg - xv * (r * r * r) * jnp.mean(tg * xv, axis=-1, keepdims=True)

        @pl.when(i == 0)
        def _():
            dg_ref[...] = jnp.zeros_like(dg_ref)
            loss_ref[...] = jnp.zeros_like(loss_ref)

        dg_ref[...] += jnp.sum(dy * xv * r, axis=0, keepdims=True)
        sq = jnp.sum(jnp.sum(err * err, axis=-1, keepdims=True), axis=0, keepdims=True)
        loss_ref[...] += jnp.zeros((1, LANE), F32) + sq * (0.5 / d)

    row = pl.BlockSpec((tm, d), lambda i: (i, 0))
    vec = pl.BlockSpec((1, d), lambda i: (0, 0))
    return pl.pallas_call(
        body, grid=(m // tm,), in_specs=[row, vec, row],
        out_specs=(row, vec, pl.BlockSpec((1, LANE), lambda i: (0, 0))),
        out_shape=(jax.ShapeDtypeStruct((m, d), F32), jax.ShapeDtypeStruct((1, d), F32),
                   jax.ShapeDtypeStruct((1, LANE), F32)),
        name="loss_head", compiler_params=_cp(1))(h, gf, tgt)


def _adamw(w, g, m, v, name):
    rows, cols = w.shape
    tr = rows
    if rows % 8 == 0:
        tr = _tile(rows, max(8, (1 << 18) // cols // 8 * 8), 8)
    c1 = 1.0 / (1.0 - ADAM_B1 ** ADAM_STEP)
    c2 = 1.0 / (1.0 - ADAM_B2 ** ADAM_STEP)

    def body(w_ref, g_ref, m_ref, v_ref, d_ref, mo_ref, vo_ref):
        gv = g_ref[...]
        mn = ADAM_B1 * m_ref[...] + (1.0 - ADAM_B1) * gv
        vn = ADAM_B2 * v_ref[...] + (1.0 - ADAM_B2) * (gv * gv)
        mo_ref[...] = mn
        vo_ref[...] = vn
        d_ref[...] = -ADAM_LR * ((mn * c1) / (jnp.sqrt(vn * c2) + ADAM_EPS) + ADAM_WD * w_ref[...])

    blk = pl.BlockSpec((tr, cols), lambda i: (i, 0))
    sh = jax.ShapeDtypeStruct((rows, cols), F32)
    return pl.pallas_call(body, grid=(rows // tr,), in_specs=[blk] * 4, out_specs=(blk,) * 3, out_shape=(sh,) * 3,
                          name=name, compiler_params=_cp(1))(w, g, m, v)


MESH = pl.DeviceIdType.MESH
ANY = pl.BlockSpec(memory_space=pl.ANY)


def _place():
    return lax.axis_index("x"), lax.axis_index("y"), lax.axis_index("c")


def _other_chips(x, y):
    return [(1 - x, y), (x, 1 - y), (1 - x, 1 - y)]


def _all_gather(xs, vmem, name):
    na = len(xs)

    def body(*refs):
        x_refs, out_refs = refs[:na], refs[na:2 * na]
        send_sems, recv_sems, local_sems = refs[2 * na:]
        x, y, c = _place()
        me, sibling = (x, y, c), (x, y, 1 - c)
        chips = _other_chips(x, y)

        def blk(p, px, py, pc):
            return out_refs[p].at[4 * px + 2 * py + pc]

        def copy(k, p, block, to, src=None):
            return pltpu.make_async_remote_copy(
                src_ref=blk(p, *block) if src is None else src, dst_ref=blk(p, *block),
                send_sem=send_sems.at[k * na + p], recv_sem=recv_sems.at[k * na + p], device_id=to,
                device_id_type=MESH)

        mine = [pltpu.make_async_copy(x_refs[p], blk(p, *me), local_sems.at[p]) for p in range(na)]
        for cp in mine:
            cp.start()
        first = [copy(0, p, me, sibling, src=x_refs[p]) for p in range(na)]
        first += [copy(1 + j, p, me, (*chip, c), src=x_refs[p]) for j, chip in enumerate(chips) for p in range(na)]
        for cp in first:
            cp.start()
        passed = []
        for j, chip in enumerate(chips):
            for p in range(na):
                copy(1 + j, p, (*chip, c), me).wait_recv()
                fwd = copy(4 + j, p, (*chip, c), sibling)
                fwd.start()
                passed.append(fwd)
        for p in range(na):
            copy(0, p, sibling, me).wait_recv()
        for j, chip in enumerate(chips):
            for p in range(na):
                copy(4 + j, p, (*chip, 1 - c), me).wait_recv()
        for cp in first + passed:
            cp.wait_send()
        for cp in mine:
            cp.wait()

    spec = pl.BlockSpec(memory_space=pltpu.VMEM) if vmem else ANY
    return pl.pallas_call(
        body, out_shape=[jax.ShapeDtypeStruct((N_DEV,) + a.shape, a.dtype) for a in xs],
        in_specs=[spec] * na, out_specs=[spec] * na,
        scratch_shapes=[pltpu.SemaphoreType.DMA((7 * na,)), pltpu.SemaphoreType.DMA((7 * na,)),
                        pltpu.SemaphoreType.DMA((na,))],
        name=name)(*xs)


def _swap_with_sibling(g4s):
    na = len(g4s)

    def body(*refs):
        g_refs, out_refs = refs[:na], refs[na:2 * na]
        send_sems, recv_sems = refs[2 * na:]
        x, y, c = _place()
        cps = [pltpu.make_async_remote_copy(src_ref=g_refs[p].at[:, 1 - c], dst_ref=out_refs[p],
                                            send_sem=send_sems.at[p], recv_sem=recv_sems.at[p],
                                            device_id=(x, y, 1 - c), device_id_type=MESH) for p in range(na)]
        for cp in cps:
            cp.start()
        for cp in cps:
            cp.wait()

    return pl.pallas_call(
        body, out_shape=[jax.ShapeDtypeStruct((g.shape[0],) + g.shape[2:], g.dtype) for g in g4s],
        in_specs=[ANY] * na, out_specs=[ANY] * na,
        scratch_shapes=[pltpu.SemaphoreType.DMA((na,)), pltpu.SemaphoreType.DMA((na,))], name="rs_sibling")(*g4s)


HBM = pl.BlockSpec(memory_space=pltpu.HBM)
SEM = pl.BlockSpec(memory_space=pltpu.SEMAPHORE)
EFFECT = pltpu.SideEffectType.DATAFLOW_SIDE_EFFECTING


def _in_hbm(a):
    return pltpu.with_memory_space_constraint(a, pltpu.HBM)


def _ici_copies(src_refs, land_refs, send_sems, recv_sems, gather):
    na = len(src_refs)
    x, y, c = _place()
    cps = []
    for j, (px, py) in enumerate(_other_chips(x, y)):
        for p in range(na):
            src = src_refs[p].at[4 * x + 2 * y + c] if gather else src_refs[p].at[2 * px + py]
            dst = src if gather else land_refs[p].at[j]
            cps.append(pltpu.make_async_remote_copy(
                src_ref=src, dst_ref=dst, send_sem=send_sems.at[j * na + p], recv_sem=recv_sems.at[j * na + p],
                device_id=(px, py, c), device_id_type=MESH))
    return cps


def _ici_start(srcs, land_shapes, deps, *, gather, name):
    na, nd = len(srcs), len(deps)
    lands = [lax.empty(s, a.dtype) for s, a in zip(land_shapes, srcs)]
    nb = na + len(lands)

    def body(*refs):
        src_refs, land_refs = refs[:na], refs[na:nb]
        send_sems, recv_sems = refs[nb + nd], refs[nb + nd + 1]
        token = refs[-1]
        for cp in _ici_copies(src_refs, land_refs, send_sems, recv_sems, gather):
            cp.start()
        token[...] = jnp.zeros_like(token)

    bufs = list(srcs) + lands
    out = pl.pallas_call(
        body, name=name,
        out_shape=(pltpu.SemaphoreType.DMA((3 * na,)), pltpu.SemaphoreType.DMA((3 * na,)),
                   *[pltpu.HBM(a.shape, a.dtype) for a in bufs], jax.ShapeDtypeStruct((8, LANE), F32)),
        in_specs=[HBM] * nb + [ANY] * nd,
        out_specs=(SEM, SEM, *([HBM] * nb), pl.BlockSpec(memory_space=pltpu.VMEM)),
        input_output_aliases={p: 2 + p for p in range(nb)},
        compiler_params=pltpu.CompilerParams(has_side_effects=EFFECT),
    )(*[_in_hbm(a) for a in bufs], *deps)
    return out[0], out[1], list(out[2:2 + na]), list(out[2 + na:2 + nb]), out[-1]


def _ici_wait(send_sems, recv_sems, srcs, lands, after, *, gather, name):
    na, nd = len(srcs), len(after)
    nb = na + len(lands)

    def body(*refs):
        src_refs, land_refs = refs[:na], refs[na:nb]
        s_sems, r_sems = refs[nb], refs[nb + 1]
        for cp in _ici_copies(src_refs, land_refs, s_sems, r_sems, gather):
            cp.wait_send()
            cp.wait_recv()

    bufs = list(srcs) + list(lands)
    out = pl.pallas_call(
        body, name=name, out_shape=tuple(pltpu.HBM(a.shape, a.dtype) for a in bufs),
        in_specs=[HBM] * nb + [SEM, SEM] + [ANY] * nd, out_specs=tuple([HBM] * nb),
        input_output_aliases={p: p for p in range(nb)},
        compiler_params=pltpu.CompilerParams(has_side_effects=EFFECT),
    )(*bufs, send_sems, recv_sems, *after)
    return list(out[:na]), list(out[na:])


def _gather_to_sibling(gs):
    na = len(gs)

    def body(*refs):
        out_refs = refs[na:2 * na]
        send_sems, recv_sems = refs[2 * na:]
        x, y, c = _place()
        sibling = (x, y, 1 - c)
        blocks = [(x, y)] + _other_chips(x, y)

        def copy(k, p, core):
            px, py = blocks[k]
            blk = out_refs[p].at[4 * px + 2 * py + core]
            return pltpu.make_async_remote_copy(src_ref=blk, dst_ref=blk, send_sem=send_sems.at[k * na + p],
                                                recv_sem=recv_sems.at[k * na + p], device_id=sibling,
                                                device_id_type=MESH)

        sends = [copy(k, p, c) for k in range(4) for p in range(na)]
        for cp in sends:
            cp.start()
        for k in range(4):
            for p in range(na):
                copy(k, p, 1 - c).wait_recv()
        for cp in sends:
            cp.wait_send()

    return pl.pallas_call(
        body, out_shape=[jax.ShapeDtypeStruct(a.shape, a.dtype) for a in gs],
        in_specs=[ANY] * na, out_specs=[ANY] * na, input_output_aliases={p: p for p in range(na)},
        scratch_shapes=[pltpu.SemaphoreType.DMA((4 * na,)), pltpu.SemaphoreType.DMA((4 * na,))],
        name="ag_sibling")(*gs)


HI_MASK = -65536


def _pack_pairs_xla(a):
    half = a.shape[-1] // 2
    bits = lax.bitcast_convert_type(a.astype(BF16), jnp.uint16).astype(jnp.uint32)
    return lax.bitcast_convert_type((bits[..., half:] << 16) | bits[..., :half], F32)


def _split_pairs(words):
    wv = lax.bitcast_convert_type(words, jnp.int32)
    return lax.bitcast_convert_type(wv << 16, F32), lax.bitcast_convert_type(wv & HI_MASK, F32)


def _join_pairs(lo, hi):
    lo_b = lax.bitcast_convert_type(lo.astype(BF16).astype(F32), jnp.int32)
    hi_b = lax.bitcast_convert_type(hi.astype(BF16).astype(F32), jnp.int32)
    return lax.bitcast_convert_type((hi_b & HI_MASK) | lax.shift_right_logical(lo_b, 16), F32)


def _unpack(xp, *, name, out_rows=None, tr=None, src_fn=None, zero_fn=None):
    r, hw = xp.shape
    out_rows = out_rows or r
    tr = tr or _tile(out_rows, 512, 16)
    src = src_fn or (lambda i: i)
    k = _blocks_per_step(out_rows // tr) if src_fn is not None else 1

    def body(*refs):
        o_ref = refs[k]
        for j in range(k):
            lo, hi = _split_pairs(refs[j][...])
            if zero_fn is not None:
                z = zero_fn(pl.program_id(0) * k + j)
                lo, hi = jnp.where(z, 0.0, lo), jnp.where(z, 0.0, hi)
            o_ref[j * tr:(j + 1) * tr, :hw] = lo.astype(BF16)
            o_ref[j * tr:(j + 1) * tr, hw:] = hi.astype(BF16)

    return pl.pallas_call(
        body, grid=(out_rows // (k * tr),),
        in_specs=[pl.BlockSpec((tr, hw), lambda i, j=j: (src(i * k + j), 0)) for j in range(k)],
        out_specs=pl.BlockSpec((k * tr, 2 * hw), lambda i: (i, 0)),
        out_shape=jax.ShapeDtypeStruct((out_rows, 2 * hw), BF16), name=name, compiler_params=_cp(1))(*([xp] * k))


def _blocks_per_step(n_blocks):
    return max(k for k in range(1, 17) if n_blocks % k == 0)


def _gather_rows(xp, *, name, out_rows, tr, src_fn):
    _, hw = xp.shape
    k = _blocks_per_step(out_rows // tr)

    def body(*refs):
        for j in range(k):
            refs[k][j * tr:(j + 1) * tr, :] = refs[j][...]

    return pl.pallas_call(
        body, grid=(out_rows // (k * tr),),
        in_specs=[pl.BlockSpec((tr, hw), lambda i, j=j: (src_fn(i * k + j), 0)) for j in range(k)],
        out_specs=pl.BlockSpec((k * tr, hw), lambda i: (i, 0)),
        out_shape=jax.ShapeDtypeStruct((out_rows, hw), xp.dtype), name=name, compiler_params=_cp(1))(*([xp] * k))


def _matmul_tn_packed(a, b, name):
    (t, m), (t2, c) = a.shape, b.shape
    assert t == t2
    hw = c // 2
    tm = _tile(m, 1088, LANE)
    tn = _tile(hw, 512, LANE)
    tk = _tile(t, 2176, LANE)
    nk, nj = t // tk, hw // tn

    def body(a_ref, bl_ref, bh_ref, o_ref, acc_lo, acc_hi):
        kk = pl.program_id(2)

        @pl.when(kk == 0)
        def _():
            acc_lo[...] = jnp.zeros_like(acc_lo)
            acc_hi[...] = jnp.zeros_like(acc_hi)

        av = a_ref[...].astype(BF16)
        acc_lo[...] += lax.dot_general(av, bl_ref[...].astype(BF16), TN, preferred_element_type=F32)
        acc_hi[...] += lax.dot_general(av, bh_ref[...].astype(BF16), TN, preferred_element_type=F32)

        @pl.when(kk == nk - 1)
        def _():
            o_ref[...] = _join_pairs(acc_lo[...], acc_hi[...])

    return pl.pallas_call(
        body, grid=(m // tm, nj, nk),
        in_specs=[pl.BlockSpec((tk, tm), lambda i, j, kk: (kk, i)), pl.BlockSpec((tk, tn), lambda i, j, kk: (kk, j)),
                  pl.BlockSpec((tk, tn), lambda i, j, kk: (kk, nj + j))],
        out_specs=pl.BlockSpec((tm, tn), lambda i, j, kk: (i, j)), out_shape=jax.ShapeDtypeStruct((m, hw), F32),
        scratch_shapes=[pltpu.VMEM((tm, tn), F32), pltpu.VMEM((tm, tn), F32)], name=name,
        compiler_params=_cp(3))(a, b, b)


def _add_sibling(g4, recv, cidx, name):
    nchip, _, r, n = g4.shape
    tr = _tile(r, 256, 8)

    def body(c_ref, a_ref, b_ref, o_ref):
        alo, ahi = _split_pairs(a_ref[...])
        blo, bhi = _split_pairs(b_ref[...])
        o_ref[...] = _join_pairs(alo + blo, ahi + bhi)

    grid_spec = pltpu.PrefetchScalarGridSpec(
        num_scalar_prefetch=1, grid=(nchip, r // tr),
        in_specs=[pl.BlockSpec((None, None, tr, n), lambda k, i, c: (k, c[0], i, 0)),
                  pl.BlockSpec((None, tr, n), lambda k, i, c: (k, i, 0))],
        out_specs=pl.BlockSpec((None, tr, n), lambda k, i, c: (k, i, 0)))
    return pl.pallas_call(body, grid_spec=grid_spec, out_shape=jax.ShapeDtypeStruct((nchip, r, n), F32),
                          name=name, compiler_params=_cp(2))(cidx, g4, recv)


def _add_chips(p4, recv3, chip_idx, name):
    _, r, n = p4.shape
    tr = _tile(r, 256, 8)

    def body(k_ref, a_ref, b_ref, o_ref):
        lo, hi = _split_pairs(a_ref[...])
        for j in range(3):
            blo, bhi = _split_pairs(b_ref[j])
            lo, hi = lo + blo, hi + bhi
        o_ref[:, :n] = lo
        o_ref[:, n:] = hi

    grid_spec = pltpu.PrefetchScalarGridSpec(
        num_scalar_prefetch=1, grid=(r // tr,),
        in_specs=[pl.BlockSpec((None, tr, n), lambda i, k: (k[0], i, 0)),
                  pl.BlockSpec((3, tr, n), lambda i, k: (0, i, 0))],
        out_specs=pl.BlockSpec((tr, 2 * n), lambda i, k: (i, 0)))
    return pl.pallas_call(body, grid_spec=grid_spec, out_shape=jax.ShapeDtypeStruct((r, 2 * n), F32),
                          name=name, compiler_params=_cp(1))(chip_idx, p4, recv3)


def _all_reduce_small(xs):
    r, n = xs.shape

    def body(x_ref, o_ref, buf, send_sems, recv_sems):
        x, y, c = _place()
        me = 4 * x + 2 * y + c
        buf[me] = x_ref[...]
        peers = []
        for k in range(1, N_DEV):
            px = (1 - x) if (k >> 2) & 1 else x
            py = (1 - y) if (k >> 1) & 1 else y
            pc = (1 - c) if k & 1 else c
            peers.append((px, py, pc))
        sends = [pltpu.make_async_remote_copy(src_ref=x_ref, dst_ref=buf.at[me], send_sem=send_sems.at[k],
                                              recv_sem=recv_sems.at[k], device_id=peer, device_id_type=MESH)
                 for k, peer in enumerate(peers)]
        for cp in sends:
            cp.start()
        for k, (px, py, pc) in enumerate(peers):
            pltpu.make_async_remote_copy(src_ref=x_ref, dst_ref=buf.at[4 * px + 2 * py + pc],
                                         send_sem=send_sems.at[k], recv_sem=recv_sems.at[k],
                                         device_id=(px, py, pc), device_id_type=MESH).wait_recv()
        for cp in sends:
            cp.wait_send()
        acc = buf[0]
        for dv in range(1, N_DEV):
            acc = acc + buf[dv]
        o_ref[...] = acc

    vm = pl.BlockSpec(memory_space=pltpu.VMEM)
    return pl.pallas_call(
        body, out_shape=jax.ShapeDtypeStruct((r, n), F32), in_specs=[vm], out_specs=vm,
        scratch_shapes=[pltpu.VMEM((N_DEV, r, n), F32), pltpu.SemaphoreType.DMA((7,)), pltpu.SemaphoreType.DMA((7,))],
        name="all_reduce_small", compiler_params=pltpu.CompilerParams(vmem_limit_bytes=VMEM_LIMIT))(xs)


def _pack_rows(arrs, quantum):
    parts, sizes = [], []
    for a in arrs:
        f = a.reshape(-1)
        pad = (-f.shape[0]) % quantum
        if pad:
            f = jnp.pad(f, (0, pad))
        parts.append(f)
        sizes.append(f.shape[0])
    return jnp.concatenate(parts).reshape(-1, LANE), sizes


def _unpack_rows(buf, sizes, shapes):
    flat = buf.reshape(-1)
    out, off = [], 0
    for sz, sh in zip(sizes, shapes):
        n = math.prod(sh)
        out.append(flat[off:off + n].reshape(sh))
        off += sz
    return out


def kernel(x, meta_tokens, mix_norm_g, w_in, conv_dw, conv_b, conv_ln_g, conv_ln_b, w_conv_out, gqa_q_norm_g, gqa_k_norm_g, w_gqa_out, mla_q_norm_g, w_mla_uq, mla_kv_norm_g, w_mla_ukv, w_mla_out, gate_b, w_out, ffn_norm_g, w_ffn_gate, w_ffn_up, w_ffn_down, final_norm_g, loss_target, m_meta_tokens, m_mix_norm_g, m_w_in, m_conv_dw, m_conv_b, m_conv_ln_g, m_conv_ln_b, m_w_conv_out, m_gqa_q_norm_g, m_gqa_k_norm_g, m_w_gqa_out, m_mla_q_norm_g, m_w_mla_uq, m_mla_kv_norm_g, m_w_mla_ukv, m_w_mla_out, m_gate_b, m_w_out, m_ffn_norm_g, m_w_ffn_gate, m_w_ffn_up, m_w_ffn_down, m_final_norm_g, v_meta_tokens, v_mix_norm_g, v_w_in, v_conv_dw, v_conv_b, v_conv_ln_g, v_conv_ln_b, v_w_conv_out, v_gqa_q_norm_g, v_gqa_k_norm_g, v_w_gqa_out, v_mla_q_norm_g, v_w_mla_uq, v_mla_kv_norm_g, v_w_mla_ukv, v_w_mla_out, v_gate_b, v_w_out, v_ffn_norm_g, v_w_ffn_gate, v_w_ffn_up, v_w_ffn_down, v_final_norm_g):
    w = dict(meta_tokens=meta_tokens, mix_norm_g=mix_norm_g, w_in=w_in, conv_dw=conv_dw, conv_b=conv_b,
             conv_ln_g=conv_ln_g, conv_ln_b=conv_ln_b, w_conv_out=w_conv_out, gqa_q_norm_g=gqa_q_norm_g,
             gqa_k_norm_g=gqa_k_norm_g, w_gqa_out=w_gqa_out, mla_q_norm_g=mla_q_norm_g, w_mla_uq=w_mla_uq,
             mla_kv_norm_g=mla_kv_norm_g, w_mla_ukv=w_mla_ukv, w_mla_out=w_mla_out, gate_b=gate_b, w_out=w_out,
             ffn_norm_g=ffn_norm_g, w_ffn_gate=w_ffn_gate, w_ffn_up=w_ffn_up, w_ffn_down=w_ffn_down,
             final_norm_g=final_norm_g)
    mom1 = dict(zip(W_NAMES, (m_meta_tokens, m_mix_norm_g, m_w_in, m_conv_dw, m_conv_b, m_conv_ln_g, m_conv_ln_b, m_w_conv_out, m_gqa_q_norm_g, m_gqa_k_norm_g, m_w_gqa_out, m_mla_q_norm_g, m_w_mla_uq, m_mla_kv_norm_g, m_w_mla_ukv, m_w_mla_out, m_gate_b, m_w_out, m_ffn_norm_g, m_w_ffn_gate, m_w_ffn_up, m_w_ffn_down, m_final_norm_g)))
    mom2 = dict(zip(W_NAMES, (v_meta_tokens, v_mix_norm_g, v_w_in, v_conv_dw, v_conv_b, v_conv_ln_g, v_conv_ln_b, v_w_conv_out, v_gqa_q_norm_g, v_gqa_k_norm_g, v_w_gqa_out, v_mla_q_norm_g, v_w_mla_uq, v_mla_kv_norm_g, v_w_mla_ukv, v_w_mla_out, v_gate_b, v_w_out, v_ffn_norm_g, v_w_ffn_gate, v_w_ffn_up, v_w_ffn_down, v_final_norm_g)))

    nb, seq, d = x.shape
    depth = w_in.shape[0]
    c_conv = conv_b.shape[1]
    kw = conv_dw.shape[1]
    hd = HEAD_DIM
    hq = w_gqa_out.shape[1] // hd
    nh = w_mla_out.shape[1] // hd
    qr, kvr = mla_q_norm_g.shape[1], mla_kv_norm_g.shape[1]
    d_in = w_in.shape[2] * N_DEV
    hkv = (d_in - 2 * c_conv - hq * hd - qr - kvr - MLA_ROPE_DIM - N_BRANCH * d) // (2 * hd)
    l_valid = seq + N_META
    lp = -(-(l_valid + kw // 2) // LANE) * LANE
    m_tok = nb * lp
    mla_qk = hd + MLA_ROPE_DIM
    gqa_scale, mla_scale = 1.0 / math.sqrt(hd), 1.0 / math.sqrt(mla_qk)

    off_gate = 0
    off_conv = N_BRANCH * d
    off_q = off_conv + 2 * c_conv
    off_cq = off_q + (hq + 2 * hkv) * hd
    off_ckv = off_cq + qr
    off_kpe = off_ckv + kvr
    n_proj = off_kpe + LANE

    x_id, y_id, c_id = _place()
    me = 4 * x_id + 2 * y_id + c_id

    tm_wide = _tile(lp, 272, 16)
    tm_mid = _tile(lp, 544, 16)

    sm_buf, sm_sizes = _pack_rows([meta_tokens, conv_dw], 8 * LANE)
    sm_all = _all_gather([sm_buf], True, "ag_small")[0]
    sm_all = sm_all.reshape(N_DEV, -1)
    o0 = 0
    meta_all = sm_all[:, o0:o0 + meta_tokens.size].reshape((N_DEV,) + meta_tokens.shape)
    meta_full = jnp.transpose(meta_all, (1, 0, 2)).reshape(N_META, d)
    o0 = sm_sizes[0]
    dw_all = sm_all[:, o0:o0 + conv_dw.size].reshape((N_DEV, depth, kw, c_conv // N_DEV))
    dw_full = jnp.transpose(dw_all, (1, 2, 0, 3)).reshape(depth, kw, c_conv)

    n_head_cols = d_in - N_BRANCH * d - MLA_ROPE_DIM
    rb = MLA_ROPE_DIM
    nb_gate, nb_head = N_BRANCH * d // rb, n_head_cols // rb
    assert n_head_cols % rb == 0 and (N_BRANCH * d) % rb == 0 and n_proj == (nb_gate + nb_head + 2) * rb

    def in_src(i):
        return jnp.where(i < nb_gate, nb_head + 1 + i,
                         jnp.where(i < nb_gate + nb_head, i - nb_gate, jnp.where(i == nb_gate + nb_head, nb_head, 0)))

    def in_dst(i):
        return jnp.where(i < nb_head, nb_gate + i, jnp.where(i == nb_head, nb_gate + nb_head, i - nb_head - 1))

    def uq_src(i):
        j = i - 2 * nh
        return jnp.where(i < 2 * nh, 3 * (i // 2) + i % 2, 3 * (j // 2) + 2)

    def uq_dst(i):
        return jnp.where(i % 3 < 2, 2 * (i // 3) + i % 3, 2 * nh + 2 * (i // 3))

    ag_pending, tokens = {}, []
    for i in range(depth):
        for gi, names in enumerate(GROUPS):
            shards = [_pack_pairs_xla(w[n][i] if n in ROW_SHARDED else w[n][i].T) for n in names]
            bufs = [lax.dynamic_update_slice(lax.empty((N_DEV,) + a.shape, F32), a[None], (me, 0, 0))
                    for a in shards]
            *pend, tok = _ici_start(bufs, [], tokens[-1:], gather=True, name="ag_ici_start_%d_%d" % (i, gi))
            ag_pending[i, gi] = pend
            tokens.append(tok)

    def all_gather_group(i, gi, after):
        bufs, _ = _ici_wait(*ag_pending[i, gi], after, gather=True, name="ag_ici_wait_%d_%d" % (i, gi))
        gath = _gather_to_sibling(bufs)
        wl = {}
        for n, g in zip(GROUPS[gi], gath):
            pk = g.reshape(-1, g.shape[-1])
            if n == 'w_in':
                wl[n] = _unpack(pk, name="unpack_w_in", out_rows=n_proj, tr=rb, src_fn=in_src,
                                zero_fn=lambda i: i == nb_gate + nb_head + 1)
            elif n == 'w_mla_uq':
                wl[n] = _unpack(pk, name="unpack_w_mla_uq", out_rows=2 * nh * hd, tr=rb, src_fn=uq_src,
                                zero_fn=lambda i: (i >= 2 * nh) & ((i - 2 * nh) % 2 == 1))
            elif n == 'w_mla_ukv':
                wl[n] = _unpack(pk, name="unpack_w_mla_ukv", out_rows=2 * nh * hd, tr=hd,
                                src_fn=lambda i: jnp.where(i < nh, 2 * i, 2 * (i - nh) + 1))
            else:
                wl[n] = _unpack(pk, name="unpack_" + n)
        return wl

    full = [None] * depth

    cos_g, sin_g = _rope_tables(lp, l_valid, hd // 4)
    cos_m, sin_m = _rope_tables(lp, l_valid, MLA_ROPE_DIM // 4)

    def vec2(a):
        return a.reshape(1, -1)

    meta_b = jnp.broadcast_to(meta_full[None], (nb, N_META, d))
    h = jnp.concatenate([meta_b, x, jnp.zeros((nb, lp - l_valid, d), F32)], axis=1).reshape(m_tok, d)
    tgt = jnp.pad(loss_target, ((0, 0), (N_META, lp - l_valid), (0, 0))).reshape(m_tok, d)

    saved = []
    for i in range(depth):
        s = {}
        s['h_in'] = h
        u = _rms_fwd(h, d, 0, vec2(mix_norm_g[i]), tm_wide, "mix_norm_fwd")
        wl = full[i] = all_gather_group(i, 0, tokens[-1:] if i == 0 else [h])
        proj = _matmul(u, wl['w_in'], mode="nt", out_dtype=F32, name="mm_in")
        cc = _conv_fwd(proj, dw_full[i], vec2(conv_b[i]), c_conv=c_conv, off_a=off_conv, nb=nb, lp=lp, l_valid=l_valid)
        sc = _ln_silu_fwd(cc, vec2(conv_ln_g[i]), vec2(conv_ln_b[i]), tm_mid)
        ya = _matmul(sc, wl['w_conv_out'], mode="nt", out_dtype=F32, name="mm_conv_out")
        qh, kh, vh = _gqa_prep_fwd(proj, vec2(gqa_q_norm_g[i]), vec2(gqa_k_norm_g[i]), cos_g, sin_g,
                                   off_q=off_q, hq=hq, hkv=hkv, nb=nb, lp=lp, tm=tm_mid, q_scale=gqa_scale)
        ob = _attn_fwd(qh, kh, vh, l_valid=l_valid, name="gqa_attn_fwd")
        yb = _matmul(ob, wl['w_gqa_out'], mode="nt", out_dtype=F32, name="mm_gqa_out")
        cqn = _rms_fwd(proj, qr, _cb(off_cq, qr), vec2(mla_q_norm_g[i]), tm_mid, "mla_q_norm_fwd")
        ckvn = _rms_fwd(proj, kvr, _cb(off_ckv, kvr), vec2(mla_kv_norm_g[i]), tm_mid, "mla_kv_norm_fwd")
        qm = _matmul(cqn, wl['w_mla_uq'], mode="nt", out_dtype=F32, name="mm_mla_uq")
        kvm = _matmul(ckvn, wl['w_mla_ukv'], mode="nt", out_dtype=F32, name="mm_mla_ukv")
        qc, kc, vc = _mla_prep_fwd(qm, kvm, proj, cos_m, sin_m, off_kpe=off_kpe, nh=nh, nb=nb, lp=lp, tm=tm_mid,
                                   q_scale=mla_scale)
        oc = _attn_fwd(qc, kc, vc, l_valid=l_valid, name="mla_attn_fwd")
        yc = _matmul(oc, wl['w_mla_out'], mode="nt", out_dtype=F32, name="mm_mla_out")
        merged = _gate_fwd(proj, vec2(gate_b[i]), ya, yb, yc, d=d, tm=tm_wide)
        h2 = _matmul(merged, wl['w_out'], mode="nn", out_dtype=F32, name="mm_out", residual=h)
        vn = _rms_fwd(h2, d, 0, vec2(ffn_norm_g[i]), tm_wide, "ffn_norm_fwd")
        wl.update(all_gather_group(i, 1, [h2]))
        gt, up, act = _ffn_up_fwd(vn, wl['w_ffn_gate'], wl['w_ffn_up'])
        h = _matmul(act, wl['w_ffn_down'], mode="nn", out_dtype=F32, name="mm_ffn_down", residual=h2)
        s.update(u=u, proj=proj, cc=cc, sc=sc, ya=ya, qh=qh, kh=kh, vh=vh, ob=ob, yb=yb, cqn=cqn, ckvn=ckvn,
                 qc=qc, kc=kc, vc=vc, oc=oc, yc=yc, merged=merged, h2=h2, vn=vn, gt=gt, up=up, act=act)
        saved.append(s)

    dh, dg_final, loss_part = _loss_head(h, vec2(final_norm_g), tgt, lp=lp, l_valid=l_valid, tm=tm_wide)

    gsmall = {n: [None] * depth for n in SMALL if n != 'final_norm_g'}
    gdw = [None] * depth
    gbig = [{} for _ in range(depth)]
    cidx = jnp.reshape(c_id, (1,)).astype(jnp.int32)
    chip_idx = jnp.reshape(2 * x_id + y_id, (1,)).astype(jnp.int32)

    def reduce_scatter_start(gl, i, gi):
        names = GROUPS[gi]
        g4s = [gl[n].reshape(N_DEV // 2, 2, -1, gl[n].shape[-1]) for n in names]
        from_sibling = _swap_with_sibling(g4s)
        p4s = [_add_sibling(g, r, cidx, "rs_add_sibling_" + n) for n, g, r in zip(names, g4s, from_sibling)]
        *pend, tok = _ici_start(p4s, [(3,) + p.shape[1:] for p in p4s], [], gather=False,
                                name="rs_ici_start_%d_%d" % (i, gi))
        return pend, tok

    def reduce_scatter_finish(pend, after, i, gi):
        p4s, from_chips = _ici_wait(*pend, after, gather=False, name="rs_ici_wait_%d_%d" % (i, gi))
        out = {}
        for n, p, r in zip(GROUPS[gi], p4s, from_chips):
            g = _add_chips(p, r, chip_idx, "rs_add_chips_" + n)
            out[n] = g if n in ROW_SHARDED else g.T
        return out

    rs_pending = [None, None]
    rs_token = []
    for i in reversed(range(depth)):
        s = saved[i]
        proj = s['proj']
        wl = full[i]
        gl = {}
        dgt, dup = _ffn_down_bwd(dh, wl['w_ffn_down'], s['gt'], s['up'], rs_token)
        gl['w_ffn_down'] = _matmul_tn_packed(s['act'], dh, "mm_ffn_down_dw")
        dvn = _matmul(dgt, wl['w_ffn_gate'], mode="nn", out_dtype=F32, name="mm_ffn_gate_dx")
        dvn = _matmul(dup, wl['w_ffn_up'], mode="nn", out_dtype=F32, name="mm_ffn_up_dx", residual=dvn)
        gl['w_ffn_gate'] = _matmul_tn_packed(dgt, s['vn'], "mm_ffn_gate_dw")
        gl['w_ffn_up'] = _matmul_tn_packed(dup, s['vn'], "mm_ffn_up_dw")
        dh2, gsmall['ffn_norm_g'][i] = _rms_bwd(s['h2'], d, 0, vec2(ffn_norm_g[i]), dvn, tm_wide, F32,
                                                "ffn_norm_bwd", add=dh)
        started, tok = reduce_scatter_start(gl, i, 1)
        if rs_pending[1] is not None:
            gbig[i + 1].update(reduce_scatter_finish(rs_pending[1], [dh2, tok], i + 1, 1))
        rs_pending[1] = started
        dmg = _matmul(dh2, wl['w_out'], mode="nt", out_dtype=F32, name="mm_out_dx", deps=[tok])
        gl['w_out'] = _matmul_tn_packed(s['merged'], dh2, "mm_out_dw")
        dya, dyb, dyc, dg0, dg1, dg2, db0, db1, db2 = _gate_bwd(dmg, proj, vec2(gate_b[i]), s['ya'], s['yb'],
                                                                s['yc'], d=d, tm=tm_wide)
        gsmall['gate_b'][i] = jnp.concatenate([db0, db1, db2], axis=1)
        dsc = _matmul(dya, wl['w_conv_out'], mode="nn", out_dtype=F32, name="mm_conv_out_dx")
        gl['w_conv_out'] = _matmul_tn_packed(dya, s['sc'], "mm_conv_out_dw")
        dcc, gsmall['conv_ln_g'][i], gsmall['conv_ln_b'][i] = _ln_silu_bwd(
            s['cc'], vec2(conv_ln_g[i]), vec2(conv_ln_b[i]), dsc, tm_mid)
        da, dgc, gdw[i], gsmall['conv_b'][i] = _conv_bwd(proj, dcc, dw_full[i], c_conv=c_conv, off_a=off_conv,
                                                         nb=nb, lp=lp, l_valid=l_valid)
        dob = _matmul(dyb, wl['w_gqa_out'], mode="nn", out_dtype=BF16, name="mm_gqa_out_dx")
        gl['w_gqa_out'] = _matmul_tn_packed(dyb, s['ob'], "mm_gqa_out_dw")
        dqh, dkh, dvh = _attn_bwd(s['qh'], s['kh'], s['vh'], dob, l_valid=l_valid, name="gqa_attn_bwd")
        dq, dkv, gsmall['gqa_q_norm_g'][i], gsmall['gqa_k_norm_g'][i] = _gqa_prep_bwd(
            proj, vec2(gqa_q_norm_g[i]), vec2(gqa_k_norm_g[i]), cos_g, sin_g, dqh, dkh, dvh,
            off_q=off_q, hq=hq, hkv=hkv, nb=nb, lp=lp, tm=tm_mid, q_scale=gqa_scale)
        doc = _matmul(dyc, wl['w_mla_out'], mode="nn", out_dtype=BF16, name="mm_mla_out_dx")
        gl['w_mla_out'] = _matmul_tn_packed(dyc, s['oc'], "mm_mla_out_dw")
        dqc, dkc, dvc = _attn_bwd(s['qc'], s['kc'], s['vc'], doc, l_valid=l_valid, name="mla_attn_bwd")
        dqm, dkvm, dkpe = _mla_prep_bwd(dqc, dkc, dvc, cos_m, sin_m, nh=nh, nb=nb, lp=lp, tm=tm_mid,
                                        q_scale=mla_scale)
        dcqn = _matmul(dqm, wl['w_mla_uq'], mode="nn", out_dtype=F32, name="mm_mla_uq_dx")
        guq = _matmul_tn_packed(dqm, s['cqn'], "mm_mla_uq_dw")
        dckvn = _matmul(dkvm, wl['w_mla_ukv'], mode="nn", out_dtype=F32, name="mm_mla_ukv_dx")
        gukv = _matmul_tn_packed(dkvm, s['ckvn'], "mm_mla_ukv_dw")
        dcq, gsmall['mla_q_norm_g'][i] = _rms_bwd(proj, qr, _cb(off_cq, qr), vec2(mla_q_norm_g[i]), dcqn, tm_mid,
                                                  BF16, "mla_q_norm_bwd")
        dckv, gsmall['mla_kv_norm_g'][i] = _rms_bwd(proj, kvr, _cb(off_ckv, kvr), vec2(mla_kv_norm_g[i]), dckvn,
                                                    tm_mid, BF16, "mla_kv_norm_bwd")
        gl['w_mla_uq'] = _gather_rows(guq, name="perm_mla_uq_dw", out_rows=nh * mla_qk, tr=rb, src_fn=uq_dst)
        gl['w_mla_ukv'] = _gather_rows(gukv, name="perm_mla_ukv_dw", out_rows=2 * nh * hd, tr=hd,
                                       src_fn=lambda i: i // 2 + nh * (i % 2))
        dproj = jnp.concatenate([dg0, dg1, dg2, da, dgc, dq, dkv, dcq, dckv, dkpe], axis=1)
        du = _matmul(dproj, wl['w_in'], mode="nn", out_dtype=F32, name="mm_in_dx")
        gin = _matmul_tn_packed(dproj, s['u'], "mm_in_dw")
        gl['w_in'] = _gather_rows(gin, name="perm_in_dw", out_rows=d_in, tr=rb, src_fn=in_dst)
        dh, gsmall['mix_norm_g'][i] = _rms_bwd(s['h_in'], d, 0, vec2(mix_norm_g[i]), du, tm_wide, F32,
                                               "mix_norm_bwd", add=dh2)
        started, tok = reduce_scatter_start(gl, i, 0)
        if rs_pending[0] is not None:
            gbig[i + 1].update(reduce_scatter_finish(rs_pending[0], [dh, tok], i + 1, 0))
        rs_pending[0], rs_token = started, [tok]

    def adamw_big(names):
        for n in names:
            sh = w[n].shape
            two_d = (sh[0] * sh[1], sh[2])
            grads[n] = jnp.stack([gbig[i][n] for i in range(depth)])
            dl, mn, vn_ = _adamw(w[n].reshape(two_d), grads[n].reshape(two_d), mom1[n].reshape(two_d),
                                 mom2[n].reshape(two_d), "adamw_" + n)
            delta[n], new_m[n], new_v[n] = dl.reshape(sh), mn.reshape(sh), vn_.reshape(sh)

    grads, delta, new_m, new_v = {}, {}, {}, {}
    gbig[0].update(reduce_scatter_finish(rs_pending[1], rs_token, 0, 1))
    adamw_big(GROUPS[1])
    gbig[0].update(reduce_scatter_finish(rs_pending[0], [delta[GROUPS[1][-1]]], 0, 0))
    adamw_big(GROUPS[0])

    dh0 = dh.reshape(nb, lp, d)
    grad_x = dh0[:, N_META:l_valid]
    gmeta_full = jnp.sum(dh0[:, :N_META], axis=0)

    small_list = [jnp.stack(gsmall[n]).reshape(w[n].shape) for n in SMALL if n != 'final_norm_g']
    small_list += [dg_final.reshape(final_norm_g.shape), gmeta_full, jnp.stack(gdw), loss_part]
    ar_buf, ar_sizes = _pack_rows(small_list, 8 * LANE)
    ar = _all_reduce_small(ar_buf)
    ar_shapes = [w[n].shape for n in SMALL] + [(N_META, d), (depth, kw, c_conv), (1, LANE)]
    ar_out = _unpack_rows(ar, ar_sizes, ar_shapes)
    for n, g in zip(SMALL, ar_out):
        grads[n] = g
    dcol = d // N_DEV
    grads['meta_tokens'] = lax.dynamic_slice(ar_out[len(SMALL)], (0, me * dcol), (N_META, dcol))
    ccol = c_conv // N_DEV
    grads['conv_dw'] = lax.dynamic_slice(ar_out[len(SMALL) + 1], (0, 0, me * ccol),
                                         (depth, kw, ccol)).reshape(conv_dw.shape)
    loss = ar_out[len(SMALL) + 2][0, 0]

    rest = [n for n in W_NAMES if n not in BIG]
    pw, psz = _pack_rows([w[n] for n in rest], 8 * LANE)
    pg, _ = _pack_rows([grads[n] for n in rest], 8 * LANE)
    pm, _ = _pack_rows([mom1[n] for n in rest], 8 * LANE)
    pv, _ = _pack_rows([mom2[n] for n in rest], 8 * LANE)
    dl, mn, vn_ = _adamw(pw, pg, pm, pv, "adamw_small")
    shapes = [w[n].shape for n in rest]
    for n, a, b, c in zip(rest, _unpack_rows(dl, psz, shapes), _unpack_rows(mn, psz, shapes),
                          _unpack_rows(vn_, psz, shapes)):
        delta[n], new_m[n], new_v[n] = a, b, c

    return (loss, grad_x, *[grads[n] for n in W_NAMES], *[delta[n] for n in W_NAMES],
            *[new_m[n] for n in W_NAMES], *[new_v[n] for n in W_NAMES])
```

```python
import functools
import math

import jax
import jax.numpy as jnp
from jax import lax
from jax.experimental import pallas as pl
from jax.experimental.pallas import tpu as pltpu

F32 = jnp.float32
BF16 = jnp.bfloat16

N_META = 16
GRID_W = 64
ROPE_THETA = 10000.0
NORM_EPS = 1e-6
HEAD_DIM = 128
MLA_ROPE_DIM = 64
N_BRANCH = 3
N_DEV = 8
LANE = 128
VMEM_LIMIT = 56 * 1024 * 1024

ADAM_LR = 0.001
ADAM_B1 = 0.9
ADAM_B2 = 0.999
ADAM_EPS = 1e-08
ADAM_WD = 0.01
ADAM_STEP = 10

W_NAMES = ['meta_tokens', 'mix_norm_g', 'w_in', 'conv_dw', 'conv_b', 'conv_ln_g', 'conv_ln_b', 'w_conv_out',
           'gqa_q_norm_g', 'gqa_k_norm_g', 'w_gqa_out', 'mla_q_norm_g', 'w_mla_uq', 'mla_kv_norm_g', 'w_mla_ukv',
           'w_mla_out', 'gate_b', 'w_out', 'ffn_norm_g', 'w_ffn_gate', 'w_ffn_up', 'w_ffn_down', 'final_norm_g']
BIG = ['w_in', 'w_conv_out', 'w_gqa_out', 'w_mla_uq', 'w_mla_ukv', 'w_mla_out', 'w_out', 'w_ffn_gate', 'w_ffn_up',
       'w_ffn_down']
ROW_SHARDED = ('w_out', 'w_ffn_down')
GROUPS = (('w_in', 'w_conv_out', 'w_gqa_out', 'w_mla_uq', 'w_mla_ukv', 'w_mla_out', 'w_out'),
          ('w_ffn_gate', 'w_ffn_up', 'w_ffn_down'))
SMALL = ['mix_norm_g', 'conv_b', 'conv_ln_g', 'conv_ln_b', 'gqa_q_norm_g', 'gqa_k_norm_g', 'mla_q_norm_g',
         'mla_kv_norm_g', 'gate_b', 'ffn_norm_g', 'final_norm_g']

NT = (((1,), (1,)), ((), ()))
TN = (((0,), (0,)), ((), ()))
NN = (((1,), (0,)), ((), ()))


def _tile(n, target, mult):
    best = None
    for t in range(mult, min(n, target) + 1, mult):
        if n % t == 0:
            best = t
    assert best is not None, (n, target, mult)
    return best


def _cp(n):
    return pltpu.CompilerParams(dimension_semantics=("arbitrary",) * n, vmem_limit_bytes=VMEM_LIMIT)


def _sig(x):
    return jax.nn.sigmoid(x)


def _cb(off, w):
    assert off % w == 0, (off, w)
    return off // w


def _matmul(a, b, *, mode, out_dtype, name, residual=None, deps=()):
    if mode == "nn":
        (m, k), (k2, n) = a.shape, b.shape
    elif mode == "nt":
        (m, k), (n, k2) = a.shape, b.shape
    else:
        (k, m), (k2, n) = a.shape, b.shape
    assert k == k2, (a.shape, b.shape, mode)
    tm = _tile(m, 1088, 128 if mode == "tn" else 16)
    tn = _tile(n, 1024, 128)
    tk = _tile(k, 2176, 128)
    nk = k // tk
    dims = {"nn": NN, "nt": NT, "tn": TN}[mode]
    a_spec = (pl.BlockSpec((tk, tm), lambda i, j, kk: (kk, i)) if mode == "tn"
              else pl.BlockSpec((tm, tk), lambda i, j, kk: (i, kk)))
    b_spec = (pl.BlockSpec((tn, tk), lambda i, j, kk: (j, kk)) if mode == "nt"
              else pl.BlockSpec((tk, tn), lambda i, j, kk: (kk, j)))
    o_spec = pl.BlockSpec((tm, tn), lambda i, j, kk: (i, j))
    has_res = residual is not None

    nd = len(deps)

    def body(*refs):
        refs = refs[:len(refs) - 2 - nd] + refs[len(refs) - 2:]
        if has_res:
            a_ref, b_ref, r_ref, o_ref, acc = refs
        else:
            a_ref, b_ref, o_ref, acc = refs
        kk = pl.program_id(2)

        @pl.when(kk == 0)
        def _():
            acc[...] = jnp.zeros_like(acc)

        acc[...] += lax.dot_general(a_ref[...].astype(BF16), b_ref[...].astype(BF16), dims,
                                    preferred_element_type=F32)

        @pl.when(kk == nk - 1)
        def _():
            r = acc[...]
            if has_res:
                r = r + r_ref[...]
            o_ref[...] = r.astype(o_ref.dtype)

    ins = [a, b] + ([residual] if has_res else []) + list(deps)
    in_specs = [a_spec, b_spec] + ([o_spec] if has_res else []) + [pl.BlockSpec(memory_space=pl.ANY)] * nd
    return pl.pallas_call(
        body, grid=(m // tm, n // tn, nk), in_specs=in_specs, out_specs=o_spec,
        out_shape=jax.ShapeDtypeStruct((m, n), out_dtype), scratch_shapes=[pltpu.VMEM((tm, tn), F32)],
        name=name, compiler_params=_cp(3))(*ins)


def _rms_fwd(x, w, cb, g, tm, name):
    m = x.shape[0]

    def body(x_ref, g_ref, o_ref):
        xv = x_ref[...]
        r = lax.rsqrt(jnp.mean(xv * xv, axis=-1, keepdims=True) + NORM_EPS)
        o_ref[...] = (xv * r * g_ref[...]).astype(o_ref.dtype)

    return pl.pallas_call(
        body, grid=(m // tm,),
        in_specs=[pl.BlockSpec((tm, w), lambda i: (i, cb)), pl.BlockSpec((1, w), lambda i: (0, 0))],
        out_specs=pl.BlockSpec((tm, w), lambda i: (i, 0)), out_shape=jax.ShapeDtypeStruct((m, w), BF16),
        name=name, compiler_params=_cp(1))(x, g)


def _rms_bwd(x, w, cb, g, dy, tm, out_dtype, name, add=None):
    m = x.shape[0]
    has_add = add is not None

    def body(*refs):
        if has_add:
            x_ref, g_ref, dy_ref, add_ref, dx_ref, dg_ref = refs
        else:
            x_ref, g_ref, dy_ref, dx_ref, dg_ref = refs
        xv = x_ref[...]
        dyv = dy_ref[...].astype(F32)
        r = lax.rsqrt(jnp.mean(xv * xv, axis=-1, keepdims=True) + NORM_EPS)
        t = dyv * g_ref[...]
        dx = r * t - xv * (r * r * r) * jnp.mean(t * xv, axis=-1, keepdims=True)
        if has_add:
            dx = dx + add_ref[...]
        dx_ref[...] = dx.astype(dx_ref.dtype)

        @pl.when(pl.program_id(0) == 0)
        def _():
            dg_ref[...] = jnp.zeros_like(dg_ref)

        dg_ref[...] += jnp.sum(dyv * xv * r, axis=0, keepdims=True)

    row = pl.BlockSpec((tm, w), lambda i: (i, 0))
    vec = pl.BlockSpec((1, w), lambda i: (0, 0))
    ins = [x, g, dy] + ([add] if has_add else [])
    in_specs = [pl.BlockSpec((tm, w), lambda i: (i, cb)), vec, row] + ([row] if has_add else [])
    return pl.pallas_call(
        body, grid=(m // tm,), in_specs=in_specs, out_specs=(row, vec),
        out_shape=(jax.ShapeDtypeStruct((m, w), out_dtype), jax.ShapeDtypeStruct((1, w), F32)),
        name=name, compiler_params=_cp(1))(*ins)


def _ln_silu_fwd(c, lg, lb, tm):
    m, w = c.shape

    def body(c_ref, g_ref, b_ref, o_ref):
        cv = c_ref[...]
        xc = cv - jnp.mean(cv, axis=-1, keepdims=True)
        r = lax.rsqrt(jnp.mean(xc * xc, axis=-1, keepdims=True) + NORM_EPS)
        yl = xc * r * g_ref[...] + b_ref[...]
        o_ref[...] = (yl * _sig(yl)).astype(o_ref.dtype)

    row = pl.BlockSpec((tm, w), lambda i: (i, 0))
    vec = pl.BlockSpec((1, w), lambda i: (0, 0))
    return pl.pallas_call(body, grid=(m // tm,), in_specs=[row, vec, vec], out_specs=row,
                          out_shape=jax.ShapeDtypeStruct((m, w), BF16), name="ln_silu_fwd",
                          compiler_params=_cp(1))(c, lg, lb)


def _ln_silu_bwd(c, lg, lb, ds, tm):
    m, w = c.shape

    def body(c_ref, g_ref, b_ref, ds_ref, dc_ref, dg_ref, db_ref):
        cv = c_ref[...]
        xc = cv - jnp.mean(cv, axis=-1, keepdims=True)
        r = lax.rsqrt(jnp.mean(xc * xc, axis=-1, keepdims=True) + NORM_EPS)
        nv = xc * r
        yl = nv * g_ref[...] + b_ref[...]
        sg = _sig(yl)
        dyl = ds_ref[...] * (sg * (1.0 + yl * (1.0 - sg)))
        dn = dyl * g_ref[...]
        dc = r * (dn - jnp.mean(dn, axis=-1, keepdims=True) - nv * jnp.mean(dn * nv, axis=-1, keepdims=True))
        dc_ref[...] = dc

        @pl.when(pl.program_id(0) == 0)
        def _():
            dg_ref[...] = jnp.zeros_like(dg_ref)
            db_ref[...] = jnp.zeros_like(db_ref)

        dg_ref[...] += jnp.sum(dyl * nv, axis=0, keepdims=True)
        db_ref[...] += jnp.sum(dyl, axis=0, keepdims=True)

    row = pl.BlockSpec((tm, w), lambda i: (i, 0))
    vec = pl.BlockSpec((1, w), lambda i: (0, 0))
    return pl.pallas_call(
        body, grid=(m // tm,), in_specs=[row, vec, vec, row], out_specs=(row, vec, vec),
        out_shape=(jax.ShapeDtypeStruct((m, w), F32), jax.ShapeDtypeStruct((1, w), F32),
                   jax.ShapeDtypeStruct((1, w), F32)),
        name="ln_silu_bwd", compiler_params=_cp(1))(c, lg, lb, ds)


CONV_MARGIN = 16
CONV_ROWS = 128


def _conv_fwd(proj, dw, bias, *, c_conv, off_a, nb, lp, l_valid):
    kw = dw.shape[0]
    half = kw // 2
    cw = LANE
    mg = CONV_MARGIN
    assert half <= mg - 1 and lp % CONV_ROWS == 0

    def body(a_ref, g_ref, w_ref, b_ref, c_ref, zp):
        t = lax.broadcasted_iota(jnp.int32, (lp, cw), 0)
        z = jnp.where(t < l_valid, a_ref[...] * _sig(g_ref[...]), 0.0)
        zp[0:mg, :] = jnp.zeros((mg, cw), F32)
        zp[mg + lp:mg + lp + mg, :] = jnp.zeros((mg, cw), F32)
        zp[mg:mg + lp, :] = z
        for r0 in range(0, lp, CONV_ROWS):
            acc = jnp.zeros((CONV_ROWS, cw), F32) + b_ref[...]
            for k in range(kw):
                s0 = mg + r0 + k - half
                acc = acc + w_ref[k:k + 1, :] * zp[s0:s0 + CONV_ROWS, :]
            c_ref[r0:r0 + CONV_ROWS, :] = acc

    ja, jg = _cb(off_a, cw), _cb(off_a + c_conv, cw)
    return pl.pallas_call(
        body, grid=(nb, c_conv // cw),
        in_specs=[pl.BlockSpec((lp, cw), lambda b, j: (b, ja + j)), pl.BlockSpec((lp, cw), lambda b, j: (b, jg + j)),
                  pl.BlockSpec((kw, cw), lambda b, j: (0, j)), pl.BlockSpec((1, cw), lambda b, j: (0, j))],
        out_specs=pl.BlockSpec((lp, cw), lambda b, j: (b, j)),
        out_shape=jax.ShapeDtypeStruct((nb * lp, c_conv), F32),
        scratch_shapes=[pltpu.VMEM((lp + 2 * mg, cw), F32)], name="conv_fwd", compiler_params=_cp(2))(proj, proj, dw, bias)


def _conv_bwd(proj, dc, dw, *, c_conv, off_a, nb, lp, l_valid):
    kw = dw.shape[0]
    half = kw // 2
    cw = LANE
    mg = CONV_MARGIN

    def body(a_ref, g_ref, dc_ref, w_ref, da_ref, dg_ref, ddw_ref, dcb_ref, zp, dcp):
        t = lax.broadcasted_iota(jnp.int32, (lp, cw), 0)
        z = jnp.where(t < l_valid, a_ref[...] * _sig(g_ref[...]), 0.0)
        for buf in (zp, dcp):
            buf[0:mg, :] = jnp.zeros((mg, cw), F32)
            buf[mg + lp:mg + lp + mg, :] = jnp.zeros((mg, cw), F32)
        zp[mg:mg + lp, :] = z
        dcv = dc_ref[...]
        dcp[mg:mg + lp, :] = dcv

        @pl.when(pl.program_id(1) == 0)
        def _():
            ddw_ref[...] = jnp.zeros_like(ddw_ref)
            dcb_ref[...] = jnp.zeros_like(dcb_ref)

        dcb_ref[...] += jnp.sum(dcv, axis=0, keepdims=True)
        for r0 in range(0, lp, CONV_ROWS):
            acc = jnp.zeros((CONV_ROWS, cw), F32)
            for k in range(kw):
                s0 = mg + r0 - k + half
                acc = acc + w_ref[k:k + 1, :] * dcp[s0:s0 + CONV_ROWS, :]
            tc = lax.broadcasted_iota(jnp.int32, (CONV_ROWS, cw), 0) + r0
            dz = jnp.where(tc < l_valid, acc, 0.0)
            sg = _sig(g_ref[r0:r0 + CONV_ROWS, :])
            da_ref[r0:r0 + CONV_ROWS, :] = (dz * sg).astype(da_ref.dtype)
            dg_ref[r0:r0 + CONV_ROWS, :] = (dz * a_ref[r0:r0 + CONV_ROWS, :] * sg * (1.0 - sg)).astype(dg_ref.dtype)
        for k in range(kw):
            acc = jnp.zeros((CONV_ROWS, cw), F32)
            for r0 in range(0, lp, CONV_ROWS):
                s0 = mg + r0 + k - half
                acc = acc + dc_ref[r0:r0 + CONV_ROWS, :] * zp[s0:s0 + CONV_ROWS, :]
            ddw_ref[k:k + 1, :] += jnp.sum(acc, axis=0, keepdims=True)

    ja, jg = _cb(off_a, cw), _cb(off_a + c_conv, cw)
    seq = pl.BlockSpec((lp, cw), lambda j, b: (b, j))
    return pl.pallas_call(
        body, grid=(c_conv // cw, nb),
        in_specs=[pl.BlockSpec((lp, cw), lambda j, b: (b, ja + j)), pl.BlockSpec((lp, cw), lambda j, b: (b, jg + j)),
                  seq, pl.BlockSpec((kw, cw), lambda j, b: (0, j))],
        out_specs=(seq, seq, pl.BlockSpec((kw, cw), lambda j, b: (0, j)), pl.BlockSpec((1, cw), lambda j, b: (0, j))),
        out_shape=(jax.ShapeDtypeStruct((nb * lp, c_conv), BF16), jax.ShapeDtypeStruct((nb * lp, c_conv), BF16),
                   jax.ShapeDtypeStruct((kw, c_conv), F32), jax.ShapeDtypeStruct((1, c_conv), F32)),
        scratch_shapes=[pltpu.VMEM((lp + 2 * mg, cw), F32), pltpu.VMEM((lp + 2 * mg, cw), F32)],
        name="conv_bwd", compiler_params=_cp(2))(proj, proj, dc, dw)


def _swap_halves(x, half):
    fwd = pltpu.roll(x, LANE - half, axis=1)
    bwd = pltpu.roll(x, half, axis=1)
    lane = lax.broadcasted_iota(jnp.int32, x.shape, 1)
    return jnp.where((lane & (2 * half - 1)) < half, fwd, bwd)


def _rope(x, cos, sin, half):
    return x * cos + _swap_halves(x, half) * sin


def _rope_t(dy, cos, sin, half):
    return dy * cos + _swap_halves(dy * sin, half)


def _rope_tables(lp, l_valid, half):
    t = jnp.arange(lp)
    n = jnp.clip(t - N_META, 0, None)
    real = (t >= N_META) & (t < l_valid)
    row = jnp.where(real, n // GRID_W, 0).astype(F32)
    col = jnp.where(real, n % GRID_W, 0).astype(F32)
    inv = ROPE_THETA ** (-jnp.arange(half, dtype=F32) / half)
    ar, ac = row[:, None] * inv[None, :], col[:, None] * inv[None, :]
    cos = jnp.concatenate([jnp.cos(ar), jnp.cos(ar), jnp.cos(ac), jnp.cos(ac)], axis=1)
    sin = jnp.concatenate([-jnp.sin(ar), jnp.sin(ar), -jnp.sin(ac), jnp.sin(ac)], axis=1)
    padw = LANE - 4 * half
    if padw:
        cos = jnp.pad(cos, ((0, 0), (0, padw)))
        sin = jnp.pad(sin, ((0, 0), (0, padw)))
    return cos.astype(F32), sin.astype(F32)


def _gqa_prep_fwd(proj, qg, kg, cos, sin, *, off_q, hq, hkv, nb, lp, tm, q_scale):
    hd = HEAD_DIM
    half = hd // 4
    wq, wkv = hq * hd, 2 * hkv * hd
    nt = lp // tm

    def body(q_ref, kv_ref, qg_ref, kg_ref, cos_ref, sin_ref, qo_ref, ko_ref, vo_ref):
        cosv, sinv = cos_ref[...], sin_ref[...]

        def norm_rope(xh, g):
            r = lax.rsqrt(jnp.mean(xh * xh, axis=-1, keepdims=True) + NORM_EPS)
            return _rope(xh * r * g, cosv, sinv, half)

        for h in range(hq):
            qo_ref[h] = (norm_rope(q_ref[:, h * hd:(h + 1) * hd], qg_ref[...]) * q_scale).astype(BF16)
        for h in range(hkv):
            ko_ref[h] = norm_rope(kv_ref[:, h * hd:(h + 1) * hd], kg_ref[...]).astype(BF16)
            vo_ref[h] = kv_ref[:, (hkv + h) * hd:(hkv + h + 1) * hd].astype(BF16)

    jq, jkv = _cb(off_q, wq), _cb(off_q + wq, wkv)
    tab = pl.BlockSpec((tm, LANE), lambda b, i: (i, 0))
    vec = pl.BlockSpec((1, hd), lambda b, i: (0, 0))

    def heads(h):
        return pl.BlockSpec((None, h, tm, hd), lambda b, i: (b, 0, i, 0))

    return pl.pallas_call(
        body, grid=(nb, nt),
        in_specs=[pl.BlockSpec((tm, wq), lambda b, i: (b * nt + i, jq)),
                  pl.BlockSpec((tm, wkv), lambda b, i: (b * nt + i, jkv)), vec, vec, tab, tab],
        out_specs=(heads(hq), heads(hkv), heads(hkv)),
        out_shape=(jax.ShapeDtypeStruct((nb, hq, lp, hd), BF16), jax.ShapeDtypeStruct((nb, hkv, lp, hd), BF16),
                   jax.ShapeDtypeStruct((nb, hkv, lp, hd), BF16)),
        name="gqa_prep_fwd", compiler_params=_cp(2))(proj, proj, qg, kg, cos, sin)


def _gqa_prep_bwd(proj, qg, kg, cos, sin, dqh, dkh, dvh, *, off_q, hq, hkv, nb, lp, tm, q_scale):
    hd = HEAD_DIM
    half = hd // 4
    wq, wkv = hq * hd, 2 * hkv * hd
    nt = lp // tm

    def body(q_ref, kv_ref, qg_ref, kg_ref, cos_ref, sin_ref, dqh_ref, dkh_ref, dvh_ref,
             dq_ref, dkv_ref, dqg_ref, dkg_ref):
        cosv, sinv = cos_ref[...], sin_ref[...]

        def back(xh, g, dyh):
            dn = _rope_t(dyh, cosv, sinv, half)
            r = lax.rsqrt(jnp.mean(xh * xh, axis=-1, keepdims=True) + NORM_EPS)
            t = dn * g
            dx = r * t - xh * (r * r * r) * jnp.mean(t * xh, axis=-1, keepdims=True)
            return dx, jnp.sum(dn * xh * r, axis=0, keepdims=True)

        first = (pl.program_id(0) == 0) & (pl.program_id(1) == 0)

        @pl.when(first)
        def _():
            dqg_ref[...] = jnp.zeros_like(dqg_ref)
            dkg_ref[...] = jnp.zeros_like(dkg_ref)

        gq = jnp.zeros((1, hd), F32)
        for h in range(hq):
            dx, dg = back(q_ref[:, h * hd:(h + 1) * hd], qg_ref[...], dqh_ref[h] * q_scale)
            dq_ref[:, h * hd:(h + 1) * hd] = dx.astype(dq_ref.dtype)
            gq = gq + dg
        dqg_ref[...] += gq
        gk = jnp.zeros((1, hd), F32)
        for h in range(hkv):
            dx, dg = back(kv_ref[:, h * hd:(h + 1) * hd], kg_ref[...], dkh_ref[h])
            dkv_ref[:, h * hd:(h + 1) * hd] = dx.astype(dkv_ref.dtype)
            dkv_ref[:, (hkv + h) * hd:(hkv + h + 1) * hd] = dvh_ref[h].astype(dkv_ref.dtype)
            gk = gk + dg
        dkg_ref[...] += gk

    jq, jkv = _cb(off_q, wq), _cb(off_q + wq, wkv)
    tab = pl.BlockSpec((tm, LANE), lambda b, i: (i, 0))
    vec = pl.BlockSpec((1, hd), lambda b, i: (0, 0))

    def heads(h):
        return pl.BlockSpec((None, h, tm, hd), lambda b, i: (b, 0, i, 0))

    return pl.pallas_call(
        body, grid=(nb, nt),
        in_specs=[pl.BlockSpec((tm, wq), lambda b, i: (b * nt + i, jq)),
                  pl.BlockSpec((tm, wkv), lambda b, i: (b * nt + i, jkv)), vec, vec, tab, tab,
                  heads(hq), heads(hkv), heads(hkv)],
        out_specs=(pl.BlockSpec((tm, wq), lambda b, i: (b * nt + i, 0)),
                   pl.BlockSpec((tm, wkv), lambda b, i: (b * nt + i, 0)), vec, vec),
        out_shape=(jax.ShapeDtypeStruct((nb * lp, wq), BF16), jax.ShapeDtypeStruct((nb * lp, wkv), BF16),
                   jax.ShapeDtypeStruct((1, hd), F32), jax.ShapeDtypeStruct((1, hd), F32)),
        name="gqa_prep_bwd", compiler_params=_cp(2))(proj, proj, qg, kg, cos, sin, dqh, dkh, dvh)


def _mla_prep_fwd(qm, kvm, proj, cos, sin, *, off_kpe, nh, nb, lp, tm, q_scale):
    hd = HEAD_DIM
    half = MLA_ROPE_DIM // 4
    nt = lp // tm
    wh = nh * hd

    def body(qm_ref, kvm_ref, kpe_ref, cos_ref, sin_ref, qo_ref, ko_ref, vo_ref):
        cosv, sinv = cos_ref[...], sin_ref[...]
        kpe = _rope(kpe_ref[...], cosv, sinv, half).astype(BF16)
        for h in range(nh):
            qo_ref[h, :, 0:hd] = (qm_ref[:, h * hd:(h + 1) * hd] * q_scale).astype(BF16)
            qo_ref[h, :, hd:2 * hd] = (_rope(qm_ref[:, wh + h * hd:wh + (h + 1) * hd], cosv, sinv, half)
                                       * q_scale).astype(BF16)
            ko_ref[h, :, 0:hd] = kvm_ref[:, h * hd:(h + 1) * hd].astype(BF16)
            ko_ref[h, :, hd:2 * hd] = kpe
            vo_ref[h] = kvm_ref[:, wh + h * hd:wh + (h + 1) * hd].astype(BF16)

    tab = pl.BlockSpec((tm, LANE), lambda b, i: (i, 0))
    wide = pl.BlockSpec((tm, 2 * wh), lambda b, i: (b * nt + i, 0))
    jk = _cb(off_kpe, LANE)
    return pl.pallas_call(
        body, grid=(nb, nt),
        in_specs=[wide, wide, pl.BlockSpec((tm, LANE), lambda b, i: (b * nt + i, jk)), tab, tab],
        out_specs=(pl.BlockSpec((None, nh, tm, 2 * hd), lambda b, i: (b, 0, i, 0)),
                   pl.BlockSpec((None, nh, tm, 2 * hd), lambda b, i: (b, 0, i, 0)),
                   pl.BlockSpec((None, nh, tm, hd), lambda b, i: (b, 0, i, 0))),
        out_shape=(jax.ShapeDtypeStruct((nb, nh, lp, 2 * hd), BF16), jax.ShapeDtypeStruct((nb, nh, lp, 2 * hd), BF16),
                   jax.ShapeDtypeStruct((nb, nh, lp, hd), BF16)),
        name="mla_prep_fwd", compiler_params=_cp(2))(qm, kvm, proj, cos, sin)


def _mla_prep_bwd(dqc, dkc, dvc, cos, sin, *, nh, nb, lp, tm, q_scale):
    hd = HEAD_DIM
    half = MLA_ROPE_DIM // 4
    nt = lp // tm
    wh = nh * hd

    def body(dq_ref, dk_ref, dv_ref, cos_ref, sin_ref, dqm_ref, dkvm_ref, dkpe_ref):
        cosv, sinv = cos_ref[...], sin_ref[...]
        acc = jnp.zeros((tm, hd), F32)
        for h in range(nh):
            dqm_ref[:, h * hd:(h + 1) * hd] = (dq_ref[h, :, 0:hd] * q_scale).astype(BF16)
            dqm_ref[:, wh + h * hd:wh + (h + 1) * hd] = (_rope_t(dq_ref[h, :, hd:2 * hd], cosv, sinv, half)
                                                         * q_scale).astype(BF16)
            dkvm_ref[:, h * hd:(h + 1) * hd] = dk_ref[h, :, 0:hd].astype(BF16)
            dkvm_ref[:, wh + h * hd:wh + (h + 1) * hd] = dv_ref[h].astype(BF16)
            acc = acc + dk_ref[h, :, hd:2 * hd]
        dkpe_ref[...] = _rope_t(acc, cosv, sinv, half).astype(BF16)

    tab = pl.BlockSpec((tm, LANE), lambda b, i: (i, 0))
    wide = pl.BlockSpec((tm, 2 * wh), lambda b, i: (b * nt + i, 0))
    return pl.pallas_call(
        body, grid=(nb, nt),
        in_specs=[pl.BlockSpec((None, nh, tm, 2 * hd), lambda b, i: (b, 0, i, 0)),
                  pl.BlockSpec((None, nh, tm, 2 * hd), lambda b, i: (b, 0, i, 0)),
                  pl.BlockSpec((None, nh, tm, hd), lambda b, i: (b, 0, i, 0)), tab, tab],
        out_specs=(wide, wide, pl.BlockSpec((tm, LANE), lambda b, i: (b * nt + i, 0))),
        out_shape=(jax.ShapeDtypeStruct((nb * lp, 2 * wh), BF16), jax.ShapeDtypeStruct((nb * lp, 2 * wh), BF16),
                   jax.ShapeDtypeStruct((nb * lp, LANE), BF16)),
        name="mla_prep_bwd", compiler_params=_cp(2))(dqc, dkc, dvc, cos, sin)


def _softmax_parts(s, lp, l_valid):
    tw = -(-(lp - l_valid) // LANE) * LANE
    main, tail = s[:, :lp - tw], s[:, lp - tw:]
    col = lax.broadcasted_iota(jnp.int32, tail.shape, 1) + (lp - tw)
    tail = jnp.where(col < l_valid, tail, -1e30)
    m = jnp.maximum(jnp.max(main, axis=-1, keepdims=True), jnp.max(tail, axis=-1, keepdims=True))
    pm, pt = jnp.exp(main - m), jnp.exp(tail - m)
    den = jnp.sum(pm, axis=-1, keepdims=True) + jnp.sum(pt, axis=-1, keepdims=True)
    return pm, pt, den, tw


def _attn_fwd(q, k, v, *, l_valid, name):
    nb, hq, lp, dk = q.shape
    hkv, dv = k.shape[1], v.shape[3]
    grp = hq // hkv
    tq = _tile(lp, 1088, 64)
    sub = tq // 4
    nq = lp // tq

    def body(q_ref, k_ref, v_ref, o_ref, p_s):
        for r0 in range(0, tq, sub):
            rows = slice(r0, r0 + sub)
            s = lax.dot_general(q_ref[rows, :], k_ref[...], NT, preferred_element_type=F32)
            pm, pt, den, tw = _softmax_parts(s, lp, l_valid)
            p_s[rows, :lp - tw] = pm.astype(BF16)
            p_s[rows, lp - tw:] = pt.astype(BF16)
            o = lax.dot_general(p_s[rows, :], v_ref[...], NN, preferred_element_type=F32)
            o_ref[rows, :] = (o / den).astype(o_ref.dtype)

    return pl.pallas_call(
        body, grid=(nb, hq, nq),
        in_specs=[pl.BlockSpec((None, None, tq, dk), lambda b, h, i: (b, h, i, 0)),
                  pl.BlockSpec((None, None, lp, dk), lambda b, h, i: (b, h // grp, 0, 0)),
                  pl.BlockSpec((None, None, lp, dv), lambda b, h, i: (b, h // grp, 0, 0))],
        out_specs=pl.BlockSpec((tq, dv), lambda b, h, i: (b * nq + i, h)),
        out_shape=jax.ShapeDtypeStruct((nb * lp, hq * dv), BF16), scratch_shapes=[pltpu.VMEM((tq, lp), BF16)],
        name=name, compiler_params=_cp(3))(q, k, v)


def _attn_bwd(q, k, v, do, *, l_valid, name):
    nb, hq, lp, dk = q.shape
    hkv, dv = k.shape[1], v.shape[3]
    grp = hq // hkv
    tq = _tile(lp, 1088, 64)
    sub = tq // 4
    nq = lp // tq

    def body(q_ref, k_ref, v_ref, do_ref, dq_ref, dk_ref, dv_ref, p_s, ds_s):
        for r0 in range(0, tq, sub):
            rows = slice(r0, r0 + sub)
            s = lax.dot_general(q_ref[rows, :], k_ref[...], NT, preferred_element_type=F32)
            pm, pt, den, tw = _softmax_parts(s, lp, l_valid)
            inv = 1.0 / den
            pm, pt = pm * inv, pt * inv
            dp = lax.dot_general(do_ref[rows, :], v_ref[...], NT, preferred_element_type=F32)
            dpm, dpt = dp[:, :lp - tw], dp[:, lp - tw:]
            dd = jnp.sum(pm * dpm, axis=-1, keepdims=True) + jnp.sum(pt * dpt, axis=-1, keepdims=True)
            p_s[rows, :lp - tw] = pm.astype(BF16)
            p_s[rows, lp - tw:] = pt.astype(BF16)
            ds_s[rows, :lp - tw] = (pm * (dpm - dd)).astype(BF16)
            ds_s[rows, lp - tw:] = (pt * (dpt - dd)).astype(BF16)
            dq_ref[rows, :] = lax.dot_general(ds_s[rows, :], k_ref[...], NN, preferred_element_type=F32)

        @pl.when((pl.program_id(2) == 0) & (pl.program_id(3) == 0))
        def _():
            dk_ref[...] = jnp.zeros_like(dk_ref)
            dv_ref[...] = jnp.zeros_like(dv_ref)

        dk_ref[...] += lax.dot_general(ds_s[...], q_ref[...], TN, preferred_element_type=F32)
        dv_ref[...] += lax.dot_general(p_s[...], do_ref[...], TN, preferred_element_type=F32)

    return pl.pallas_call(
        body, grid=(nb, hkv, grp, nq), scratch_shapes=[pltpu.VMEM((tq, lp), BF16), pltpu.VMEM((tq, lp), BF16)],
        in_specs=[pl.BlockSpec((None, None, tq, dk), lambda b, h, g, i: (b, h * grp + g, i, 0)),
                  pl.BlockSpec((None, None, lp, dk), lambda b, h, g, i: (b, h, 0, 0)),
                  pl.BlockSpec((None, None, lp, dv), lambda b, h, g, i: (b, h, 0, 0)),
                  pl.BlockSpec((tq, dv), lambda b, h, g, i: (b * nq + i, h * grp + g))],
        out_specs=(pl.BlockSpec((None, None, tq, dk), lambda b, h, g, i: (b, h * grp + g, i, 0)),
                   pl.BlockSpec((None, None, lp, dk), lambda b, h, g, i: (b, h, 0, 0)),
                   pl.BlockSpec((None, None, lp, dv), lambda b, h, g, i: (b, h, 0, 0))),
        out_shape=(jax.ShapeDtypeStruct((nb, hq, lp, dk), F32), jax.ShapeDtypeStruct((nb, hkv, lp, dk), F32),
                   jax.ShapeDtypeStruct((nb, hkv, lp, dv), F32)),
        name=name, compiler_params=_cp(4))(q, k, v, do)


def _gate_fwd(proj, gb, ya, yb, yc, *, d, tm):
    m = ya.shape[0]
    cw = _tile(d, 512, LANE)
    nj = d // cw

    def body(g0, g1, g2, b0, b1, b2, ya_ref, yb_ref, yc_ref, o_ref):
        o = (_sig(g0[...] + b0[...]) * ya_ref[...] + _sig(g1[...] + b1[...]) * yb_ref[...]
             + _sig(g2[...] + b2[...]) * yc_ref[...])
        o_ref[...] = o.astype(o_ref.dtype)

    def gl(br):
        return pl.BlockSpec((tm, cw), lambda j, i: (i, br * nj + j))

    def gbs(br):
        return pl.BlockSpec((1, cw), lambda j, i: (0, br * nj + j))

    blk = pl.BlockSpec((tm, cw), lambda j, i: (i, j))
    return pl.pallas_call(
        body, grid=(nj, m // tm), in_specs=[gl(0), gl(1), gl(2), gbs(0), gbs(1), gbs(2), blk, blk, blk],
        out_specs=blk, out_shape=jax.ShapeDtypeStruct((m, d), BF16), name="gate_fwd",
        compiler_params=_cp(2))(proj, proj, proj, gb, gb, gb, ya, yb, yc)


def _gate_bwd(dm, proj, gb, ya, yb, yc, *, d, tm):
    m = ya.shape[0]
    cw = _tile(d, 512, LANE)
    nj = d // cw

    def body(dm_ref, g0, g1, g2, b0, b1, b2, ya_ref, yb_ref, yc_ref,
             dya, dyb, dyc, dg0, dg1, dg2, db0, db1, db2):
        dmv = dm_ref[...]

        @pl.when(pl.program_id(1) == 0)
        def _():
            for r in (db0, db1, db2):
                r[...] = jnp.zeros_like(r)

        for g, b, y, dy, dgl, db in ((g0, b0, ya_ref, dya, dg0, db0), (g1, b1, yb_ref, dyb, dg1, db1),
                                     (g2, b2, yc_ref, dyc, dg2, db2)):
            sg = _sig(g[...] + b[...])
            dy[...] = (sg * dmv).astype(dy.dtype)
            dl = dmv * y[...] * sg * (1.0 - sg)
            dgl[...] = dl.astype(dgl.dtype)
            db[...] += jnp.sum(dl, axis=0, keepdims=True)

    def gl(br):
        return pl.BlockSpec((tm, cw), lambda j, i: (i, br * nj + j))

    def gbs(br):
        return pl.BlockSpec((1, cw), lambda j, i: (0, br * nj + j))

    blk = pl.BlockSpec((tm, cw), lambda j, i: (i, j))
    vec = pl.BlockSpec((1, cw), lambda j, i: (0, j))
    act = jax.ShapeDtypeStruct((m, d), BF16)
    vsh = jax.ShapeDtypeStruct((1, d), F32)
    return pl.pallas_call(
        body, grid=(nj, m // tm), in_specs=[blk, gl(0), gl(1), gl(2), gbs(0), gbs(1), gbs(2), blk, blk, blk],
        out_specs=(blk,) * 6 + (vec,) * 3, out_shape=(act,) * 6 + (vsh,) * 3, name="gate_bwd",
        compiler_params=_cp(2))(dm, proj, proj, proj, gb, gb, gb, ya, yb, yc)


def _ffn_up_fwd(vn, wg_t, wu_t):
    m, d = vn.shape
    f = wg_t.shape[0]
    tm = _tile(m, 1088, 16)
    tn = _tile(f, 512, LANE)

    def body(v_ref, wg_ref, wu_ref, g_ref, u_ref, a_ref):
        v = v_ref[...]
        g = lax.dot_general(v, wg_ref[...], NT, preferred_element_type=F32)
        u = lax.dot_general(v, wu_ref[...], NT, preferred_element_type=F32)
        g_ref[...] = g
        u_ref[...] = u
        a_ref[...] = (g * _sig(g) * u).astype(a_ref.dtype)

    row = pl.BlockSpec((tm, d), lambda i, j: (i, 0))
    wsp = pl.BlockSpec((tn, d), lambda i, j: (j, 0))
    blk = pl.BlockSpec((tm, tn), lambda i, j: (i, j))
    return pl.pallas_call(
        body, grid=(m // tm, f // tn), in_specs=[row, wsp, wsp], out_specs=(blk, blk, blk),
        out_shape=(jax.ShapeDtypeStruct((m, f), F32), jax.ShapeDtypeStruct((m, f), F32),
                   jax.ShapeDtypeStruct((m, f), BF16)),
        name="ffn_up_fwd", compiler_params=_cp(2))(vn, wg_t, wu_t)


def _ffn_down_bwd(dh, wd, gt, up, deps):
    m, d = dh.shape
    f = wd.shape[0]
    tm = _tile(m, 1088, 16)
    tn = _tile(f, 512, LANE)
    nd = len(deps)

    def body(*refs):
        dh_ref, wd_ref, g_ref, u_ref = refs[:4]
        dg_ref, du_ref = refs[4 + nd:]
        da = lax.dot_general(dh_ref[...].astype(BF16), wd_ref[...], NT, preferred_element_type=F32)
        g = g_ref[...]
        sg = _sig(g)
        dg_ref[...] = (da * u_ref[...] * sg * (1.0 + g * (1.0 - sg))).astype(dg_ref.dtype)
        du_ref[...] = (da * g * sg).astype(du_ref.dtype)

    row = pl.BlockSpec((tm, d), lambda i, j: (i, 0))
    wsp = pl.BlockSpec((tn, d), lambda i, j: (j, 0))
    blk = pl.BlockSpec((tm, tn), lambda i, j: (i, j))
    sh = jax.ShapeDtypeStruct((m, f), BF16)
    return pl.pallas_call(
        body, grid=(m // tm, f // tn), in_specs=[row, wsp, blk, blk] + [pl.BlockSpec(memory_space=pl.ANY)] * nd,
        out_specs=(blk, blk), out_shape=(sh, sh), name="ffn_down_bwd", compiler_params=_cp(2))(dh, wd, gt, up, *deps)


def _loss_head(h, gf, tgt, *, lp, l_valid, tm):
    m, d = h.shape
    nt = lp // tm

    def body(h_ref, g_ref, t_ref, dh_ref, dg_ref, loss_ref):
        i = pl.program_id(0)
        xv = h_ref[...]
        r = lax.rsqrt(jnp.mean(xv * xv, axis=-1, keepdims=True) + NORM_EPS)
        y = xv * r * g_ref[...]
        t = lax.broadcasted_iota(jnp.int32, (tm, d), 0) + (i % nt) * tm
        err = jnp.where((t >= N_META) & (t < l_valid), y - t_ref[...], 0.0)
        dy = err * (1.0 / d)
        tg = dy * g_ref[...]
        dh_ref[...] = r * tg - xv * (r * r * r) * jnp.mean(tg * xv, axis=-1, keepdims=True)

        @pl.when(i == 0)
        def _():
            dg_ref[...] = jnp.zeros_like(dg_ref)
            loss_ref[...] = jnp.zeros_like(loss_ref)

        dg_ref[...] += jnp.sum(dy * xv * r, axis=0, keepdims=True)
        sq = jnp.sum(jnp.sum(err * err, axis=-1, keepdims=True), axis=0, keepdims=True)
        loss_ref[...] += jnp.zeros((1, LANE), F32) + sq * (0.5 / d)

    row = pl.BlockSpec((tm, d), lambda i: (i, 0))
    vec = pl.BlockSpec((1, d), lambda i: (0, 0))
    return pl.pallas_call(
        body, grid=(m // tm,), in_specs=[row, vec, row],
        out_specs=(row, vec, pl.BlockSpec((1, LANE), lambda i: (0, 0))),
        out_shape=(jax.ShapeDtypeStruct((m, d), F32), jax.ShapeDtypeStruct((1, d), F32),
                   jax.ShapeDtypeStruct((1, LANE), F32)),
        name="loss_head", compiler_params=_cp(1))(h, gf, tgt)


def _adamw(w, g, m, v, name):
    rows, cols = w.shape
    tr = rows
    if rows % 8 == 0:
        tr = _tile(rows, max(8, (1 << 18) // cols // 8 * 8), 8)
    c1 = 1.0 / (1.0 - ADAM_B1 ** ADAM_STEP)
    c2 = 1.0 / (1.0 - ADAM_B2 ** ADAM_STEP)

    def body(w_ref, g_ref, m_ref, v_ref, d_ref, mo_ref, vo_ref):
        gv = g_ref[...]
        mn = ADAM_B1 * m_ref[...] + (1.0 - ADAM_B1) * gv
        vn = ADAM_B2 * v_ref[...] + (1.0 - ADAM_B2) * (gv * gv)
        mo_ref[...] = mn
        vo_ref[...] = vn
        d_ref[...] = -ADAM_LR * ((mn * c1) / (jnp.sqrt(vn * c2) + ADAM_EPS) + ADAM_WD * w_ref[...])

    blk = pl.BlockSpec((tr, cols), lambda i: (i, 0))
    sh = jax.ShapeDtypeStruct((rows, cols), F32)
    return pl.pallas_call(body, grid=(rows // tr,), in_specs=[blk] * 4, out_specs=(blk,) * 3, out_shape=(sh,) * 3,
                          name=name, compiler_params=_cp(1))(w, g, m, v)


MESH = pl.DeviceIdType.MESH
ANY = pl.BlockSpec(memory_space=pl.ANY)


def _place():
    return lax.axis_index("x"), lax.axis_index("y"), lax.axis_index("c")


def _other_chips(x, y):
    return [(1 - x, y), (x, 1 - y), (1 - x, 1 - y)]


def _all_gather(xs, vmem, name):
    na = len(xs)

    def body(*refs):
        x_refs, out_refs = refs[:na], refs[na:2 * na]
        send_sems, recv_sems, local_sems = refs[2 * na:]
        x, y, c = _place()
        me, sibling = (x, y, c), (x, y, 1 - c)
        chips = _other_chips(x, y)

        def blk(p, px, py, pc):
            return out_refs[p].at[4 * px + 2 * py + pc]

        def copy(k, p, block, to, src=None):
            return pltpu.make_async_remote_copy(
                src_ref=blk(p, *block) if src is None else src, dst_ref=blk(p, *block),
                send_sem=send_sems.at[k * na + p], recv_sem=recv_sems.at[k * na + p], device_id=to,
                device_id_type=MESH)

        mine = [pltpu.make_async_copy(x_refs[p], blk(p, *me), local_sems.at[p]) for p in range(na)]
        for cp in mine:
            cp.start()
        first = [copy(0, p, me, sibling, src=x_refs[p]) for p in range(na)]
        first += [copy(1 + j, p, me, (*chip, c), src=x_refs[p]) for j, chip in enumerate(chips) for p in range(na)]
        for cp in first:
            cp.start()
        passed = []
        for j, chip in enumerate(chips):
            for p in range(na):
                copy(1 + j, p, (*chip, c), me).wait_recv()
                fwd = copy(4 + j, p, (*chip, c), sibling)
                fwd.start()
                passed.append(fwd)
        for p in range(na):
            copy(0, p, sibling, me).wait_recv()
        for j, chip in enumerate(chips):
            for p in range(na):
                copy(4 + j, p, (*chip, 1 - c), me).wait_recv()
        for cp in first + passed:
            cp.wait_send()
        for cp in mine:
            cp.wait()

    spec = pl.BlockSpec(memory_space=pltpu.VMEM) if vmem else ANY
    return pl.pallas_call(
        body, out_shape=[jax.ShapeDtypeStruct((N_DEV,) + a.shape, a.dtype) for a in xs],
        in_specs=[spec] * na, out_specs=[spec] * na,
        scratch_shapes=[pltpu.SemaphoreType.DMA((7 * na,)), pltpu.SemaphoreType.DMA((7 * na,)),
                        pltpu.SemaphoreType.DMA((na,))],
        name=name)(*xs)


def _swap_with_sibling(g4s):
    na = len(g4s)

    def body(*refs):
        g_refs, out_refs = refs[:na], refs[na:2 * na]
        send_sems, recv_sems = refs[2 * na:]
        x, y, c = _place()
        cps = [pltpu.make_async_remote_copy(src_ref=g_refs[p].at[:, 1 - c], dst_ref=out_refs[p],
                                            send_sem=send_sems.at[p], recv_sem=recv_sems.at[p],
                                            device_id=(x, y, 1 - c), device_id_type=MESH) for p in range(na)]
        for cp in cps:
            cp.start()
        for cp in cps:
            cp.wait()

    return pl.pallas_call(
        body, out_shape=[jax.ShapeDtypeStruct((g.shape[0],) + g.shape[2:], g.dtype) for g in g4s],
        in_specs=[ANY] * na, out_specs=[ANY] * na,
        scratch_shapes=[pltpu.SemaphoreType.DMA((na,)), pltpu.SemaphoreType.DMA((na,))], name="rs_sibling")(*g4s)


HBM = pl.BlockSpec(memory_space=pltpu.HBM)
SEM = pl.BlockSpec(memory_space=pltpu.SEMAPHORE)
EFFECT = pltpu.SideEffectType.DATAFLOW_SIDE_EFFECTING


def _in_hbm(a):
    return pltpu.with_memory_space_constraint(a, pltpu.HBM)


def _ici_copies(src_refs, land_refs, send_sems, recv_sems, gather):
    na = len(src_refs)
    x, y, c = _place()
    cps = []
    if gather == "sibling":
        for k, (px, py) in enumerate([(x, y)] + _other_chips(x, y)):
            for p in range(na):
                blk = src_refs[p].at[4 * px + 2 * py + c]
                cps.append(pltpu.make_async_remote_copy(
                    src_ref=blk, dst_ref=blk, send_sem=send_sems.at[k * na + p], recv_sem=recv_sems.at[k * na + p],
                    device_id=(x, y, 1 - c), device_id_type=MESH))
        return cps
    for j, (px, py) in enumerate(_other_chips(x, y)):
        for p in range(na):
            src = src_refs[p].at[4 * x + 2 * y + c] if gather else src_refs[p].at[2 * px + py]
            dst = src if gather else land_refs[p].at[j]
            cps.append(pltpu.make_async_remote_copy(
                src_ref=src, dst_ref=dst, send_sem=send_sems.at[j * na + p], recv_sem=recv_sems.at[j * na + p],
                device_id=(px, py, c), device_id_type=MESH))
    return cps


def _ici_start(srcs, land_shapes, deps, *, gather, name):
    na, nd = len(srcs), len(deps)
    lands = [lax.empty(s, a.dtype) for s, a in zip(land_shapes, srcs)]
    nb = na + len(lands)

    def body(*refs):
        src_refs, land_refs = refs[:na], refs[na:nb]
        send_sems, recv_sems = refs[nb + nd], refs[nb + nd + 1]
        token = refs[-1]
        for cp in _ici_copies(src_refs, land_refs, send_sems, recv_sems, gather):
            cp.start()
        token[...] = jnp.zeros_like(token)

    bufs = list(srcs) + lands
    n_sem = (4 if gather == "sibling" else 3) * na
    out = pl.pallas_call(
        body, name=name,
        out_shape=(pltpu.SemaphoreType.DMA((n_sem,)), pltpu.SemaphoreType.DMA((n_sem,)),
                   *[pltpu.HBM(a.shape, a.dtype) for a in bufs], jax.ShapeDtypeStruct((8, LANE), F32)),
        in_specs=[HBM] * nb + [ANY] * nd,
        out_specs=(SEM, SEM, *([HBM] * nb), pl.BlockSpec(memory_space=pltpu.VMEM)),
        input_output_aliases={p: 2 + p for p in range(nb)},
        compiler_params=pltpu.CompilerParams(has_side_effects=EFFECT),
    )(*[_in_hbm(a) for a in bufs], *deps)
    return out[0], out[1], list(out[2:2 + na]), list(out[2 + na:2 + nb]), out[-1]


def _ici_wait(send_sems, recv_sems, srcs, lands, after, *, gather, name):
    na, nd = len(srcs), len(after)
    nb = na + len(lands)

    def body(*refs):
        src_refs, land_refs = refs[:na], refs[na:nb]
        s_sems, r_sems = refs[nb], refs[nb + 1]
        for cp in _ici_copies(src_refs, land_refs, s_sems, r_sems, gather):
            cp.wait_send()
            cp.wait_recv()

    bufs = list(srcs) + list(lands)
    out = pl.pallas_call(
        body, name=name, out_shape=tuple(pltpu.HBM(a.shape, a.dtype) for a in bufs),
        in_specs=[HBM] * nb + [SEM, SEM] + [ANY] * nd, out_specs=tuple([HBM] * nb),
        input_output_aliases={p: p for p in range(nb)},
        compiler_params=pltpu.CompilerParams(has_side_effects=EFFECT),
    )(*bufs, send_sems, recv_sems, *after)
    return list(out[:na]), list(out[na:])


HI_MASK = -65536


def _pack_pairs_xla(a):
    half = a.shape[-1] // 2
    bits = lax.bitcast_convert_type(a.astype(BF16), jnp.uint16).astype(jnp.uint32)
    return lax.bitcast_convert_type((bits[..., half:] << 16) | bits[..., :half], F32)


def _split_pairs(words):
    wv = lax.bitcast_convert_type(words, jnp.int32)
    return lax.bitcast_convert_type(wv << 16, F32), lax.bitcast_convert_type(wv & HI_MASK, F32)


def _join_pairs(lo, hi):
    lo_b = lax.bitcast_convert_type(lo.astype(BF16).astype(F32), jnp.int32)
    hi_b = lax.bitcast_convert_type(hi.astype(BF16).astype(F32), jnp.int32)
    return lax.bitcast_convert_type((hi_b & HI_MASK) | lax.shift_right_logical(lo_b, 16), F32)


def _unpack(xp, *, name, out_rows=None, tr=None, src_fn=None, zero_fn=None):
    r, hw = xp.shape
    out_rows = out_rows or r
    tr = tr or _tile(out_rows, 512, 16)
    src = src_fn or (lambda i: i)
    k = _blocks_per_step(out_rows // tr) if src_fn is not None else 1

    def body(*refs):
        o_ref = refs[k]
        for j in range(k):
            lo, hi = _split_pairs(refs[j][...])
            if zero_fn is not None:
                z = zero_fn(pl.program_id(0) * k + j)
                lo, hi = jnp.where(z, 0.0, lo), jnp.where(z, 0.0, hi)
            o_ref[j * tr:(j + 1) * tr, :hw] = lo.astype(BF16)
            o_ref[j * tr:(j + 1) * tr, hw:] = hi.astype(BF16)

    return pl.pallas_call(
        body, grid=(out_rows // (k * tr),),
        in_specs=[pl.BlockSpec((tr, hw), lambda i, j=j: (src(i * k + j), 0)) for j in range(k)],
        out_specs=pl.BlockSpec((k * tr, 2 * hw), lambda i: (i, 0)),
        out_shape=jax.ShapeDtypeStruct((out_rows, 2 * hw), BF16), name=name, compiler_params=_cp(1))(*([xp] * k))


def _blocks_per_step(n_blocks):
    return max(k for k in range(1, 17) if n_blocks % k == 0)


def _gather_rows(xp, *, name, out_rows, tr, src_fn):
    _, hw = xp.shape
    k = _blocks_per_step(out_rows // tr)

    def body(*refs):
        for j in range(k):
            refs[k][j * tr:(j + 1) * tr, :] = refs[j][...]

    return pl.pallas_call(
        body, grid=(out_rows // (k * tr),),
        in_specs=[pl.BlockSpec((tr, hw), lambda i, j=j: (src_fn(i * k + j), 0)) for j in range(k)],
        out_specs=pl.BlockSpec((k * tr, hw), lambda i: (i, 0)),
        out_shape=jax.ShapeDtypeStruct((out_rows, hw), xp.dtype), name=name, compiler_params=_cp(1))(*([xp] * k))


def _matmul_tn_packed(a, b, name):
    (t, m), (t2, c) = a.shape, b.shape
    assert t == t2
    hw = c // 2
    tm = _tile(m, 1088, LANE)
    tn = _tile(hw, 512, LANE)
    tk = _tile(t, 2176, LANE)
    nk, nj = t // tk, hw // tn

    def body(a_ref, bl_ref, bh_ref, o_ref, acc_lo, acc_hi):
        kk = pl.program_id(2)

        @pl.when(kk == 0)
        def _():
            acc_lo[...] = jnp.zeros_like(acc_lo)
            acc_hi[...] = jnp.zeros_like(acc_hi)

        av = a_ref[...].astype(BF16)
        acc_lo[...] += lax.dot_general(av, bl_ref[...].astype(BF16), TN, preferred_element_type=F32)
        acc_hi[...] += lax.dot_general(av, bh_ref[...].astype(BF16), TN, preferred_element_type=F32)

        @pl.when(kk == nk - 1)
        def _():
            o_ref[...] = _join_pairs(acc_lo[...], acc_hi[...])

    return pl.pallas_call(
        body, grid=(m // tm, nj, nk),
        in_specs=[pl.BlockSpec((tk, tm), lambda i, j, kk: (kk, i)), pl.BlockSpec((tk, tn), lambda i, j, kk: (kk, j)),
                  pl.BlockSpec((tk, tn), lambda i, j, kk: (kk, nj + j))],
        out_specs=pl.BlockSpec((tm, tn), lambda i, j, kk: (i, j)), out_shape=jax.ShapeDtypeStruct((m, hw), F32),
        scratch_shapes=[pltpu.VMEM((tm, tn), F32), pltpu.VMEM((tm, tn), F32)], name=name,
        compiler_params=_cp(3))(a, b, b)


def _add_sibling(g4, recv, cidx, name):
    nchip, _, r, n = g4.shape
    tr = _tile(r, 256, 8)

    def body(c_ref, a_ref, b_ref, o_ref):
        alo, ahi = _split_pairs(a_ref[...])
        blo, bhi = _split_pairs(b_ref[...])
        o_ref[...] = _join_pairs(alo + blo, ahi + bhi)

    grid_spec = pltpu.PrefetchScalarGridSpec(
        num_scalar_prefetch=1, grid=(nchip, r // tr),
        in_specs=[pl.BlockSpec((None, None, tr, n), lambda k, i, c: (k, c[0], i, 0)),
                  pl.BlockSpec((None, tr, n), lambda k, i, c: (k, i, 0))],
        out_specs=pl.BlockSpec((None, tr, n), lambda k, i, c: (k, i, 0)))
    return pl.pallas_call(body, grid_spec=grid_spec, out_shape=jax.ShapeDtypeStruct((nchip, r, n), F32),
                          name=name, compiler_params=_cp(2))(cidx, g4, recv)


def _add_chips(p4, recv3, chip_idx, name):
    _, r, n = p4.shape
    tr = _tile(r, 256, 8)

    def body(k_ref, a_ref, b_ref, o_ref):
        lo, hi = _split_pairs(a_ref[...])
        for j in range(3):
            blo, bhi = _split_pairs(b_ref[j])
            lo, hi = lo + blo, hi + bhi
        o_ref[:, :n] = lo
        o_ref[:, n:] = hi

    grid_spec = pltpu.PrefetchScalarGridSpec(
        num_scalar_prefetch=1, grid=(r // tr,),
        in_specs=[pl.BlockSpec((None, tr, n), lambda i, k: (k[0], i, 0)),
                  pl.BlockSpec((3, tr, n), lambda i, k: (0, i, 0))],
        out_specs=pl.BlockSpec((tr, 2 * n), lambda i, k: (i, 0)))
    return pl.pallas_call(body, grid_spec=grid_spec, out_shape=jax.ShapeDtypeStruct((r, 2 * n), F32),
                          name=name, compiler_params=_cp(1))(chip_idx, p4, recv3)


def _all_reduce_small(xs):
    r, n = xs.shape

    def body(x_ref, o_ref, buf, send_sems, recv_sems):
        x, y, c = _place()
        me = 4 * x + 2 * y + c
        buf[me] = x_ref[...]
        peers = []
        for k in range(1, N_DEV):
            px = (1 - x) if (k >> 2) & 1 else x
            py = (1 - y) if (k >> 1) & 1 else y
            pc = (1 - c) if k & 1 else c
            peers.append((px, py, pc))
        sends = [pltpu.make_async_remote_copy(src_ref=x_ref, dst_ref=buf.at[me], send_sem=send_sems.at[k],
                                              recv_sem=recv_sems.at[k], device_id=peer, device_id_type=MESH)
                 for k, peer in enumerate(peers)]
        for cp in sends:
            cp.start()
        for k, (px, py, pc) in enumerate(peers):
            pltpu.make_async_remote_copy(src_ref=x_ref, dst_ref=buf.at[4 * px + 2 * py + pc],
                                         send_sem=send_sems.at[k], recv_sem=recv_sems.at[k],
                                         device_id=(px, py, pc), device_id_type=MESH).wait_recv()
        for cp in sends:
            cp.wait_send()
        acc = buf[0]
        for dv in range(1, N_DEV):
            acc = acc + buf[dv]
        o_ref[...] = acc

    vm = pl.BlockSpec(memory_space=pltpu.VMEM)
    return pl.pallas_call(
        body, out_shape=jax.ShapeDtypeStruct((r, n), F32), in_specs=[vm], out_specs=vm,
        scratch_shapes=[pltpu.VMEM((N_DEV, r, n), F32), pltpu.SemaphoreType.DMA((7,)), pltpu.SemaphoreType.DMA((7,))],
        name="all_reduce_small", compiler_params=pltpu.CompilerParams(vmem_limit_bytes=VMEM_LIMIT))(xs)


def _pack_rows(arrs, quantum):
    parts, sizes = [], []
    for a in arrs:
        f = a.reshape(-1)
        pad = (-f.shape[0]) % quantum
        if pad:
            f = jnp.pad(f, (0, pad))
        parts.append(f)
        sizes.append(f.shape[0])
    return jnp.concatenate(parts).reshape(-1, LANE), sizes


def _unpack_rows(buf, sizes, shapes):
    flat = buf.reshape(-1)
    out, off = [], 0
    for sz, sh in zip(sizes, shapes):
        n = math.prod(sh)
        out.append(flat[off:off + n].reshape(sh))
        off += sz
    return out


def kernel(x, meta_tokens, mix_norm_g, w_in, conv_dw, conv_b, conv_ln_g, conv_ln_b, w_conv_out, gqa_q_norm_g, gqa_k_norm_g, w_gqa_out, mla_q_norm_g, w_mla_uq, mla_kv_norm_g, w_mla_ukv, w_mla_out, gate_b, w_out, ffn_norm_g, w_ffn_gate, w_ffn_up, w_ffn_down, final_norm_g, loss_target, m_meta_tokens, m_mix_norm_g, m_w_in, m_conv_dw, m_conv_b, m_conv_ln_g, m_conv_ln_b, m_w_conv_out, m_gqa_q_norm_g, m_gqa_k_norm_g, m_w_gqa_out, m_mla_q_norm_g, m_w_mla_uq, m_mla_kv_norm_g, m_w_mla_ukv, m_w_mla_out, m_gate_b, m_w_out, m_ffn_norm_g, m_w_ffn_gate, m_w_ffn_up, m_w_ffn_down, m_final_norm_g, v_meta_tokens, v_mix_norm_g, v_w_in, v_conv_dw, v_conv_b, v_conv_ln_g, v_conv_ln_b, v_w_conv_out, v_gqa_q_norm_g, v_gqa_k_norm_g, v_w_gqa_out, v_mla_q_norm_g, v_w_mla_uq, v_mla_kv_norm_g, v_w_mla_ukv, v_w_mla_out, v_gate_b, v_w_out, v_ffn_norm_g, v_w_ffn_gate, v_w_ffn_up, v_w_ffn_down, v_final_norm_g):
    w = dict(meta_tokens=meta_tokens, mix_norm_g=mix_norm_g, w_in=w_in, conv_dw=conv_dw, conv_b=conv_b,
             conv_ln_g=conv_ln_g, conv_ln_b=conv_ln_b, w_conv_out=w_conv_out, gqa_q_norm_g=gqa_q_norm_g,
             gqa_k_norm_g=gqa_k_norm_g, w_gqa_out=w_gqa_out, mla_q_norm_g=mla_q_norm_g, w_mla_uq=w_mla_uq,
             mla_kv_norm_g=mla_kv_norm_g, w_mla_ukv=w_mla_ukv, w_mla_out=w_mla_out, gate_b=gate_b, w_out=w_out,
             ffn_norm_g=ffn_norm_g, w_ffn_gate=w_ffn_gate, w_ffn_up=w_ffn_up, w_ffn_down=w_ffn_down,
             final_norm_g=final_norm_g)
    mom1 = dict(zip(W_NAMES, (m_meta_tokens, m_mix_norm_g, m_w_in, m_conv_dw, m_conv_b, m_conv_ln_g, m_conv_ln_b, m_w_conv_out, m_gqa_q_norm_g, m_gqa_k_norm_g, m_w_gqa_out, m_mla_q_norm_g, m_w_mla_uq, m_mla_kv_norm_g, m_w_mla_ukv, m_w_mla_out, m_gate_b, m_w_out, m_ffn_norm_g, m_w_ffn_gate, m_w_ffn_up, m_w_ffn_down, m_final_norm_g)))
    mom2 = dict(zip(W_NAMES, (v_meta_tokens, v_mix_norm_g, v_w_in, v_conv_dw, v_conv_b, v_conv_ln_g, v_conv_ln_b, v_w_conv_out, v_gqa_q_norm_g, v_gqa_k_norm_g, v_w_gqa_out, v_mla_q_norm_g, v_w_mla_uq, v_mla_kv_norm_g, v_w_mla_ukv, v_w_mla_out, v_gate_b, v_w_out, v_ffn_norm_g, v_w_ffn_gate, v_w_ffn_up, v_w_ffn_down, v_final_norm_g)))

    nb, seq, d = x.shape
    depth = w_in.shape[0]
    c_conv = conv_b.shape[1]
    kw = conv_dw.shape[1]
    hd = HEAD_DIM
    hq = w_gqa_out.shape[1] // hd
    nh = w_mla_out.shape[1] // hd
    qr, kvr = mla_q_norm_g.shape[1], mla_kv_norm_g.shape[1]
    d_in = w_in.shape[2] * N_DEV
    hkv = (d_in - 2 * c_conv - hq * hd - qr - kvr - MLA_ROPE_DIM - N_BRANCH * d) // (2 * hd)
    l_valid = seq + N_META
    lp = -(-(l_valid + kw // 2) // LANE) * LANE
    m_tok = nb * lp
    mla_qk = hd + MLA_ROPE_DIM
    gqa_scale, mla_scale = 1.0 / math.sqrt(hd), 1.0 / math.sqrt(mla_qk)

    off_gate = 0
    off_conv = N_BRANCH * d
    off_q = off_conv + 2 * c_conv
    off_cq = off_q + (hq + 2 * hkv) * hd
    off_ckv = off_cq + qr
    off_kpe = off_ckv + kvr
    n_proj = off_kpe + LANE

    x_id, y_id, c_id = _place()
    me = 4 * x_id + 2 * y_id + c_id

    tm_wide = _tile(lp, 272, 16)
    tm_mid = _tile(lp, 544, 16)

    sm_buf, sm_sizes = _pack_rows([meta_tokens, conv_dw], 8 * LANE)
    sm_all = _all_gather([sm_buf], True, "ag_small")[0]
    sm_all = sm_all.reshape(N_DEV, -1)
    o0 = 0
    meta_all = sm_all[:, o0:o0 + meta_tokens.size].reshape((N_DEV,) + meta_tokens.shape)
    meta_full = jnp.transpose(meta_all, (1, 0, 2)).reshape(N_META, d)
    o0 = sm_sizes[0]
    dw_all = sm_all[:, o0:o0 + conv_dw.size].reshape((N_DEV, depth, kw, c_conv // N_DEV))
    dw_full = jnp.transpose(dw_all, (1, 2, 0, 3)).reshape(depth, kw, c_conv)

    n_head_cols = d_in - N_BRANCH * d - MLA_ROPE_DIM
    rb = MLA_ROPE_DIM
    nb_gate, nb_head = N_BRANCH * d // rb, n_head_cols // rb
    assert n_head_cols % rb == 0 and (N_BRANCH * d) % rb == 0 and n_proj == (nb_gate + nb_head + 2) * rb

    def in_src(i):
        return jnp.where(i < nb_gate, nb_head + 1 + i,
                         jnp.where(i < nb_gate + nb_head, i - nb_gate, jnp.where(i == nb_gate + nb_head, nb_head, 0)))

    def in_dst(i):
        return jnp.where(i < nb_head, nb_gate + i, jnp.where(i == nb_head, nb_gate + nb_head, i - nb_head - 1))

    def uq_src(i):
        j = i - 2 * nh
        return jnp.where(i < 2 * nh, 3 * (i // 2) + i % 2, 3 * (j // 2) + 2)

    def uq_dst(i):
        return jnp.where(i % 3 < 2, 2 * (i // 3) + i % 3, 2 * nh + 2 * (i // 3))

    ag_pending, tokens = {}, []
    for i in range(depth):
        for gi, names in enumerate(GROUPS):
            shards = [_pack_pairs_xla(w[n][i] if n in ROW_SHARDED else w[n][i].T) for n in names]
            bufs = [lax.dynamic_update_slice(lax.empty((N_DEV,) + a.shape, F32), a[None], (me, 0, 0))
                    for a in shards]
            *pend, tok = _ici_start(bufs, [], tokens[-1:], gather=True, name="ag_ici_start_%d_%d" % (i, gi))
            ag_pending[i, gi] = pend
            tokens.append(tok)

    sib_pending = {}

    def sibling_start(i, gi, after):
        bufs, _ = _ici_wait(*ag_pending[i, gi], after, gather=True, name="ag_ici_wait_%d_%d" % (i, gi))
        *sib_pending[i, gi], tok = _ici_start(bufs, [], [], gather="sibling", name="ag_sib_start_%d_%d" % (i, gi))
        return tok

    def all_gather_group(i, gi, after):
        gath, _ = _ici_wait(*sib_pending[i, gi], after, gather="sibling", name="ag_sib_wait_%d_%d" % (i, gi))
        wl = {}
        for n, g in zip(GROUPS[gi], gath):
            pk = g.reshape(-1, g.shape[-1])
            if n == 'w_in':
                wl[n] = _unpack(pk, name="unpack_w_in", out_rows=n_proj, tr=rb, src_fn=in_src,
                                zero_fn=lambda i: i == nb_gate + nb_head + 1)
            elif n == 'w_mla_uq':
                wl[n] = _unpack(pk, name="unpack_w_mla_uq", out_rows=2 * nh * hd, tr=rb, src_fn=uq_src,
                                zero_fn=lambda i: (i >= 2 * nh) & ((i - 2 * nh) % 2 == 1))
            elif n == 'w_mla_ukv':
                wl[n] = _unpack(pk, name="unpack_w_mla_ukv", out_rows=2 * nh * hd, tr=hd,
                                src_fn=lambda i: jnp.where(i < nh, 2 * i, 2 * (i - nh) + 1))
            else:
                wl[n] = _unpack(pk, name="unpack_" + n)
        return wl

    full = [None] * depth

    cos_g, sin_g = _rope_tables(lp, l_valid, hd // 4)
    cos_m, sin_m = _rope_tables(lp, l_valid, MLA_ROPE_DIM // 4)

    def vec2(a):
        return a.reshape(1, -1)

    meta_b = jnp.broadcast_to(meta_full[None], (nb, N_META, d))
    h = jnp.concatenate([meta_b, x, jnp.zeros((nb, lp - l_valid, d), F32)], axis=1).reshape(m_tok, d)
    tgt = jnp.pad(loss_target, ((0, 0), (N_META, lp - l_valid), (0, 0))).reshape(m_tok, d)

    saved = []
    first_tok = [sibling_start(0, 0, tokens[-1:])]
    sib_tok = []
    for i in range(depth):
        s = {}
        s['h_in'] = h
        u = _rms_fwd(h, d, 0, vec2(mix_norm_g[i]), tm_wide, "mix_norm_fwd")
        wl = full[i] = all_gather_group(i, 0, first_tok if i == 0 else [h])
        proj = _matmul(u, wl['w_in'], mode="nt", out_dtype=F32, name="mm_in", deps=sib_tok)
        cc = _conv_fwd(proj, dw_full[i], vec2(conv_b[i]), c_conv=c_conv, off_a=off_conv, nb=nb, lp=lp, l_valid=l_valid)
        sc = _ln_silu_fwd(cc, vec2(conv_ln_g[i]), vec2(conv_ln_b[i]), tm_mid)
        ya = _matmul(sc, wl['w_conv_out'], mode="nt", out_dtype=F32, name="mm_conv_out")
        qh, kh, vh = _gqa_prep_fwd(proj, vec2(gqa_q_norm_g[i]), vec2(gqa_k_norm_g[i]), cos_g, sin_g,
                                   off_q=off_q, hq=hq, hkv=hkv, nb=nb, lp=lp, tm=tm_mid, q_scale=gqa_scale)
        ob = _attn_fwd(qh, kh, vh, l_valid=l_valid, name="gqa_attn_fwd")
        yb = _matmul(ob, wl['w_gqa_out'], mode="nt", out_dtype=F32, name="mm_gqa_out")
        cqn = _rms_fwd(proj, qr, _cb(off_cq, qr), vec2(mla_q_norm_g[i]), tm_mid, "mla_q_norm_fwd")
        ckvn = _rms_fwd(proj, kvr, _cb(off_ckv, kvr), vec2(mla_kv_norm_g[i]), tm_mid, "mla_kv_norm_fwd")
        qm = _matmul(cqn, wl['w_mla_uq'], mode="nt", out_dtype=F32, name="mm_mla_uq")
        kvm = _matmul(ckvn, wl['w_mla_ukv'], mode="nt", out_dtype=F32, name="mm_mla_ukv")
        qc, kc, vc = _mla_prep_fwd(qm, kvm, proj, cos_m, sin_m, off_kpe=off_kpe, nh=nh, nb=nb, lp=lp, tm=tm_mid,
                                   q_scale=mla_scale)
        oc = _attn_fwd(qc, kc, vc, l_valid=l_valid, name="mla_attn_fwd")
        yc = _matmul(oc, wl['w_mla_out'], mode="nt", out_dtype=F32, name="mm_mla_out")
        merged = _gate_fwd(proj, vec2(gate_b[i]), ya, yb, yc, d=d, tm=tm_wide)
        h2 = _matmul(merged, wl['w_out'], mode="nn", out_dtype=F32, name="mm_out", residual=h)
        vn = _rms_fwd(h2, d, 0, vec2(ffn_norm_g[i]), tm_wide, "ffn_norm_fwd")
        wl.update(all_gather_group(i, 1, [h2, sibling_start(0, 1, [h2])] if i == 0 else [h2]))
        gt, up, act = _ffn_up_fwd(vn, wl['w_ffn_gate'], wl['w_ffn_up'])
        ahead = [sibling_start(i + 1, 0, [h2])] if i + 1 < depth else []
        h = _matmul(act, wl['w_ffn_down'], mode="nn", out_dtype=F32, name="mm_ffn_down", residual=h2, deps=ahead)
        sib_tok = [sibling_start(i + 1, 1, [h])] if i + 1 < depth else []
        s.update(u=u, proj=proj, cc=cc, sc=sc, ya=ya, qh=qh, kh=kh, vh=vh, ob=ob, yb=yb, cqn=cqn, ckvn=ckvn,
                 qc=qc, kc=kc, vc=vc, oc=oc, yc=yc, merged=merged, h2=h2, vn=vn, gt=gt, up=up, act=act)
        saved.append(s)

    dh, dg_final, loss_part = _loss_head(h, vec2(final_norm_g), tgt, lp=lp, l_valid=l_valid, tm=tm_wide)

    gsmall = {n: [None] * depth for n in SMALL if n != 'final_norm_g'}
    gdw = [None] * depth
    gbig = [{} for _ in range(depth)]
    cidx = jnp.reshape(c_id, (1,)).astype(jnp.int32)
    chip_idx = jnp.reshape(2 * x_id + y_id, (1,)).astype(jnp.int32)

    def reduce_scatter_start(gl, i, gi):
        names = GROUPS[gi]
        g4s = [gl[n].reshape(N_DEV // 2, 2, -1, gl[n].shape[-1]) for n in names]
        from_sibling = _swap_with_sibling(g4s)
        p4s = [_add_sibling(g, r, cidx, "rs_add_sibling_" + n) for n, g, r in zip(names, g4s, from_sibling)]
        *pend, tok = _ici_start(p4s, [(3,) + p.shape[1:] for p in p4s], [], gather=False,
                                name="rs_ici_start_%d_%d" % (i, gi))
        return pend, tok

    def reduce_scatter_finish(pend, after, i, gi):
        p4s, from_chips = _ici_wait(*pend, after, gather=False, name="rs_ici_wait_%d_%d" % (i, gi))
        out = {}
        for n, p, r in zip(GROUPS[gi], p4s, from_chips):
            g = _add_chips(p, r, chip_idx, "rs_add_chips_" + n)
            out[n] = g if n in ROW_SHARDED else g.T
        return out

    rs_pending = [None, None]
    rs_token = []
    for i in reversed(range(depth)):
        s = saved[i]
        proj = s['proj']
        wl = full[i]
        gl = {}
        dgt, dup = _ffn_down_bwd(dh, wl['w_ffn_down'], s['gt'], s['up'], rs_token)
        gl['w_ffn_down'] = _matmul_tn_packed(s['act'], dh, "mm_ffn_down_dw")
        dvn = _matmul(dgt, wl['w_ffn_gate'], mode="nn", out_dtype=F32, name="mm_ffn_gate_dx")
        dvn = _matmul(dup, wl['w_ffn_up'], mode="nn", out_dtype=F32, name="mm_ffn_up_dx", residual=dvn)
        gl['w_ffn_gate'] = _matmul_tn_packed(dgt, s['vn'], "mm_ffn_gate_dw")
        gl['w_ffn_up'] = _matmul_tn_packed(dup, s['vn'], "mm_ffn_up_dw")
        dh2, gsmall['ffn_norm_g'][i] = _rms_bwd(s['h2'], d, 0, vec2(ffn_norm_g[i]), dvn, tm_wide, F32,
                                                "ffn_norm_bwd", add=dh)
        started, tok = reduce_scatter_start(gl, i, 1)
        if rs_pending[1] is not None:
            gbig[i + 1].update(reduce_scatter_finish(rs_pending[1], [dh2, tok], i + 1, 1))
        rs_pending[1] = started
        dmg = _matmul(dh2, wl['w_out'], mode="nt", out_dtype=F32, name="mm_out_dx", deps=[tok])
        gl['w_out'] = _matmul_tn_packed(s['merged'], dh2, "mm_out_dw")
        dya, dyb, dyc, dg0, dg1, dg2, db0, db1, db2 = _gate_bwd(dmg, proj, vec2(gate_b[i]), s['ya'], s['yb'],
                                                                s['yc'], d=d, tm=tm_wide)
        gsmall['gate_b'][i] = jnp.concatenate([db0, db1, db2], axis=1)
        dsc = _matmul(dya, wl['w_conv_out'], mode="nn", out_dtype=F32, name="mm_conv_out_dx")
        gl['w_conv_out'] = _matmul_tn_packed(dya, s['sc'], "mm_conv_out_dw")
        dcc, gsmall['conv_ln_g'][i], gsmall['conv_ln_b'][i] = _ln_silu_bwd(
            s['cc'], vec2(conv_ln_g[i]), vec2(conv_ln_b[i]), dsc, tm_mid)
        da, dgc, gdw[i], gsmall['conv_b'][i] = _conv_bwd(proj, dcc, dw_full[i], c_conv=c_conv, off_a=off_conv,
                                                         nb=nb, lp=lp, l_valid=l_valid)
        dob = _matmul(dyb, wl['w_gqa_out'], mode="nn", out_dtype=BF16, name="mm_gqa_out_dx")
        gl['w_gqa_out'] = _matmul_tn_packed(dyb, s['ob'], "mm_gqa_out_dw")
        dqh, dkh, dvh = _attn_bwd(s['qh'], s['kh'], s['vh'], dob, l_valid=l_valid, name="gqa_attn_bwd")
        dq, dkv, gsmall['gqa_q_norm_g'][i], gsmall['gqa_k_norm_g'][i] = _gqa_prep_bwd(
            proj, vec2(gqa_q_norm_g[i]), vec2(gqa_k_norm_g[i]), cos_g, sin_g, dqh, dkh, dvh,
            off_q=off_q, hq=hq, hkv=hkv, nb=nb, lp=lp, tm=tm_mid, q_scale=gqa_scale)
        doc = _matmul(dyc, wl['w_mla_out'], mode="nn", out_dtype=BF16, name="mm_mla_out_dx")
        gl['w_mla_out'] = _matmul_tn_packed(dyc, s['oc'], "mm_mla_out_dw")
        dqc, dkc, dvc = _attn_bwd(s['qc'], s['kc'], s['vc'], doc, l_valid=l_valid, name="mla_attn_bwd")
        dqm, dkvm, dkpe = _mla_prep_bwd(dqc, dkc, dvc, cos_m, sin_m, nh=nh, nb=nb, lp=lp, tm=tm_mid,
                                        q_scale=mla_scale)
        dcqn = _matmul(dqm, wl['w_mla_uq'], mode="nn", out_dtype=F32, name="mm_mla_uq_dx")
        guq = _matmul_tn_packed(dqm, s['cqn'], "mm_mla_uq_dw")
        dckvn = _matmul(dkvm, wl['w_mla_ukv'], mode="nn", out_dtype=F32, name="mm_mla_ukv_dx")
        gukv = _matmul_tn_packed(dkvm, s['ckvn'], "mm_mla_ukv_dw")
        dcq, gsmall['mla_q_norm_g'][i] = _rms_bwd(proj, qr, _cb(off_cq, qr), vec2(mla_q_norm_g[i]), dcqn, tm_mid,
                                                  BF16, "mla_q_norm_bwd")
        dckv, gsmall['mla_kv_norm_g'][i] = _rms_bwd(proj, kvr, _cb(off_ckv, kvr), vec2(mla_kv_norm_g[i]), dckvn,
                                                    tm_mid, BF16, "mla_kv_norm_bwd")
        gl['w_mla_uq'] = _gather_rows(guq, name="perm_mla_uq_dw", out_rows=nh * mla_qk, tr=rb, src_fn=uq_dst)
        gl['w_mla_ukv'] = _gather_rows(gukv, name="perm_mla_ukv_dw", out_rows=2 * nh * hd, tr=hd,
                                       src_fn=lambda i: i // 2 + nh * (i % 2))
        dproj = jnp.concatenate([dg0, dg1, dg2, da, dgc, dq, dkv, dcq, dckv, dkpe], axis=1)
        du = _matmul(dproj, wl['w_in'], mode="nn", out_dtype=F32, name="mm_in_dx")
        gin = _matmul_tn_packed(dproj, s['u'], "mm_in_dw")
        gl['w_in'] = _gather_rows(gin, name="perm_in_dw", out_rows=d_in, tr=rb, src_fn=in_dst)
        dh, gsmall['mix_norm_g'][i] = _rms_bwd(s['h_in'], d, 0, vec2(mix_norm_g[i]), du, tm_wide, F32,
                                               "mix_norm_bwd", add=dh2)
        started, tok = reduce_scatter_start(gl, i, 0)
        if rs_pending[0] is not None:
            gbig[i + 1].update(reduce_scatter_finish(rs_pending[0], [dh, tok], i + 1, 0))
        rs_pending[0], rs_token = started, [tok]

    def adamw_big(names):
        for n in names:
            sh = w[n].shape
            two_d = (sh[0] * sh[1], sh[2])
            grads[n] = jnp.stack([gbig[i][n] for i in range(depth)])
            dl, mn, vn_ = _adamw(w[n].reshape(two_d), grads[n].reshape(two_d), mom1[n].reshape(two_d),
                                 mom2[n].reshape(two_d), "adamw_" + n)
            delta[n], new_m[n], new_v[n] = dl.reshape(sh), mn.reshape(sh), vn_.reshape(sh)

    grads, delta, new_m, new_v = {}, {}, {}, {}
    gbig[0].update(reduce_scatter_finish(rs_pending[1], rs_token, 0, 1))
    adamw_big(GROUPS[1])
    gbig[0].update(reduce_scatter_finish(rs_pending[0], [delta[GROUPS[1][-1]]], 0, 0))
    adamw_big(GROUPS[0])

    dh0 = dh.reshape(nb, lp, d)
    grad_x = dh0[:, N_META:l_valid]
    gmeta_full = jnp.sum(dh0[:, :N_META], axis=0)

    small_list = [jnp.stack(gsmall[n]).reshape(w[n].shape) for n in SMALL if n != 'final_norm_g']
    small_list += [dg_final.reshape(final_norm_g.shape), gmeta_full, jnp.stack(gdw), loss_part]
    ar_buf, ar_sizes = _pack_rows(small_list, 8 * LANE)
    ar = _all_reduce_small(ar_buf)
    ar_shapes = [w[n].shape for n in SMALL] + [(N_META, d), (depth, kw, c_conv), (1, LANE)]
    ar_out = _unpack_rows(ar, ar_sizes, ar_shapes)
    for n, g in zip(SMALL, ar_out):
        grads[n] = g
    dcol = d // N_DEV
    grads['meta_tokens'] = lax.dynamic_slice(ar_out[len(SMALL)], (0, me * dcol), (N_META, dcol))
    ccol = c_conv // N_DEV
    grads['conv_dw'] = lax.dynamic_slice(ar_out[len(SMALL) + 1], (0, 0, me * ccol),
                                         (depth, kw, ccol)).reshape(conv_dw.shape)
    loss = ar_out[len(SMALL) + 2][0, 0]

    rest = [n for n in W_NAMES if n not in BIG]
    pw, psz = _pack_rows([w[n] for n in rest], 8 * LANE)
    pg, _ = _pack_rows([grads[n] for n in rest], 8 * LANE)
    pm, _ = _pack_rows([mom1[n] for n in rest], 8 * LANE)
    pv, _ = _pack_rows([mom2[n] for n in rest], 8 * LANE)
    dl, mn, vn_ = _adamw(pw, pg, pm, pv, "adamw_small")
    shapes = [w[n].shape for n in rest]
    for n, a, b, c in zip(rest, _unpack_rows(dl, psz, shapes), _unpack_rows(mn, psz, shapes),
                          _unpack_rows(vn_, psz, shapes)):
        delta[n], new_m[n], new_v[n] = a, b, c

    return (loss, grad_x, *[grads[n] for n in W_NAMES], *[delta[n] for n in W_NAMES],
            *[new_m[n] for n in W_NAMES], *[new_v[n] for n in W_NAMES])
```

```python
import functools
import math

import jax
import jax.numpy as jnp
from jax import lax
from jax.experimental import pallas as pl
from jax.experimental.pallas import tpu as pltpu

F32 = jnp.float32
BF16 = jnp.bfloat16

N_META = 16
GRID_W = 64
ROPE_THETA = 10000.0
NORM_EPS = 1e-6
HEAD_DIM = 128
MLA_ROPE_DIM = 64
N_BRANCH = 3
N_DEV = 8
LANE = 128
VMEM_LIMIT = 56 * 1024 * 1024

ADAM_LR = 0.001
ADAM_B1 = 0.9
ADAM_B2 = 0.999
ADAM_EPS = 1e-08
ADAM_WD = 0.01
ADAM_STEP = 10

W_NAMES = ['meta_tokens', 'mix_norm_g', 'w_in', 'conv_dw', 'conv_b', 'conv_ln_g', 'conv_ln_b', 'w_conv_out',
           'gqa_q_norm_g', 'gqa_k_norm_g', 'w_gqa_out', 'mla_q_norm_g', 'w_mla_uq', 'mla_kv_norm_g', 'w_mla_ukv',
           'w_mla_out', 'gate_b', 'w_out', 'ffn_norm_g', 'w_ffn_gate', 'w_ffn_up', 'w_ffn_down', 'final_norm_g']
BIG = ['w_in', 'w_conv_out', 'w_gqa_out', 'w_mla_uq', 'w_mla_ukv', 'w_mla_out', 'w_out', 'w_ffn_gate', 'w_ffn_up',
       'w_ffn_down']
ROW_SHARDED = ('w_out', 'w_ffn_down')
GROUPS = (('w_in', 'w_conv_out', 'w_gqa_out', 'w_mla_uq', 'w_mla_ukv', 'w_mla_out', 'w_out'),
          ('w_ffn_gate', 'w_ffn_up', 'w_ffn_down'))
SMALL = ['mix_norm_g', 'conv_b', 'conv_ln_g', 'conv_ln_b', 'gqa_q_norm_g', 'gqa_k_norm_g', 'mla_q_norm_g',
         'mla_kv_norm_g', 'gate_b', 'ffn_norm_g', 'final_norm_g']

NT = (((1,), (1,)), ((), ()))
TN = (((0,), (0,)), ((), ()))
NN = (((1,), (0,)), ((), ()))


def _tile(n, target, mult):
    best = None
    for t in range(mult, min(n, target) + 1, mult):
        if n % t == 0:
            best = t
    assert best is not None, (n, target, mult)
    return best


def _cp(n):
    return pltpu.CompilerParams(dimension_semantics=("arbitrary",) * n, vmem_limit_bytes=VMEM_LIMIT)


def _sig(x):
    return jax.nn.sigmoid(x)


def _cb(off, w):
    assert off % w == 0, (off, w)
    return off // w


def _matmul(a, b, *, mode, out_dtype, name, residual=None, deps=()):
    if mode == "nn":
        (m, k), (k2, n) = a.shape, b.shape
    elif mode == "nt":
        (m, k), (n, k2) = a.shape, b.shape
    else:
        (k, m), (k2, n) = a.shape, b.shape
    assert k == k2, (a.shape, b.shape, mode)
    tm = _tile(m, 1088, 128 if mode == "tn" else 16)
    tn = _tile(n, 1024, 128)
    tk = _tile(k, 2176, 128)
    nk = k // tk
    dims = {"nn": NN, "nt": NT, "tn": TN}[mode]
    a_spec = (pl.BlockSpec((tk, tm), lambda i, j, kk: (kk, i)) if mode == "tn"
              else pl.BlockSpec((tm, tk), lambda i, j, kk: (i, kk)))
    b_spec = (pl.BlockSpec((tn, tk), lambda i, j, kk: (j, kk)) if mode == "nt"
              else pl.BlockSpec((tk, tn), lambda i, j, kk: (kk, j)))
    o_spec = pl.BlockSpec((tm, tn), lambda i, j, kk: (i, j))
    has_res = residual is not None

    nd = len(deps)

    def body(*refs):
        refs = refs[:len(refs) - 2 - nd] + refs[len(refs) - 2:]
        if has_res:
            a_ref, b_ref, r_ref, o_ref, acc = refs
        else:
            a_ref, b_ref, o_ref, acc = refs
        kk = pl.program_id(2)

        @pl.when(kk == 0)
        def _():
            acc[...] = jnp.zeros_like(acc)

        acc[...] += lax.dot_general(a_ref[...].astype(BF16), b_ref[...].astype(BF16), dims,
                                    preferred_element_type=F32)

        @pl.when(kk == nk - 1)
        def _():
            r = acc[...]
            if has_res:
                r = r + r_ref[...]
            o_ref[...] = r.astype(o_ref.dtype)

    ins = [a, b] + ([residual] if has_res else []) + list(deps)
    in_specs = [a_spec, b_spec] + ([o_spec] if has_res else []) + [pl.BlockSpec(memory_space=pl.ANY)] * nd
    return pl.pallas_call(
        body, grid=(m // tm, n // tn, nk), in_specs=in_specs, out_specs=o_spec,
        out_shape=jax.ShapeDtypeStruct((m, n), out_dtype), scratch_shapes=[pltpu.VMEM((tm, tn), F32)],
        name=name, compiler_params=_cp(3))(*ins)


def _rms_fwd(x, w, cb, g, tm, name):
    m = x.shape[0]

    def body(x_ref, g_ref, o_ref):
        xv = x_ref[...]
        r = lax.rsqrt(jnp.mean(xv * xv, axis=-1, keepdims=True) + NORM_EPS)
        o_ref[...] = (xv * r * g_ref[...]).astype(o_ref.dtype)

    return pl.pallas_call(
        body, grid=(m // tm,),
        in_specs=[pl.BlockSpec((tm, w), lambda i: (i, cb)), pl.BlockSpec((1, w), lambda i: (0, 0))],
        out_specs=pl.BlockSpec((tm, w), lambda i: (i, 0)), out_shape=jax.ShapeDtypeStruct((m, w), BF16),
        name=name, compiler_params=_cp(1))(x, g)


def _rms_bwd(x, w, cb, g, dy, tm, out_dtype, name, add=None):
    m = x.shape[0]
    has_add = add is not None

    def body(*refs):
        if has_add:
            x_ref, g_ref, dy_ref, add_ref, dx_ref, dg_ref = refs
        else:
            x_ref, g_ref, dy_ref, dx_ref, dg_ref = refs
        xv = x_ref[...]
        dyv = dy_ref[...].astype(F32)
        r = lax.rsqrt(jnp.mean(xv * xv, axis=-1, keepdims=True) + NORM_EPS)
        t = dyv * g_ref[...]
        dx = r * t - xv * (r * r * r) * jnp.mean(t * xv, axis=-1, keepdims=True)
        if has_add:
            dx = dx + add_ref[...]
        dx_ref[...] = dx.astype(dx_ref.dtype)

        @pl.when(pl.program_id(0) == 0)
        def _():
            dg_ref[...] = jnp.zeros_like(dg_ref)

        dg_ref[...] += jnp.sum(dyv * xv * r, axis=0, keepdims=True)

    row = pl.BlockSpec((tm, w), lambda i: (i, 0))
    vec = pl.BlockSpec((1, w), lambda i: (0, 0))
    ins = [x, g, dy] + ([add] if has_add else [])
    in_specs = [pl.BlockSpec((tm, w), lambda i: (i, cb)), vec, row] + ([row] if has_add else [])
    return pl.pallas_call(
        body, grid=(m // tm,), in_specs=in_specs, out_specs=(row, vec),
        out_shape=(jax.ShapeDtypeStruct((m, w), out_dtype), jax.ShapeDtypeStruct((1, w), F32)),
        name=name, compiler_params=_cp(1))(*ins)


def _ln_silu_fwd(c, lg, lb, tm):
    m, w = c.shape

    def body(c_ref, g_ref, b_ref, o_ref):
        cv = c_ref[...]
        xc = cv - jnp.mean(cv, axis=-1, keepdims=True)
        r = lax.rsqrt(jnp.mean(xc * xc, axis=-1, keepdims=True) + NORM_EPS)
        yl = xc * r * g_ref[...] + b_ref[...]
        o_ref[...] = (yl * _sig(yl)).astype(o_ref.dtype)

    row = pl.BlockSpec((tm, w), lambda i: (i, 0))
    vec = pl.BlockSpec((1, w), lambda i: (0, 0))
    return pl.pallas_call(body, grid=(m // tm,), in_specs=[row, vec, vec], out_specs=row,
                          out_shape=jax.ShapeDtypeStruct((m, w), BF16), name="ln_silu_fwd",
                          compiler_params=_cp(1))(c, lg, lb)


def _ln_silu_bwd(c, lg, lb, ds, tm):
    m, w = c.shape

    def body(c_ref, g_ref, b_ref, ds_ref, dc_ref, dg_ref, db_ref):
        cv = c_ref[...]
        xc = cv - jnp.mean(cv, axis=-1, keepdims=True)
        r = lax.rsqrt(jnp.mean(xc * xc, axis=-1, keepdims=True) + NORM_EPS)
        nv = xc * r
        yl = nv * g_ref[...] + b_ref[...]
        sg = _sig(yl)
        dyl = ds_ref[...] * (sg * (1.0 + yl * (1.0 - sg)))
        dn = dyl * g_ref[...]
        dc = r * (dn - jnp.mean(dn, axis=-1, keepdims=True) - nv * jnp.mean(dn * nv, axis=-1, keepdims=True))
        dc_ref[...] = dc

        @pl.when(pl.program_id(0) == 0)
        def _():
            dg_ref[...] = jnp.zeros_like(dg_ref)
            db_ref[...] = jnp.zeros_like(db_ref)

        dg_ref[...] += jnp.sum(dyl * nv, axis=0, keepdims=True)
        db_ref[...] += jnp.sum(dyl, axis=0, keepdims=True)

    row = pl.BlockSpec((tm, w), lambda i: (i, 0))
    vec = pl.BlockSpec((1, w), lambda i: (0, 0))
    return pl.pallas_call(
        body, grid=(m // tm,), in_specs=[row, vec, vec, row], out_specs=(row, vec, vec),
        out_shape=(jax.ShapeDtypeStruct((m, w), F32), jax.ShapeDtypeStruct((1, w), F32),
                   jax.ShapeDtypeStruct((1, w), F32)),
        name="ln_silu_bwd", compiler_params=_cp(1))(c, lg, lb, ds)


CONV_MARGIN = 16
CONV_ROWS = 128


def _conv_fwd(proj, dw, bias, *, c_conv, off_a, nb, lp, l_valid):
    kw = dw.shape[0]
    half = kw // 2
    cw = LANE
    mg = CONV_MARGIN
    assert half <= mg - 1 and lp % CONV_ROWS == 0

    def body(a_ref, g_ref, w_ref, b_ref, c_ref, zp):
        t = lax.broadcasted_iota(jnp.int32, (lp, cw), 0)
        z = jnp.where(t < l_valid, a_ref[...] * _sig(g_ref[...]), 0.0)
        zp[0:mg, :] = jnp.zeros((mg, cw), F32)
        zp[mg + lp:mg + lp + mg, :] = jnp.zeros((mg, cw), F32)
        zp[mg:mg + lp, :] = z
        for r0 in range(0, lp, CONV_ROWS):
            acc = jnp.zeros((CONV_ROWS, cw), F32) + b_ref[...]
            for k in range(kw):
                s0 = mg + r0 + k - half
                acc = acc + w_ref[k:k + 1, :] * zp[s0:s0 + CONV_ROWS, :]
            c_ref[r0:r0 + CONV_ROWS, :] = acc

    ja, jg = _cb(off_a, cw), _cb(off_a + c_conv, cw)
    return pl.pallas_call(
        body, grid=(nb, c_conv // cw),
        in_specs=[pl.BlockSpec((lp, cw), lambda b, j: (b, ja + j)), pl.BlockSpec((lp, cw), lambda b, j: (b, jg + j)),
                  pl.BlockSpec((kw, cw), lambda b, j: (0, j)), pl.BlockSpec((1, cw), lambda b, j: (0, j))],
        out_specs=pl.BlockSpec((lp, cw), lambda b, j: (b, j)),
        out_shape=jax.ShapeDtypeStruct((nb * lp, c_conv), F32),
        scratch_shapes=[pltpu.VMEM((lp + 2 * mg, cw), F32)], name="conv_fwd", compiler_params=_cp(2))(proj, proj, dw, bias)


def _conv_bwd(proj, dc, dw, *, c_conv, off_a, nb, lp, l_valid):
    kw = dw.shape[0]
    half = kw // 2
    cw = LANE
    mg = CONV_MARGIN

    def body(a_ref, g_ref, dc_ref, w_ref, da_ref, dg_ref, ddw_ref, dcb_ref, zp, dcp):
        t = lax.broadcasted_iota(jnp.int32, (lp, cw), 0)
        z = jnp.where(t < l_valid, a_ref[...] * _sig(g_ref[...]), 0.0)
        for buf in (zp, dcp):
            buf[0:mg, :] = jnp.zeros((mg, cw), F32)
            buf[mg + lp:mg + lp + mg, :] = jnp.zeros((mg, cw), F32)
        zp[mg:mg + lp, :] = z
        dcv = dc_ref[...]
        dcp[mg:mg + lp, :] = dcv

        @pl.when(pl.program_id(1) == 0)
        def _():
            ddw_ref[...] = jnp.zeros_like(ddw_ref)
            dcb_ref[...] = jnp.zeros_like(dcb_ref)

        dcb_ref[...] += jnp.sum(dcv, axis=0, keepdims=True)
        for r0 in range(0, lp, CONV_ROWS):
            acc = jnp.zeros((CONV_ROWS, cw), F32)
            for k in range(kw):
                s0 = mg + r0 - k + half
                acc = acc + w_ref[k:k + 1, :] * dcp[s0:s0 + CONV_ROWS, :]
            tc = lax.broadcasted_iota(jnp.int32, (CONV_ROWS, cw), 0) + r0
            dz = jnp.where(tc < l_valid, acc, 0.0)
            sg = _sig(g_ref[r0:r0 + CONV_ROWS, :])
            da_ref[r0:r0 + CONV_ROWS, :] = (dz * sg).astype(da_ref.dtype)
            dg_ref[r0:r0 + CONV_ROWS, :] = (dz * a_ref[r0:r0 + CONV_ROWS, :] * sg * (1.0 - sg)).astype(dg_ref.dtype)
        for k in range(kw):
            acc = jnp.zeros((CONV_ROWS, cw), F32)
            for r0 in range(0, lp, CONV_ROWS):
                s0 = mg + r0 + k - half
                acc = acc + dc_ref[r0:r0 + CONV_ROWS, :] * zp[s0:s0 + CONV_ROWS, :]
            ddw_ref[k:k + 1, :] += jnp.sum(acc, axis=0, keepdims=True)

    ja, jg = _cb(off_a, cw), _cb(off_a + c_conv, cw)
    seq = pl.BlockSpec((lp, cw), lambda j, b: (b, j))
    return pl.pallas_call(
        body, grid=(c_conv // cw, nb),
        in_specs=[pl.BlockSpec((lp, cw), lambda j, b: (b, ja + j)), pl.BlockSpec((lp, cw), lambda j, b: (b, jg + j)),
                  seq, pl.BlockSpec((kw, cw), lambda j, b: (0, j))],
        out_specs=(seq, seq, pl.BlockSpec((kw, cw), lambda j, b: (0, j)), pl.BlockSpec((1, cw), lambda j, b: (0, j))),
        out_shape=(jax.ShapeDtypeStruct((nb * lp, c_conv), BF16), jax.ShapeDtypeStruct((nb * lp, c_conv), BF16),
                   jax.ShapeDtypeStruct((kw, c_conv), F32), jax.ShapeDtypeStruct((1, c_conv), F32)),
        scratch_shapes=[pltpu.VMEM((lp + 2 * mg, cw), F32), pltpu.VMEM((lp + 2 * mg, cw), F32)],
        name="conv_bwd", compiler_params=_cp(2))(proj, proj, dc, dw)


def _swap_halves(x, half):
    fwd = pltpu.roll(x, LANE - half, axis=1)
    bwd = pltpu.roll(x, half, axis=1)
    lane = lax.broadcasted_iota(jnp.int32, x.shape, 1)
    return jnp.where((lane & (2 * half - 1)) < half, fwd, bwd)


def _rope(x, cos, sin, half):
    return x * cos + _swap_halves(x, half) * sin


def _rope_t(dy, cos, sin, half):
    return dy * cos + _swap_halves(dy * sin, half)


def _rope_tables(lp, l_valid, half):
    t = jnp.arange(lp)
    n = jnp.clip(t - N_META, 0, None)
    real = (t >= N_META) & (t < l_valid)
    row = jnp.where(real, n // GRID_W, 0).astype(F32)
    col = jnp.where(real, n % GRID_W, 0).astype(F32)
    inv = ROPE_THETA ** (-jnp.arange(half, dtype=F32) / half)
    ar, ac = row[:, None] * inv[None, :], col[:, None] * inv[None, :]
    cos = jnp.concatenate([jnp.cos(ar), jnp.cos(ar), jnp.cos(ac), jnp.cos(ac)], axis=1)
    sin = jnp.concatenate([-jnp.sin(ar), jnp.sin(ar), -jnp.sin(ac), jnp.sin(ac)], axis=1)
    padw = LANE - 4 * half
    if padw:
        cos = jnp.pad(cos, ((0, 0), (0, padw)))
        sin = jnp.pad(sin, ((0, 0), (0, padw)))
    return cos.astype(F32), sin.astype(F32)


def _gqa_prep_fwd(proj, qg, kg, cos, sin, *, off_q, hq, hkv, nb, lp, tm, q_scale):
    hd = HEAD_DIM
    half = hd // 4
    wq, wkv = hq * hd, 2 * hkv * hd
    nt = lp // tm

    def body(q_ref, kv_ref, qg_ref, kg_ref, cos_ref, sin_ref, qo_ref, ko_ref, vo_ref):
        cosv, sinv = cos_ref[...], sin_ref[...]

        def norm_rope(xh, g):
            r = lax.rsqrt(jnp.mean(xh * xh, axis=-1, keepdims=True) + NORM_EPS)
            return _rope(xh * r * g, cosv, sinv, half)

        for h in range(hq):
            qo_ref[h] = (norm_rope(q_ref[:, h * hd:(h + 1) * hd], qg_ref[...]) * q_scale).astype(BF16)
        for h in range(hkv):
            ko_ref[h] = norm_rope(kv_ref[:, h * hd:(h + 1) * hd], kg_ref[...]).astype(BF16)
            vo_ref[h] = kv_ref[:, (hkv + h) * hd:(hkv + h + 1) * hd].astype(BF16)

    jq, jkv = _cb(off_q, wq), _cb(off_q + wq, wkv)
    tab = pl.BlockSpec((tm, LANE), lambda b, i: (i, 0))
    vec = pl.BlockSpec((1, hd), lambda b, i: (0, 0))

    def heads(h):
        return pl.BlockSpec((None, h, tm, hd), lambda b, i: (b, 0, i, 0))

    return pl.pallas_call(
        body, grid=(nb, nt),
        in_specs=[pl.BlockSpec((tm, wq), lambda b, i: (b * nt + i, jq)),
                  pl.BlockSpec((tm, wkv), lambda b, i: (b * nt + i, jkv)), vec, vec, tab, tab],
        out_specs=(heads(hq), heads(hkv), heads(hkv)),
        out_shape=(jax.ShapeDtypeStruct((nb, hq, lp, hd), BF16), jax.ShapeDtypeStruct((nb, hkv, lp, hd), BF16),
                   jax.ShapeDtypeStruct((nb, hkv, lp, hd), BF16)),
        name="gqa_prep_fwd", compiler_params=_cp(2))(proj, proj, qg, kg, cos, sin)


def _gqa_prep_bwd(proj, qg, kg, cos, sin, dqh, dkh, dvh, *, off_q, hq, hkv, nb, lp, tm, q_scale):
    hd = HEAD_DIM
    half = hd // 4
    wq, wkv = hq * hd, 2 * hkv * hd
    nt = lp // tm

    def body(q_ref, kv_ref, qg_ref, kg_ref, cos_ref, sin_ref, dqh_ref, dkh_ref, dvh_ref,
             dq_ref, dkv_ref, dqg_ref, dkg_ref):
        cosv, sinv = cos_ref[...], sin_ref[...]

        def back(xh, g, dyh):
            dn = _rope_t(dyh, cosv, sinv, half)
            r = lax.rsqrt(jnp.mean(xh * xh, axis=-1, keepdims=True) + NORM_EPS)
            t = dn * g
            dx = r * t - xh * (r * r * r) * jnp.mean(t * xh, axis=-1, keepdims=True)
            return dx, jnp.sum(dn * xh * r, axis=0, keepdims=True)

        first = (pl.program_id(0) == 0) & (pl.program_id(1) == 0)

        @pl.when(first)
        def _():
            dqg_ref[...] = jnp.zeros_like(dqg_ref)
            dkg_ref[...] = jnp.zeros_like(dkg_ref)

        gq = jnp.zeros((1, hd), F32)
        for h in range(hq):
            dx, dg = back(q_ref[:, h * hd:(h + 1) * hd], qg_ref[...], dqh_ref[h] * q_scale)
            dq_ref[:, h * hd:(h + 1) * hd] = dx.astype(dq_ref.dtype)
            gq = gq + dg
        dqg_ref[...] += gq
        gk = jnp.zeros((1, hd), F32)
        for h in range(hkv):
            dx, dg = back(kv_ref[:, h * hd:(h + 1) * hd], kg_ref[...], dkh_ref[h])
            dkv_ref[:, h * hd:(h + 1) * hd] = dx.astype(dkv_ref.dtype)
            dkv_ref[:, (hkv + h) * hd:(hkv + h + 1) * hd] = dvh_ref[h].astype(dkv_ref.dtype)
            gk = gk + dg
        dkg_ref[...] += gk

    jq, jkv = _cb(off_q, wq), _cb(off_q + wq, wkv)
    tab = pl.BlockSpec((tm, LANE), lambda b, i: (i, 0))
    vec = pl.BlockSpec((1, hd), lambda b, i: (0, 0))

    def heads(h):
        return pl.BlockSpec((None, h, tm, hd), lambda b, i: (b, 0, i, 0))

    return pl.pallas_call(
        body, grid=(nb, nt),
        in_specs=[pl.BlockSpec((tm, wq), lambda b, i: (b * nt + i, jq)),
                  pl.BlockSpec((tm, wkv), lambda b, i: (b * nt + i, jkv)), vec, vec, tab, tab,
                  heads(hq), heads(hkv), heads(hkv)],
        out_specs=(pl.BlockSpec((tm, wq), lambda b, i: (b * nt + i, 0)),
                   pl.BlockSpec((tm, wkv), lambda b, i: (b * nt + i, 0)), vec, vec),
        out_shape=(jax.ShapeDtypeStruct((nb * lp, wq), BF16), jax.ShapeDtypeStruct((nb * lp, wkv), BF16),
                   jax.ShapeDtypeStruct((1, hd), F32), jax.ShapeDtypeStruct((1, hd), F32)),
        name="gqa_prep_bwd", compiler_params=_cp(2))(proj, proj, qg, kg, cos, sin, dqh, dkh, dvh)


def _mla_prep_fwd(qm, kvm, proj, cos, sin, *, off_kpe, nh, nb, lp, tm, q_scale):
    hd = HEAD_DIM
    half = MLA_ROPE_DIM // 4
    nt = lp // tm
    wh = nh * hd

    def body(qm_ref, kvm_ref, kpe_ref, cos_ref, sin_ref, qo_ref, ko_ref, vo_ref):
        cosv, sinv = cos_ref[...], sin_ref[...]
        kpe = _rope(kpe_ref[...], cosv, sinv, half).astype(BF16)
        for h in range(nh):
            qo_ref[h, :, 0:hd] = (qm_ref[:, h * hd:(h + 1) * hd] * q_scale).astype(BF16)
            qo_ref[h, :, hd:2 * hd] = (_rope(qm_ref[:, wh + h * hd:wh + (h + 1) * hd], cosv, sinv, half)
                                       * q_scale).astype(BF16)
            ko_ref[h, :, 0:hd] = kvm_ref[:, h * hd:(h + 1) * hd].astype(BF16)
            ko_ref[h, :, hd:2 * hd] = kpe
            vo_ref[h] = kvm_ref[:, wh + h * hd:wh + (h + 1) * hd].astype(BF16)

    tab = pl.BlockSpec((tm, LANE), lambda b, i: (i, 0))
    wide = pl.BlockSpec((tm, 2 * wh), lambda b, i: (b * nt + i, 0))
    jk = _cb(off_kpe, LANE)
    return pl.pallas_call(
        body, grid=(nb, nt),
        in_specs=[wide, wide, pl.BlockSpec((tm, LANE), lambda b, i: (b * nt + i, jk)), tab, tab],
        out_specs=(pl.BlockSpec((None, nh, tm, 2 * hd), lambda b, i: (b, 0, i, 0)),
                   pl.BlockSpec((None, nh, tm, 2 * hd), lambda b, i: (b, 0, i, 0)),
                   pl.BlockSpec((None, nh, tm, hd), lambda b, i: (b, 0, i, 0))),
        out_shape=(jax.ShapeDtypeStruct((nb, nh, lp, 2 * hd), BF16), jax.ShapeDtypeStruct((nb, nh, lp, 2 * hd), BF16),
                   jax.ShapeDtypeStruct((nb, nh, lp, hd), BF16)),
        name="mla_prep_fwd", compiler_params=_cp(2))(qm, kvm, proj, cos, sin)


def _mla_prep_bwd(dqc, dkc, dvc, cos, sin, *, nh, nb, lp, tm, q_scale):
    hd = HEAD_DIM
    half = MLA_ROPE_DIM // 4
    nt = lp // tm
    wh = nh * hd

    def body(dq_ref, dk_ref, dv_ref, cos_ref, sin_ref, dqm_ref, dkvm_ref, dkpe_ref):
        cosv, sinv = cos_ref[...], sin_ref[...]
        acc = jnp.zeros((tm, hd), F32)
        for h in range(nh):
            dqm_ref[:, h * hd:(h + 1) * hd] = (dq_ref[h, :, 0:hd] * q_scale).astype(BF16)
            dqm_ref[:, wh + h * hd:wh + (h + 1) * hd] = (_rope_t(dq_ref[h, :, hd:2 * hd], cosv, sinv, half)
                                                         * q_scale).astype(BF16)
            dkvm_ref[:, h * hd:(h + 1) * hd] = dk_ref[h, :, 0:hd].astype(BF16)
            dkvm_ref[:, wh + h * hd:wh + (h + 1) * hd] = dv_ref[h].astype(BF16)
            acc = acc + dk_ref[h, :, hd:2 * hd]
        dkpe_ref[...] = _rope_t(acc, cosv, sinv, half).astype(BF16)

    tab = pl.BlockSpec((tm, LANE), lambda b, i: (i, 0))
    wide = pl.BlockSpec((tm, 2 * wh), lambda b, i: (b * nt + i, 0))
    return pl.pallas_call(
        body, grid=(nb, nt),
        in_specs=[pl.BlockSpec((None, nh, tm, 2 * hd), lambda b, i: (b, 0, i, 0)),
                  pl.BlockSpec((None, nh, tm, 2 * hd), lambda b, i: (b, 0, i, 0)),
                  pl.BlockSpec((None, nh, tm, hd), lambda b, i: (b, 0, i, 0)), tab, tab],
        out_specs=(wide, wide, pl.BlockSpec((tm, LANE), lambda b, i: (b * nt + i, 0))),
        out_shape=(jax.ShapeDtypeStruct((nb * lp, 2 * wh), BF16), jax.ShapeDtypeStruct((nb * lp, 2 * wh), BF16),
                   jax.ShapeDtypeStruct((nb * lp, LANE), BF16)),
        name="mla_prep_bwd", compiler_params=_cp(2))(dqc, dkc, dvc, cos, sin)


def _softmax_parts(s, lp, l_valid):
    tw = -(-(lp - l_valid) // LANE) * LANE
    main, tail = s[:, :lp - tw], s[:, lp - tw:]
    col = lax.broadcasted_iota(jnp.int32, tail.shape, 1) + (lp - tw)
    tail = jnp.where(col < l_valid, tail, -1e30)
    m = jnp.maximum(jnp.max(main, axis=-1, keepdims=True), jnp.max(tail, axis=-1, keepdims=True))
    pm, pt = jnp.exp(main - m), jnp.exp(tail - m)
    den = jnp.sum(pm, axis=-1, keepdims=True) + jnp.sum(pt, axis=-1, keepdims=True)
    return pm, pt, den, tw


def _attn_fwd(q, k, v, *, l_valid, name):
    nb, hq, lp, dk = q.shape
    hkv, dv = k.shape[1], v.shape[3]
    grp = hq // hkv
    tq = _tile(lp, 2176, 128)
    sub = tq // 8
    nq = lp // tq

    def body(q_ref, k_ref, v_ref, o_ref, p_s):
        for r0 in range(0, tq, sub):
            rows = slice(r0, r0 + sub)
            s = lax.dot_general(q_ref[rows, :], k_ref[...], NT, preferred_element_type=F32)
            pm, pt, den, tw = _softmax_parts(s, lp, l_valid)
            p_s[rows, :lp - tw] = pm.astype(BF16)
            p_s[rows, lp - tw:] = pt.astype(BF16)
            o = lax.dot_general(p_s[rows, :], v_ref[...], NN, preferred_element_type=F32)
            o_ref[rows, :] = (o / den).astype(o_ref.dtype)

    return pl.pallas_call(
        body, grid=(nb, hq, nq),
        in_specs=[pl.BlockSpec((None, None, tq, dk), lambda b, h, i: (b, h, i, 0)),
                  pl.BlockSpec((None, None, lp, dk), lambda b, h, i: (b, h // grp, 0, 0)),
                  pl.BlockSpec((None, None, lp, dv), lambda b, h, i: (b, h // grp, 0, 0))],
        out_specs=pl.BlockSpec((tq, dv), lambda b, h, i: (b * nq + i, h)),
        out_shape=jax.ShapeDtypeStruct((nb * lp, hq * dv), BF16), scratch_shapes=[pltpu.VMEM((tq, lp), BF16)],
        name=name, compiler_params=_cp(3))(q, k, v)


def _attn_bwd(q, k, v, do, *, l_valid, name):
    nb, hq, lp, dk = q.shape
    hkv, dv = k.shape[1], v.shape[3]
    grp = hq // hkv
    tq = _tile(lp, 1088, 64)
    sub = tq // 4
    nq = lp // tq

    def body(q_ref, k_ref, v_ref, do_ref, dq_ref, dk_ref, dv_ref, p_s, ds_s):
        for r0 in range(0, tq, sub):
            rows = slice(r0, r0 + sub)
            s = lax.dot_general(q_ref[rows, :], k_ref[...], NT, preferred_element_type=F32)
            pm, pt, den, tw = _softmax_parts(s, lp, l_valid)
            inv = 1.0 / den
            pm, pt = pm * inv, pt * inv
            dp = lax.dot_general(do_ref[rows, :], v_ref[...], NT, preferred_element_type=F32)
            dpm, dpt = dp[:, :lp - tw], dp[:, lp - tw:]
            dd = jnp.sum(pm * dpm, axis=-1, keepdims=True) + jnp.sum(pt * dpt, axis=-1, keepdims=True)
            p_s[rows, :lp - tw] = pm.astype(BF16)
            p_s[rows, lp - tw:] = pt.astype(BF16)
            ds_s[rows, :lp - tw] = (pm * (dpm - dd)).astype(BF16)
            ds_s[rows, lp - tw:] = (pt * (dpt - dd)).astype(BF16)
            dq_ref[rows, :] = lax.dot_general(ds_s[rows, :], k_ref[...], NN, preferred_element_type=F32)

        @pl.when((pl.program_id(2) == 0) & (pl.program_id(3) == 0))
        def _():
            dk_ref[...] = jnp.zeros_like(dk_ref)
            dv_ref[...] = jnp.zeros_like(dv_ref)

        dk_ref[...] += lax.dot_general(ds_s[...], q_ref[...], TN, preferred_element_type=F32)
        dv_ref[...] += lax.dot_general(p_s[...], do_ref[...], TN, preferred_element_type=F32)

    return pl.pallas_call(
        body, grid=(nb, hkv, grp, nq), scratch_shapes=[pltpu.VMEM((tq, lp), BF16), pltpu.VMEM((tq, lp), BF16)],
        in_specs=[pl.BlockSpec((None, None, tq, dk), lambda b, h, g, i: (b, h * grp + g, i, 0)),
                  pl.BlockSpec((None, None, lp, dk), lambda b, h, g, i: (b, h, 0, 0)),
                  pl.BlockSpec((None, None, lp, dv), lambda b, h, g, i: (b, h, 0, 0)),
                  pl.BlockSpec((tq, dv), lambda b, h, g, i: (b * nq + i, h * grp + g))],
        out_specs=(pl.BlockSpec((None, None, tq, dk), lambda b, h, g, i: (b, h * grp + g, i, 0)),
                   pl.BlockSpec((None, None, lp, dk), lambda b, h, g, i: (b, h, 0, 0)),
                   pl.BlockSpec((None, None, lp, dv), lambda b, h, g, i: (b, h, 0, 0))),
        out_shape=(jax.ShapeDtypeStruct((nb, hq, lp, dk), F32), jax.ShapeDtypeStruct((nb, hkv, lp, dk), F32),
                   jax.ShapeDtypeStruct((nb, hkv, lp, dv), F32)),
        name=name, compiler_params=_cp(4))(q, k, v, do)


def _gate_fwd(proj, gb, ya, yb, yc, *, d, tm):
    m = ya.shape[0]
    cw = _tile(d, 512, LANE)
    nj = d // cw

    def body(g0, g1, g2, b0, b1, b2, ya_ref, yb_ref, yc_ref, o_ref):
        o = (_sig(g0[...] + b0[...]) * ya_ref[...] + _sig(g1[...] + b1[...]) * yb_ref[...]
             + _sig(g2[...] + b2[...]) * yc_ref[...])
        o_ref[...] = o.astype(o_ref.dtype)

    def gl(br):
        return pl.BlockSpec((tm, cw), lambda j, i: (i, br * nj + j))

    def gbs(br):
        return pl.BlockSpec((1, cw), lambda j, i: (0, br * nj + j))

    blk = pl.BlockSpec((tm, cw), lambda j, i: (i, j))
    return pl.pallas_call(
        body, grid=(nj, m // tm), in_specs=[gl(0), gl(1), gl(2), gbs(0), gbs(1), gbs(2), blk, blk, blk],
        out_specs=blk, out_shape=jax.ShapeDtypeStruct((m, d), BF16), name="gate_fwd",
        compiler_params=_cp(2))(proj, proj, proj, gb, gb, gb, ya, yb, yc)


def _gate_bwd(dm, proj, gb, ya, yb, yc, *, d, tm):
    m = ya.shape[0]
    cw = _tile(d, 512, LANE)
    nj = d // cw

    def body(dm_ref, g0, g1, g2, b0, b1, b2, ya_ref, yb_ref, yc_ref,
             dya, dyb, dyc, dg0, dg1, dg2, db0, db1, db2):
        dmv = dm_ref[...]

        @pl.when(pl.program_id(1) == 0)
        def _():
            for r in (db0, db1, db2):
                r[...] = jnp.zeros_like(r)

        for g, b, y, dy, dgl, db in ((g0, b0, ya_ref, dya, dg0, db0), (g1, b1, yb_ref, dyb, dg1, db1),
                                     (g2, b2, yc_ref, dyc, dg2, db2)):
            sg = _sig(g[...] + b[...])
            dy[...] = (sg * dmv).astype(dy.dtype)
            dl = dmv * y[...] * sg * (1.0 - sg)
            dgl[...] = dl.astype(dgl.dtype)
            db[...] += jnp.sum(dl, axis=0, keepdims=True)

    def gl(br):
        return pl.BlockSpec((tm, cw), lambda j, i: (i, br * nj + j))

    def gbs(br):
        return pl.BlockSpec((1, cw), lambda j, i: (0, br * nj + j))

    blk = pl.BlockSpec((tm, cw), lambda j, i: (i, j))
    vec = pl.BlockSpec((1, cw), lambda j, i: (0, j))
    act = jax.ShapeDtypeStruct((m, d), BF16)
    vsh = jax.ShapeDtypeStruct((1, d), F32)
    return pl.pallas_call(
        body, grid=(nj, m // tm), in_specs=[blk, gl(0), gl(1), gl(2), gbs(0), gbs(1), gbs(2), blk, blk, blk],
        out_specs=(blk,) * 6 + (vec,) * 3, out_shape=(act,) * 6 + (vsh,) * 3, name="gate_bwd",
        compiler_params=_cp(2))(dm, proj, proj, proj, gb, gb, gb, ya, yb, yc)


def _ffn_up_fwd(vn, wg_t, wu_t):
    m, d = vn.shape
    f = wg_t.shape[0]
    tm = _tile(m, 1088, 16)
    tn = _tile(f, 512, LANE)

    def body(v_ref, wg_ref, wu_ref, g_ref, u_ref, a_ref):
        v = v_ref[...]
        g = lax.dot_general(v, wg_ref[...], NT, preferred_element_type=F32)
        u = lax.dot_general(v, wu_ref[...], NT, preferred_element_type=F32)
        g_ref[...] = g
        u_ref[...] = u
        a_ref[...] = (g * _sig(g) * u).astype(a_ref.dtype)

    row = pl.BlockSpec((tm, d), lambda i, j: (i, 0))
    wsp = pl.BlockSpec((tn, d), lambda i, j: (j, 0))
    blk = pl.BlockSpec((tm, tn), lambda i, j: (i, j))
    return pl.pallas_call(
        body, grid=(m // tm, f // tn), in_specs=[row, wsp, wsp], out_specs=(blk, blk, blk),
        out_shape=(jax.ShapeDtypeStruct((m, f), F32), jax.ShapeDtypeStruct((m, f), F32),
                   jax.ShapeDtypeStruct((m, f), BF16)),
        name="ffn_up_fwd", compiler_params=_cp(2))(vn, wg_t, wu_t)


def _ffn_down_bwd(dh, wd, gt, up, deps):
    m, d = dh.shape
    f = wd.shape[0]
    tm = _tile(m, 1088, 16)
    tn = _tile(f, 512, LANE)
    nd = len(deps)

    def body(*refs):
        dh_ref, wd_ref, g_ref, u_ref = refs[:4]
        dg_ref, du_ref = refs[4 + nd:]
        da = lax.dot_general(dh_ref[...].astype(BF16), wd_ref[...], NT, preferred_element_type=F32)
        g = g_ref[...]
        sg = _sig(g)
        dg_ref[...] = (da * u_ref[...] * sg * (1.0 + g * (1.0 - sg))).astype(dg_ref.dtype)
        du_ref[...] = (da * g * sg).astype(du_ref.dtype)

    row = pl.BlockSpec((tm, d), lambda i, j: (i, 0))
    wsp = pl.BlockSpec((tn, d), lambda i, j: (j, 0))
    blk = pl.BlockSpec((tm, tn), lambda i, j: (i, j))
    sh = jax.ShapeDtypeStruct((m, f), BF16)
    return pl.pallas_call(
        body, grid=(m // tm, f // tn), in_specs=[row, wsp, blk, blk] + [pl.BlockSpec(memory_space=pl.ANY)] * nd,
        out_specs=(blk, blk), out_shape=(sh, sh), name="ffn_down_bwd", compiler_params=_cp(2))(dh, wd, gt, up, *deps)


def _loss_head(h, gf, tgt, *, lp, l_valid, tm):
    m, d = h.shape
    nt = lp // tm

    def body(h_ref, g_ref, t_ref, dh_ref, dg_ref, loss_ref):
        i = pl.program_id(0)
        xv = h_ref[...]
        r = lax.rsqrt(jnp.mean(xv * xv, axis=-1, keepdims=True) + NORM_EPS)
        y = xv * r * g_ref[...]
        t = lax.broadcasted_iota(jnp.int32, (tm, d), 0) + (i % nt) * tm
        err = jnp.where((t >= N_META) & (t < l_valid), y - t_ref[...], 0.0)
        dy = err * (1.0 / d)
        tg = dy * g_ref[...]
        dh_ref[...] = r * tg - xv * (r * r * r) * jnp.mean(tg * xv, axis=-1, keepdims=True)

        @pl.when(i == 0)
        def _():
            dg_ref[...] = jnp.zeros_like(dg_ref)
            loss_ref[...] = jnp.zeros_like(loss_ref)

        dg_ref[...] += jnp.sum(dy * xv * r, axis=0, keepdims=True)
        sq = jnp.sum(jnp.sum(err * err, axis=-1, keepdims=True), axis=0, keepdims=True)
        loss_ref[...] += jnp.zeros((1, LANE), F32) + sq * (0.5 / d)

    row = pl.BlockSpec((tm, d), lambda i: (i, 0))
    vec = pl.BlockSpec((1, d), lambda i: (0, 0))
    return pl.pallas_call(
        body, grid=(m // tm,), in_specs=[row, vec, row],
        out_specs=(row, vec, pl.BlockSpec((1, LANE), lambda i: (0, 0))),
        out_shape=(jax.ShapeDtypeStruct((m, d), F32), jax.ShapeDtypeStruct((1, d), F32),
                   jax.ShapeDtypeStruct((1, LANE), F32)),
        name="loss_head", compiler_params=_cp(1))(h, gf, tgt)


def _adamw(w, g, m, v, name):
    rows, cols = w.shape
    tr = rows
    if rows % 8 == 0:
        tr = _tile(rows, max(8, (1 << 18) // cols // 8 * 8), 8)
    c1 = 1.0 / (1.0 - ADAM_B1 ** ADAM_STEP)
    c2 = 1.0 / (1.0 - ADAM_B2 ** ADAM_STEP)

    def body(w_ref, g_ref, m_ref, v_ref, d_ref, mo_ref, vo_ref):
        gv = g_ref[...]
        mn = ADAM_B1 * m_ref[...] + (1.0 - ADAM_B1) * gv
        vn = ADAM_B2 * v_ref[...] + (1.0 - ADAM_B2) * (gv * gv)
        mo_ref[...] = mn
        vo_ref[...] = vn
        d_ref[...] = -ADAM_LR * ((mn * c1) / (jnp.sqrt(vn * c2) + ADAM_EPS) + ADAM_WD * w_ref[...])

    blk = pl.BlockSpec((tr, cols), lambda i: (i, 0))
    sh = jax.ShapeDtypeStruct((rows, cols), F32)
    return pl.pallas_call(body, grid=(rows // tr,), in_specs=[blk] * 4, out_specs=(blk,) * 3, out_shape=(sh,) * 3,
                          name=name, compiler_params=_cp(1))(w, g, m, v)


MESH = pl.DeviceIdType.MESH
ANY = pl.BlockSpec(memory_space=pl.ANY)


def _place():
    return lax.axis_index("x"), lax.axis_index("y"), lax.axis_index("c")


def _other_chips(x, y):
    return [(1 - x, y), (x, 1 - y), (1 - x, 1 - y)]


def _all_gather(xs, vmem, name):
    na = len(xs)

    def body(*refs):
        x_refs, out_refs = refs[:na], refs[na:2 * na]
        send_sems, recv_sems, local_sems = refs[2 * na:]
        x, y, c = _place()
        me, sibling = (x, y, c), (x, y, 1 - c)
        chips = _other_chips(x, y)

        def blk(p, px, py, pc):
            return out_refs[p].at[4 * px + 2 * py + pc]

        def copy(k, p, block, to, src=None):
            return pltpu.make_async_remote_copy(
                src_ref=blk(p, *block) if src is None else src, dst_ref=blk(p, *block),
                send_sem=send_sems.at[k * na + p], recv_sem=recv_sems.at[k * na + p], device_id=to,
                device_id_type=MESH)

        mine = [pltpu.make_async_copy(x_refs[p], blk(p, *me), local_sems.at[p]) for p in range(na)]
        for cp in mine:
            cp.start()
        first = [copy(0, p, me, sibling, src=x_refs[p]) for p in range(na)]
        first += [copy(1 + j, p, me, (*chip, c), src=x_refs[p]) for j, chip in enumerate(chips) for p in range(na)]
        for cp in first:
            cp.start()
        passed = []
        for j, chip in enumerate(chips):
            for p in range(na):
                copy(1 + j, p, (*chip, c), me).wait_recv()
                fwd = copy(4 + j, p, (*chip, c), sibling)
                fwd.start()
                passed.append(fwd)
        for p in range(na):
            copy(0, p, sibling, me).wait_recv()
        for j, chip in enumerate(chips):
            for p in range(na):
                copy(4 + j, p, (*chip, 1 - c), me).wait_recv()
        for cp in first + passed:
            cp.wait_send()
        for cp in mine:
            cp.wait()

    spec = pl.BlockSpec(memory_space=pltpu.VMEM) if vmem else ANY
    return pl.pallas_call(
        body, out_shape=[jax.ShapeDtypeStruct((N_DEV,) + a.shape, a.dtype) for a in xs],
        in_specs=[spec] * na, out_specs=[spec] * na,
        scratch_shapes=[pltpu.SemaphoreType.DMA((7 * na,)), pltpu.SemaphoreType.DMA((7 * na,)),
                        pltpu.SemaphoreType.DMA((na,))],
        name=name)(*xs)


def _swap_with_sibling(g4s):
    na = len(g4s)

    def body(*refs):
        g_refs, out_refs = refs[:na], refs[na:2 * na]
        send_sems, recv_sems = refs[2 * na:]
        x, y, c = _place()
        cps = [pltpu.make_async_remote_copy(src_ref=g_refs[p].at[:, 1 - c], dst_ref=out_refs[p],
                                            send_sem=send_sems.at[p], recv_sem=recv_sems.at[p],
                                            device_id=(x, y, 1 - c), device_id_type=MESH) for p in range(na)]
        for cp in cps:
            cp.start()
        for cp in cps:
            cp.wait()

    return pl.pallas_call(
        body, out_shape=[jax.ShapeDtypeStruct((g.shape[0],) + g.shape[2:], g.dtype) for g in g4s],
        in_specs=[ANY] * na, out_specs=[ANY] * na,
        scratch_shapes=[pltpu.SemaphoreType.DMA((na,)), pltpu.SemaphoreType.DMA((na,))], name="rs_sibling")(*g4s)


HBM = pl.BlockSpec(memory_space=pltpu.HBM)
SEM = pl.BlockSpec(memory_space=pltpu.SEMAPHORE)
EFFECT = pltpu.SideEffectType.DATAFLOW_SIDE_EFFECTING


def _in_hbm(a):
    return pltpu.with_memory_space_constraint(a, pltpu.HBM)


def _ici_copies(src_refs, land_refs, send_sems, recv_sems, gather):
    na = len(src_refs)
    x, y, c = _place()
    cps = []
    if gather == "sibling":
        for k, (px, py) in enumerate([(x, y)] + _other_chips(x, y)):
            for p in range(na):
                blk = src_refs[p].at[4 * px + 2 * py + c]
                cps.append(pltpu.make_async_remote_copy(
                    src_ref=blk, dst_ref=blk, send_sem=send_sems.at[k * na + p], recv_sem=recv_sems.at[k * na + p],
                    device_id=(x, y, 1 - c), device_id_type=MESH))
        return cps
    for j, (px, py) in enumerate(_other_chips(x, y)):
        for p in range(na):
            src = src_refs[p].at[4 * x + 2 * y + c] if gather else src_refs[p].at[2 * px + py]
            dst = src if gather else land_refs[p].at[j]
            cps.append(pltpu.make_async_remote_copy(
                src_ref=src, dst_ref=dst, send_sem=send_sems.at[j * na + p], recv_sem=recv_sems.at[j * na + p],
                device_id=(px, py, c), device_id_type=MESH))
    return cps


def _ici_start(srcs, land_shapes, deps, *, gather, name):
    na, nd = len(srcs), len(deps)
    lands = [lax.empty(s, a.dtype) for s, a in zip(land_shapes, srcs)]
    nb = na + len(lands)

    def body(*refs):
        src_refs, land_refs = refs[:na], refs[na:nb]
        send_sems, recv_sems = refs[nb + nd], refs[nb + nd + 1]
        token = refs[-1]
        for cp in _ici_copies(src_refs, land_refs, send_sems, recv_sems, gather):
            cp.start()
        token[...] = jnp.zeros_like(token)

    bufs = list(srcs) + lands
    n_sem = (4 if gather == "sibling" else 3) * na
    out = pl.pallas_call(
        body, name=name,
        out_shape=(pltpu.SemaphoreType.DMA((n_sem,)), pltpu.SemaphoreType.DMA((n_sem,)),
                   *[pltpu.HBM(a.shape, a.dtype) for a in bufs], jax.ShapeDtypeStruct((8, LANE), F32)),
        in_specs=[HBM] * nb + [ANY] * nd,
        out_specs=(SEM, SEM, *([HBM] * nb), pl.BlockSpec(memory_space=pltpu.VMEM)),
        input_output_aliases={p: 2 + p for p in range(nb)},
        compiler_params=pltpu.CompilerParams(has_side_effects=EFFECT),
    )(*[_in_hbm(a) for a in bufs], *deps)
    return out[0], out[1], list(out[2:2 + na]), list(out[2 + na:2 + nb]), out[-1]


def _ici_wait(send_sems, recv_sems, srcs, lands, after, *, gather, name):
    na, nd = len(srcs), len(after)
    nb = na + len(lands)

    def body(*refs):
        src_refs, land_refs = refs[:na], refs[na:nb]
        s_sems, r_sems = refs[nb], refs[nb + 1]
        for cp in _ici_copies(src_refs, land_refs, s_sems, r_sems, gather):
            cp.wait_send()
            cp.wait_recv()

    bufs = list(srcs) + list(lands)
    out = pl.pallas_call(
        body, name=name, out_shape=tuple(pltpu.HBM(a.shape, a.dtype) for a in bufs),
        in_specs=[HBM] * nb + [SEM, SEM] + [ANY] * nd, out_specs=tuple([HBM] * nb),
        input_output_aliases={p: p for p in range(nb)},
        compiler_params=pltpu.CompilerParams(has_side_effects=EFFECT),
    )(*bufs, send_sems, recv_sems, *after)
    return list(out[:na]), list(out[na:])


HI_MASK = -65536


def _pack_pairs_xla(a):
    half = a.shape[-1] // 2
    bits = lax.bitcast_convert_type(a.astype(BF16), jnp.uint16).astype(jnp.uint32)
    return lax.bitcast_convert_type((bits[..., half:] << 16) | bits[..., :half], F32)


def _split_pairs(words):
    wv = lax.bitcast_convert_type(words, jnp.int32)
    return lax.bitcast_convert_type(wv << 16, F32), lax.bitcast_convert_type(wv & HI_MASK, F32)


def _join_pairs(lo, hi):
    lo_b = lax.bitcast_convert_type(lo.astype(BF16).astype(F32), jnp.int32)
    hi_b = lax.bitcast_convert_type(hi.astype(BF16).astype(F32), jnp.int32)
    return lax.bitcast_convert_type((hi_b & HI_MASK) | lax.shift_right_logical(lo_b, 16), F32)


def _unpack(xp, *, name, out_rows=None, tr=None, src_fn=None, zero_fn=None):
    r, hw = xp.shape
    out_rows = out_rows or r
    tr = tr or _tile(out_rows, 512, 16)
    src = src_fn or (lambda i: i)
    k = _blocks_per_step(out_rows // tr) if src_fn is not None else 1

    def body(*refs):
        o_ref = refs[k]
        for j in range(k):
            lo, hi = _split_pairs(refs[j][...])
            if zero_fn is not None:
                z = zero_fn(pl.program_id(0) * k + j)
                lo, hi = jnp.where(z, 0.0, lo), jnp.where(z, 0.0, hi)
            o_ref[j * tr:(j + 1) * tr, :hw] = lo.astype(BF16)
            o_ref[j * tr:(j + 1) * tr, hw:] = hi.astype(BF16)

    return pl.pallas_call(
        body, grid=(out_rows // (k * tr),),
        in_specs=[pl.BlockSpec((tr, hw), lambda i, j=j: (src(i * k + j), 0)) for j in range(k)],
        out_specs=pl.BlockSpec((k * tr, 2 * hw), lambda i: (i, 0)),
        out_shape=jax.ShapeDtypeStruct((out_rows, 2 * hw), BF16), name=name, compiler_params=_cp(1))(*([xp] * k))


def _blocks_per_step(n_blocks):
    return max(k for k in range(1, 17) if n_blocks % k == 0)


def _gather_rows(xp, *, name, out_rows, tr, src_fn):
    _, hw = xp.shape
    k = _blocks_per_step(out_rows // tr)

    def body(*refs):
        for j in range(k):
            refs[k][j * tr:(j + 1) * tr, :] = refs[j][...]

    return pl.pallas_call(
        body, grid=(out_rows // (k * tr),),
        in_specs=[pl.BlockSpec((tr, hw), lambda i, j=j: (src_fn(i * k + j), 0)) for j in range(k)],
        out_specs=pl.BlockSpec((k * tr, hw), lambda i: (i, 0)),
        out_shape=jax.ShapeDtypeStruct((out_rows, hw), xp.dtype), name=name, compiler_params=_cp(1))(*([xp] * k))


def _matmul_tn_packed(a, b, name):
    (t, m), (t2, c) = a.shape, b.shape
    assert t == t2
    hw = c // 2
    tm = _tile(m, 1088, LANE)
    tn = _tile(hw, 512, LANE)
    tk = _tile(t, 2176, LANE)
    nk, nj = t // tk, hw // tn

    def body(a_ref, bl_ref, bh_ref, o_ref, acc_lo, acc_hi):
        kk = pl.program_id(2)

        @pl.when(kk == 0)
        def _():
            acc_lo[...] = jnp.zeros_like(acc_lo)
            acc_hi[...] = jnp.zeros_like(acc_hi)

        av = a_ref[...].astype(BF16)
        acc_lo[...] += lax.dot_general(av, bl_ref[...].astype(BF16), TN, preferred_element_type=F32)
        acc_hi[...] += lax.dot_general(av, bh_ref[...].astype(BF16), TN, preferred_element_type=F32)

        @pl.when(kk == nk - 1)
        def _():
            o_ref[...] = _join_pairs(acc_lo[...], acc_hi[...])

    return pl.pallas_call(
        body, grid=(m // tm, nj, nk),
        in_specs=[pl.BlockSpec((tk, tm), lambda i, j, kk: (kk, i)), pl.BlockSpec((tk, tn), lambda i, j, kk: (kk, j)),
                  pl.BlockSpec((tk, tn), lambda i, j, kk: (kk, nj + j))],
        out_specs=pl.BlockSpec((tm, tn), lambda i, j, kk: (i, j)), out_shape=jax.ShapeDtypeStruct((m, hw), F32),
        scratch_shapes=[pltpu.VMEM((tm, tn), F32), pltpu.VMEM((tm, tn), F32)], name=name,
        compiler_params=_cp(3))(a, b, b)


def _add_sibling(g4, recv, cidx, name):
    nchip, _, r, n = g4.shape
    tr = _tile(r, 256, 8)

    def body(c_ref, a_ref, b_ref, o_ref):
        alo, ahi = _split_pairs(a_ref[...])
        blo, bhi = _split_pairs(b_ref[...])
        o_ref[...] = _join_pairs(alo + blo, ahi + bhi)

    grid_spec = pltpu.PrefetchScalarGridSpec(
        num_scalar_prefetch=1, grid=(nchip, r // tr),
        in_specs=[pl.BlockSpec((None, None, tr, n), lambda k, i, c: (k, c[0], i, 0)),
                  pl.BlockSpec((None, tr, n), lambda k, i, c: (k, i, 0))],
        out_specs=pl.BlockSpec((None, tr, n), lambda k, i, c: (k, i, 0)))
    return pl.pallas_call(body, grid_spec=grid_spec, out_shape=jax.ShapeDtypeStruct((nchip, r, n), F32),
                          name=name, compiler_params=_cp(2))(cidx, g4, recv)


def _add_chips(p4, recv3, chip_idx, name):
    _, r, n = p4.shape
    tr = _tile(r, 256, 8)

    def body(k_ref, a_ref, b_ref, o_ref):
        lo, hi = _split_pairs(a_ref[...])
        for j in range(3):
            blo, bhi = _split_pairs(b_ref[j])
            lo, hi = lo + blo, hi + bhi
        o_ref[:, :n] = lo
        o_ref[:, n:] = hi

    grid_spec = pltpu.PrefetchScalarGridSpec(
        num_scalar_prefetch=1, grid=(r // tr,),
        in_specs=[pl.BlockSpec((None, tr, n), lambda i, k: (k[0], i, 0)),
                  pl.BlockSpec((3, tr, n), lambda i, k: (0, i, 0))],
        out_specs=pl.BlockSpec((tr, 2 * n), lambda i, k: (i, 0)))
    return pl.pallas_call(body, grid_spec=grid_spec, out_shape=jax.ShapeDtypeStruct((r, 2 * n), F32),
                          name=name, compiler_params=_cp(1))(chip_idx, p4, recv3)


def _all_reduce_small(xs):
    r, n = xs.shape

    def body(x_ref, o_ref, buf, send_sems, recv_sems):
        x, y, c = _place()
        me = 4 * x + 2 * y + c
        buf[me] = x_ref[...]
        peers = []
        for k in range(1, N_DEV):
            px = (1 - x) if (k >> 2) & 1 else x
            py = (1 - y) if (k >> 1) & 1 else y
            pc = (1 - c) if k & 1 else c
            peers.append((px, py, pc))
        sends = [pltpu.make_async_remote_copy(src_ref=x_ref, dst_ref=buf.at[me], send_sem=send_sems.at[k],
                                              recv_sem=recv_sems.at[k], device_id=peer, device_id_type=MESH)
                 for k, peer in enumerate(peers)]
        for cp in sends:
            cp.start()
        for k, (px, py, pc) in enumerate(peers):
            pltpu.make_async_remote_copy(src_ref=x_ref, dst_ref=buf.at[4 * px + 2 * py + pc],
                                         send_sem=send_sems.at[k], recv_sem=recv_sems.at[k],
                                         device_id=(px, py, pc), device_id_type=MESH).wait_recv()
        for cp in sends:
            cp.wait_send()
        acc = buf[0]
        for dv in range(1, N_DEV):
            acc = acc + buf[dv]
        o_ref[...] = acc

    vm = pl.BlockSpec(memory_space=pltpu.VMEM)
    return pl.pallas_call(
        body, out_shape=jax.ShapeDtypeStruct((r, n), F32), in_specs=[vm], out_specs=vm,
        scratch_shapes=[pltpu.VMEM((N_DEV, r, n), F32), pltpu.SemaphoreType.DMA((7,)), pltpu.SemaphoreType.DMA((7,))],
        name="all_reduce_small", compiler_params=pltpu.CompilerParams(vmem_limit_bytes=VMEM_LIMIT))(xs)


def _pack_rows(arrs, quantum):
    parts, sizes = [], []
    for a in arrs:
        f = a.reshape(-1)
        pad = (-f.shape[0]) % quantum
        if pad:
            f = jnp.pad(f, (0, pad))
        parts.append(f)
        sizes.append(f.shape[0])
    return jnp.concatenate(parts).reshape(-1, LANE), sizes


def _unpack_rows(buf, sizes, shapes):
    flat = buf.reshape(-1)
    out, off = [], 0
    for sz, sh in zip(sizes, shapes):
        n = math.prod(sh)
        out.append(flat[off:off + n].reshape(sh))
        off += sz
    return out


def kernel(x, meta_tokens, mix_norm_g, w_in, conv_dw, conv_b, conv_ln_g, conv_ln_b, w_conv_out, gqa_q_norm_g, gqa_k_norm_g, w_gqa_out, mla_q_norm_g, w_mla_uq, mla_kv_norm_g, w_mla_ukv, w_mla_out, gate_b, w_out, ffn_norm_g, w_ffn_gate, w_ffn_up, w_ffn_down, final_norm_g, loss_target, m_meta_tokens, m_mix_norm_g, m_w_in, m_conv_dw, m_conv_b, m_conv_ln_g, m_conv_ln_b, m_w_conv_out, m_gqa_q_norm_g, m_gqa_k_norm_g, m_w_gqa_out, m_mla_q_norm_g, m_w_mla_uq, m_mla_kv_norm_g, m_w_mla_ukv, m_w_mla_out, m_gate_b, m_w_out, m_ffn_norm_g, m_w_ffn_gate, m_w_ffn_up, m_w_ffn_down, m_final_norm_g, v_meta_tokens, v_mix_norm_g, v_w_in, v_conv_dw, v_conv_b, v_conv_ln_g, v_conv_ln_b, v_w_conv_out, v_gqa_q_norm_g, v_gqa_k_norm_g, v_w_gqa_out, v_mla_q_norm_g, v_w_mla_uq, v_mla_kv_norm_g, v_w_mla_ukv, v_w_mla_out, v_gate_b, v_w_out, v_ffn_norm_g, v_w_ffn_gate, v_w_ffn_up, v_w_ffn_down, v_final_norm_g):
    w = dict(meta_tokens=meta_tokens, mix_norm_g=mix_norm_g, w_in=w_in, conv_dw=conv_dw, conv_b=conv_b,
             conv_ln_g=conv_ln_g, conv_ln_b=conv_ln_b, w_conv_out=w_conv_out, gqa_q_norm_g=gqa_q_norm_g,
             gqa_k_norm_g=gqa_k_norm_g, w_gqa_out=w_gqa_out, mla_q_norm_g=mla_q_norm_g, w_mla_uq=w_mla_uq,
             mla_kv_norm_g=mla_kv_norm_g, w_mla_ukv=w_mla_ukv, w_mla_out=w_mla_out, gate_b=gate_b, w_out=w_out,
             ffn_norm_g=ffn_norm_g, w_ffn_gate=w_ffn_gate, w_ffn_up=w_ffn_up, w_ffn_down=w_ffn_down,
             final_norm_g=final_norm_g)
    mom1 = dict(zip(W_NAMES, (m_meta_tokens, m_mix_norm_g, m_w_in, m_conv_dw, m_conv_b, m_conv_ln_g, m_conv_ln_b, m_w_conv_out, m_gqa_q_norm_g, m_gqa_k_norm_g, m_w_gqa_out, m_mla_q_norm_g, m_w_mla_uq, m_mla_kv_norm_g, m_w_mla_ukv, m_w_mla_out, m_gate_b, m_w_out, m_ffn_norm_g, m_w_ffn_gate, m_w_ffn_up, m_w_ffn_down, m_final_norm_g)))
    mom2 = dict(zip(W_NAMES, (v_meta_tokens, v_mix_norm_g, v_w_in, v_conv_dw, v_conv_b, v_conv_ln_g, v_conv_ln_b, v_w_conv_out, v_gqa_q_norm_g, v_gqa_k_norm_g, v_w_gqa_out, v_mla_q_norm_g, v_w_mla_uq, v_mla_kv_norm_g, v_w_mla_ukv, v_w_mla_out, v_gate_b, v_w_out, v_ffn_norm_g, v_w_ffn_gate, v_w_ffn_up, v_w_ffn_down, v_final_norm_g)))

    nb, seq, d = x.shape
    depth = w_in.shape[0]
    c_conv = conv_b.shape[1]
    kw = conv_dw.shape[1]
    hd = HEAD_DIM
    hq = w_gqa_out.shape[1] // hd
    nh = w_mla_out.shape[1] // hd
    qr, kvr = mla_q_norm_g.shape[1], mla_kv_norm_g.shape[1]
    d_in = w_in.shape[2] * N_DEV
    hkv = (d_in - 2 * c_conv - hq * hd - qr - kvr - MLA_ROPE_DIM - N_BRANCH * d) // (2 * hd)
    l_valid = seq + N_META
    lp = -(-(l_valid + kw // 2) // LANE) * LANE
    m_tok = nb * lp
    mla_qk = hd + MLA_ROPE_DIM
    gqa_scale, mla_scale = 1.0 / math.sqrt(hd), 1.0 / math.sqrt(mla_qk)

    off_gate = 0
    off_conv = N_BRANCH * d
    off_q = off_conv + 2 * c_conv
    off_cq = off_q + (hq + 2 * hkv) * hd
    off_ckv = off_cq + qr
    off_kpe = off_ckv + kvr
    n_proj = off_kpe + LANE

    x_id, y_id, c_id = _place()
    me = 4 * x_id + 2 * y_id + c_id

    tm_wide = _tile(lp, 272, 16)
    tm_mid = _tile(lp, 544, 16)

    sm_buf, sm_sizes = _pack_rows([meta_tokens, conv_dw], 8 * LANE)
    sm_all = _all_gather([sm_buf], True, "ag_small")[0]
    sm_all = sm_all.reshape(N_DEV, -1)
    o0 = 0
    meta_all = sm_all[:, o0:o0 + meta_tokens.size].reshape((N_DEV,) + meta_tokens.shape)
    meta_full = jnp.transpose(meta_all, (1, 0, 2)).reshape(N_META, d)
    o0 = sm_sizes[0]
    dw_all = sm_all[:, o0:o0 + conv_dw.size].reshape((N_DEV, depth, kw, c_conv // N_DEV))
    dw_full = jnp.transpose(dw_all, (1, 2, 0, 3)).reshape(depth, kw, c_conv)

    n_head_cols = d_in - N_BRANCH * d - MLA_ROPE_DIM
    rb = MLA_ROPE_DIM
    nb_gate, nb_head = N_BRANCH * d // rb, n_head_cols // rb
    assert n_head_cols % rb == 0 and (N_BRANCH * d) % rb == 0 and n_proj == (nb_gate + nb_head + 2) * rb

    def in_src(i):
        return jnp.where(i < nb_gate, nb_head + 1 + i,
                         jnp.where(i < nb_gate + nb_head, i - nb_gate, jnp.where(i == nb_gate + nb_head, nb_head, 0)))

    def in_dst(i):
        return jnp.where(i < nb_head, nb_gate + i, jnp.where(i == nb_head, nb_gate + nb_head, i - nb_head - 1))

    def uq_src(i):
        j = i - 2 * nh
        return jnp.where(i < 2 * nh, 3 * (i // 2) + i % 2, 3 * (j // 2) + 2)

    def uq_dst(i):
        return jnp.where(i % 3 < 2, 2 * (i // 3) + i % 3, 2 * nh + 2 * (i // 3))

    ag_pending, tokens = {}, []
    for i in range(depth):
        for gi, names in enumerate(GROUPS):
            shards = [_pack_pairs_xla(w[n][i] if n in ROW_SHARDED else w[n][i].T) for n in names]
            bufs = [lax.dynamic_update_slice(lax.empty((N_DEV,) + a.shape, F32), a[None], (me, 0, 0))
                    for a in shards]
            *pend, tok = _ici_start(bufs, [], tokens[-1:], gather=True, name="ag_ici_start_%d_%d" % (i, gi))
            ag_pending[i, gi] = pend
            tokens.append(tok)

    sib_pending = {}

    def sibling_start(i, gi, after):
        bufs, _ = _ici_wait(*ag_pending[i, gi], after, gather=True, name="ag_ici_wait_%d_%d" % (i, gi))
        *sib_pending[i, gi], tok = _ici_start(bufs, [], [], gather="sibling", name="ag_sib_start_%d_%d" % (i, gi))
        return tok

    def all_gather_group(i, gi, after):
        gath, _ = _ici_wait(*sib_pending[i, gi], after, gather="sibling", name="ag_sib_wait_%d_%d" % (i, gi))
        wl = {}
        for n, g in zip(GROUPS[gi], gath):
            pk = g.reshape(-1, g.shape[-1])
            if n == 'w_in':
                wl[n] = _unpack(pk, name="unpack_w_in", out_rows=n_proj, tr=rb, src_fn=in_src,
                                zero_fn=lambda i: i == nb_gate + nb_head + 1)
            elif n == 'w_mla_uq':
                wl[n] = _unpack(pk, name="unpack_w_mla_uq", out_rows=2 * nh * hd, tr=rb, src_fn=uq_src,
                                zero_fn=lambda i: (i >= 2 * nh) & ((i - 2 * nh) % 2 == 1))
            elif n == 'w_mla_ukv':
                wl[n] = _unpack(pk, name="unpack_w_mla_ukv", out_rows=2 * nh * hd, tr=hd,
                                src_fn=lambda i: jnp.where(i < nh, 2 * i, 2 * (i - nh) + 1))
            else:
                wl[n] = _unpack(pk, name="unpack_" + n)
        return wl

    full = [None] * depth

    cos_g, sin_g = _rope_tables(lp, l_valid, hd // 4)
    cos_m, sin_m = _rope_tables(lp, l_valid, MLA_ROPE_DIM // 4)

    def vec2(a):
        return a.reshape(1, -1)

    meta_b = jnp.broadcast_to(meta_full[None], (nb, N_META, d))
    h = jnp.concatenate([meta_b, x, jnp.zeros((nb, lp - l_valid, d), F32)], axis=1).reshape(m_tok, d)
    tgt = jnp.pad(loss_target, ((0, 0), (N_META, lp - l_valid), (0, 0))).reshape(m_tok, d)

    saved = []
    first_tok = [sibling_start(0, 0, tokens[-1:])]
    sib_tok = []
    for i in range(depth):
        s = {}
        s['h_in'] = h
        u = _rms_fwd(h, d, 0, vec2(mix_norm_g[i]), tm_wide, "mix_norm_fwd")
        wl = full[i] = all_gather_group(i, 0, first_tok if i == 0 else [h])
        proj = _matmul(u, wl['w_in'], mode="nt", out_dtype=F32, name="mm_in", deps=sib_tok)
        cc = _conv_fwd(proj, dw_full[i], vec2(conv_b[i]), c_conv=c_conv, off_a=off_conv, nb=nb, lp=lp, l_valid=l_valid)
        sc = _ln_silu_fwd(cc, vec2(conv_ln_g[i]), vec2(conv_ln_b[i]), tm_mid)
        ya = _matmul(sc, wl['w_conv_out'], mode="nt", out_dtype=F32, name="mm_conv_out")
        qh, kh, vh = _gqa_prep_fwd(proj, vec2(gqa_q_norm_g[i]), vec2(gqa_k_norm_g[i]), cos_g, sin_g,
                                   off_q=off_q, hq=hq, hkv=hkv, nb=nb, lp=lp, tm=tm_mid, q_scale=gqa_scale)
        ob = _attn_fwd(qh, kh, vh, l_valid=l_valid, name="gqa_attn_fwd")
        yb = _matmul(ob, wl['w_gqa_out'], mode="nt", out_dtype=F32, name="mm_gqa_out")
        cqn = _rms_fwd(proj, qr, _cb(off_cq, qr), vec2(mla_q_norm_g[i]), tm_mid, "mla_q_norm_fwd")
        ckvn = _rms_fwd(proj, kvr, _cb(off_ckv, kvr), vec2(mla_kv_norm_g[i]), tm_mid, "mla_kv_norm_fwd")
        qm = _matmul(cqn, wl['w_mla_uq'], mode="nt", out_dtype=F32, name="mm_mla_uq")
        kvm = _matmul(ckvn, wl['w_mla_ukv'], mode="nt", out_dtype=F32, name="mm_mla_ukv")
        qc, kc, vc = _mla_prep_fwd(qm, kvm, proj, cos_m, sin_m, off_kpe=off_kpe, nh=nh, nb=nb, lp=lp, tm=tm_mid,
                                   q_scale=mla_scale)
        oc = _attn_fwd(qc, kc, vc, l_valid=l_valid, name="mla_attn_fwd")
        yc = _matmul(oc, wl['w_mla_out'], mode="nt", out_dtype=F32, name="mm_mla_out")
        merged = _gate_fwd(proj, vec2(gate_b[i]), ya, yb, yc, d=d, tm=tm_wide)
        h2 = _matmul(merged, wl['w_out'], mode="nn", out_dtype=F32, name="mm_out", residual=h)
        vn = _rms_fwd(h2, d, 0, vec2(ffn_norm_g[i]), tm_wide, "ffn_norm_fwd")
        wl.update(all_gather_group(i, 1, [h2, sibling_start(0, 1, [h2])] if i == 0 else [h2]))
        gt, up, act = _ffn_up_fwd(vn, wl['w_ffn_gate'], wl['w_ffn_up'])
        ahead = [sibling_start(i + 1, 0, [h2])] if i + 1 < depth else []
        h = _matmul(act, wl['w_ffn_down'], mode="nn", out_dtype=F32, name="mm_ffn_down", residual=h2, deps=ahead)
        sib_tok = [sibling_start(i + 1, 1, [h])] if i + 1 < depth else []
        s.update(u=u, proj=proj, cc=cc, sc=sc, ya=ya, qh=qh, kh=kh, vh=vh, ob=ob, yb=yb, cqn=cqn, ckvn=ckvn,
                 qc=qc, kc=kc, vc=vc, oc=oc, yc=yc, merged=merged, h2=h2, vn=vn, gt=gt, up=up, act=act)
        saved.append(s)

    dh, dg_final, loss_part = _loss_head(h, vec2(final_norm_g), tgt, lp=lp, l_valid=l_valid, tm=tm_wide)

    gsmall = {n: [None] * depth for n in SMALL if n != 'final_norm_g'}
    gdw = [None] * depth
    gbig = [{} for _ in range(depth)]
    cidx = jnp.reshape(c_id, (1,)).astype(jnp.int32)
    chip_idx = jnp.reshape(2 * x_id + y_id, (1,)).astype(jnp.int32)

    def reduce_scatter_start(gl, i, gi):
        names = GROUPS[gi]
        g4s = [gl[n].reshape(N_DEV // 2, 2, -1, gl[n].shape[-1]) for n in names]
        from_sibling = _swap_with_sibling(g4s)
        p4s = [_add_sibling(g, r, cidx, "rs_add_sibling_" + n) for n, g, r in zip(names, g4s, from_sibling)]
        *pend, tok = _ici_start(p4s, [(3,) + p.shape[1:] for p in p4s], [], gather=False,
                                name="rs_ici_start_%d_%d" % (i, gi))
        return pend, tok

    def reduce_scatter_finish(pend, after, i, gi):
        p4s, from_chips = _ici_wait(*pend, after, gather=False, name="rs_ici_wait_%d_%d" % (i, gi))
        out = {}
        for n, p, r in zip(GROUPS[gi], p4s, from_chips):
            g = _add_chips(p, r, chip_idx, "rs_add_chips_" + n)
            out[n] = g if n in ROW_SHARDED else g.T
        return out

    rs_pending = [None, None]
    rs_token = []
    for i in reversed(range(depth)):
        s = saved[i]
        proj = s['proj']
        wl = full[i]
        gl = {}
        dgt, dup = _ffn_down_bwd(dh, wl['w_ffn_down'], s['gt'], s['up'], rs_token)
        gl['w_ffn_down'] = _matmul_tn_packed(s['act'], dh, "mm_ffn_down_dw")
        dvn = _matmul(dgt, wl['w_ffn_gate'], mode="nn", out_dtype=F32, name="mm_ffn_gate_dx")
        dvn = _matmul(dup, wl['w_ffn_up'], mode="nn", out_dtype=F32, name="mm_ffn_up_dx", residual=dvn)
        gl['w_ffn_gate'] = _matmul_tn_packed(dgt, s['vn'], "mm_ffn_gate_dw")
        gl['w_ffn_up'] = _matmul_tn_packed(dup, s['vn'], "mm_ffn_up_dw")
        dh2, gsmall['ffn_norm_g'][i] = _rms_bwd(s['h2'], d, 0, vec2(ffn_norm_g[i]), dvn, tm_wide, F32,
                                                "ffn_norm_bwd", add=dh)
        started, tok = reduce_scatter_start(gl, i, 1)
        if rs_pending[1] is not None:
            gbig[i + 1].update(reduce_scatter_finish(rs_pending[1], [dh2, tok], i + 1, 1))
        rs_pending[1] = started
        dmg = _matmul(dh2, wl['w_out'], mode="nt", out_dtype=F32, name="mm_out_dx", deps=[tok])
        gl['w_out'] = _matmul_tn_packed(s['merged'], dh2, "mm_out_dw")
        dya, dyb, dyc, dg0, dg1, dg2, db0, db1, db2 = _gate_bwd(dmg, proj, vec2(gate_b[i]), s['ya'], s['yb'],
                                                                s['yc'], d=d, tm=tm_wide)
        gsmall['gate_b'][i] = jnp.concatenate([db0, db1, db2], axis=1)
        dsc = _matmul(dya, wl['w_conv_out'], mode="nn", out_dtype=F32, name="mm_conv_out_dx")
        gl['w_conv_out'] = _matmul_tn_packed(dya, s['sc'], "mm_conv_out_dw")
        dcc, gsmall['conv_ln_g'][i], gsmall['conv_ln_b'][i] = _ln_silu_bwd(
            s['cc'], vec2(conv_ln_g[i]), vec2(conv_ln_b[i]), dsc, tm_mid)
        da, dgc, gdw[i], gsmall['conv_b'][i] = _conv_bwd(proj, dcc, dw_full[i], c_conv=c_conv, off_a=off_conv,
                                                         nb=nb, lp=lp, l_valid=l_valid)
        dob = _matmul(dyb, wl['w_gqa_out'], mode="nn", out_dtype=BF16, name="mm_gqa_out_dx")
        gl['w_gqa_out'] = _matmul_tn_packed(dyb, s['ob'], "mm_gqa_out_dw")
        dqh, dkh, dvh = _attn_bwd(s['qh'], s['kh'], s['vh'], dob, l_valid=l_valid, name="gqa_attn_bwd")
        dq, dkv, gsmall['gqa_q_norm_g'][i], gsmall['gqa_k_norm_g'][i] = _gqa_prep_bwd(
            proj, vec2(gqa_q_norm_g[i]), vec2(gqa_k_norm_g[i]), cos_g, sin_g, dqh, dkh, dvh,
            off_q=off_q, hq=hq, hkv=hkv, nb=nb, lp=lp, tm=tm_mid, q_scale=gqa_scale)
        doc = _matmul(dyc, wl['w_mla_out'], mode="nn", out_dtype=BF16, name="mm_mla_out_dx")
        gl['w_mla_out'] = _matmul_tn_packed(dyc, s['oc'], "mm_mla_out_dw")
        dqc, dkc, dvc = _attn_bwd(s['qc'], s['kc'], s['vc'], doc, l_valid=l_valid, name="mla_attn_bwd")
        dqm, dkvm, dkpe = _mla_prep_bwd(dqc, dkc, dvc, cos_m, sin_m, nh=nh, nb=nb, lp=lp, tm=tm_mid,
                                        q_scale=mla_scale)
        dcqn = _matmul(dqm, wl['w_mla_uq'], mode="nn", out_dtype=F32, name="mm_mla_uq_dx")
        guq = _matmul_tn_packed(dqm, s['cqn'], "mm_mla_uq_dw")
        dckvn = _matmul(dkvm, wl['w_mla_ukv'], mode="nn", out_dtype=F32, name="mm_mla_ukv_dx")
        gukv = _matmul_tn_packed(dkvm, s['ckvn'], "mm_mla_ukv_dw")
        dcq, gsmall['mla_q_norm_g'][i] = _rms_bwd(proj, qr, _cb(off_cq, qr), vec2(mla_q_norm_g[i]), dcqn, tm_mid,
                                                  BF16, "mla_q_norm_bwd")
        dckv, gsmall['mla_kv_norm_g'][i] = _rms_bwd(proj, kvr, _cb(off_ckv, kvr), vec2(mla_kv_norm_g[i]), dckvn,
                                                    tm_mid, BF16, "mla_kv_norm_bwd")
        gl['w_mla_uq'] = _gather_rows(guq, name="perm_mla_uq_dw", out_rows=nh * mla_qk, tr=rb, src_fn=uq_dst)
        gl['w_mla_ukv'] = _gather_rows(gukv, name="perm_mla_ukv_dw", out_rows=2 * nh * hd, tr=hd,
                                       src_fn=lambda i: i // 2 + nh * (i % 2))
        dproj = jnp.concatenate([dg0, dg1, dg2, da, dgc, dq, dkv, dcq, dckv, dkpe], axis=1)
        du = _matmul(dproj, wl['w_in'], mode="nn", out_dtype=F32, name="mm_in_dx")
        gin = _matmul_tn_packed(dproj, s['u'], "mm_in_dw")
        gl['w_in'] = _gather_rows(gin, name="perm_in_dw", out_rows=d_in, tr=rb, src_fn=in_dst)
        dh, gsmall['mix_norm_g'][i] = _rms_bwd(s['h_in'], d, 0, vec2(mix_norm_g[i]), du, tm_wide, F32,
                                               "mix_norm_bwd", add=dh2)
        started, tok = reduce_scatter_start(gl, i, 0)
        if rs_pending[0] is not None:
            gbig[i + 1].update(reduce_scatter_finish(rs_pending[0], [dh, tok], i + 1, 0))
        rs_pending[0], rs_token = started, [tok]

    def adamw_big(names):
        for n in names:
            sh = w[n].shape
            two_d = (sh[0] * sh[1], sh[2])
            grads[n] = jnp.stack([gbig[i][n] for i in range(depth)])
            dl, mn, vn_ = _adamw(w[n].reshape(two_d), grads[n].reshape(two_d), mom1[n].reshape(two_d),
                                 mom2[n].reshape(two_d), "adamw_" + n)
            delta[n], new_m[n], new_v[n] = dl.reshape(sh), mn.reshape(sh), vn_.reshape(sh)

    grads, delta, new_m, new_v = {}, {}, {}, {}
    gbig[0].update(reduce_scatter_finish(rs_pending[1], rs_token, 0, 1))
    adamw_big(GROUPS[1])
    gbig[0].update(reduce_scatter_finish(rs_pending[0], [delta[GROUPS[1][-1]]], 0, 0))
    adamw_big(GROUPS[0])

    dh0 = dh.reshape(nb, lp, d)
    grad_x = dh0[:, N_META:l_valid]
    gmeta_full = jnp.sum(dh0[:, :N_META], axis=0)

    small_list = [jnp.stack(gsmall[n]).reshape(w[n].shape) for n in SMALL if n != 'final_norm_g']
    small_list += [dg_final.reshape(final_norm_g.shape), gmeta_full, jnp.stack(gdw), loss_part]
    ar_buf, ar_sizes = _pack_rows(small_list, 8 * LANE)
    ar = _all_reduce_small(ar_buf)
    ar_shapes = [w[n].shape for n in SMALL] + [(N_META, d), (depth, kw, c_conv), (1, LANE)]
    ar_out = _unpack_rows(ar, ar_sizes, ar_shapes)
    for n, g in zip(SMALL, ar_out):
        grads[n] = g
    dcol = d // N_DEV
    grads['meta_tokens'] = lax.dynamic_slice(ar_out[len(SMALL)], (0, me * dcol), (N_META, dcol))
    ccol = c_conv // N_DEV
    grads['conv_dw'] = lax.dynamic_slice(ar_out[len(SMALL) + 1], (0, 0, me * ccol),
                                         (depth, kw, ccol)).reshape(conv_dw.shape)
    loss = ar_out[len(SMALL) + 2][0, 0]

    rest = [n for n in W_NAMES if n not in BIG]
    pw, psz = _pack_rows([w[n] for n in rest], 8 * LANE)
    pg, _ = _pack_rows([grads[n] for n in rest], 8 * LANE)
    pm, _ = _pack_rows([mom1[n] for n in rest], 8 * LANE)
    pv, _ = _pack_rows([mom2[n] for n in rest], 8 * LANE)
    dl, mn, vn_ = _adamw(pw, pg, pm, pv, "adamw_small")
    shapes = [w[n].shape for n in rest]
    for n, a, b, c in zip(rest, _unpack_rows(dl, psz, shapes), _unpack_rows(mn, psz, shapes),
                          _unpack_rows(vn_, psz, shapes)):
        delta[n], new_m[n], new_v[n] = a, b, c

    return (loss, grad_x, *[grads[n] for n in W_NAMES], *[delta[n] for n in W_NAMES],
            *[new_m[n] for n in W_NAMES], *[new_v[n] for n in W_NAMES])
```

```python
import functools
import math

import jax
import jax.numpy as jnp
from jax import lax
from jax.experimental import pallas as pl
from jax.experimental.pallas import tpu as pltpu

F32 = jnp.float32
BF16 = jnp.bfloat16

N_META = 16
GRID_W = 64
ROPE_THETA = 10000.0
NORM_EPS = 1e-6
HEAD_DIM = 128
MLA_ROPE_DIM = 64
N_BRANCH = 3
N_DEV = 8
LANE = 128
VMEM_LIMIT = 56 * 1024 * 1024

ADAM_LR = 0.001
ADAM_B1 = 0.9
ADAM_B2 = 0.999
ADAM_EPS = 1e-08
ADAM_WD = 0.01
ADAM_STEP = 10

W_NAMES = ['meta_tokens', 'mix_norm_g', 'w_in', 'conv_dw', 'conv_b', 'conv_ln_g', 'conv_ln_b', 'w_conv_out',
           'gqa_q_norm_g', 'gqa_k_norm_g', 'w_gqa_out', 'mla_q_norm_g', 'w_mla_uq', 'mla_kv_norm_g', 'w_mla_ukv',
           'w_mla_out', 'gate_b', 'w_out', 'ffn_norm_g', 'w_ffn_gate', 'w_ffn_up', 'w_ffn_down', 'final_norm_g']
BIG = ['w_in', 'w_conv_out', 'w_gqa_out', 'w_mla_uq', 'w_mla_ukv', 'w_mla_out', 'w_out', 'w_ffn_gate', 'w_ffn_up',
       'w_ffn_down']
ROW_SHARDED = ('w_out', 'w_ffn_down')
GROUPS = (('w_in', 'w_conv_out', 'w_gqa_out', 'w_mla_uq', 'w_mla_ukv', 'w_mla_out', 'w_out'),
          ('w_ffn_gate', 'w_ffn_up', 'w_ffn_down'))
SMALL = ['mix_norm_g', 'conv_b', 'conv_ln_g', 'conv_ln_b', 'gqa_q_norm_g', 'gqa_k_norm_g', 'mla_q_norm_g',
         'mla_kv_norm_g', 'gate_b', 'ffn_norm_g', 'final_norm_g']

NT = (((1,), (1,)), ((), ()))
TN = (((0,), (0,)), ((), ()))
NN = (((1,), (0,)), ((), ()))


def _tile(n, target, mult):
    best = None
    for t in range(mult, min(n, target) + 1, mult):
        if n % t == 0:
            best = t
    assert best is not None, (n, target, mult)
    return best


def _cp(n):
    return pltpu.CompilerParams(dimension_semantics=("arbitrary",) * n, vmem_limit_bytes=VMEM_LIMIT)


def _sig(x):
    return jax.nn.sigmoid(x)


def _cb(off, w):
    assert off % w == 0, (off, w)
    return off // w


def _matmul(a, b, *, mode, out_dtype, name, residual=None, deps=()):
    if mode == "nn":
        (m, k), (k2, n) = a.shape, b.shape
    elif mode == "nt":
        (m, k), (n, k2) = a.shape, b.shape
    else:
        (k, m), (k2, n) = a.shape, b.shape
    assert k == k2, (a.shape, b.shape, mode)
    tm = _tile(m, 1088, 128 if mode == "tn" else 16)
    tn = _tile(n, 1024, 128)
    tk = _tile(k, 2176, 128)
    nk = k // tk
    dims = {"nn": NN, "nt": NT, "tn": TN}[mode]
    a_spec = (pl.BlockSpec((tk, tm), lambda i, j, kk: (kk, i)) if mode == "tn"
              else pl.BlockSpec((tm, tk), lambda i, j, kk: (i, kk)))
    b_spec = (pl.BlockSpec((tn, tk), lambda i, j, kk: (j, kk)) if mode == "nt"
              else pl.BlockSpec((tk, tn), lambda i, j, kk: (kk, j)))
    o_spec = pl.BlockSpec((tm, tn), lambda i, j, kk: (i, j))
    has_res = residual is not None

    nd = len(deps)

    def body(*refs):
        refs = refs[:len(refs) - 2 - nd] + refs[len(refs) - 2:]
        if has_res:
            a_ref, b_ref, r_ref, o_ref, acc = refs
        else:
            a_ref, b_ref, o_ref, acc = refs
        kk = pl.program_id(2)

        @pl.when(kk == 0)
        def _():
            acc[...] = jnp.zeros_like(acc)

        acc[...] += lax.dot_general(a_ref[...].astype(BF16), b_ref[...].astype(BF16), dims,
                                    preferred_element_type=F32)

        @pl.when(kk == nk - 1)
        def _():
            r = acc[...]
            if has_res:
                r = r + r_ref[...]
            o_ref[...] = r.astype(o_ref.dtype)

    ins = [a, b] + ([residual] if has_res else []) + list(deps)
    in_specs = [a_spec, b_spec] + ([o_spec] if has_res else []) + [pl.BlockSpec(memory_space=pl.ANY)] * nd
    return pl.pallas_call(
        body, grid=(m // tm, n // tn, nk), in_specs=in_specs, out_specs=o_spec,
        out_shape=jax.ShapeDtypeStruct((m, n), out_dtype), scratch_shapes=[pltpu.VMEM((tm, tn), F32)],
        name=name, compiler_params=_cp(3))(*ins)


def _rms_fwd(x, w, cb, g, tm, name):
    m = x.shape[0]

    def body(x_ref, g_ref, o_ref):
        xv = x_ref[...]
        r = lax.rsqrt(jnp.mean(xv * xv, axis=-1, keepdims=True) + NORM_EPS)
        o_ref[...] = (xv * r * g_ref[...]).astype(o_ref.dtype)

    return pl.pallas_call(
        body, grid=(m // tm,),
        in_specs=[pl.BlockSpec((tm, w), lambda i: (i, cb)), pl.BlockSpec((1, w), lambda i: (0, 0))],
        out_specs=pl.BlockSpec((tm, w), lambda i: (i, 0)), out_shape=jax.ShapeDtypeStruct((m, w), BF16),
        name=name, compiler_params=_cp(1))(x, g)


def _rms_bwd(x, w, cb, g, dy, tm, out_dtype, name, add=None, bf16_copy=False):
    m = x.shape[0]
    has_add = add is not None

    def body(*refs):
        if bf16_copy:
            refs, dxb_ref = refs[:-1], refs[-1]
        if has_add:
            x_ref, g_ref, dy_ref, add_ref, dx_ref, dg_ref = refs
        else:
            x_ref, g_ref, dy_ref, dx_ref, dg_ref = refs
        xv = x_ref[...]
        dyv = dy_ref[...].astype(F32)
        r = lax.rsqrt(jnp.mean(xv * xv, axis=-1, keepdims=True) + NORM_EPS)
        t = dyv * g_ref[...]
        dx = r * t - xv * (r * r * r) * jnp.mean(t * xv, axis=-1, keepdims=True)
        if has_add:
            dx = dx + add_ref[...]
        dx_ref[...] = dx.astype(dx_ref.dtype)
        if bf16_copy:
            dxb_ref[...] = dx.astype(BF16)

        @pl.when(pl.program_id(0) == 0)
        def _():
            dg_ref[...] = jnp.zeros_like(dg_ref)

        dg_ref[...] += jnp.sum(dyv * xv * r, axis=0, keepdims=True)

    row = pl.BlockSpec((tm, w), lambda i: (i, 0))
    vec = pl.BlockSpec((1, w), lambda i: (0, 0))
    ins = [x, g, dy] + ([add] if has_add else [])
    in_specs = [pl.BlockSpec((tm, w), lambda i: (i, cb)), vec, row] + ([row] if has_add else [])
    extra = bf16_copy * (jax.ShapeDtypeStruct((m, w), BF16),)
    return pl.pallas_call(
        body, grid=(m // tm,), in_specs=in_specs, out_specs=(row, vec) + bf16_copy * (row,),
        out_shape=(jax.ShapeDtypeStruct((m, w), out_dtype), jax.ShapeDtypeStruct((1, w), F32)) + extra,
        name=name, compiler_params=_cp(1))(*ins)


def _ln_silu_fwd(c, lg, lb, tm):
    m, w = c.shape

    def body(c_ref, g_ref, b_ref, o_ref):
        cv = c_ref[...]
        xc = cv - jnp.mean(cv, axis=-1, keepdims=True)
        r = lax.rsqrt(jnp.mean(xc * xc, axis=-1, keepdims=True) + NORM_EPS)
        yl = xc * r * g_ref[...] + b_ref[...]
        o_ref[...] = (yl * _sig(yl)).astype(o_ref.dtype)

    row = pl.BlockSpec((tm, w), lambda i: (i, 0))
    vec = pl.BlockSpec((1, w), lambda i: (0, 0))
    return pl.pallas_call(body, grid=(m // tm,), in_specs=[row, vec, vec], out_specs=row,
                          out_shape=jax.ShapeDtypeStruct((m, w), BF16), name="ln_silu_fwd",
                          compiler_params=_cp(1))(c, lg, lb)


def _ln_silu_bwd(c, lg, lb, ds, tm):
    m, w = c.shape

    def body(c_ref, g_ref, b_ref, ds_ref, dc_ref, dg_ref, db_ref):
        cv = c_ref[...]
        xc = cv - jnp.mean(cv, axis=-1, keepdims=True)
        r = lax.rsqrt(jnp.mean(xc * xc, axis=-1, keepdims=True) + NORM_EPS)
        nv = xc * r
        yl = nv * g_ref[...] + b_ref[...]
        sg = _sig(yl)
        dyl = ds_ref[...] * (sg * (1.0 + yl * (1.0 - sg)))
        dn = dyl * g_ref[...]
        dc = r * (dn - jnp.mean(dn, axis=-1, keepdims=True) - nv * jnp.mean(dn * nv, axis=-1, keepdims=True))
        dc_ref[...] = dc

        @pl.when(pl.program_id(0) == 0)
        def _():
            dg_ref[...] = jnp.zeros_like(dg_ref)
            db_ref[...] = jnp.zeros_like(db_ref)

        dg_ref[...] += jnp.sum(dyl * nv, axis=0, keepdims=True)
        db_ref[...] += jnp.sum(dyl, axis=0, keepdims=True)

    row = pl.BlockSpec((tm, w), lambda i: (i, 0))
    vec = pl.BlockSpec((1, w), lambda i: (0, 0))
    return pl.pallas_call(
        body, grid=(m // tm,), in_specs=[row, vec, vec, row], out_specs=(row, vec, vec),
        out_shape=(jax.ShapeDtypeStruct((m, w), F32), jax.ShapeDtypeStruct((1, w), F32),
                   jax.ShapeDtypeStruct((1, w), F32)),
        name="ln_silu_bwd", compiler_params=_cp(1))(c, lg, lb, ds)


CONV_MARGIN = 16
CONV_ROWS = 128


def _conv_fwd(proj, dw, bias, *, c_conv, off_a, nb, lp, l_valid):
    kw = dw.shape[0]
    half = kw // 2
    cw = LANE
    mg = CONV_MARGIN
    assert half <= mg - 1 and lp % CONV_ROWS == 0

    def body(a_ref, g_ref, w_ref, b_ref, c_ref, zp):
        t = lax.broadcasted_iota(jnp.int32, (lp, cw), 0)
        z = jnp.where(t < l_valid, a_ref[...] * _sig(g_ref[...]), 0.0)
        zp[0:mg, :] = jnp.zeros((mg, cw), F32)
        zp[mg + lp:mg + lp + mg, :] = jnp.zeros((mg, cw), F32)
        zp[mg:mg + lp, :] = z
        for r0 in range(0, lp, CONV_ROWS):
            acc = jnp.zeros((CONV_ROWS, cw), F32) + b_ref[...]
            for k in range(kw):
                s0 = mg + r0 + k - half
                acc = acc + w_ref[k:k + 1, :] * zp[s0:s0 + CONV_ROWS, :]
            c_ref[r0:r0 + CONV_ROWS, :] = acc

    ja, jg = _cb(off_a, cw), _cb(off_a + c_conv, cw)
    return pl.pallas_call(
        body, grid=(nb, c_conv // cw),
        in_specs=[pl.BlockSpec((lp, cw), lambda b, j: (b, ja + j)), pl.BlockSpec((lp, cw), lambda b, j: (b, jg + j)),
                  pl.BlockSpec((kw, cw), lambda b, j: (0, j)), pl.BlockSpec((1, cw), lambda b, j: (0, j))],
        out_specs=pl.BlockSpec((lp, cw), lambda b, j: (b, j)),
        out_shape=jax.ShapeDtypeStruct((nb * lp, c_conv), F32),
        scratch_shapes=[pltpu.VMEM((lp + 2 * mg, cw), F32)], name="conv_fwd", compiler_params=_cp(2))(proj, proj, dw, bias)


def _conv_bwd(proj, dc, dw, *, c_conv, off_a, nb, lp, l_valid):
    kw = dw.shape[0]
    half = kw // 2
    cw = LANE
    mg = CONV_MARGIN

    def body(a_ref, g_ref, dc_ref, w_ref, da_ref, dg_ref, ddw_ref, dcb_ref, zp, dcp):
        t = lax.broadcasted_iota(jnp.int32, (lp, cw), 0)
        z = jnp.where(t < l_valid, a_ref[...] * _sig(g_ref[...]), 0.0)
        for buf in (zp, dcp):
            buf[0:mg, :] = jnp.zeros((mg, cw), F32)
            buf[mg + lp:mg + lp + mg, :] = jnp.zeros((mg, cw), F32)
        zp[mg:mg + lp, :] = z
        dcv = dc_ref[...]
        dcp[mg:mg + lp, :] = dcv

        @pl.when(pl.program_id(1) == 0)
        def _():
            ddw_ref[...] = jnp.zeros_like(ddw_ref)
            dcb_ref[...] = jnp.zeros_like(dcb_ref)

        dcb_ref[...] += jnp.sum(dcv, axis=0, keepdims=True)
        for r0 in range(0, lp, CONV_ROWS):
            acc = jnp.zeros((CONV_ROWS, cw), F32)
            for k in range(kw):
                s0 = mg + r0 - k + half
                acc = acc + w_ref[k:k + 1, :] * dcp[s0:s0 + CONV_ROWS, :]
            tc = lax.broadcasted_iota(jnp.int32, (CONV_ROWS, cw), 0) + r0
            dz = jnp.where(tc < l_valid, acc, 0.0)
            sg = _sig(g_ref[r0:r0 + CONV_ROWS, :])
            da_ref[r0:r0 + CONV_ROWS, :] = (dz * sg).astype(da_ref.dtype)
            dg_ref[r0:r0 + CONV_ROWS, :] = (dz * a_ref[r0:r0 + CONV_ROWS, :] * sg * (1.0 - sg)).astype(dg_ref.dtype)
        for k in range(kw):
            acc = jnp.zeros((CONV_ROWS, cw), F32)
            for r0 in range(0, lp, CONV_ROWS):
                s0 = mg + r0 + k - half
                acc = acc + dc_ref[r0:r0 + CONV_ROWS, :] * zp[s0:s0 + CONV_ROWS, :]
            ddw_ref[k:k + 1, :] += jnp.sum(acc, axis=0, keepdims=True)

    ja, jg = _cb(off_a, cw), _cb(off_a + c_conv, cw)
    seq = pl.BlockSpec((lp, cw), lambda j, b: (b, j))
    return pl.pallas_call(
        body, grid=(c_conv // cw, nb),
        in_specs=[pl.BlockSpec((lp, cw), lambda j, b: (b, ja + j)), pl.BlockSpec((lp, cw), lambda j, b: (b, jg + j)),
                  seq, pl.BlockSpec((kw, cw), lambda j, b: (0, j))],
        out_specs=(seq, seq, pl.BlockSpec((kw, cw), lambda j, b: (0, j)), pl.BlockSpec((1, cw), lambda j, b: (0, j))),
        out_shape=(jax.ShapeDtypeStruct((nb * lp, c_conv), BF16), jax.ShapeDtypeStruct((nb * lp, c_conv), BF16),
                   jax.ShapeDtypeStruct((kw, c_conv), F32), jax.ShapeDtypeStruct((1, c_conv), F32)),
        scratch_shapes=[pltpu.VMEM((lp + 2 * mg, cw), F32), pltpu.VMEM((lp + 2 * mg, cw), F32)],
        name="conv_bwd", compiler_params=_cp(2))(proj, proj, dc, dw)


def _swap_halves(x, half):
    fwd = pltpu.roll(x, LANE - half, axis=1)
    bwd = pltpu.roll(x, half, axis=1)
    lane = lax.broadcasted_iota(jnp.int32, x.shape, 1)
    return jnp.where((lane & (2 * half - 1)) < half, fwd, bwd)


def _rope(x, cos, sin, half):
    return x * cos + _swap_halves(x, half) * sin


def _rope_t(dy, cos, sin, half):
    return dy * cos + _swap_halves(dy * sin, half)


def _rope_tables(lp, l_valid, half):
    t = jnp.arange(lp)
    n = jnp.clip(t - N_META, 0, None)
    real = (t >= N_META) & (t < l_valid)
    row = jnp.where(real, n // GRID_W, 0).astype(F32)
    col = jnp.where(real, n % GRID_W, 0).astype(F32)
    inv = ROPE_THETA ** (-jnp.arange(half, dtype=F32) / half)
    ar, ac = row[:, None] * inv[None, :], col[:, None] * inv[None, :]
    cos = jnp.concatenate([jnp.cos(ar), jnp.cos(ar), jnp.cos(ac), jnp.cos(ac)], axis=1)
    sin = jnp.concatenate([-jnp.sin(ar), jnp.sin(ar), -jnp.sin(ac), jnp.sin(ac)], axis=1)
    padw = LANE - 4 * half
    if padw:
        cos = jnp.pad(cos, ((0, 0), (0, padw)))
        sin = jnp.pad(sin, ((0, 0), (0, padw)))
    return cos.astype(F32), sin.astype(F32)


def _gqa_prep_fwd(proj, qg, kg, cos, sin, *, off_q, hq, hkv, nb, lp, tm, q_scale):
    hd = HEAD_DIM
    half = hd // 4
    wq, wkv = hq * hd, 2 * hkv * hd
    nt = lp // tm

    def body(q_ref, kv_ref, qg_ref, kg_ref, cos_ref, sin_ref, qo_ref, ko_ref, vo_ref):
        cosv, sinv = cos_ref[...], sin_ref[...]

        def norm_rope(xh, g):
            r = lax.rsqrt(jnp.mean(xh * xh, axis=-1, keepdims=True) + NORM_EPS)
            return _rope(xh * r * g, cosv, sinv, half)

        for h in range(hq):
            qo_ref[h] = (norm_rope(q_ref[:, h * hd:(h + 1) * hd], qg_ref[...]) * q_scale).astype(BF16)
        for h in range(hkv):
            ko_ref[h] = norm_rope(kv_ref[:, h * hd:(h + 1) * hd], kg_ref[...]).astype(BF16)
            vo_ref[h] = kv_ref[:, (hkv + h) * hd:(hkv + h + 1) * hd].astype(BF16)

    jq, jkv = _cb(off_q, wq), _cb(off_q + wq, wkv)
    tab = pl.BlockSpec((tm, LANE), lambda b, i: (i, 0))
    vec = pl.BlockSpec((1, hd), lambda b, i: (0, 0))

    def heads(h):
        return pl.BlockSpec((None, h, tm, hd), lambda b, i: (b, 0, i, 0))

    return pl.pallas_call(
        body, grid=(nb, nt),
        in_specs=[pl.BlockSpec((tm, wq), lambda b, i: (b * nt + i, jq)),
                  pl.BlockSpec((tm, wkv), lambda b, i: (b * nt + i, jkv)), vec, vec, tab, tab],
        out_specs=(heads(hq), heads(hkv), heads(hkv)),
        out_shape=(jax.ShapeDtypeStruct((nb, hq, lp, hd), BF16), jax.ShapeDtypeStruct((nb, hkv, lp, hd), BF16),
                   jax.ShapeDtypeStruct((nb, hkv, lp, hd), BF16)),
        name="gqa_prep_fwd", compiler_params=_cp(2))(proj, proj, qg, kg, cos, sin)


def _gqa_prep_bwd(proj, qg, kg, cos, sin, dqh, dkh, dvh, *, off_q, hq, hkv, nb, lp, tm, q_scale):
    hd = HEAD_DIM
    half = hd // 4
    wq, wkv = hq * hd, 2 * hkv * hd
    nt = lp // tm

    def body(q_ref, kv_ref, qg_ref, kg_ref, cos_ref, sin_ref, dqh_ref, dkh_ref, dvh_ref,
             dq_ref, dkv_ref, dqg_ref, dkg_ref):
        cosv, sinv = cos_ref[...], sin_ref[...]

        def back(xh, g, dyh):
            dn = _rope_t(dyh, cosv, sinv, half)
            r = lax.rsqrt(jnp.mean(xh * xh, axis=-1, keepdims=True) + NORM_EPS)
            t = dn * g
            dx = r * t - xh * (r * r * r) * jnp.mean(t * xh, axis=-1, keepdims=True)
            return dx, jnp.sum(dn * xh * r, axis=0, keepdims=True)

        first = (pl.program_id(0) == 0) & (pl.program_id(1) == 0)

        @pl.when(first)
        def _():
            dqg_ref[...] = jnp.zeros_like(dqg_ref)
            dkg_ref[...] = jnp.zeros_like(dkg_ref)

        gq = jnp.zeros((1, hd), F32)
        for h in range(hq):
            dx, dg = back(q_ref[:, h * hd:(h + 1) * hd], qg_ref[...], dqh_ref[h] * q_scale)
            dq_ref[:, h * hd:(h + 1) * hd] = dx.astype(dq_ref.dtype)
            gq = gq + dg
        dqg_ref[...] += gq
        gk = jnp.zeros((1, hd), F32)
        for h in range(hkv):
            dx, dg = back(kv_ref[:, h * hd:(h + 1) * hd], kg_ref[...], dkh_ref[h])
            dkv_ref[:, h * hd:(h + 1) * hd] = dx.astype(dkv_ref.dtype)
            dkv_ref[:, (hkv + h) * hd:(hkv + h + 1) * hd] = dvh_ref[h].astype(dkv_ref.dtype)
            gk = gk + dg
        dkg_ref[...] += gk

    jq, jkv = _cb(off_q, wq), _cb(off_q + wq, wkv)
    tab = pl.BlockSpec((tm, LANE), lambda b, i: (i, 0))
    vec = pl.BlockSpec((1, hd), lambda b, i: (0, 0))

    def heads(h):
        return pl.BlockSpec((None, h, tm, hd), lambda b, i: (b, 0, i, 0))

    return pl.pallas_call(
        body, grid=(nb, nt),
        in_specs=[pl.BlockSpec((tm, wq), lambda b, i: (b * nt + i, jq)),
                  pl.BlockSpec((tm, wkv), lambda b, i: (b * nt + i, jkv)), vec, vec, tab, tab,
                  heads(hq), heads(hkv), heads(hkv)],
        out_specs=(pl.BlockSpec((tm, wq), lambda b, i: (b * nt + i, 0)),
                   pl.BlockSpec((tm, wkv), lambda b, i: (b * nt + i, 0)), vec, vec),
        out_shape=(jax.ShapeDtypeStruct((nb * lp, wq), BF16), jax.ShapeDtypeStruct((nb * lp, wkv), BF16),
                   jax.ShapeDtypeStruct((1, hd), F32), jax.ShapeDtypeStruct((1, hd), F32)),
        name="gqa_prep_bwd", compiler_params=_cp(2))(proj, proj, qg, kg, cos, sin, dqh, dkh, dvh)


def _mla_prep_fwd(qm, kvm, proj, cos, sin, *, off_kpe, nh, nb, lp, tm, q_scale):
    hd = HEAD_DIM
    half = MLA_ROPE_DIM // 4
    nt = lp // tm
    wh = nh * hd

    def body(qm_ref, kvm_ref, kpe_ref, cos_ref, sin_ref, qo_ref, ko_ref, vo_ref):
        cosv, sinv = cos_ref[...], sin_ref[...]
        kpe = _rope(kpe_ref[...], cosv, sinv, half).astype(BF16)
        for h in range(nh):
            qo_ref[h, :, 0:hd] = (qm_ref[:, h * hd:(h + 1) * hd] * q_scale).astype(BF16)
            qo_ref[h, :, hd:2 * hd] = (_rope(qm_ref[:, wh + h * hd:wh + (h + 1) * hd], cosv, sinv, half)
                                       * q_scale).astype(BF16)
            ko_ref[h, :, 0:hd] = kvm_ref[:, h * hd:(h + 1) * hd].astype(BF16)
            ko_ref[h, :, hd:2 * hd] = kpe
            vo_ref[h] = kvm_ref[:, wh + h * hd:wh + (h + 1) * hd].astype(BF16)

    tab = pl.BlockSpec((tm, LANE), lambda b, i: (i, 0))
    wide = pl.BlockSpec((tm, 2 * wh), lambda b, i: (b * nt + i, 0))
    jk = _cb(off_kpe, LANE)
    return pl.pallas_call(
        body, grid=(nb, nt),
        in_specs=[wide, wide, pl.BlockSpec((tm, LANE), lambda b, i: (b * nt + i, jk)), tab, tab],
        out_specs=(pl.BlockSpec((None, nh, tm, 2 * hd), lambda b, i: (b, 0, i, 0)),
                   pl.BlockSpec((None, nh, tm, 2 * hd), lambda b, i: (b, 0, i, 0)),
                   pl.BlockSpec((None, nh, tm, hd), lambda b, i: (b, 0, i, 0))),
        out_shape=(jax.ShapeDtypeStruct((nb, nh, lp, 2 * hd), BF16), jax.ShapeDtypeStruct((nb, nh, lp, 2 * hd), BF16),
                   jax.ShapeDtypeStruct((nb, nh, lp, hd), BF16)),
        name="mla_prep_fwd", compiler_params=_cp(2))(qm, kvm, proj, cos, sin)


def _mla_prep_bwd(dqc, dkc, dvc, cos, sin, *, nh, nb, lp, tm, q_scale):
    hd = HEAD_DIM
    half = MLA_ROPE_DIM // 4
    nt = lp // tm
    wh = nh * hd

    def body(dq_ref, dk_ref, dv_ref, cos_ref, sin_ref, dqm_ref, dkvm_ref, dkpe_ref):
        cosv, sinv = cos_ref[...], sin_ref[...]
        acc = jnp.zeros((tm, hd), F32)
        for h in range(nh):
            dqm_ref[:, h * hd:(h + 1) * hd] = (dq_ref[h, :, 0:hd] * q_scale).astype(BF16)
            dqm_ref[:, wh + h * hd:wh + (h + 1) * hd] = (_rope_t(dq_ref[h, :, hd:2 * hd], cosv, sinv, half)
                                                         * q_scale).astype(BF16)
            dkvm_ref[:, h * hd:(h + 1) * hd] = dk_ref[h, :, 0:hd].astype(BF16)
            dkvm_ref[:, wh + h * hd:wh + (h + 1) * hd] = dv_ref[h].astype(BF16)
            acc = acc + dk_ref[h, :, hd:2 * hd]
        dkpe_ref[...] = _rope_t(acc, cosv, sinv, half).astype(BF16)

    tab = pl.BlockSpec((tm, LANE), lambda b, i: (i, 0))
    wide = pl.BlockSpec((tm, 2 * wh), lambda b, i: (b * nt + i, 0))
    return pl.pallas_call(
        body, grid=(nb, nt),
        in_specs=[pl.BlockSpec((None, nh, tm, 2 * hd), lambda b, i: (b, 0, i, 0)),
                  pl.BlockSpec((None, nh, tm, 2 * hd), lambda b, i: (b, 0, i, 0)),
                  pl.BlockSpec((None, nh, tm, hd), lambda b, i: (b, 0, i, 0)), tab, tab],
        out_specs=(wide, wide, pl.BlockSpec((tm, LANE), lambda b, i: (b * nt + i, 0))),
        out_shape=(jax.ShapeDtypeStruct((nb * lp, 2 * wh), BF16), jax.ShapeDtypeStruct((nb * lp, 2 * wh), BF16),
                   jax.ShapeDtypeStruct((nb * lp, LANE), BF16)),
        name="mla_prep_bwd", compiler_params=_cp(2))(dqc, dkc, dvc, cos, sin)


def _softmax_parts(s, lp, l_valid):
    tw = -(-(lp - l_valid) // LANE) * LANE
    main, tail = s[:, :lp - tw], s[:, lp - tw:]
    col = lax.broadcasted_iota(jnp.int32, tail.shape, 1) + (lp - tw)
    tail = jnp.where(col < l_valid, tail, -1e30)
    m = jnp.maximum(jnp.max(main, axis=-1, keepdims=True), jnp.max(tail, axis=-1, keepdims=True))
    pm, pt = jnp.exp(main - m), jnp.exp(tail - m)
    den = jnp.sum(pm, axis=-1, keepdims=True) + jnp.sum(pt, axis=-1, keepdims=True)
    return pm, pt, den, tw


def _attn_fwd(q, k, v, *, l_valid, name):
    nb, hq, lp, dk = q.shape
    hkv, dv = k.shape[1], v.shape[3]
    grp = hq // hkv
    tq = _tile(lp, 2176, 128)
    sub = tq // 8
    nq = lp // tq

    def body(q_ref, k_ref, v_ref, o_ref, p_s):
        for r0 in range(0, tq, sub):
            rows = slice(r0, r0 + sub)
            s = lax.dot_general(q_ref[rows, :], k_ref[...], NT, preferred_element_type=F32)
            pm, pt, den, tw = _softmax_parts(s, lp, l_valid)
            p_s[rows, :lp - tw] = pm.astype(BF16)
            p_s[rows, lp - tw:] = pt.astype(BF16)
            o = lax.dot_general(p_s[rows, :], v_ref[...], NN, preferred_element_type=F32)
            o_ref[rows, :] = (o / den).astype(o_ref.dtype)

    return pl.pallas_call(
        body, grid=(nb, hq, nq),
        in_specs=[pl.BlockSpec((None, None, tq, dk), lambda b, h, i: (b, h, i, 0)),
                  pl.BlockSpec((None, None, lp, dk), lambda b, h, i: (b, h // grp, 0, 0)),
                  pl.BlockSpec((None, None, lp, dv), lambda b, h, i: (b, h // grp, 0, 0))],
        out_specs=pl.BlockSpec((tq, dv), lambda b, h, i: (b * nq + i, h)),
        out_shape=jax.ShapeDtypeStruct((nb * lp, hq * dv), BF16), scratch_shapes=[pltpu.VMEM((tq, lp), BF16)],
        name=name, compiler_params=_cp(3))(q, k, v)


def _attn_bwd(q, k, v, do, *, l_valid, name):
    nb, hq, lp, dk = q.shape
    hkv, dv = k.shape[1], v.shape[3]
    grp = hq // hkv
    tq = _tile(lp, 1088, 64)
    sub = tq // 4
    nq = lp // tq

    def body(q_ref, k_ref, v_ref, do_ref, dq_ref, dk_ref, dv_ref, p_s, ds_s):
        for r0 in range(0, tq, sub):
            rows = slice(r0, r0 + sub)
            s = lax.dot_general(q_ref[rows, :], k_ref[...], NT, preferred_element_type=F32)
            pm, pt, den, tw = _softmax_parts(s, lp, l_valid)
            inv = 1.0 / den
            pm, pt = pm * inv, pt * inv
            dp = lax.dot_general(do_ref[rows, :], v_ref[...], NT, preferred_element_type=F32)
            dpm, dpt = dp[:, :lp - tw], dp[:, lp - tw:]
            dd = jnp.sum(pm * dpm, axis=-1, keepdims=True) + jnp.sum(pt * dpt, axis=-1, keepdims=True)
            p_s[rows, :lp - tw] = pm.astype(BF16)
            p_s[rows, lp - tw:] = pt.astype(BF16)
            ds_s[rows, :lp - tw] = (pm * (dpm - dd)).astype(BF16)
            ds_s[rows, lp - tw:] = (pt * (dpt - dd)).astype(BF16)
            dq_ref[rows, :] = lax.dot_general(ds_s[rows, :], k_ref[...], NN, preferred_element_type=F32)

        @pl.when((pl.program_id(2) == 0) & (pl.program_id(3) == 0))
        def _():
            dk_ref[...] = jnp.zeros_like(dk_ref)
            dv_ref[...] = jnp.zeros_like(dv_ref)

        dk_ref[...] += lax.dot_general(ds_s[...], q_ref[...], TN, preferred_element_type=F32)
        dv_ref[...] += lax.dot_general(p_s[...], do_ref[...], TN, preferred_element_type=F32)

    return pl.pallas_call(
        body, grid=(nb, hkv, grp, nq), scratch_shapes=[pltpu.VMEM((tq, lp), BF16), pltpu.VMEM((tq, lp), BF16)],
        in_specs=[pl.BlockSpec((None, None, tq, dk), lambda b, h, g, i: (b, h * grp + g, i, 0)),
                  pl.BlockSpec((None, None, lp, dk), lambda b, h, g, i: (b, h, 0, 0)),
                  pl.BlockSpec((None, None, lp, dv), lambda b, h, g, i: (b, h, 0, 0)),
                  pl.BlockSpec((tq, dv), lambda b, h, g, i: (b * nq + i, h * grp + g))],
        out_specs=(pl.BlockSpec((None, None, tq, dk), lambda b, h, g, i: (b, h * grp + g, i, 0)),
                   pl.BlockSpec((None, None, lp, dk), lambda b, h, g, i: (b, h, 0, 0)),
                   pl.BlockSpec((None, None, lp, dv), lambda b, h, g, i: (b, h, 0, 0))),
        out_shape=(jax.ShapeDtypeStruct((nb, hq, lp, dk), F32), jax.ShapeDtypeStruct((nb, hkv, lp, dk), F32),
                   jax.ShapeDtypeStruct((nb, hkv, lp, dv), F32)),
        name=name, compiler_params=_cp(4))(q, k, v, do)


def _gate_fwd(proj, gb, ya, yb, yc, *, d, tm):
    m = ya.shape[0]
    cw = _tile(d, 512, LANE)
    nj = d // cw

    def body(g0, g1, g2, b0, b1, b2, ya_ref, yb_ref, yc_ref, o_ref):
        o = (_sig(g0[...] + b0[...]) * ya_ref[...] + _sig(g1[...] + b1[...]) * yb_ref[...]
             + _sig(g2[...] + b2[...]) * yc_ref[...])
        o_ref[...] = o.astype(o_ref.dtype)

    def gl(br):
        return pl.BlockSpec((tm, cw), lambda j, i: (i, br * nj + j))

    def gbs(br):
        return pl.BlockSpec((1, cw), lambda j, i: (0, br * nj + j))

    blk = pl.BlockSpec((tm, cw), lambda j, i: (i, j))
    return pl.pallas_call(
        body, grid=(nj, m // tm), in_specs=[gl(0), gl(1), gl(2), gbs(0), gbs(1), gbs(2), blk, blk, blk],
        out_specs=blk, out_shape=jax.ShapeDtypeStruct((m, d), BF16), name="gate_fwd",
        compiler_params=_cp(2))(proj, proj, proj, gb, gb, gb, ya, yb, yc)


def _gate_bwd(dm, proj, gb, ya, yb, yc, *, d, tm):
    m = ya.shape[0]
    cw = _tile(d, 512, LANE)
    nj = d // cw

    def body(dm_ref, g0, g1, g2, b0, b1, b2, ya_ref, yb_ref, yc_ref,
             dya, dyb, dyc, dg0, dg1, dg2, db0, db1, db2):
        dmv = dm_ref[...]

        @pl.when(pl.program_id(1) == 0)
        def _():
            for r in (db0, db1, db2):
                r[...] = jnp.zeros_like(r)

        for g, b, y, dy, dgl, db in ((g0, b0, ya_ref, dya, dg0, db0), (g1, b1, yb_ref, dyb, dg1, db1),
                                     (g2, b2, yc_ref, dyc, dg2, db2)):
            sg = _sig(g[...] + b[...])
            dy[...] = (sg * dmv).astype(dy.dtype)
            dl = dmv * y[...] * sg * (1.0 - sg)
            dgl[...] = dl.astype(dgl.dtype)
            db[...] += jnp.sum(dl, axis=0, keepdims=True)

    def gl(br):
        return pl.BlockSpec((tm, cw), lambda j, i: (i, br * nj + j))

    def gbs(br):
        return pl.BlockSpec((1, cw), lambda j, i: (0, br * nj + j))

    blk = pl.BlockSpec((tm, cw), lambda j, i: (i, j))
    vec = pl.BlockSpec((1, cw), lambda j, i: (0, j))
    act = jax.ShapeDtypeStruct((m, d), BF16)
    vsh = jax.ShapeDtypeStruct((1, d), F32)
    return pl.pallas_call(
        body, grid=(nj, m // tm), in_specs=[blk, gl(0), gl(1), gl(2), gbs(0), gbs(1), gbs(2), blk, blk, blk],
        out_specs=(blk,) * 6 + (vec,) * 3, out_shape=(act,) * 6 + (vsh,) * 3, name="gate_bwd",
        compiler_params=_cp(2))(dm, proj, proj, proj, gb, gb, gb, ya, yb, yc)


def _ffn_up_fwd(vn, wg_t, wu_t):
    m, d = vn.shape
    f = wg_t.shape[0]
    tm = _tile(m, 1088, 16)
    tn = _tile(f, 512, LANE)

    def body(v_ref, wg_ref, wu_ref, g_ref, u_ref, a_ref):
        v = v_ref[...]
        g = lax.dot_general(v, wg_ref[...], NT, preferred_element_type=F32)
        u = lax.dot_general(v, wu_ref[...], NT, preferred_element_type=F32)
        g_ref[...] = g
        u_ref[...] = u
        a_ref[...] = (g * _sig(g) * u).astype(a_ref.dtype)

    row = pl.BlockSpec((tm, d), lambda i, j: (i, 0))
    wsp = pl.BlockSpec((tn, d), lambda i, j: (j, 0))
    blk = pl.BlockSpec((tm, tn), lambda i, j: (i, j))
    return pl.pallas_call(
        body, grid=(m // tm, f // tn), in_specs=[row, wsp, wsp], out_specs=(blk, blk, blk),
        out_shape=(jax.ShapeDtypeStruct((m, f), F32), jax.ShapeDtypeStruct((m, f), F32),
                   jax.ShapeDtypeStruct((m, f), BF16)),
        name="ffn_up_fwd", compiler_params=_cp(2))(vn, wg_t, wu_t)


def _ffn_down_bwd(dh, wd, gt, up, deps):
    m, d = dh.shape
    f = wd.shape[0]
    tm = _tile(m, 1088, 16)
    tn = _tile(f, 512, LANE)
    nd = len(deps)

    def body(*refs):
        dh_ref, wd_ref, g_ref, u_ref = refs[:4]
        dg_ref, du_ref = refs[4 + nd:]
        da = lax.dot_general(dh_ref[...].astype(BF16), wd_ref[...], NT, preferred_element_type=F32)
        g = g_ref[...]
        sg = _sig(g)
        dg_ref[...] = (da * u_ref[...] * sg * (1.0 + g * (1.0 - sg))).astype(dg_ref.dtype)
        du_ref[...] = (da * g * sg).astype(du_ref.dtype)

    row = pl.BlockSpec((tm, d), lambda i, j: (i, 0))
    wsp = pl.BlockSpec((tn, d), lambda i, j: (j, 0))
    blk = pl.BlockSpec((tm, tn), lambda i, j: (i, j))
    sh = jax.ShapeDtypeStruct((m, f), BF16)
    return pl.pallas_call(
        body, grid=(m // tm, f // tn), in_specs=[row, wsp, blk, blk] + [pl.BlockSpec(memory_space=pl.ANY)] * nd,
        out_specs=(blk, blk), out_shape=(sh, sh), name="ffn_down_bwd", compiler_params=_cp(2))(dh, wd, gt, up, *deps)


def _loss_head(h, gf, tgt, *, lp, l_valid, tm):
    m, d = h.shape
    nt = lp // tm

    def body(h_ref, g_ref, t_ref, dh_ref, dg_ref, loss_ref):
        i = pl.program_id(0)
        xv = h_ref[...]
        r = lax.rsqrt(jnp.mean(xv * xv, axis=-1, keepdims=True) + NORM_EPS)
        y = xv * r * g_ref[...]
        t = lax.broadcasted_iota(jnp.int32, (tm, d), 0) + (i % nt) * tm
        err = jnp.where((t >= N_META) & (t < l_valid), y - t_ref[...], 0.0)
        dy = err * (1.0 / d)
        tg = dy * g_ref[...]
        dh_ref[...] = r * tg - xv * (r * r * r) * jnp.mean(tg * xv, axis=-1, keepdims=True)

        @pl.when(i == 0)
        def _():
            dg_ref[...] = jnp.zeros_like(dg_ref)
            loss_ref[...] = jnp.zeros_like(loss_ref)

        dg_ref[...] += jnp.sum(dy * xv * r, axis=0, keepdims=True)
        sq = jnp.sum(jnp.sum(err * err, axis=-1, keepdims=True), axis=0, keepdims=True)
        loss_ref[...] += jnp.zeros((1, LANE), F32) + sq * (0.5 / d)

    row = pl.BlockSpec((tm, d), lambda i: (i, 0))
    vec = pl.BlockSpec((1, d), lambda i: (0, 0))
    return pl.pallas_call(
        body, grid=(m // tm,), in_specs=[row, vec, row],
        out_specs=(row, vec, pl.BlockSpec((1, LANE), lambda i: (0, 0))),
        out_shape=(jax.ShapeDtypeStruct((m, d), F32), jax.ShapeDtypeStruct((1, d), F32),
                   jax.ShapeDtypeStruct((1, LANE), F32)),
        name="loss_head", compiler_params=_cp(1))(h, gf, tgt)


def _adamw(w, g, m, v, name):
    rows, cols = w.shape
    tr = rows
    if rows % 8 == 0:
        tr = _tile(rows, max(8, (1 << 18) // cols // 8 * 8), 8)
    c1 = 1.0 / (1.0 - ADAM_B1 ** ADAM_STEP)
    c2 = 1.0 / (1.0 - ADAM_B2 ** ADAM_STEP)

    def body(w_ref, g_ref, m_ref, v_ref, d_ref, mo_ref, vo_ref):
        gv = g_ref[...]
        mn = ADAM_B1 * m_ref[...] + (1.0 - ADAM_B1) * gv
        vn = ADAM_B2 * v_ref[...] + (1.0 - ADAM_B2) * (gv * gv)
        mo_ref[...] = mn
        vo_ref[...] = vn
        d_ref[...] = -ADAM_LR * ((mn * c1) / (jnp.sqrt(vn * c2) + ADAM_EPS) + ADAM_WD * w_ref[...])

    blk = pl.BlockSpec((tr, cols), lambda i: (i, 0))
    sh = jax.ShapeDtypeStruct((rows, cols), F32)
    return pl.pallas_call(body, grid=(rows // tr,), in_specs=[blk] * 4, out_specs=(blk,) * 3, out_shape=(sh,) * 3,
                          name=name, compiler_params=_cp(1))(w, g, m, v)


MESH = pl.DeviceIdType.MESH
ANY = pl.BlockSpec(memory_space=pl.ANY)


def _place():
    return lax.axis_index("x"), lax.axis_index("y"), lax.axis_index("c")


def _other_chips(x, y):
    return [(1 - x, y), (x, 1 - y), (1 - x, 1 - y)]


def _all_gather(xs, vmem, name):
    na = len(xs)

    def body(*refs):
        x_refs, out_refs = refs[:na], refs[na:2 * na]
        send_sems, recv_sems, local_sems = refs[2 * na:]
        x, y, c = _place()
        me, sibling = (x, y, c), (x, y, 1 - c)
        chips = _other_chips(x, y)

        def blk(p, px, py, pc):
            return out_refs[p].at[4 * px + 2 * py + pc]

        def copy(k, p, block, to, src=None):
            return pltpu.make_async_remote_copy(
                src_ref=blk(p, *block) if src is None else src, dst_ref=blk(p, *block),
                send_sem=send_sems.at[k * na + p], recv_sem=recv_sems.at[k * na + p], device_id=to,
                device_id_type=MESH)

        mine = [pltpu.make_async_copy(x_refs[p], blk(p, *me), local_sems.at[p]) for p in range(na)]
        for cp in mine:
            cp.start()
        first = [copy(0, p, me, sibling, src=x_refs[p]) for p in range(na)]
        first += [copy(1 + j, p, me, (*chip, c), src=x_refs[p]) for j, chip in enumerate(chips) for p in range(na)]
        for cp in first:
            cp.start()
        passed = []
        for j, chip in enumerate(chips):
            for p in range(na):
                copy(1 + j, p, (*chip, c), me).wait_recv()
                fwd = copy(4 + j, p, (*chip, c), sibling)
                fwd.start()
                passed.append(fwd)
        for p in range(na):
            copy(0, p, sibling, me).wait_recv()
        for j, chip in enumerate(chips):
            for p in range(na):
                copy(4 + j, p, (*chip, 1 - c), me).wait_recv()
        for cp in first + passed:
            cp.wait_send()
        for cp in mine:
            cp.wait()

    spec = pl.BlockSpec(memory_space=pltpu.VMEM) if vmem else ANY
    return pl.pallas_call(
        body, out_shape=[jax.ShapeDtypeStruct((N_DEV,) + a.shape, a.dtype) for a in xs],
        in_specs=[spec] * na, out_specs=[spec] * na,
        scratch_shapes=[pltpu.SemaphoreType.DMA((7 * na,)), pltpu.SemaphoreType.DMA((7 * na,)),
                        pltpu.SemaphoreType.DMA((na,))],
        name=name)(*xs)


def _swap_with_sibling(g4s):
    na = len(g4s)

    def body(*refs):
        g_refs, out_refs = refs[:na], refs[na:2 * na]
        send_sems, recv_sems = refs[2 * na:]
        x, y, c = _place()
        cps = [pltpu.make_async_remote_copy(src_ref=g_refs[p].at[:, 1 - c], dst_ref=out_refs[p],
                                            send_sem=send_sems.at[p], recv_sem=recv_sems.at[p],
                                            device_id=(x, y, 1 - c), device_id_type=MESH) for p in range(na)]
        for cp in cps:
            cp.start()
        for cp in cps:
            cp.wait()

    return pl.pallas_call(
        body, out_shape=[jax.ShapeDtypeStruct((g.shape[0],) + g.shape[2:], g.dtype) for g in g4s],
        in_specs=[ANY] * na, out_specs=[ANY] * na,
        scratch_shapes=[pltpu.SemaphoreType.DMA((na,)), pltpu.SemaphoreType.DMA((na,))], name="rs_sibling")(*g4s)


HBM = pl.BlockSpec(memory_space=pltpu.HBM)
SEM = pl.BlockSpec(memory_space=pltpu.SEMAPHORE)
EFFECT = pltpu.SideEffectType.DATAFLOW_SIDE_EFFECTING


def _in_hbm(a):
    return pltpu.with_memory_space_constraint(a, pltpu.HBM)


def _ici_copies(src_refs, land_refs, send_sems, recv_sems, gather):
    na = len(src_refs)
    x, y, c = _place()
    cps = []
    if gather == "sibling":
        for k, (px, py) in enumerate([(x, y)] + _other_chips(x, y)):
            for p in range(na):
                blk = src_refs[p].at[4 * px + 2 * py + c]
                cps.append(pltpu.make_async_remote_copy(
                    src_ref=blk, dst_ref=blk, send_sem=send_sems.at[k * na + p], recv_sem=recv_sems.at[k * na + p],
                    device_id=(x, y, 1 - c), device_id_type=MESH))
        return cps
    for j, (px, py) in enumerate(_other_chips(x, y)):
        for p in range(na):
            src = src_refs[p].at[4 * x + 2 * y + c] if gather else src_refs[p].at[2 * px + py]
            dst = src if gather else land_refs[p].at[j]
            cps.append(pltpu.make_async_remote_copy(
                src_ref=src, dst_ref=dst, send_sem=send_sems.at[j * na + p], recv_sem=recv_sems.at[j * na + p],
                device_id=(px, py, c), device_id_type=MESH))
    return cps


def _ici_start(srcs, land_shapes, deps, *, gather, name):
    na, nd = len(srcs), len(deps)
    lands = [lax.empty(s, a.dtype) for s, a in zip(land_shapes, srcs)]
    nb = na + len(lands)

    def body(*refs):
        src_refs, land_refs = refs[:na], refs[na:nb]
        send_sems, recv_sems = refs[nb + nd], refs[nb + nd + 1]
        token = refs[-1]
        for cp in _ici_copies(src_refs, land_refs, send_sems, recv_sems, gather):
            cp.start()
        token[...] = jnp.zeros_like(token)

    bufs = list(srcs) + lands
    n_sem = (4 if gather == "sibling" else 3) * na
    out = pl.pallas_call(
        body, name=name,
        out_shape=(pltpu.SemaphoreType.DMA((n_sem,)), pltpu.SemaphoreType.DMA((n_sem,)),
                   *[pltpu.HBM(a.shape, a.dtype) for a in bufs], jax.ShapeDtypeStruct((8, LANE), F32)),
        in_specs=[HBM] * nb + [ANY] * nd,
        out_specs=(SEM, SEM, *([HBM] * nb), pl.BlockSpec(memory_space=pltpu.VMEM)),
        input_output_aliases={p: 2 + p for p in range(nb)},
        compiler_params=pltpu.CompilerParams(has_side_effects=EFFECT),
    )(*[_in_hbm(a) for a in bufs], *deps)
    return out[0], out[1], list(out[2:2 + na]), list(out[2 + na:2 + nb]), out[-1]


def _ici_wait(send_sems, recv_sems, srcs, lands, after, *, gather, name):
    na, nd = len(srcs), len(after)
    nb = na + len(lands)

    def body(*refs):
        src_refs, land_refs = refs[:na], refs[na:nb]
        s_sems, r_sems = refs[nb], refs[nb + 1]
        for cp in _ici_copies(src_refs, land_refs, s_sems, r_sems, gather):
            cp.wait_send()
            cp.wait_recv()

    bufs = list(srcs) + list(lands)
    out = pl.pallas_call(
        body, name=name, out_shape=tuple(pltpu.HBM(a.shape, a.dtype) for a in bufs),
        in_specs=[HBM] * nb + [SEM, SEM] + [ANY] * nd, out_specs=tuple([HBM] * nb),
        input_output_aliases={p: p for p in range(nb)},
        compiler_params=pltpu.CompilerParams(has_side_effects=EFFECT),
    )(*bufs, send_sems, recv_sems, *after)
    return list(out[:na]), list(out[na:])


HI_MASK = -65536


def _pack_pairs_xla(a):
    half = a.shape[-1] // 2
    bits = lax.bitcast_convert_type(a.astype(BF16), jnp.uint16).astype(jnp.uint32)
    return lax.bitcast_convert_type((bits[..., half:] << 16) | bits[..., :half], F32)


def _split_pairs(words):
    wv = lax.bitcast_convert_type(words, jnp.int32)
    return lax.bitcast_convert_type(wv << 16, F32), lax.bitcast_convert_type(wv & HI_MASK, F32)


def _join_pairs(lo, hi):
    lo_b = lax.bitcast_convert_type(lo.astype(BF16).astype(F32), jnp.int32)
    hi_b = lax.bitcast_convert_type(hi.astype(BF16).astype(F32), jnp.int32)
    return lax.bitcast_convert_type((hi_b & HI_MASK) | lax.shift_right_logical(lo_b, 16), F32)


def _unpack(xp, *, name, out_rows=None, tr=None, src_fn=None, zero_fn=None):
    r, hw = xp.shape
    out_rows = out_rows or r
    tr = tr or _tile(out_rows, 512, 16)
    src = src_fn or (lambda i: i)
    k = _blocks_per_step(out_rows // tr) if src_fn is not None else 1

    def body(*refs):
        o_ref = refs[k]
        for j in range(k):
            lo, hi = _split_pairs(refs[j][...])
            if zero_fn is not None:
                z = zero_fn(pl.program_id(0) * k + j)
                lo, hi = jnp.where(z, 0.0, lo), jnp.where(z, 0.0, hi)
            o_ref[j * tr:(j + 1) * tr, :hw] = lo.astype(BF16)
            o_ref[j * tr:(j + 1) * tr, hw:] = hi.astype(BF16)

    return pl.pallas_call(
        body, grid=(out_rows // (k * tr),),
        in_specs=[pl.BlockSpec((tr, hw), lambda i, j=j: (src(i * k + j), 0)) for j in range(k)],
        out_specs=pl.BlockSpec((k * tr, 2 * hw), lambda i: (i, 0)),
        out_shape=jax.ShapeDtypeStruct((out_rows, 2 * hw), BF16), name=name, compiler_params=_cp(1))(*([xp] * k))


def _blocks_per_step(n_blocks):
    return max(k for k in range(1, 17) if n_blocks % k == 0)


def _gather_rows(xp, *, name, out_rows, tr, src_fn):
    _, hw = xp.shape
    k = _blocks_per_step(out_rows // tr)

    def body(*refs):
        for j in range(k):
            refs[k][j * tr:(j + 1) * tr, :] = refs[j][...]

    return pl.pallas_call(
        body, grid=(out_rows // (k * tr),),
        in_specs=[pl.BlockSpec((tr, hw), lambda i, j=j: (src_fn(i * k + j), 0)) for j in range(k)],
        out_specs=pl.BlockSpec((k * tr, hw), lambda i: (i, 0)),
        out_shape=jax.ShapeDtypeStruct((out_rows, hw), xp.dtype), name=name, compiler_params=_cp(1))(*([xp] * k))


def _matmul_tn_packed(a, b, name):
    (t, m), (t2, c) = a.shape, b.shape
    assert t == t2
    hw = c // 2
    tm = _tile(m, 1088, LANE)
    tn = _tile(hw, 512, LANE)
    tk = _tile(t, 2176, LANE)
    nk, nj = t // tk, hw // tn

    def body(a_ref, bl_ref, bh_ref, o_ref, acc_lo, acc_hi):
        kk = pl.program_id(2)

        @pl.when(kk == 0)
        def _():
            acc_lo[...] = jnp.zeros_like(acc_lo)
            acc_hi[...] = jnp.zeros_like(acc_hi)

        av = a_ref[...].astype(BF16)
        acc_lo[...] += lax.dot_general(av, bl_ref[...].astype(BF16), TN, preferred_element_type=F32)
        acc_hi[...] += lax.dot_general(av, bh_ref[...].astype(BF16), TN, preferred_element_type=F32)

        @pl.when(kk == nk - 1)
        def _():
            o_ref[...] = _join_pairs(acc_lo[...], acc_hi[...])

    return pl.pallas_call(
        body, grid=(m // tm, nj, nk),
        in_specs=[pl.BlockSpec((tk, tm), lambda i, j, kk: (kk, i)), pl.BlockSpec((tk, tn), lambda i, j, kk: (kk, j)),
                  pl.BlockSpec((tk, tn), lambda i, j, kk: (kk, nj + j))],
        out_specs=pl.BlockSpec((tm, tn), lambda i, j, kk: (i, j)), out_shape=jax.ShapeDtypeStruct((m, hw), F32),
        scratch_shapes=[pltpu.VMEM((tm, tn), F32), pltpu.VMEM((tm, tn), F32)], name=name,
        compiler_params=_cp(3))(a, b, b)


def _add_sibling(g4, recv, cidx, name):
    nchip, _, r, n = g4.shape
    tr = _tile(r, 256, 8)

    def body(c_ref, a_ref, b_ref, o_ref):
        alo, ahi = _split_pairs(a_ref[...])
        blo, bhi = _split_pairs(b_ref[...])
        o_ref[...] = _join_pairs(alo + blo, ahi + bhi)

    grid_spec = pltpu.PrefetchScalarGridSpec(
        num_scalar_prefetch=1, grid=(nchip, r // tr),
        in_specs=[pl.BlockSpec((None, None, tr, n), lambda k, i, c: (k, c[0], i, 0)),
                  pl.BlockSpec((None, tr, n), lambda k, i, c: (k, i, 0))],
        out_specs=pl.BlockSpec((None, tr, n), lambda k, i, c: (k, i, 0)))
    return pl.pallas_call(body, grid_spec=grid_spec, out_shape=jax.ShapeDtypeStruct((nchip, r, n), F32),
                          name=name, compiler_params=_cp(2))(cidx, g4, recv)


def _add_chips(p4, recv3, chip_idx, name):
    _, r, n = p4.shape
    tr = _tile(r, 256, 8)

    def body(k_ref, a_ref, b_ref, o_ref):
        lo, hi = _split_pairs(a_ref[...])
        for j in range(3):
            blo, bhi = _split_pairs(b_ref[j])
            lo, hi = lo + blo, hi + bhi
        o_ref[:, :n] = lo
        o_ref[:, n:] = hi

    grid_spec = pltpu.PrefetchScalarGridSpec(
        num_scalar_prefetch=1, grid=(r // tr,),
        in_specs=[pl.BlockSpec((None, tr, n), lambda i, k: (k[0], i, 0)),
                  pl.BlockSpec((3, tr, n), lambda i, k: (0, i, 0))],
        out_specs=pl.BlockSpec((tr, 2 * n), lambda i, k: (i, 0)))
    return pl.pallas_call(body, grid_spec=grid_spec, out_shape=jax.ShapeDtypeStruct((r, 2 * n), F32),
                          name=name, compiler_params=_cp(1))(chip_idx, p4, recv3)


def _all_reduce_small(xs):
    r, n = xs.shape

    def body(x_ref, o_ref, buf, send_sems, recv_sems):
        x, y, c = _place()
        me = 4 * x + 2 * y + c
        buf[me] = x_ref[...]
        peers = []
        for k in range(1, N_DEV):
            px = (1 - x) if (k >> 2) & 1 else x
            py = (1 - y) if (k >> 1) & 1 else y
            pc = (1 - c) if k & 1 else c
            peers.append((px, py, pc))
        sends = [pltpu.make_async_remote_copy(src_ref=x_ref, dst_ref=buf.at[me], send_sem=send_sems.at[k],
                                              recv_sem=recv_sems.at[k], device_id=peer, device_id_type=MESH)
                 for k, peer in enumerate(peers)]
        for cp in sends:
            cp.start()
        for k, (px, py, pc) in enumerate(peers):
            pltpu.make_async_remote_copy(src_ref=x_ref, dst_ref=buf.at[4 * px + 2 * py + pc],
                                         send_sem=send_sems.at[k], recv_sem=recv_sems.at[k],
                                         device_id=(px, py, pc), device_id_type=MESH).wait_recv()
        for cp in sends:
            cp.wait_send()
        acc = buf[0]
        for dv in range(1, N_DEV):
            acc = acc + buf[dv]
        o_ref[...] = acc

    vm = pl.BlockSpec(memory_space=pltpu.VMEM)
    return pl.pallas_call(
        body, out_shape=jax.ShapeDtypeStruct((r, n), F32), in_specs=[vm], out_specs=vm,
        scratch_shapes=[pltpu.VMEM((N_DEV, r, n), F32), pltpu.SemaphoreType.DMA((7,)), pltpu.SemaphoreType.DMA((7,))],
        name="all_reduce_small", compiler_params=pltpu.CompilerParams(vmem_limit_bytes=VMEM_LIMIT))(xs)


def _pack_rows(arrs, quantum):
    parts, sizes = [], []
    for a in arrs:
        f = a.reshape(-1)
        pad = (-f.shape[0]) % quantum
        if pad:
            f = jnp.pad(f, (0, pad))
        parts.append(f)
        sizes.append(f.shape[0])
    return jnp.concatenate(parts).reshape(-1, LANE), sizes


def _unpack_rows(buf, sizes, shapes):
    flat = buf.reshape(-1)
    out, off = [], 0
    for sz, sh in zip(sizes, shapes):
        n = math.prod(sh)
        out.append(flat[off:off + n].reshape(sh))
        off += sz
    return out


def kernel(x, meta_tokens, mix_norm_g, w_in, conv_dw, conv_b, conv_ln_g, conv_ln_b, w_conv_out, gqa_q_norm_g, gqa_k_norm_g, w_gqa_out, mla_q_norm_g, w_mla_uq, mla_kv_norm_g, w_mla_ukv, w_mla_out, gate_b, w_out, ffn_norm_g, w_ffn_gate, w_ffn_up, w_ffn_down, final_norm_g, loss_target, m_meta_tokens, m_mix_norm_g, m_w_in, m_conv_dw, m_conv_b, m_conv_ln_g, m_conv_ln_b, m_w_conv_out, m_gqa_q_norm_g, m_gqa_k_norm_g, m_w_gqa_out, m_mla_q_norm_g, m_w_mla_uq, m_mla_kv_norm_g, m_w_mla_ukv, m_w_mla_out, m_gate_b, m_w_out, m_ffn_norm_g, m_w_ffn_gate, m_w_ffn_up, m_w_ffn_down, m_final_norm_g, v_meta_tokens, v_mix_norm_g, v_w_in, v_conv_dw, v_conv_b, v_conv_ln_g, v_conv_ln_b, v_w_conv_out, v_gqa_q_norm_g, v_gqa_k_norm_g, v_w_gqa_out, v_mla_q_norm_g, v_w_mla_uq, v_mla_kv_norm_g, v_w_mla_ukv, v_w_mla_out, v_gate_b, v_w_out, v_ffn_norm_g, v_w_ffn_gate, v_w_ffn_up, v_w_ffn_down, v_final_norm_g):
    w = dict(meta_tokens=meta_tokens, mix_norm_g=mix_norm_g, w_in=w_in, conv_dw=conv_dw, conv_b=conv_b,
             conv_ln_g=conv_ln_g, conv_ln_b=conv_ln_b, w_conv_out=w_conv_out, gqa_q_norm_g=gqa_q_norm_g,
             gqa_k_norm_g=gqa_k_norm_g, w_gqa_out=w_gqa_out, mla_q_norm_g=mla_q_norm_g, w_mla_uq=w_mla_uq,
             mla_kv_norm_g=mla_kv_norm_g, w_mla_ukv=w_mla_ukv, w_mla_out=w_mla_out, gate_b=gate_b, w_out=w_out,
             ffn_norm_g=ffn_norm_g, w_ffn_gate=w_ffn_gate, w_ffn_up=w_ffn_up, w_ffn_down=w_ffn_down,
             final_norm_g=final_norm_g)
    mom1 = dict(zip(W_NAMES, (m_meta_tokens, m_mix_norm_g, m_w_in, m_conv_dw, m_conv_b, m_conv_ln_g, m_conv_ln_b, m_w_conv_out, m_gqa_q_norm_g, m_gqa_k_norm_g, m_w_gqa_out, m_mla_q_norm_g, m_w_mla_uq, m_mla_kv_norm_g, m_w_mla_ukv, m_w_mla_out, m_gate_b, m_w_out, m_ffn_norm_g, m_w_ffn_gate, m_w_ffn_up, m_w_ffn_down, m_final_norm_g)))
    mom2 = dict(zip(W_NAMES, (v_meta_tokens, v_mix_norm_g, v_w_in, v_conv_dw, v_conv_b, v_conv_ln_g, v_conv_ln_b, v_w_conv_out, v_gqa_q_norm_g, v_gqa_k_norm_g, v_w_gqa_out, v_mla_q_norm_g, v_w_mla_uq, v_mla_kv_norm_g, v_w_mla_ukv, v_w_mla_out, v_gate_b, v_w_out, v_ffn_norm_g, v_w_ffn_gate, v_w_ffn_up, v_w_ffn_down, v_final_norm_g)))

    nb, seq, d = x.shape
    depth = w_in.shape[0]
    c_conv = conv_b.shape[1]
    kw = conv_dw.shape[1]
    hd = HEAD_DIM
    hq = w_gqa_out.shape[1] // hd
    nh = w_mla_out.shape[1] // hd
    qr, kvr = mla_q_norm_g.shape[1], mla_kv_norm_g.shape[1]
    d_in = w_in.shape[2] * N_DEV
    hkv = (d_in - 2 * c_conv - hq * hd - qr - kvr - MLA_ROPE_DIM - N_BRANCH * d) // (2 * hd)
    l_valid = seq + N_META
    lp = -(-(l_valid + kw // 2) // LANE) * LANE
    m_tok = nb * lp
    mla_qk = hd + MLA_ROPE_DIM
    gqa_scale, mla_scale = 1.0 / math.sqrt(hd), 1.0 / math.sqrt(mla_qk)

    off_gate = 0
    off_conv = N_BRANCH * d
    off_q = off_conv + 2 * c_conv
    off_cq = off_q + (hq + 2 * hkv) * hd
    off_ckv = off_cq + qr
    off_kpe = off_ckv + kvr
    n_proj = off_kpe + LANE

    x_id, y_id, c_id = _place()
    me = 4 * x_id + 2 * y_id + c_id

    tm_wide = _tile(lp, 272, 16)
    tm_mid = _tile(lp, 544, 16)

    sm_buf, sm_sizes = _pack_rows([meta_tokens, conv_dw], 8 * LANE)
    sm_all = _all_gather([sm_buf], True, "ag_small")[0]
    sm_all = sm_all.reshape(N_DEV, -1)
    o0 = 0
    meta_all = sm_all[:, o0:o0 + meta_tokens.size].reshape((N_DEV,) + meta_tokens.shape)
    meta_full = jnp.transpose(meta_all, (1, 0, 2)).reshape(N_META, d)
    o0 = sm_sizes[0]
    dw_all = sm_all[:, o0:o0 + conv_dw.size].reshape((N_DEV, depth, kw, c_conv // N_DEV))
    dw_full = jnp.transpose(dw_all, (1, 2, 0, 3)).reshape(depth, kw, c_conv)

    n_head_cols = d_in - N_BRANCH * d - MLA_ROPE_DIM
    rb = MLA_ROPE_DIM
    nb_gate, nb_head = N_BRANCH * d // rb, n_head_cols // rb
    assert n_head_cols % rb == 0 and (N_BRANCH * d) % rb == 0 and n_proj == (nb_gate + nb_head + 2) * rb

    def in_src(i):
        return jnp.where(i < nb_gate, nb_head + 1 + i,
                         jnp.where(i < nb_gate + nb_head, i - nb_gate, jnp.where(i == nb_gate + nb_head, nb_head, 0)))

    def in_dst(i):
        return jnp.where(i < nb_head, nb_gate + i, jnp.where(i == nb_head, nb_gate + nb_head, i - nb_head - 1))

    def uq_src(i):
        j = i - 2 * nh
        return jnp.where(i < 2 * nh, 3 * (i // 2) + i % 2, 3 * (j // 2) + 2)

    def uq_dst(i):
        return jnp.where(i % 3 < 2, 2 * (i // 3) + i % 3, 2 * nh + 2 * (i // 3))

    ag_pending, tokens = {}, []
    for i in range(depth):
        for gi, names in enumerate(GROUPS):
            shards = [_pack_pairs_xla(w[n][i] if n in ROW_SHARDED else w[n][i].T) for n in names]
            bufs = [lax.dynamic_update_slice(lax.empty((N_DEV,) + a.shape, F32), a[None], (me, 0, 0))
                    for a in shards]
            *pend, tok = _ici_start(bufs, [], tokens[-1:], gather=True, name="ag_ici_start_%d_%d" % (i, gi))
            ag_pending[i, gi] = pend
            tokens.append(tok)

    sib_pending = {}

    def sibling_start(i, gi, after):
        bufs, _ = _ici_wait(*ag_pending[i, gi], after, gather=True, name="ag_ici_wait_%d_%d" % (i, gi))
        *sib_pending[i, gi], tok = _ici_start(bufs, [], [], gather="sibling", name="ag_sib_start_%d_%d" % (i, gi))
        return tok

    def all_gather_group(i, gi, after):
        gath, _ = _ici_wait(*sib_pending[i, gi], after, gather="sibling", name="ag_sib_wait_%d_%d" % (i, gi))
        wl = {}
        for n, g in zip(GROUPS[gi], gath):
            pk = g.reshape(-1, g.shape[-1])
            if n == 'w_in':
                wl[n] = _unpack(pk, name="unpack_w_in", out_rows=n_proj, tr=rb, src_fn=in_src,
                                zero_fn=lambda i: i == nb_gate + nb_head + 1)
            elif n == 'w_mla_uq':
                wl[n] = _unpack(pk, name="unpack_w_mla_uq", out_rows=2 * nh * hd, tr=rb, src_fn=uq_src,
                                zero_fn=lambda i: (i >= 2 * nh) & ((i - 2 * nh) % 2 == 1))
            elif n == 'w_mla_ukv':
                wl[n] = _unpack(pk, name="unpack_w_mla_ukv", out_rows=2 * nh * hd, tr=hd,
                                src_fn=lambda i: jnp.where(i < nh, 2 * i, 2 * (i - nh) + 1))
            else:
                wl[n] = _unpack(pk, name="unpack_" + n)
        return wl

    full = [None] * depth

    cos_g, sin_g = _rope_tables(lp, l_valid, hd // 4)
    cos_m, sin_m = _rope_tables(lp, l_valid, MLA_ROPE_DIM // 4)

    def vec2(a):
        return a.reshape(1, -1)

    meta_b = jnp.broadcast_to(meta_full[None], (nb, N_META, d))
    h = jnp.concatenate([meta_b, x, jnp.zeros((nb, lp - l_valid, d), F32)], axis=1).reshape(m_tok, d)
    tgt = jnp.pad(loss_target, ((0, 0), (N_META, lp - l_valid), (0, 0))).reshape(m_tok, d)

    saved = []
    first_tok = [sibling_start(0, 0, tokens[-1:])]
    sib_tok = []
    for i in range(depth):
        s = {}
        s['h_in'] = h
        u = _rms_fwd(h, d, 0, vec2(mix_norm_g[i]), tm_wide, "mix_norm_fwd")
        wl = full[i] = all_gather_group(i, 0, first_tok if i == 0 else [h])
        proj = _matmul(u, wl['w_in'], mode="nt", out_dtype=F32, name="mm_in", deps=sib_tok)
        cc = _conv_fwd(proj, dw_full[i], vec2(conv_b[i]), c_conv=c_conv, off_a=off_conv, nb=nb, lp=lp, l_valid=l_valid)
        sc = _ln_silu_fwd(cc, vec2(conv_ln_g[i]), vec2(conv_ln_b[i]), tm_mid)
        ya = _matmul(sc, wl['w_conv_out'], mode="nt", out_dtype=F32, name="mm_conv_out")
        qh, kh, vh = _gqa_prep_fwd(proj, vec2(gqa_q_norm_g[i]), vec2(gqa_k_norm_g[i]), cos_g, sin_g,
                                   off_q=off_q, hq=hq, hkv=hkv, nb=nb, lp=lp, tm=tm_mid, q_scale=gqa_scale)
        ob = _attn_fwd(qh, kh, vh, l_valid=l_valid, name="gqa_attn_fwd")
        yb = _matmul(ob, wl['w_gqa_out'], mode="nt", out_dtype=F32, name="mm_gqa_out")
        cqn = _rms_fwd(proj, qr, _cb(off_cq, qr), vec2(mla_q_norm_g[i]), tm_mid, "mla_q_norm_fwd")
        ckvn = _rms_fwd(proj, kvr, _cb(off_ckv, kvr), vec2(mla_kv_norm_g[i]), tm_mid, "mla_kv_norm_fwd")
        qm = _matmul(cqn, wl['w_mla_uq'], mode="nt", out_dtype=F32, name="mm_mla_uq")
        kvm = _matmul(ckvn, wl['w_mla_ukv'], mode="nt", out_dtype=F32, name="mm_mla_ukv")
        qc, kc, vc = _mla_prep_fwd(qm, kvm, proj, cos_m, sin_m, off_kpe=off_kpe, nh=nh, nb=nb, lp=lp, tm=tm_mid,
                                   q_scale=mla_scale)
        oc = _attn_fwd(qc, kc, vc, l_valid=l_valid, name="mla_attn_fwd")
        yc = _matmul(oc, wl['w_mla_out'], mode="nt", out_dtype=F32, name="mm_mla_out")
        merged = _gate_fwd(proj, vec2(gate_b[i]), ya, yb, yc, d=d, tm=tm_wide)
        h2 = _matmul(merged, wl['w_out'], mode="nn", out_dtype=F32, name="mm_out", residual=h)
        vn = _rms_fwd(h2, d, 0, vec2(ffn_norm_g[i]), tm_wide, "ffn_norm_fwd")
        wl.update(all_gather_group(i, 1, [h2, sibling_start(0, 1, [h2])] if i == 0 else [h2]))
        gt, up, act = _ffn_up_fwd(vn, wl['w_ffn_gate'], wl['w_ffn_up'])
        ahead = [sibling_start(i + 1, 0, [h2])] if i + 1 < depth else []
        h = _matmul(act, wl['w_ffn_down'], mode="nn", out_dtype=F32, name="mm_ffn_down", residual=h2, deps=ahead)
        sib_tok = [sibling_start(i + 1, 1, [h])] if i + 1 < depth else []
        s.update(u=u, proj=proj, cc=cc, sc=sc, ya=ya, qh=qh, kh=kh, vh=vh, ob=ob, yb=yb, cqn=cqn, ckvn=ckvn,
                 qc=qc, kc=kc, vc=vc, oc=oc, yc=yc, merged=merged, h2=h2, vn=vn, gt=gt, up=up, act=act)
        saved.append(s)

    dh, dg_final, loss_part = _loss_head(h, vec2(final_norm_g), tgt, lp=lp, l_valid=l_valid, tm=tm_wide)

    gsmall = {n: [None] * depth for n in SMALL if n != 'final_norm_g'}
    gdw = [None] * depth
    gbig = [{} for _ in range(depth)]
    cidx = jnp.reshape(c_id, (1,)).astype(jnp.int32)
    chip_idx = jnp.reshape(2 * x_id + y_id, (1,)).astype(jnp.int32)

    def reduce_scatter_start(gl, i, gi):
        names = GROUPS[gi]
        g4s = [gl[n].reshape(N_DEV // 2, 2, -1, gl[n].shape[-1]) for n in names]
        from_sibling = _swap_with_sibling(g4s)
        p4s = [_add_sibling(g, r, cidx, "rs_add_sibling_" + n) for n, g, r in zip(names, g4s, from_sibling)]
        *pend, tok = _ici_start(p4s, [(3,) + p.shape[1:] for p in p4s], [], gather=False,
                                name="rs_ici_start_%d_%d" % (i, gi))
        return pend, tok

    def reduce_scatter_finish(pend, after, i, gi):
        p4s, from_chips = _ici_wait(*pend, after, gather=False, name="rs_ici_wait_%d_%d" % (i, gi))
        out = {}
        for n, p, r in zip(GROUPS[gi], p4s, from_chips):
            g = _add_chips(p, r, chip_idx, "rs_add_chips_" + n)
            out[n] = g if n in ROW_SHARDED else g.T
        return out

    rs_pending = [None, None]
    rs_token = []
    for i in reversed(range(depth)):
        s = saved[i]
        proj = s['proj']
        wl = full[i]
        gl = {}
        dhb = dh if i == depth - 1 else dhb
        dgt, dup = _ffn_down_bwd(dhb, wl['w_ffn_down'], s['gt'], s['up'], rs_token)
        gl['w_ffn_down'] = _matmul_tn_packed(s['act'], dhb, "mm_ffn_down_dw")
        dvn = _matmul(dgt, wl['w_ffn_gate'], mode="nn", out_dtype=F32, name="mm_ffn_gate_dx")
        dvn = _matmul(dup, wl['w_ffn_up'], mode="nn", out_dtype=F32, name="mm_ffn_up_dx", residual=dvn)
        gl['w_ffn_gate'] = _matmul_tn_packed(dgt, s['vn'], "mm_ffn_gate_dw")
        gl['w_ffn_up'] = _matmul_tn_packed(dup, s['vn'], "mm_ffn_up_dw")
        dh2, gsmall['ffn_norm_g'][i], dh2b = _rms_bwd(s['h2'], d, 0, vec2(ffn_norm_g[i]), dvn, tm_wide, F32,
                                                      "ffn_norm_bwd", add=dh, bf16_copy=True)
        started, tok = reduce_scatter_start(gl, i, 1)
        if rs_pending[1] is not None:
            gbig[i + 1].update(reduce_scatter_finish(rs_pending[1], [dh2, tok], i + 1, 1))
        rs_pending[1] = started
        dmg = _matmul(dh2b, wl['w_out'], mode="nt", out_dtype=F32, name="mm_out_dx", deps=[tok])
        gl['w_out'] = _matmul_tn_packed(s['merged'], dh2b, "mm_out_dw")
        dya, dyb, dyc, dg0, dg1, dg2, db0, db1, db2 = _gate_bwd(dmg, proj, vec2(gate_b[i]), s['ya'], s['yb'],
                                                                s['yc'], d=d, tm=tm_wide)
        gsmall['gate_b'][i] = jnp.concatenate([db0, db1, db2], axis=1)
        dsc = _matmul(dya, wl['w_conv_out'], mode="nn", out_dtype=F32, name="mm_conv_out_dx")
        gl['w_conv_out'] = _matmul_tn_packed(dya, s['sc'], "mm_conv_out_dw")
        dcc, gsmall['conv_ln_g'][i], gsmall['conv_ln_b'][i] = _ln_silu_bwd(
            s['cc'], vec2(conv_ln_g[i]), vec2(conv_ln_b[i]), dsc, tm_mid)
        da, dgc, gdw[i], gsmall['conv_b'][i] = _conv_bwd(proj, dcc, dw_full[i], c_conv=c_conv, off_a=off_conv,
                                                         nb=nb, lp=lp, l_valid=l_valid)
        dob = _matmul(dyb, wl['w_gqa_out'], mode="nn", out_dtype=BF16, name="mm_gqa_out_dx")
        gl['w_gqa_out'] = _matmul_tn_packed(dyb, s['ob'], "mm_gqa_out_dw")
        dqh, dkh, dvh = _attn_bwd(s['qh'], s['kh'], s['vh'], dob, l_valid=l_valid, name="gqa_attn_bwd")
        dq, dkv, gsmall['gqa_q_norm_g'][i], gsmall['gqa_k_norm_g'][i] = _gqa_prep_bwd(
            proj, vec2(gqa_q_norm_g[i]), vec2(gqa_k_norm_g[i]), cos_g, sin_g, dqh, dkh, dvh,
            off_q=off_q, hq=hq, hkv=hkv, nb=nb, lp=lp, tm=tm_mid, q_scale=gqa_scale)
        doc = _matmul(dyc, wl['w_mla_out'], mode="nn", out_dtype=BF16, name="mm_mla_out_dx")
        gl['w_mla_out'] = _matmul_tn_packed(dyc, s['oc'], "mm_mla_out_dw")
        dqc, dkc, dvc = _attn_bwd(s['qc'], s['kc'], s['vc'], doc, l_valid=l_valid, name="mla_attn_bwd")
        dqm, dkvm, dkpe = _mla_prep_bwd(dqc, dkc, dvc, cos_m, sin_m, nh=nh, nb=nb, lp=lp, tm=tm_mid,
                                        q_scale=mla_scale)
        dcqn = _matmul(dqm, wl['w_mla_uq'], mode="nn", out_dtype=F32, name="mm_mla_uq_dx")
        guq = _matmul_tn_packed(dqm, s['cqn'], "mm_mla_uq_dw")
        dckvn = _matmul(dkvm, wl['w_mla_ukv'], mode="nn", out_dtype=F32, name="mm_mla_ukv_dx")
        gukv = _matmul_tn_packed(dkvm, s['ckvn'], "mm_mla_ukv_dw")
        dcq, gsmall['mla_q_norm_g'][i] = _rms_bwd(proj, qr, _cb(off_cq, qr), vec2(mla_q_norm_g[i]), dcqn, tm_mid,
                                                  BF16, "mla_q_norm_bwd")
        dckv, gsmall['mla_kv_norm_g'][i] = _rms_bwd(proj, kvr, _cb(off_ckv, kvr), vec2(mla_kv_norm_g[i]), dckvn,
                                                    tm_mid, BF16, "mla_kv_norm_bwd")
        gl['w_mla_uq'] = _gather_rows(guq, name="perm_mla_uq_dw", out_rows=nh * mla_qk, tr=rb, src_fn=uq_dst)
        gl['w_mla_ukv'] = _gather_rows(gukv, name="perm_mla_ukv_dw", out_rows=2 * nh * hd, tr=hd,
                                       src_fn=lambda i: i // 2 + nh * (i % 2))
        dproj = jnp.concatenate([dg0, dg1, dg2, da, dgc, dq, dkv, dcq, dckv, dkpe], axis=1)
        du = _matmul(dproj, wl['w_in'], mode="nn", out_dtype=F32, name="mm_in_dx")
        gin = _matmul_tn_packed(dproj, s['u'], "mm_in_dw")
        gl['w_in'] = _gather_rows(gin, name="perm_in_dw", out_rows=d_in, tr=rb, src_fn=in_dst)
        dh, gsmall['mix_norm_g'][i], dhb = _rms_bwd(s['h_in'], d, 0, vec2(mix_norm_g[i]), du, tm_wide, F32,
                                                    "mix_norm_bwd", add=dh2, bf16_copy=True)
        started, tok = reduce_scatter_start(gl, i, 0)
        if rs_pending[0] is not None:
            gbig[i + 1].update(reduce_scatter_finish(rs_pending[0], [dh, tok], i + 1, 0))
        rs_pending[0], rs_token = started, [tok]

    def adamw_big(names):
        for n in names:
            sh = w[n].shape
            two_d = (sh[0] * sh[1], sh[2])
            grads[n] = jnp.stack([gbig[i][n] for i in range(depth)])
            dl, mn, vn_ = _adamw(w[n].reshape(two_d), grads[n].reshape(two_d), mom1[n].reshape(two_d),
                                 mom2[n].reshape(two_d), "adamw_" + n)
            delta[n], new_m[n], new_v[n] = dl.reshape(sh), mn.reshape(sh), vn_.reshape(sh)

    grads, delta, new_m, new_v = {}, {}, {}, {}
    gbig[0].update(reduce_scatter_finish(rs_pending[1], rs_token, 0, 1))
    adamw_big(GROUPS[1])
    gbig[0].update(reduce_scatter_finish(rs_pending[0], [delta[GROUPS[1][-1]]], 0, 0))
    adamw_big(GROUPS[0])

    dh0 = dh.reshape(nb, lp, d)
    grad_x = dh0[:, N_META:l_valid]
    gmeta_full = jnp.sum(dh0[:, :N_META], axis=0)

    small_list = [jnp.stack(gsmall[n]).reshape(w[n].shape) for n in SMALL if n != 'final_norm_g']
    small_list += [dg_final.reshape(final_norm_g.shape), gmeta_full, jnp.stack(gdw), loss_part]
    ar_buf, ar_sizes = _pack_rows(small_list, 8 * LANE)
    ar = _all_reduce_small(ar_buf)
    ar_shapes = [w[n].shape for n in SMALL] + [(N_META, d), (depth, kw, c_conv), (1, LANE)]
    ar_out = _unpack_rows(ar, ar_sizes, ar_shapes)
    for n, g in zip(SMALL, ar_out):
        grads[n] = g
    dcol = d // N_DEV
    grads['meta_tokens'] = lax.dynamic_slice(ar_out[len(SMALL)], (0, me * dcol), (N_META, dcol))
    ccol = c_conv // N_DEV
    grads['conv_dw'] = lax.dynamic_slice(ar_out[len(SMALL) + 1], (0, 0, me * ccol),
                                         (depth, kw, ccol)).reshape(conv_dw.shape)
    loss = ar_out[len(SMALL) + 2][0, 0]

    rest = [n for n in W_NAMES if n not in BIG]
    pw, psz = _pack_rows([w[n] for n in rest], 8 * LANE)
    pg, _ = _pack_rows([grads[n] for n in rest], 8 * LANE)
    pm, _ = _pack_rows([mom1[n] for n in rest], 8 * LANE)
    pv, _ = _pack_rows([mom2[n] for n in rest], 8 * LANE)
    dl, mn, vn_ = _adamw(pw, pg, pm, pv, "adamw_small")
    shapes = [w[n].shape for n in rest]
    for n, a, b, c in zip(rest, _unpack_rows(dl, psz, shapes), _unpack_rows(mn, psz, shapes),
                          _unpack_rows(vn_, psz, shapes)):
        delta[n], new_m[n], new_v[n] = a, b, c

    return (loss, grad_x, *[grads[n] for n in W_NAMES], *[delta[n] for n in W_NAMES],
            *[new_m[n] for n in W_NAMES], *[new_v[n] for n in W_NAMES])
```
